```python
import math
import jax, jax.numpy as jnp
from jax import lax
import numpy as np

D_MODEL = 1024
BATCH = 8
SEQ = 2048
DEPTH = 2
DEC_BATCH = 128
DEC_SEQ = 1
PAST_LEN = 16384
PAGE_SIZE = 128

N_EVEN = (DEPTH + 1) // 2
N_ODD = DEPTH // 2
CHUNK = 64
EPS = 1e-6
NEG = -1e30
A_HEADS = 4
A_DK = 128
A_DV = 128
CONV_W = 4
A_QK = A_HEADS * A_DK
A_V = A_HEADS * A_DV
A_CONV_CH = 2 * A_QK + A_V
B_HEADS = 4
B_DK = 128
B_DV = 128
B_QK = B_HEADS * B_DK
B_V = B_HEADS * B_DV
ROPE_BASE = 10000.0
IN_A = A_CONV_CH + 2 * A_HEADS + A_V + 2 * B_QK + 2 * B_V
MIX_A = A_V + B_V
C_HEADS = 4
C_DK = 128
C_DV = 256
C_QK = C_HEADS * C_DK
C_V = C_HEADS * C_DV
IN_C = 2 * C_QK + 2 * C_V + 2 * C_HEADS
D_FF = 4 * D_MODEL

kernel_name = 'hybrid_gdn_retention_mlstm_step'


def rmsnorm(x, g):
    xf = x.astype(jnp.float32)
    y = xf * lax.rsqrt(jnp.mean(xf * xf, axis=-1, keepdims=True) + EPS)
    return (y * g.astype(jnp.float32)).astype(x.dtype)


def _l2norm(t):
    return t * lax.rsqrt(jnp.sum(t * t, axis=-1, keepdims=True) + EPS)


def _heads(t, h):
    b, l, _ = t.shape
    return t.reshape(b, l, h, -1).transpose(0, 2, 1, 3)


def _merge(t):
    b, h, l, d = t.shape
    return t.transpose(0, 2, 1, 3).reshape(b, l, h * d)


def _rope(t, pos):
    half = t.shape[-1] // 2
    inv = ROPE_BASE ** (-jnp.arange(half, dtype=jnp.float32) / half)
    ang = pos.astype(jnp.float32)[:, None] * inv[None, :]
    cos, sin = jnp.cos(ang), jnp.sin(ang)
    t1, t2 = t[..., :half], t[..., half:]
    return jnp.concatenate([t1 * cos - t2 * sin, t1 * sin + t2 * cos], axis=-1)


def causal_conv(u, buf, w):
    l = u.shape[1]
    xp = jnp.concatenate([buf, u], axis=1)
    out = sum(xp[:, i:i + l] * w[i] for i in range(CONV_W))
    return out, xp[:, -(CONV_W - 1):]


def _chunk_layout(l):
    c = min(CHUNK, l)
    n = -(-l // c)
    return c, n, n * c - l


def _to_chunks(t, c, n, pad, pad_value=0.0):
    widths = [(0, 0), (0, 0), (0, pad)] + [(0, 0)] * (t.ndim - 3)
    t = jnp.pad(t, widths, constant_values=pad_value)
    t = t.reshape(t.shape[:2] + (n, c) + t.shape[3:])
    return jnp.moveaxis(t, 2, 0)


def _from_chunks(o, l):
    o = jnp.moveaxis(o, 0, 2)
    o = o.reshape(o.shape[:2] + (-1,) + o.shape[4:])
    return o[:, :, :l]


def _masks(c):
    idx = jnp.arange(c)
    return idx[:, None] > idx[None, :], idx[:, None] >= idx[None, :]


def gated_delta_rule(q, k, v, beta, log_a, s0):
    l = q.shape[2]
    c, n, pad = _chunk_layout(l)
    q, k, v, beta, log_a = (_to_chunks(t, c, n, pad) for t in (q, k, v, beta, log_a))
    strict, causal = _masks(c)
    g = jnp.cumsum(log_a, axis=-1)
    diff = g[..., :, None] - g[..., None, :]
    decay_strict = jnp.exp(jnp.where(strict, diff, -jnp.inf))
    decay_causal = jnp.exp(jnp.where(causal, diff, -jnp.inf))
    kb = k * beta[..., None]
    lower = jnp.einsum('nbhik,nbhjk->nbhij', kb, k) * decay_strict
    a_mat = lower + jnp.eye(c, dtype=lower.dtype)
    dv = v.shape[-1]
    rhs = jnp.concatenate([v * beta[..., None], kb * jnp.exp(g)[..., None]], axis=-1)
    sol = lax.linalg.triangular_solve(a_mat, rhs, left_side=True, lower=True)
    u, w = sol[..., :dv], sol[..., dv:]
    qk = jnp.einsum('nbhik,nbhjk->nbhij', q, k) * decay_causal
    q_dec = q * jnp.exp(g)[..., None]
    k_dec = k * jnp.exp(g[..., -1:] - g)[..., None]
    g_last = jnp.exp(g[..., -1])

    def step(s, xs):
        u_c, w_c, qk_c, qd_c, kd_c, gl_c = xs
        v_new = u_c - jnp.einsum('bhik,bhkv->bhiv', w_c, s)
        o = jnp.einsum('bhik,bhkv->bhiv', qd_c, s) + jnp.einsum('bhij,bhjv->bhiv', qk_c, v_new)
        s = s * gl_c[..., None, None] + jnp.einsum('bhjk,bhjv->bhkv', kd_c, v_new)
        return s, o

    s, o = lax.scan(step, s0, (u, w, qk, q_dec, k_dec, g_last))
    return _from_chunks(o, l), s


def decayed_linear_attention(q, k, v, log_a, s0):
    l = q.shape[2]
    c, n, pad = _chunk_layout(l)
    q, k, v, log_a = (_to_chunks(t, c, n, pad) for t in (q, k, v, log_a))
    _, causal = _masks(c)
    g = jnp.cumsum(log_a, axis=-1)
    diff = g[..., :, None] - g[..., None, :]
    decay_causal = jnp.exp(jnp.where(causal, diff, -jnp.inf))
    qk = jnp.einsum('nbhik,nbhjk->nbhij', q, k) * decay_causal
    q_dec = q * jnp.exp(g)[..., None]
    k_dec = k * jnp.exp(g[..., -1:] - g)[..., None]
    g_last = jnp.exp(g[..., -1])

    def step(s, xs):
        qk_c, qd_c, kd_c, v_c, gl_c = xs
        o = jnp.einsum('bhik,bhkv->bhiv', qd_c, s) + jnp.einsum('bhij,bhjv->bhiv', qk_c, v_c)
        s = s * gl_c[..., None, None] + jnp.einsum('bhjk,bhjv->bhkv', kd_c, v_c)
        return s, o

    s, o = lax.scan(step, s0, (qk, q_dec, k_dec, v, g_last))
    return _from_chunks(o, l), s


def mlstm_chunkwise(q, k, v, i_pre, log_f, c0, n0, m0):
    l = q.shape[2]
    c, n, pad = _chunk_layout(l)
    q, k, v, log_f = (_to_chunks(t, c, n, pad) for t in (q, k, v, log_f))
    i_pre = _to_chunks(i_pre, c, n, pad, NEG)
    _, causal = _masks(c)
    b = jnp.cumsum(log_f, axis=-1)
    d_log = jnp.where(causal, b[..., :, None] - b[..., None, :] + i_pre[..., None, :], NEG)
    d_max = jnp.max(d_log, axis=-1)
    qk = jnp.einsum('nbhik,nbhjk->nbhij', q, k)
    k_log = b[..., -1:] - b + i_pre

    def step(carry, xs):
        cm, nv, m = carry
        q_c, k_c, v_c, b_c, dl_c, dm_c, qk_c, kl_c = xs
        inter = b_c + m[..., None]
        m_row = jnp.maximum(inter, dm_c)
        w_intra = jnp.exp(dl_c - m_row[..., None]) * qk_c
        w_inter = jnp.exp(inter - m_row)
        num = w_inter[..., None] * jnp.einsum('bhik,bhkv->bhiv', q_c, cm) + jnp.einsum('bhij,bhjv->bhiv', w_intra, v_c)
        den = w_inter * jnp.einsum('bhik,bhk->bhi', q_c, nv) + jnp.sum(w_intra, axis=-1)
        h = num / jnp.maximum(jnp.abs(den), jnp.exp(-m_row))[..., None]
        m_new = m_row[..., -1]
        f_state = jnp.exp(b_c[..., -1] + m - m_new)
        kw = k_c * jnp.exp(kl_c - m_new[..., None])[..., None]
        cm = cm * f_state[..., None, None] + jnp.einsum('bhjk,bhjv->bhkv', kw, v_c)
        nv = nv * f_state[..., None] + jnp.sum(kw, axis=-2)
        return (cm, nv, m_new), h

    (cm, nv, m), h = lax.scan(step, (c0, n0, m0), (q, k, v, b, d_log, d_max, qk, k_log))
    return _from_chunks(h, l), cm, nv, m


def mixer_ab(xn, pos, conv_buf, s_delta, s_ret, w_in, w_conv, a_log, dt_bias, gdn_gain, ret_gain, w_out):
    f32 = jnp.float32
    proj = (xn @ w_in).astype(f32)
    cuts = np.cumsum([A_CONV_CH, A_HEADS, A_HEADS, A_V, B_QK, B_QK, B_V]).tolist()
    qkv_a, a_raw, b_raw, g_a, q_b, k_b, v_b, g_b = jnp.split(proj, cuts, axis=-1)
    conv_out, new_buf = causal_conv(qkv_a, conv_buf.astype(f32), w_conv.astype(f32))
    conv_out = jax.nn.silu(conv_out)
    q_a, k_a, v_a = jnp.split(conv_out, [A_QK, 2 * A_QK], axis=-1)
    q_a = _l2norm(_heads(q_a, A_HEADS)) * A_DK ** -0.5
    k_a = _l2norm(_heads(k_a, A_HEADS))
    v_a = _heads(v_a, A_HEADS)
    beta = jax.nn.sigmoid(b_raw).transpose(0, 2, 1)
    log_alpha = (-jnp.exp(a_log.astype(f32)) * jax.nn.softplus(a_raw + dt_bias.astype(f32))).transpose(0, 2, 1)
    o_a, s_delta_new = gated_delta_rule(q_a, k_a, v_a, beta, log_alpha, s_delta.astype(f32))
    o_a = rmsnorm(o_a, gdn_gain) * jax.nn.silu(_heads(g_a, A_HEADS))
    q_b = _rope(_heads(q_b, B_HEADS), pos)
    k_b = _rope(_heads(k_b, B_HEADS), pos) * B_DK ** -0.5
    v_b = _heads(v_b, B_HEADS)
    log_gamma = jnp.log1p(-jnp.exp2(-5.0 - jnp.arange(B_HEADS, dtype=f32)))
    log_gamma = jnp.broadcast_to(log_gamma[None, :, None], q_b.shape[:3])
    o_b, s_ret_new = decayed_linear_attention(q_b, k_b, v_b, log_gamma, s_ret.astype(f32))
    o_b = rmsnorm(o_b, ret_gain) * jax.nn.silu(_heads(g_b, B_HEADS))
    mixed = jnp.concatenate([_merge(o_a), _merge(o_b)], axis=-1).astype(xn.dtype)
    return mixed @ w_out, new_buf, s_delta_new, s_ret_new


def mixer_c(xn, s_c, s_n, s_m, w_in, b_gate, mlstm_gain, w_out):
    f32 = jnp.float32
    proj = (xn @ w_in).astype(f32)
    q, k, v, o_pre, gates = jnp.split(proj, np.cumsum([C_QK, C_QK, C_V, C_V]).tolist(), axis=-1)
    gates = (gates + b_gate.astype(f32)).transpose(0, 2, 1)
    i_pre, f_pre = gates[:, :C_HEADS], gates[:, C_HEADS:]
    h, cm, nv, m = mlstm_chunkwise(_heads(q, C_HEADS), _heads(k, C_HEADS) * C_DK ** -0.5, _heads(v, C_HEADS),
                                   i_pre, jax.nn.log_sigmoid(f_pre), s_c.astype(f32), s_n.astype(f32), s_m.astype(f32))
    h = rmsnorm(h, mlstm_gain) * jax.nn.sigmoid(_heads(o_pre, C_HEADS))
    return _merge(h).astype(xn.dtype) @ w_out, cm, nv, m


def squared_relu_mlp(xn, w_up, w_down):
    hid = jax.nn.relu(xn @ w_up)
    return (hid * hid) @ w_down


def run_group(x, pos, conv_a, delta_a, ret_b, m_c, m_n, m_m, p):
    h = x
    new_conv, new_delta, new_ret, new_c, new_n, new_m = [], [], [], [], [], []
    for layer in range(DEPTH):
        i = layer // 2
        if layer % 2 == 0:
            xn = rmsnorm(h, p['norm_mix_a'][i])
            mix, cb, sd, sr = mixer_ab(xn, pos, conv_a[i], delta_a[i], ret_b[i], p['w_in_a'][i], p['w_conv_a'][i],
                                       p['a_log'][i], p['dt_bias'][i], p['gdn_gain'][i], p['ret_gain'][i], p['w_out_a'][i])
            new_conv.append(cb)
            new_delta.append(sd)
            new_ret.append(sr)
        else:
            xn = rmsnorm(h, p['norm_mix_c'][i])
            mix, cm, nv, m = mixer_c(xn, m_c[i], m_n[i], m_m[i], p['w_in_c'][i], p['b_gate_c'][i],
                                     p['mlstm_gain'][i], p['w_out_c'][i])
            new_c.append(cm)
            new_n.append(nv)
            new_m.append(m)
        h = h + mix
        h = h + squared_relu_mlp(rmsnorm(h, p['norm_ffn'][layer]), p['w_up'][layer], p['w_down'][layer])
    y = rmsnorm(h, p['final_gain'])
    return (y, jnp.stack(new_conv), jnp.stack(new_delta), jnp.stack(new_ret),
            jnp.stack(new_c), jnp.stack(new_n), jnp.stack(new_m))


def setup_inputs(seed: int = 0) -> dict:
    key = jax.random.key(seed)
    ks = list(jax.random.split(key, 32))
    f32 = jnp.float32

    def nrm(k, shape, scale):
        return jax.random.normal(k, shape, f32) * scale

    dt = jnp.exp(jax.random.uniform(ks[20], (N_EVEN, A_HEADS), f32, math.log(1e-3), math.log(1e-1)))
    dt_bias = dt + jnp.log(-jnp.expm1(-dt))
    a_log = jnp.log(jax.random.uniform(ks[21], (N_EVEN, A_HEADS), f32, 1.0, 16.0))
    b_gate = jnp.concatenate([nrm(ks[22], (N_ODD, C_HEADS), 0.1),
                              jax.random.uniform(ks[23], (N_ODD, C_HEADS), f32, 3.0, 6.0)], axis=-1)
    return {
        'x_prompt': nrm(ks[0], (BATCH, SEQ, D_MODEL), 1.0),
        'x_sample': nrm(ks[1], (DEC_BATCH, DEC_SEQ, D_MODEL), 1.0),
        'state_conv_a': nrm(ks[2], (N_EVEN, DEC_BATCH, CONV_W - 1, A_CONV_CH), 1.0),
        'state_delta_a': nrm(ks[3], (N_EVEN, DEC_BATCH, A_HEADS, A_DK, A_DV), 0.5),
        'state_ret_b': nrm(ks[4], (N_EVEN, DEC_BATCH, B_HEADS, B_DK, B_DV), 1.0),
        'state_mlstm_C': nrm(ks[5], (N_ODD, DEC_BATCH, C_HEADS, C_DK, C_DV), 0.5),
        'state_mlstm_n': nrm(ks[6], (N_ODD, DEC_BATCH, C_HEADS, C_DK), 0.5),
        'state_mlstm_m': nrm(ks[7], (N_ODD, DEC_BATCH, C_HEADS), 1.0),
        'norm_mix_a': 1.0 + nrm(ks[8], (N_EVEN, D_MODEL), 0.02),
        'w_in_a': nrm(ks[9], (N_EVEN, D_MODEL, IN_A), D_MODEL ** -0.5),
        'w_conv_a': nrm(ks[10], (N_EVEN, CONV_W, A_CONV_CH), CONV_W ** -0.5),
        'a_log': a_log,
        'dt_bias': dt_bias,
        'gdn_gain': 1.0 + nrm(ks[11], (N_EVEN, A_DV), 0.02),
        'ret_gain': 1.0 + nrm(ks[12], (N_EVEN, B_DV), 0.02),
        'w_out_a': nrm(ks[13], (N_EVEN, MIX_A, D_MODEL), MIX_A ** -0.5),
        'norm_mix_c': 1.0 + nrm(ks[14], (N_ODD, D_MODEL), 0.02),
        'w_in_c': nrm(ks[15], (N_ODD, D_MODEL, IN_C), D_MODEL ** -0.5),
        'b_gate_c': b_gate,
        'mlstm_gain': 1.0 + nrm(ks[16], (N_ODD, C_DV), 0.02),
        'w_out_c': nrm(ks[17], (N_ODD, C_V, D_MODEL), C_V ** -0.5),
        'norm_ffn': 1.0 + nrm(ks[18], (DEPTH, D_MODEL), 0.02),
        'w_up': nrm(ks[19], (DEPTH, D_MODEL, D_FF), D_MODEL ** -0.5),
        'w_down': nrm(ks[24], (DEPTH, D_FF, D_MODEL), D_FF ** -0.5),
        'final_gain': 1.0 + nrm(ks[25], (D_MODEL,), 0.02),
    }


def reference(x_prompt, x_sample, state_conv_a, state_delta_a, state_ret_b, state_mlstm_C, state_mlstm_n, state_mlstm_m,
              norm_mix_a, w_in_a, w_conv_a, a_log, dt_bias, gdn_gain, ret_gain, w_out_a,
              norm_mix_c, w_in_c, b_gate_c, mlstm_gain, w_out_c, norm_ffn, w_up, w_down, final_gain):
    p = dict(norm_mix_a=norm_mix_a, w_in_a=w_in_a, w_conv_a=w_conv_a, a_log=a_log, dt_bias=dt_bias,
             gdn_gain=gdn_gain, ret_gain=ret_gain, w_out_a=w_out_a, norm_mix_c=norm_mix_c, w_in_c=w_in_c,
             b_gate_c=b_gate_c, mlstm_gain=mlstm_gain, w_out_c=w_out_c, norm_ffn=norm_ffn, w_up=w_up,
             w_down=w_down, final_gain=final_gain)
    f32 = jnp.float32
    bp = x_prompt.shape[0]
    z_conv = jnp.zeros((N_EVEN, bp, CONV_W - 1, A_CONV_CH), f32)
    z_delta = jnp.zeros((N_EVEN, bp, A_HEADS, A_DK, A_DV), f32)
    z_ret = jnp.zeros((N_EVEN, bp, B_HEADS, B_DK, B_DV), f32)
    z_c = jnp.zeros((N_ODD, bp, C_HEADS, C_DK, C_DV), f32)
    z_n = jnp.zeros((N_ODD, bp, C_HEADS, C_DK), f32)
    z_m = jnp.zeros((N_ODD, bp, C_HEADS), f32)
    pos_prompt = jnp.arange(x_prompt.shape[1])
    y_prompt, p_conv, p_delta, p_ret, p_c, p_n, p_m = run_group(
        x_prompt, pos_prompt, z_conv, z_delta, z_ret, z_c, z_n, z_m, p)
    pos_sample = PAST_LEN + jnp.arange(x_sample.shape[1])
    y_sample, s_conv, s_delta, s_ret, s_c, s_n, s_m = run_group(
        x_sample, pos_sample, state_conv_a, state_delta_a, state_ret_b, state_mlstm_C, state_mlstm_n, state_mlstm_m, p)
    return (y_prompt, y_sample, p_conv, p_delta, p_ret, p_c, p_n, p_m, s_conv, s_delta, s_ret, s_c, s_n, s_m)
```

```python
import functools
import math

import jax
import jax.numpy as jnp
from jax import lax
from jax.experimental import pallas as pl
from jax.experimental.pallas import tpu as pltpu

F32 = jnp.float32
BF16 = jnp.bfloat16

D_MODEL = 1024
D_FF = 4 * D_MODEL
CHUNK = 64
EPS = 1e-6
NEG = -1e30
HEADS = 4
DK = 128
A_DV = 128
C_DV = 256
CONV_W = 4
A_CONV_CH = 3 * HEADS * DK
ROPE_BASE = 10000.0
PAST_LEN = 16384
LOG_GAMMA = tuple(math.log1p(-(2.0 ** (-5.0 - h))) for h in range(HEADS))

VMEM_LIMIT_BYTES = 56 * 1024 * 1024


def _cparams(*sem):
    return pltpu.CompilerParams(dimension_semantics=sem, vmem_limit_bytes=VMEM_LIMIT_BYTES)


def _mm(a, b):
    return jnp.dot(a.astype(BF16), b.astype(BF16), preferred_element_type=F32)


def _mm_nt(a, b):
    return lax.dot_general(a.astype(BF16), b.astype(BF16), (((1,), (1,)), ((), ())), preferred_element_type=F32)


def _mm_tn(a, b):
    return lax.dot_general(a.astype(BF16), b.astype(BF16), (((0,), (0,)), ((), ())), preferred_element_type=F32)


def _split(a):
    hi = a.astype(BF16)
    lo = (a - hi.astype(F32)).astype(BF16)
    return hi, lo


def _mm3(a, b):
    ah, al = _split(a)
    bh, bl = _split(b)
    d = lambda x, y: jnp.dot(x, y, preferred_element_type=F32)
    return d(ah, bh) + (d(ah, bl) + d(al, bh))


def _mm3_tn(a, b):
    ah, al = _split(a)
    bh, bl = _split(b)
    d = lambda x, y: lax.dot_general(x, y, (((0,), (0,)), ((), ())), preferred_element_type=F32)
    return d(ah, bh) + (d(ah, bl) + d(al, bh))


def _softplus(x):
    return jnp.maximum(x, 0.0) + jnp.log1p(jnp.exp(-jnp.abs(x)))


def _log_sigmoid(x):
    return -_softplus(-x)


def _silu(x):
    return x * jax.nn.sigmoid(x)


def _rms(x, gain):
    return x * lax.rsqrt(jnp.mean(x * x, axis=-1, keepdims=True) + EPS) * gain


def _l2(t):
    return t * lax.rsqrt(jnp.sum(t * t, axis=-1, keepdims=True) + EPS)


def _tri_inv(n):
    c = n.shape[0]
    eye = (lax.broadcasted_iota(jnp.int32, (c, c), 0) == lax.broadcasted_iota(jnp.int32, (c, c), 1)).astype(F32)
    t = eye - n
    p = n
    steps = int(math.log2(c)) - 1
    for _ in range(steps):
        p = _mm3(p, p)
        t = t + _mm3(t, p)
    return t


def _norm_proj_kernel(x_ref, g_ref, w_ref, *out_refs, splits):
    xn = _rms(x_ref[...], g_ref[...]).astype(BF16)
    off = 0
    for o_ref, n in zip(out_refs, splits):
        o_ref[...] = jnp.dot(xn, w_ref[:, off:off + n], preferred_element_type=F32)
        off += n


def _norm_proj(x, gain, w, splits, tm):
    t = x.shape[0]
    n = w.shape[1]
    assert sum(splits) == n and t % tm == 0
    return pl.pallas_call(
        functools.partial(_norm_proj_kernel, splits=splits),
        grid=(t // tm,),
        in_specs=[pl.BlockSpec((tm, D_MODEL), lambda i: (i, 0)),
                  pl.BlockSpec((1, D_MODEL), lambda i: (0, 0)),
                  pl.BlockSpec((D_MODEL, n), lambda i: (0, 0))],
        out_specs=[pl.BlockSpec((tm, s), lambda i: (i, 0)) for s in splits],
        out_shape=[jax.ShapeDtypeStruct((t, s), F32) for s in splits],
        compiler_params=_cparams("parallel"),
        name="norm_proj",
    )(x, gain.reshape(1, D_MODEL), w)


def _out_mlp_kernel(*refs, n_mix, final):
    h_ref = refs[0]
    mix_refs = refs[1:1 + n_mix]
    wout_ref, gffn_ref, wup_ref, wdown_ref = refs[1 + n_mix:5 + n_mix]
    rest = refs[5 + n_mix:]
    if final:
        gfin_ref, o_ref = rest
    else:
        (o_ref,) = rest
    h = h_ref[...]
    off = 0
    for m_ref in mix_refs:
        w = m_ref.shape[1]
        h = h + jnp.dot(m_ref[...].astype(BF16), wout_ref[off:off + w, :], preferred_element_type=F32)
        off += w
    xn = _rms(h, gffn_ref[...]).astype(BF16)
    acc = h
    step = 1024
    for j in range(D_FF // step):
        hid = jnp.dot(xn, wup_ref[:, j * step:(j + 1) * step], preferred_element_type=F32)
        hid = jnp.maximum(hid, 0.0)
        acc = acc + jnp.dot((hid * hid).astype(BF16), wdown_ref[j * step:(j + 1) * step, :],
                            preferred_element_type=F32)
    if final:
        acc = _rms(acc, gfin_ref[...])
    o_ref[...] = acc


def _out_mlp(h, mixes, w_out, g_ffn, w_up, w_down, g_final, tm):
    t = h.shape[0]
    assert t % tm == 0
    final = g_final is not None
    row = lambda i: (i, 0)
    const = lambda i: (0, 0)
    in_specs = [pl.BlockSpec((tm, D_MODEL), row)]
    in_specs += [pl.BlockSpec((tm, m.shape[1]), row) for m in mixes]
    in_specs += [pl.BlockSpec(w_out.shape, const), pl.BlockSpec((1, D_MODEL), const),
                 pl.BlockSpec(w_up.shape, const), pl.BlockSpec(w_down.shape, const)]
    args = [h, *mixes, w_out, g_ffn.reshape(1, D_MODEL), w_up, w_down]
    if final:
        in_specs.append(pl.BlockSpec((1, D_MODEL), const))
        args.append(g_final.reshape(1, D_MODEL))
    return pl.pallas_call(
        functools.partial(_out_mlp_kernel, n_mix=len(mixes), final=final),
        grid=(t // tm,),
        in_specs=in_specs,
        out_specs=pl.BlockSpec((tm, D_MODEL), row),
        out_shape=jax.ShapeDtypeStruct((t, D_MODEL), F32),
        compiler_params=_cparams("parallel"),
        name="out_mlp",
    )(*args)


def _chunk_masks(c):
    ii = lax.broadcasted_iota(jnp.int32, (c, c), 0)
    jj = lax.broadcasted_iota(jnp.int32, (c, c), 1)
    return ii, jj


def _mix_ab_kernel(qkv_ref, rest_ref, gates_ref, cos_ref, sin_ref, wconv_ref, prow_ref, pcol_ref,
                   ggain_ref, rgain_ref, mixed_ref, conv_ref, sd_ref, sr_ref, xp_ref):
    c = CHUNK
    step = pl.program_id(1)

    @pl.when(step == 0)
    def _():
        sd_ref[...] = jnp.zeros_like(sd_ref)
        sr_ref[...] = jnp.zeros_like(sr_ref)
        xp_ref[0:8, :] = jnp.zeros((8, A_CONV_CH), F32)

    u = qkv_ref[...]
    xp_ref[8:8 + c, :] = u
    w = wconv_ref[...]
    conv = (xp_ref[5:5 + c, :] * w[0:1] + xp_ref[6:6 + c, :] * w[1:2]
            + xp_ref[7:7 + c, :] * w[2:3] + u * w[3:4])
    xp_ref[0:8, :] = u[c - 8:c, :]
    conv_ref[0] = u[c - 3:c, :]
    act = _silu(conv)

    g = gates_ref[...]
    g_t = g.T
    neg_a_row = -jnp.exp(prow_ref[0:1, :])
    dt_row = prow_ref[1:2, :]
    neg_a_col = -jnp.exp(pcol_ref[0:HEADS, 0:1])
    dt_col = pcol_ref[0:HEADS, 1:2]
    la_cols = neg_a_row * _softplus(g + dt_row)
    beta_cols = jax.nn.sigmoid(g)
    la_rows = neg_a_col * _softplus(g_t[0:HEADS, :] + dt_col)

    ii, jj = _chunk_masks(c)
    causal = ii >= jj
    strict = ii > jj
    upper = ii <= jj
    dm = (ii - jj).astype(F32)
    pos_col = lax.broadcasted_iota(jnp.int32, (c, 1), 0).astype(F32)

    rest = rest_ref
    ggain = ggain_ref[...]
    rgain = rgain_ref[...]
    cosf = cos_ref[...]
    sinf = sin_ref[...]

    for h in range(HEADS):
        la_row = la_rows[h:h + 1, :]
        la_col = la_cols[:, h:h + 1]
        beta = beta_cols[:, HEADS + h:HEADS + h + 1]
        g_col = jnp.sum(jnp.where(causal, la_row, 0.0), axis=1, keepdims=True)
        g_row = jnp.sum(jnp.where(upper, la_col, 0.0), axis=0, keepdims=True)
        g_last = jnp.sum(la_row, axis=1, keepdims=True)
        dec_causal = jnp.exp(jnp.where(causal, g_col - g_row, NEG))
        dec_strict = jnp.where(strict, dec_causal, 0.0)
        eg = jnp.exp(g_col)

        q = _l2(act[:, h * DK:(h + 1) * DK]) * (DK ** -0.5)
        k = _l2(act[:, (HEADS + h) * DK:(HEADS + h + 1) * DK])
        v = act[:, (2 * HEADS + h) * DK:(2 * HEADS + h + 1) * DK]
        kb = k * beta
        prod = _mm_nt(jnp.concatenate([q, kb], axis=0), k)
        qk = prod[0:c] * dec_causal
        t_inv = _tri_inv(prod[c:2 * c] * dec_strict)
        sol = _mm3(t_inv, jnp.concatenate([v * beta, kb * eg], axis=1))
        u_c = sol[:, 0:A_DV]
        w_c = sol[:, A_DV:2 * A_DV]
        s = sd_ref[0, h]
        r = _mm(jnp.concatenate([w_c, q * eg], axis=0), s)
        v_new = u_c - r[0:c]
        o = r[c:2 * c] + _mm(qk, v_new)
        k_dec = k * jnp.exp(g_last - g_col)
        sd_ref[0, h] = s * jnp.exp(g_last) + _mm_tn(k_dec, v_new)
        ga = rest[:, h * DK:(h + 1) * DK]
        mixed_ref[:, h * A_DV:(h + 1) * A_DV] = _rms(o, ggain) * _silu(ga)

        base = HEADS * DK
        qb = rest[:, base + h * DK:base + (h + 1) * DK]
        kb2 = rest[:, 2 * base + h * DK:2 * base + (h + 1) * DK]
        vb = rest[:, 3 * base + h * DK:3 * base + (h + 1) * DK]
        gb = rest[:, 4 * base + h * DK:4 * base + (h + 1) * DK]
        qr = qb * cosf + pltpu.roll(qb, DK // 2, 1) * sinf
        kr = (kb2 * cosf + pltpu.roll(kb2, DK // 2, 1) * sinf) * (DK ** -0.5)
        lg = LOG_GAMMA[h]
        dec = jnp.exp(jnp.where(causal, dm * lg, NEG))
        qd = qr * jnp.exp((pos_col + 1.0) * lg)
        kd = kr * jnp.exp((float(c - 1) - pos_col) * lg)
        sr = sr_ref[0, h]
        o_b = _mm(qd, sr) + _mm(_mm_nt(qr, kr) * dec, vb)
        sr_ref[0, h] = sr * math.exp(c * lg) + _mm_tn(kd, vb)
        mixed_ref[:, base + h * DK:base + (h + 1) * DK] = _rms(o_b, rgain) * _silu(gb)


def _rope_tables(pos):
    half = DK // 2
    inv = ROPE_BASE ** (-jnp.arange(half, dtype=F32) / half)
    ang = pos.astype(F32)[:, None] * inv[None, :]
    cos, sin = jnp.cos(ang), jnp.sin(ang)
    return jnp.concatenate([cos, cos], axis=-1), jnp.concatenate([-sin, sin], axis=-1)


def _gate_params(a_log, dt_bias):
    prow = jnp.zeros((8, 128), F32).at[0, 0:HEADS].set(a_log).at[1, 0:HEADS].set(dt_bias)
    pcol = jnp.zeros((8, 128), F32).at[0:HEADS, 0].set(a_log).at[0:HEADS, 1].set(dt_bias)
    return prow, pcol


def _mix_ab_prompt(qkv, rest, gates, batch, seq, w_conv, a_log, dt_bias, gdn_gain, ret_gain):
    nc = seq // CHUNK
    cosf, sinf = _rope_tables(jnp.arange(seq))
    prow, pcol = _gate_params(a_log, dt_bias)
    tok = lambda b, c: (b * nc + c, 0)
    const = lambda b, c: (0, 0)
    mixed, conv, sd, sr = pl.pallas_call(
        _mix_ab_kernel,
        grid=(batch, nc),
        in_specs=[pl.BlockSpec((CHUNK, A_CONV_CH), tok),
                  pl.BlockSpec((CHUNK, rest.shape[1]), tok),
                  pl.BlockSpec((CHUNK, 128), tok),
                  pl.BlockSpec((CHUNK, DK), lambda b, c: (c, 0)),
                  pl.BlockSpec((CHUNK, DK), lambda b, c: (c, 0)),
                  pl.BlockSpec((CONV_W, A_CONV_CH), const),
                  pl.BlockSpec((8, 128), const),
                  pl.BlockSpec((8, 128), const),
                  pl.BlockSpec((1, A_DV), const),
                  pl.BlockSpec((1, DK), const)],
        out_specs=[pl.BlockSpec((CHUNK, 2 * HEADS * DK), tok),
                   pl.BlockSpec((1, CONV_W - 1, A_CONV_CH), lambda b, c: (b, 0, 0)),
                   pl.BlockSpec((1, HEADS, DK, A_DV), lambda b, c: (b, 0, 0, 0)),
                   pl.BlockSpec((1, HEADS, DK, DK), lambda b, c: (b, 0, 0, 0))],
        out_shape=[jax.ShapeDtypeStruct((batch * seq, 2 * HEADS * DK), F32),
                   jax.ShapeDtypeStruct((batch, CONV_W - 1, A_CONV_CH), F32),
                   jax.ShapeDtypeStruct((batch, HEADS, DK, A_DV), F32),
                   jax.ShapeDtypeStruct((batch, HEADS, DK, DK), F32)],
        scratch_shapes=[pltpu.VMEM((CHUNK + 8, A_CONV_CH), F32)],
        compiler_params=_cparams("parallel", "arbitrary"),
        name="mix_ab_prompt",
    )(qkv, rest, gates, cosf, sinf, w_conv, prow, pcol, gdn_gain.reshape(1, A_DV), ret_gain.reshape(1, DK))
    return mixed, conv, sd, sr


def _mix_c_kernel(qk_ref, v_ref, opre_ref, gates_ref, brow_ref, bcol_ref, gain_ref,
                  h_ref, cm_ref, nv_ref, m_ref):
    c = CHUNK
    step = pl.program_id(1)

    @pl.when(step == 0)
    def _():
        cm_ref[...] = jnp.zeros_like(cm_ref)
        nv_ref[...] = jnp.zeros_like(nv_ref)
        m_ref[...] = jnp.zeros_like(m_ref)

    g = gates_ref[...] + brow_ref[0:1, :]
    g_t = gates_ref[...].T[0:2 * HEADS, :] + bcol_ref[0:2 * HEADS, 0:1]
    logf_cols = _log_sigmoid(g)
    logf_rows = _log_sigmoid(g_t[HEADS:2 * HEADS, :])
    ii, jj = _chunk_masks(c)
    causal = ii >= jj
    upper = ii <= jj
    gain = gain_ref[...]

    for h in range(HEADS):
        i_col = g[:, h:h + 1]
        i_row = g_t[h:h + 1, :]
        lf_col = logf_cols[:, HEADS + h:HEADS + h + 1]
        lf_row = logf_rows[h:h + 1, :]
        b_col = jnp.sum(jnp.where(causal, lf_row, 0.0), axis=1, keepdims=True)
        b_row = jnp.sum(jnp.where(upper, lf_col, 0.0), axis=0, keepdims=True)
        b_last = jnp.sum(lf_row, axis=1, keepdims=True)
        d_log = jnp.where(causal, b_col - b_row + i_row, NEG)
        d_max = jnp.max(d_log, axis=1, keepdims=True)
        q = qk_ref[:, h * DK:(h + 1) * DK]
        k = qk_ref[:, (HEADS + h) * DK:(HEADS + h + 1) * DK] * (DK ** -0.5)
        v = v_ref[:, h * C_DV:(h + 1) * C_DV]
        qk = _mm_nt(q, k)
        k_log = b_last - b_col + i_col

        cm = cm_ref[0, h]
        nv = nv_ref[0, h:h + 1, :]
        m = m_ref[0, h:h + 1, 0:1]
        inter = b_col + m
        m_row = jnp.maximum(inter, d_max)
        w_intra = jnp.exp(d_log - m_row) * qk
        w_inter = jnp.exp(inter - m_row)
        num = w_inter * _mm(q, cm) + _mm(w_intra, v)
        den = w_inter * jnp.sum(q * nv, axis=1, keepdims=True) + jnp.sum(w_intra, axis=1, keepdims=True)
        hh = num / jnp.maximum(jnp.abs(den), jnp.exp(-m_row))
        m_new = m_row[c - 1:c, :]
        f_state = jnp.exp(b_last + m - m_new)
        kw = k * jnp.exp(k_log - m_new)
        cm_ref[0, h] = cm * f_state + _mm_tn(kw, v)
        nv_ref[0, h:h + 1, :] = nv * f_state + jnp.sum(kw, axis=0, keepdims=True)
        m_ref[0, h:h + 1, :] = jnp.broadcast_to(m_new, (1, 128))
        op = opre_ref[:, h * C_DV:(h + 1) * C_DV]
        h_ref[:, h * C_DV:(h + 1) * C_DV] = _rms(hh, gain) * jax.nn.sigmoid(op)


def _bias_params(b_gate):
    brow = jnp.zeros((8, 128), F32).at[0, 0:2 * HEADS].set(b_gate)
    bcol = jnp.zeros((8, 128), F32).at[0:2 * HEADS, 0].set(b_gate)
    return brow, bcol


def _mix_c_prompt(qk, v, opre, gates, batch, seq, b_gate, gain):
    nc = seq // CHUNK
    brow, bcol = _bias_params(b_gate)
    tok = lambda b, c: (b * nc + c, 0)
    const = lambda b, c: (0, 0)
    return pl.pallas_call(
        _mix_c_kernel,
        grid=(batch, nc),
        in_specs=[pl.BlockSpec((CHUNK, 2 * HEADS * DK), tok),
                  pl.BlockSpec((CHUNK, HEADS * C_DV), tok),
                  pl.BlockSpec((CHUNK, HEADS * C_DV), tok),
                  pl.BlockSpec((CHUNK, 128), tok),
                  pl.BlockSpec((8, 128), const),
                  pl.BlockSpec((8, 128), const),
                  pl.BlockSpec((1, C_DV), const)],
        out_specs=[pl.BlockSpec((CHUNK, HEADS * C_DV), tok),
                   pl.BlockSpec((1, HEADS, DK, C_DV), lambda b, c: (b, 0, 0, 0)),
                   pl.BlockSpec((1, HEADS, DK), lambda b, c: (b, 0, 0)),
                   pl.BlockSpec((1, HEADS, 128), lambda b, c: (b, 0, 0))],
        out_shape=[jax.ShapeDtypeStruct((batch * seq, HEADS * C_DV), F32),
                   jax.ShapeDtypeStruct((batch, HEADS, DK, C_DV), F32),
                   jax.ShapeDtypeStruct((batch, HEADS, DK), F32),
                   jax.ShapeDtypeStruct((batch, HEADS, 128), F32)],
        compiler_params=_cparams("parallel", "arbitrary"),
        name="mix_c_prompt",
    )(qk, v, opre, gates, brow, bcol, gain.reshape(1, C_DV))


def _row0(x):
    rows = lax.broadcasted_iota(jnp.int32, (8, x.shape[1]), 0)
    return jnp.where(rows == 0, jnp.broadcast_to(x, (8, x.shape[1])), 0.0)


def _vec_mat(x, s):
    return _mm(_row0(x), s)[0:1, :]


def _outer(a, b):
    return _mm3_tn(_row0(a), _row0(b))


def _step_ab_kernel(qkv_ref, rest_ref, gates_ref, cos_ref, sin_ref, wconv_ref, prow_ref, ggain_ref, rgain_ref,
                    buf_ref, sd_ref, sr_ref, mixed_ref, nbuf_ref, nsd_ref, nsr_ref):
    u = qkv_ref[0]
    buf = buf_ref[0]
    w = wconv_ref[...]
    conv = buf[0:1] * w[0:1] + buf[1:2] * w[1:2] + buf[2:3] * w[2:3] + u * w[3:4]
    nbuf_ref[0, 0:2, :] = buf[1:3]
    nbuf_ref[0, 2:3, :] = u
    act = _silu(conv)
    g = gates_ref[0]
    la = -jnp.exp(prow_ref[0:1, :]) * _softplus(g + prow_ref[1:2, :])
    beta_all = jax.nn.sigmoid(g)
    rest = rest_ref[0]
    cosf = cos_ref[...]
    sinf = sin_ref[...]
    base = HEADS * DK
    for h in range(HEADS):
        eg = jnp.exp(la[:, h:h + 1])
        beta = beta_all[:, HEADS + h:HEADS + h + 1]
        q = _l2(act[:, h * DK:(h + 1) * DK]) * (DK ** -0.5)
        k = _l2(act[:, (HEADS + h) * DK:(HEADS + h + 1) * DK])
        v = act[:, (2 * HEADS + h) * DK:(2 * HEADS + h + 1) * DK]
        s = sd_ref[0, h]
        v_new = beta * (v - eg * _vec_mat(k, s))
        s_new = s * eg + _outer(k, v_new)
        nsd_ref[0, h] = s_new
        o = _vec_mat(q, s_new)
        ga = rest[:, h * DK:(h + 1) * DK]
        mixed_ref[0, :, h * DK:(h + 1) * DK] = _rms(o, ggain_ref[...]) * _silu(ga)

        qb = rest[:, base + h * DK:base + (h + 1) * DK]
        kb = rest[:, 2 * base + h * DK:2 * base + (h + 1) * DK]
        vb = rest[:, 3 * base + h * DK:3 * base + (h + 1) * DK]
        gb = rest[:, 4 * base + h * DK:4 * base + (h + 1) * DK]
        qr = qb * cosf + pltpu.roll(qb, DK // 2, 1) * sinf
        kr = (kb * cosf + pltpu.roll(kb, DK // 2, 1) * sinf) * (DK ** -0.5)
        sr_new = sr_ref[0, h] * math.exp(LOG_GAMMA[h]) + _outer(kr, vb)
        nsr_ref[0, h] = sr_new
        o_b = _vec_mat(qr, sr_new)
        mixed_ref[0, :, base + h * DK:base + (h + 1) * DK] = _rms(o_b, rgain_ref[...]) * _silu(gb)


def _mix_ab_sample(qkv, rest, gates, conv_buf, s_delta, s_ret, w_conv, a_log, dt_bias, gdn_gain, ret_gain):
    nb = qkv.shape[0]
    cosf, sinf = _rope_tables(PAST_LEN + jnp.arange(1))
    prow, _ = _gate_params(a_log, dt_bias)
    tok3 = lambda b: (b, 0, 0)
    tok4 = lambda b: (b, 0, 0, 0)
    const = lambda b: (0, 0)
    n_rest = rest.shape[1]
    mixed, nbuf, nsd, nsr = pl.pallas_call(
        _step_ab_kernel,
        grid=(nb,),
        in_specs=[pl.BlockSpec((1, 1, A_CONV_CH), tok3),
                  pl.BlockSpec((1, 1, n_rest), tok3),
                  pl.BlockSpec((1, 1, 128), tok3),
                  pl.BlockSpec((1, DK), const),
                  pl.BlockSpec((1, DK), const),
                  pl.BlockSpec((CONV_W, A_CONV_CH), const),
                  pl.BlockSpec((8, 128), const),
                  pl.BlockSpec((1, A_DV), const),
                  pl.BlockSpec((1, DK), const),
                  pl.BlockSpec((1, CONV_W - 1, A_CONV_CH), tok3),
                  pl.BlockSpec((1, HEADS, DK, A_DV), tok4),
                  pl.BlockSpec((1, HEADS, DK, DK), tok4)],
        out_specs=[pl.BlockSpec((1, 1, 2 * HEADS * DK), tok3),
                   pl.BlockSpec((1, CONV_W - 1, A_CONV_CH), tok3),
                   pl.BlockSpec((1, HEADS, DK, A_DV), tok4),
                   pl.BlockSpec((1, HEADS, DK, DK), tok4)],
        out_shape=[jax.ShapeDtypeStruct((nb, 1, 2 * HEADS * DK), F32),
                   jax.ShapeDtypeStruct((nb, CONV_W - 1, A_CONV_CH), F32),
                   jax.ShapeDtypeStruct((nb, HEADS, DK, A_DV), F32),
                   jax.ShapeDtypeStruct((nb, HEADS, DK, DK), F32)],
        compiler_params=_cparams("parallel"),
        name="mix_ab_sample",
    )(qkv.reshape(nb, 1, -1), rest.reshape(nb, 1, -1), gates.reshape(nb, 1, -1), cosf, sinf, w_conv, prow,
      gdn_gain.reshape(1, A_DV), ret_gain.reshape(1, DK), conv_buf, s_delta, s_ret)
    return mixed.reshape(nb, -1), nbuf, nsd, nsr


def _step_c_kernel(qk_ref, v_ref, opre_ref, gates_ref, brow_ref, gain_ref, cm_ref, nv_ref, m_ref,
                   h_ref, ncm_ref, nnv_ref, nm_ref):
    g = gates_ref[0] + brow_ref[0:1, :]
    logf = _log_sigmoid(g)
    qk = qk_ref[0]
    vv = v_ref[0]
    op = opre_ref[0]
    for h in range(HEADS):
        i_pre = g[:, h:h + 1]
        lf = logf[:, HEADS + h:HEADS + h + 1]
        q = qk[:, h * DK:(h + 1) * DK]
        k = qk[:, (HEADS + h) * DK:(HEADS + h + 1) * DK] * (DK ** -0.5)
        v = vv[:, h * C_DV:(h + 1) * C_DV]
        m = m_ref[0, h:h + 1, :]
        nv = nv_ref[0, h:h + 1, :]
        inter = lf + m
        m_new = jnp.maximum(inter, i_pre)
        f_state = jnp.exp(inter - m_new)
        kw = k * jnp.exp(i_pre - m_new)
        cm_new = cm_ref[0, h] * f_state + _outer(kw, v)
        nv_new = nv * f_state + kw
        ncm_ref[0, h] = cm_new
        nnv_ref[0, h:h + 1, :] = nv_new
        nm_ref[0, h:h + 1, :] = m_new
        num = _vec_mat(q, cm_new)
        den = jnp.sum(q * nv_new, axis=1, keepdims=True)
        hh = num / jnp.maximum(jnp.abs(den), jnp.exp(-m_new))
        h_ref[0, :, h * C_DV:(h + 1) * C_DV] = (_rms(hh, gain_ref[...])
                                                * jax.nn.sigmoid(op[:, h * C_DV:(h + 1) * C_DV]))


def _mix_c_sample(qk, v, opre, gates, s_c, s_n, s_m, b_gate, gain):
    nb = qk.shape[0]
    brow, _ = _bias_params(b_gate)
    tok3 = lambda b: (b, 0, 0)
    tok4 = lambda b: (b, 0, 0, 0)
    const = lambda b: (0, 0)
    hh, ncm, nnv, nm = pl.pallas_call(
        _step_c_kernel,
        grid=(nb,),
        in_specs=[pl.BlockSpec((1, 1, 2 * HEADS * DK), tok3),
                  pl.BlockSpec((1, 1, HEADS * C_DV), tok3),
                  pl.BlockSpec((1, 1, HEADS * C_DV), tok3),
                  pl.BlockSpec((1, 1, 128), tok3),
                  pl.BlockSpec((8, 128), const),
                  pl.BlockSpec((1, C_DV), const),
                  pl.BlockSpec((1, HEADS, DK, C_DV), tok4),
                  pl.BlockSpec((1, HEADS, DK), tok3),
                  pl.BlockSpec((1, HEADS, 1), tok3)],
        out_specs=[pl.BlockSpec((1, 1, HEADS * C_DV), tok3),
                   pl.BlockSpec((1, HEADS, DK, C_DV), tok4),
                   pl.BlockSpec((1, HEADS, DK), tok3),
                   pl.BlockSpec((1, HEADS, 1), tok3)],
        out_shape=[jax.ShapeDtypeStruct((nb, 1, HEADS * C_DV), F32),
                   jax.ShapeDtypeStruct((nb, HEADS, DK, C_DV), F32),
                   jax.ShapeDtypeStruct((nb, HEADS, DK), F32),
                   jax.ShapeDtypeStruct((nb, HEADS, 1), F32)],
        compiler_params=_cparams("parallel"),
        name="mix_c_sample",
    )(qk.reshape(nb, 1, -1), v.reshape(nb, 1, -1), opre.reshape(nb, 1, -1), gates.reshape(nb, 1, -1),
      brow, gain.reshape(1, C_DV), s_c, s_n, s_m.reshape(nb, HEADS, 1))
    return hh.reshape(nb, -1), ncm, nnv, nm.reshape(nb, HEADS)


A_SPLITS = (A_CONV_CH, 5 * HEADS * DK, 128)
C_SPLITS = (2 * HEADS * DK, HEADS * C_DV, HEADS * C_DV, 128)


def _prep_w_in_a(w):
    n_gate = 2 * HEADS
    qkv = w[:, :A_CONV_CH]
    gates = w[:, A_CONV_CH:A_CONV_CH + n_gate]
    rest = w[:, A_CONV_CH + n_gate:]
    pad = jnp.zeros((D_MODEL, 128 - n_gate), w.dtype)
    return jnp.concatenate([qkv, rest, gates, pad], axis=1).astype(BF16)


def _prep_w_in_c(w):
    n_gate = 2 * HEADS
    main = w[:, :2 * HEADS * DK + 2 * HEADS * C_DV]
    gates = w[:, 2 * HEADS * DK + 2 * HEADS * C_DV:]
    pad = jnp.zeros((D_MODEL, 128 - n_gate), w.dtype)
    return jnp.concatenate([main, gates, pad], axis=1).astype(BF16)


def _run_group(x, prompt, states, p, tm):
    batch, seq, _ = x.shape
    h = x.reshape(batch * seq, D_MODEL)
    qkv, rest, gates = _norm_proj(h, p['norm_mix_a'][0], p['w_in_a'], A_SPLITS, tm)
    if prompt:
        mixed, conv, sd, sr = _mix_ab_prompt(qkv, rest, gates, batch, seq, p['w_conv_a'][0], p['a_log'][0],
                                             p['dt_bias'][0], p['gdn_gain'][0], p['ret_gain'][0])
    else:
        mixed, conv, sd, sr = _mix_ab_sample(qkv, rest, gates, states['conv'][0], states['delta'][0],
                                             states['ret'][0], p['w_conv_a'][0], p['a_log'][0], p['dt_bias'][0],
                                             p['gdn_gain'][0], p['ret_gain'][0])
    h = _out_mlp(h, [mixed], p['w_out_a'], p['norm_ffn'][0], p['w_up'][0], p['w_down'][0], None, tm)
    qk, v, opre, gates_c = _norm_proj(h, p['norm_mix_c'][0], p['w_in_c'], C_SPLITS, tm)
    if prompt:
        hm, cm, nv, m = _mix_c_prompt(qk, v, opre, gates_c, batch, seq, p['b_gate_c'][0], p['mlstm_gain'][0])
        m = m[:, :, 0]
    else:
        hm, cm, nv, m = _mix_c_sample(qk, v, opre, gates_c, states['c'][0], states['n'][0], states['m'][0],
                                      p['b_gate_c'][0], p['mlstm_gain'][0])
    y = _out_mlp(h, [hm], p['w_out_c'], p['norm_ffn'][1], p['w_up'][1], p['w_down'][1], p['final_gain'], tm)
    return (y.reshape(batch, seq, D_MODEL), conv[None], sd[None], sr[None], cm[None], nv[None], m[None])


def kernel(x_prompt, x_sample, state_conv_a, state_delta_a, state_ret_b, state_mlstm_C, state_mlstm_n, state_mlstm_m,
           norm_mix_a, w_in_a, w_conv_a, a_log, dt_bias, gdn_gain, ret_gain, w_out_a,
           norm_mix_c, w_in_c, b_gate_c, mlstm_gain, w_out_c, norm_ffn, w_up, w_down, final_gain):
    p = dict(norm_mix_a=norm_mix_a, w_in_a=_prep_w_in_a(w_in_a[0]), w_conv_a=w_conv_a, a_log=a_log, dt_bias=dt_bias,
             gdn_gain=gdn_gain, ret_gain=ret_gain, w_out_a=w_out_a[0].astype(BF16), norm_mix_c=norm_mix_c,
             w_in_c=_prep_w_in_c(w_in_c[0]), b_gate_c=b_gate_c, mlstm_gain=mlstm_gain,
             w_out_c=w_out_c[0].astype(BF16), norm_ffn=norm_ffn, w_up=w_up.astype(BF16),
             w_down=w_down.astype(BF16), final_gain=final_gain)
    outs_p = _run_group(x_prompt, True, None, p, 256)
    states = dict(conv=state_conv_a, delta=state_delta_a, ret=state_ret_b, c=state_mlstm_C, n=state_mlstm_n,
                  m=state_mlstm_m)
    outs_s = _run_group(x_sample, False, states, p, x_sample.shape[0] * x_sample.shape[1])
    return (outs_p[0], outs_s[0]) + outs_p[1:] + outs_s[1:]
```

```python
import functools
import math

import jax
import jax.numpy as jnp
from jax import lax
from jax.experimental import pallas as pl
from jax.experimental.pallas import tpu as pltpu

F32 = jnp.float32
BF16 = jnp.bfloat16

D_MODEL = 1024
D_FF = 4 * D_MODEL
CHUNK = 64
EPS = 1e-6
NEG = -1e30
HEADS = 4
DK = 128
A_DV = 128
C_DV = 256
CONV_W = 4
A_CONV_CH = 3 * HEADS * DK
ROPE_BASE = 10000.0
PAST_LEN = 16384
LOG_GAMMA = tuple(math.log1p(-(2.0 ** (-5.0 - h))) for h in range(HEADS))

VMEM_LIMIT_BYTES = 56 * 1024 * 1024
AB_SUB = 4
C_SUB = 4
SAMPLE_TB = 8


def _cparams(*sem):
    return pltpu.CompilerParams(dimension_semantics=sem, vmem_limit_bytes=VMEM_LIMIT_BYTES)


def _mm(a, b):
    return jnp.dot(a.astype(BF16), b.astype(BF16), preferred_element_type=F32)


def _mm_nt(a, b):
    return lax.dot_general(a.astype(BF16), b.astype(BF16), (((1,), (1,)), ((), ())), preferred_element_type=F32)


def _mm_tn(a, b):
    return lax.dot_general(a.astype(BF16), b.astype(BF16), (((0,), (0,)), ((), ())), preferred_element_type=F32)


def _split(a):
    hi = a.astype(BF16)
    lo = (a - hi.astype(F32)).astype(BF16)
    return hi, lo


def _mm3(a, b):
    ah, al = _split(a)
    bh, bl = _split(b)
    d = lambda x, y: jnp.dot(x, y, preferred_element_type=F32)
    return d(ah, bh) + (d(ah, bl) + d(al, bh))


def _mm3_tn(a, b):
    ah, al = _split(a)
    bh, bl = _split(b)
    d = lambda x, y: lax.dot_general(x, y, (((0,), (0,)), ((), ())), preferred_element_type=F32)
    return d(ah, bh) + (d(ah, bl) + d(al, bh))


def _softplus(x):
    return jnp.maximum(x, 0.0) + jnp.log1p(jnp.exp(-jnp.abs(x)))


def _log_sigmoid(x):
    return -_softplus(-x)


def _silu(x):
    return x * jax.nn.sigmoid(x)


def _rms(x, gain):
    return x * lax.rsqrt(jnp.mean(x * x, axis=-1, keepdims=True) + EPS) * gain


def _l2(t):
    return t * lax.rsqrt(jnp.sum(t * t, axis=-1, keepdims=True) + EPS)


def _interleave(gens):
    gens = list(gens)
    while gens:
        alive = []
        for gen in gens:
            try:
                next(gen)
                alive.append(gen)
            except StopIteration:
                pass
        gens = alive


def _tri_inv_many(ns):
    c = ns[0].shape[0]
    eye = (lax.broadcasted_iota(jnp.int32, (c, c), 0) == lax.broadcasted_iota(jnp.int32, (c, c), 1)).astype(F32)
    ts = [eye - n for n in ns]
    ps = list(ns)
    steps = int(math.log2(c)) - 1
    for _ in range(steps):
        ps = [_mm3(p, p) for p in ps]
        ts = [t + _mm3(t, p) for t, p in zip(ts, ps)]
    return ts


def _norm_proj_kernel(x_ref, g_ref, w_ref, *out_refs, splits):
    xn = _rms(x_ref[...], g_ref[...]).astype(BF16)
    off = 0
    for o_ref, n in zip(out_refs, splits):
        o_ref[...] = jnp.dot(xn, w_ref[:, off:off + n], preferred_element_type=F32)
        off += n


def _norm_proj(x, gain, w, splits, tm):
    t = x.shape[0]
    n = w.shape[1]
    assert sum(splits) == n and t % tm == 0
    return pl.pallas_call(
        functools.partial(_norm_proj_kernel, splits=splits),
        grid=(t // tm,),
        in_specs=[pl.BlockSpec((tm, D_MODEL), lambda i: (i, 0)),
                  pl.BlockSpec((1, D_MODEL), lambda i: (0, 0)),
                  pl.BlockSpec((D_MODEL, n), lambda i: (0, 0))],
        out_specs=[pl.BlockSpec((tm, s), lambda i: (i, 0)) for s in splits],
        out_shape=[jax.ShapeDtypeStruct((t, s), F32) for s in splits],
        compiler_params=_cparams("parallel"),
        name="norm_proj",
    )(x, gain.reshape(1, D_MODEL), w)


def _out_mlp_kernel(*refs, n_mix, final):
    h_ref = refs[0]
    mix_refs = refs[1:1 + n_mix]
    wout_ref, gffn_ref, wup_ref, wdown_ref = refs[1 + n_mix:5 + n_mix]
    rest = refs[5 + n_mix:]
    if final:
        gfin_ref, o_ref = rest
    else:
        (o_ref,) = rest
    h = h_ref[...]
    off = 0
    for m_ref in mix_refs:
        w = m_ref.shape[1]
        h = h + jnp.dot(m_ref[...].astype(BF16), wout_ref[off:off + w, :], preferred_element_type=F32)
        off += w
    xn = _rms(h, gffn_ref[...]).astype(BF16)
    acc = h
    step = 1024
    for j in range(D_FF // step):
        hid = jnp.dot(xn, wup_ref[:, j * step:(j + 1) * step], preferred_element_type=F32)
        hid = jnp.maximum(hid, 0.0)
        acc = acc + jnp.dot((hid * hid).astype(BF16), wdown_ref[j * step:(j + 1) * step, :],
                            preferred_element_type=F32)
    if final:
        acc = _rms(acc, gfin_ref[...])
    o_ref[...] = acc


def _out_mlp(h, mixes, w_out, g_ffn, w_up, w_down, g_final, tm):
    t = h.shape[0]
    assert t % tm == 0
    final = g_final is not None
    row = lambda i: (i, 0)
    const = lambda i: (0, 0)
    in_specs = [pl.BlockSpec((tm, D_MODEL), row)]
    in_specs += [pl.BlockSpec((tm, m.shape[1]), row) for m in mixes]
    in_specs += [pl.BlockSpec(w_out.shape, const), pl.BlockSpec((1, D_MODEL), const),
                 pl.BlockSpec(w_up.shape, const), pl.BlockSpec(w_down.shape, const)]
    args = [h, *mixes, w_out, g_ffn.reshape(1, D_MODEL), w_up, w_down]
    if final:
        in_specs.append(pl.BlockSpec((1, D_MODEL), const))
        args.append(g_final.reshape(1, D_MODEL))
    return pl.pallas_call(
        functools.partial(_out_mlp_kernel, n_mix=len(mixes), final=final),
        grid=(t // tm,),
        in_specs=in_specs,
        out_specs=pl.BlockSpec((tm, D_MODEL), row),
        out_shape=jax.ShapeDtypeStruct((t, D_MODEL), F32),
        compiler_params=_cparams("parallel"),
        name="out_mlp",
    )(*args)


def _chunk_masks(c):
    ii = lax.broadcasted_iota(jnp.int32, (c, c), 0)
    jj = lax.broadcasted_iota(jnp.int32, (c, c), 1)
    return ii, jj


def _mix_ab_kernel(qkv_ref, rest_ref, gates_ref, cos_ref, sin_ref, wconv_ref, prow_ref, pcol_ref,
                   ggain_ref, rgain_ref, mixed_ref, conv_ref, sd_ref, sr_ref, xp_ref, *, n_sub):
    c = CHUNK
    lb = n_sub * c
    step = pl.program_id(1)

    @pl.when(step == 0)
    def _():
        sd_ref[...] = jnp.zeros_like(sd_ref)
        sr_ref[...] = jnp.zeros_like(sr_ref)
        xp_ref[0:8, :] = jnp.zeros((8, A_CONV_CH), F32)

    u = qkv_ref[...]
    xp_ref[8:8 + lb, :] = u
    w = wconv_ref[...]
    conv = (xp_ref[5:5 + lb, :] * w[0:1] + xp_ref[6:6 + lb, :] * w[1:2]
            + xp_ref[7:7 + lb, :] * w[2:3] + u * w[3:4])
    xp_ref[0:8, :] = u[lb - 8:lb, :]
    conv_ref[0] = u[lb - 3:lb, :]
    act = _silu(conv)

    g = gates_ref[...]
    g_t = g.T
    neg_a_row = -jnp.exp(prow_ref[0:1, :])
    dt_row = prow_ref[1:2, :]
    neg_a_col = -jnp.exp(pcol_ref[0:HEADS, 0:1])
    dt_col = pcol_ref[0:HEADS, 1:2]
    la_cols = neg_a_row * _softplus(g + dt_row)
    beta_cols = jax.nn.sigmoid(g)
    la_rows = neg_a_col * _softplus(g_t[0:HEADS, :] + dt_col)

    ii, jj = _chunk_masks(c)
    causal = ii >= jj
    strict = ii > jj
    upper = ii <= jj
    dm = (ii - jj).astype(F32)
    pos_col = lax.broadcasted_iota(jnp.int32, (c, 1), 0).astype(F32)

    ggain = ggain_ref[...]
    rgain = rgain_ref[...]
    base = HEADS * DK
    s_delta = [sd_ref[0, h] for h in range(HEADS)]
    s_ret = [sr_ref[0, h] for h in range(HEADS)]

    items = [(ci, h) for ci in range(n_sub) for h in range(HEADS)]

    rng = lambda ci: (ci * c, (ci + 1) * c)
    pre = {it: {} for it in items}
    pre_b = {it: {} for it in items}

    def gdn_stage(ci, h):
        d = pre[ci, h]
        r0, r1 = rng(ci)
        la_row = la_rows[h:h + 1, r0:r1]
        la_col = la_cols[r0:r1, h:h + 1]
        beta = beta_cols[r0:r1, HEADS + h:HEADS + h + 1]
        g_col = jnp.sum(jnp.where(causal, la_row, 0.0), axis=1, keepdims=True)
        g_row = jnp.sum(jnp.where(upper, la_col, 0.0), axis=0, keepdims=True)
        g_last = jnp.sum(la_row, axis=1, keepdims=True)
        q_raw = act[r0:r1, h * DK:(h + 1) * DK]
        k_raw = act[r0:r1, (HEADS + h) * DK:(HEADS + h + 1) * DK]
        ssq_q = jnp.sum(q_raw * q_raw, axis=-1, keepdims=True)
        ssq_k = jnp.sum(k_raw * k_raw, axis=-1, keepdims=True)
        yield
        dec_causal = jnp.exp(jnp.where(causal, g_col - g_row, NEG))
        dec_strict = jnp.where(strict, dec_causal, 0.0)
        eg = jnp.exp(g_col)
        q = q_raw * lax.rsqrt(ssq_q + EPS) * (DK ** -0.5)
        k = k_raw * lax.rsqrt(ssq_k + EPS)
        yield
        v = act[r0:r1, (2 * HEADS + h) * DK:(2 * HEADS + h + 1) * DK]
        kb = k * beta
        prod = _mm_nt(jnp.concatenate([q, kb], axis=0), k)
        d.update(rhs=jnp.concatenate([v * beta, kb * eg], axis=1), qd=q * eg,
                 kd=k * jnp.exp(g_last - g_col), gl=jnp.exp(g_last))
        yield
        d.update(n=prod[c:2 * c] * dec_strict, qk=prod[0:c] * dec_causal)

    _interleave(gdn_stage(ci, h) for ci, h in items)
    t_invs = _tri_inv_many([pre[it]['n'] for it in items])
    for it, t_inv in zip(items, t_invs):
        pre[it]['sol'] = _mm3(t_inv, pre[it]['rhs'])

    def ret_stage(ci, h):
        d = pre_b[ci, h]
        r0, r1 = rng(ci)
        cosf = cos_ref[r0:r1, :]
        sinf = sin_ref[r0:r1, :]
        qb = rest_ref[r0:r1, base + h * DK:base + (h + 1) * DK]
        kb2 = rest_ref[r0:r1, 2 * base + h * DK:2 * base + (h + 1) * DK]
        q_rot = pltpu.roll(qb, DK // 2, 1)
        k_rot = pltpu.roll(kb2, DK // 2, 1)
        yield
        qr = qb * cosf + q_rot * sinf
        kr = (kb2 * cosf + k_rot * sinf) * (DK ** -0.5)
        lg = LOG_GAMMA[h]
        dec = jnp.exp(jnp.where(causal, dm * lg, NEG))
        qk = _mm_nt(qr, kr)
        d['qd'] = qr * jnp.exp((pos_col + 1.0) * lg)
        kd = kr * jnp.exp((float(c - 1) - pos_col) * lg)
        yield
        vb = rest_ref[r0:r1, 3 * base + h * DK:3 * base + (h + 1) * DK]
        d['intra'] = _mm(qk * dec, vb)
        d['kv'] = _mm_tn(kd, vb)

    _interleave(ret_stage(ci, h) for ci, h in items)

    for ci, h in items:
        pre_b[ci, h]['s_in'] = s_ret[h]
        s_ret[h] = s_ret[h] * math.exp(c * LOG_GAMMA[h]) + pre_b[ci, h]['kv']

    def ret_out_stage(ci, h):
        d = pre_b[ci, h]
        r0, r1 = rng(ci)
        o = _mm(d['qd'], d['s_in']) + d['intra']
        yield
        ms_ = jnp.mean(o * o, axis=-1, keepdims=True)
        yield
        gb = rest_ref[r0:r1, 4 * base + h * DK:4 * base + (h + 1) * DK]
        mixed_ref[r0:r1, base + h * DK:base + (h + 1) * DK] = o * lax.rsqrt(ms_ + EPS) * rgain * _silu(gb)

    _interleave(ret_out_stage(ci, h) for ci, h in items)

    def delta_step(ci, h):
        d = pre[ci, h]
        r = _mm(jnp.concatenate([d['sol'][:, A_DV:2 * A_DV], d['qd']], axis=0), s_delta[h])
        yield
        v_new = d['sol'][:, 0:A_DV] - r[0:c]
        d['o'] = r[c:2 * c] + _mm(d['qk'], v_new)
        s_delta[h] = s_delta[h] * d['gl'] + _mm_tn(d['kd'], v_new)

    for ci in range(n_sub):
        _interleave(delta_step(ci, h) for h in range(HEADS))

    def gdn_out_stage(ci, h):
        o = pre[ci, h]['o']
        r0, r1 = rng(ci)
        ms_ = jnp.mean(o * o, axis=-1, keepdims=True)
        yield
        ga = rest_ref[r0:r1, h * DK:(h + 1) * DK]
        mixed_ref[r0:r1, h * A_DV:(h + 1) * A_DV] = o * lax.rsqrt(ms_ + EPS) * ggain * _silu(ga)

    _interleave(gdn_out_stage(ci, h) for ci, h in items)

    for h in range(HEADS):
        sd_ref[0, h] = s_delta[h]
        sr_ref[0, h] = s_ret[h]


def _rope_tables(pos):
    half = DK // 2
    inv = ROPE_BASE ** (-jnp.arange(half, dtype=F32) / half)
    ang = pos.astype(F32)[:, None] * inv[None, :]
    cos, sin = jnp.cos(ang), jnp.sin(ang)
    return jnp.concatenate([cos, cos], axis=-1), jnp.concatenate([-sin, sin], axis=-1)


def _gate_params(a_log, dt_bias):
    prow = jnp.zeros((8, 128), F32).at[0, 0:HEADS].set(a_log).at[1, 0:HEADS].set(dt_bias)
    pcol = jnp.zeros((8, 128), F32).at[0:HEADS, 0].set(a_log).at[0:HEADS, 1].set(dt_bias)
    return prow, pcol


def _mix_ab_prompt(qkv, rest, gates, batch, seq, w_conv, a_log, dt_bias, gdn_gain, ret_gain):
    lb = AB_SUB * CHUNK
    assert seq % lb == 0
    nc = seq // lb
    cosf, sinf = _rope_tables(jnp.arange(seq))
    prow, pcol = _gate_params(a_log, dt_bias)
    tok = lambda b, c: (b * nc + c, 0)
    const = lambda b, c: (0, 0)
    mixed, conv, sd, sr = pl.pallas_call(
        functools.partial(_mix_ab_kernel, n_sub=AB_SUB),
        grid=(batch, nc),
        in_specs=[pl.BlockSpec((lb, A_CONV_CH), tok),
                  pl.BlockSpec((lb, rest.shape[1]), tok),
                  pl.BlockSpec((lb, 128), tok),
                  pl.BlockSpec((lb, DK), lambda b, c: (c, 0)),
                  pl.BlockSpec((lb, DK), lambda b, c: (c, 0)),
                  pl.BlockSpec((CONV_W, A_CONV_CH), const),
                  pl.BlockSpec((8, 128), const),
                  pl.BlockSpec((8, 128), const),
                  pl.BlockSpec((1, A_DV), const),
                  pl.BlockSpec((1, DK), const)],
        out_specs=[pl.BlockSpec((lb, 2 * HEADS * DK), tok),
                   pl.BlockSpec((1, CONV_W - 1, A_CONV_CH), lambda b, c: (b, 0, 0)),
                   pl.BlockSpec((1, HEADS, DK, A_DV), lambda b, c: (b, 0, 0, 0)),
                   pl.BlockSpec((1, HEADS, DK, DK), lambda b, c: (b, 0, 0, 0))],
        out_shape=[jax.ShapeDtypeStruct((batch * seq, 2 * HEADS * DK), F32),
                   jax.ShapeDtypeStruct((batch, CONV_W - 1, A_CONV_CH), F32),
                   jax.ShapeDtypeStruct((batch, HEADS, DK, A_DV), F32),
                   jax.ShapeDtypeStruct((batch, HEADS, DK, DK), F32)],
        scratch_shapes=[pltpu.VMEM((lb + 8, A_CONV_CH), F32)],
        compiler_params=_cparams("parallel", "arbitrary"),
        name="mix_ab_prompt",
    )(qkv, rest, gates, cosf, sinf, w_conv, prow, pcol, gdn_gain.reshape(1, A_DV), ret_gain.reshape(1, DK))
    return mixed, conv, sd, sr


def _mix_c_kernel(qk_ref, v_ref, opre_ref, gates_ref, brow_ref, bcol_ref, gain_ref,
                  h_ref, cm_ref, nv_ref, m_ref, *, n_sub):
    c = CHUNK
    step = pl.program_id(1)

    @pl.when(step == 0)
    def _():
        cm_ref[...] = jnp.zeros_like(cm_ref)
        nv_ref[...] = jnp.zeros_like(nv_ref)
        m_ref[...] = jnp.zeros_like(m_ref)

    g = gates_ref[...] + brow_ref[0:1, :]
    g_t = gates_ref[...].T[0:2 * HEADS, :] + bcol_ref[0:2 * HEADS, 0:1]
    logf_cols = _log_sigmoid(g)
    logf_rows = _log_sigmoid(g_t[HEADS:2 * HEADS, :])
    ii, jj = _chunk_masks(c)
    causal = ii >= jj
    upper = ii <= jj
    gain = gain_ref[...]

    items = [(ci, h) for ci in range(n_sub) for h in range(HEADS)]
    rng = lambda ci: (ci * c, (ci + 1) * c)

    pre = {it: {} for it in items}

    def gates_stage(ci, h):
        d = pre[ci, h]
        r0, r1 = rng(ci)
        i_col = g[r0:r1, h:h + 1]
        i_row = g_t[h:h + 1, r0:r1]
        lf_col = logf_cols[r0:r1, HEADS + h:HEADS + h + 1]
        lf_row = logf_rows[h:h + 1, r0:r1]
        b_col = jnp.sum(jnp.where(causal, lf_row, 0.0), axis=1, keepdims=True)
        b_row = jnp.sum(jnp.where(upper, lf_col, 0.0), axis=0, keepdims=True)
        b_last = jnp.sum(lf_row, axis=1, keepdims=True)
        yield
        d_log = jnp.where(causal, b_col - b_row + i_row, NEG)
        d.update(b_col=b_col, b_last=b_last, d_log=d_log, k_log=b_last - b_col + i_col)
        yield
        d['d_max'] = jnp.max(d_log, axis=1, keepdims=True)
        yield
        q = qk_ref[r0:r1, h * DK:(h + 1) * DK]
        k = qk_ref[r0:r1, (HEADS + h) * DK:(HEADS + h + 1) * DK] * (DK ** -0.5)
        d.update(q=q, k=k, qk=_mm_nt(q, k))

    _interleave(gates_stage(ci, h) for ci, h in items)

    ms = [m_ref[0, h:h + 1, 0:1] for h in range(HEADS)]
    for ci, h in items:
        d = pre[ci, h]
        inter = d['b_col'] + ms[h]
        m_row = jnp.maximum(inter, d['d_max'])
        m_new = m_row[c - 1:c, :]
        d.update(inter=inter, m_row=m_row, m_old=ms[h], m_new=m_new)
        ms[h] = m_new

    def weights_stage(ci, h):
        d = pre[ci, h]
        r0, r1 = rng(ci)
        d['w_inter'] = jnp.exp(d['inter'] - d['m_row'])
        d['f_state'] = jnp.exp(d['b_last'] + d['m_old'] - d['m_new'])
        yield
        d['w_intra'] = jnp.exp(d['d_log'] - d['m_row']) * d['qk']
        d['kw'] = d['k'] * jnp.exp(d['k_log'] - d['m_new'])
        yield
        v = v_ref[r0:r1, h * C_DV:(h + 1) * C_DV]
        d['intra'] = _mm(d['w_intra'], v)
        d['kv'] = _mm_tn(d['kw'], v)
        yield
        d['sum_intra'] = jnp.sum(d['w_intra'], axis=1, keepdims=True)
        d['sum_kw'] = jnp.sum(d['kw'], axis=0, keepdims=True)
        d['inv_floor'] = jnp.exp(-d['m_row'])

    _interleave(weights_stage(ci, h) for ci, h in items)

    cms = [cm_ref[0, h] for h in range(HEADS)]
    nvs = [nv_ref[0, h:h + 1, :] for h in range(HEADS)]
    for ci, h in items:
        d = pre[ci, h]
        d['cm_in'], d['nv_in'] = cms[h], nvs[h]
        cms[h] = cms[h] * d['f_state'] + d['kv']
        nvs[h] = nvs[h] * d['f_state'] + d['sum_kw']

    def output_stage(ci, h):
        d = pre[ci, h]
        r0, r1 = rng(ci)
        qc = _mm(d['q'], d['cm_in'])
        qn = jnp.sum(d['q'] * d['nv_in'], axis=1, keepdims=True)
        yield
        num = d['w_inter'] * qc + d['intra']
        den = d['w_inter'] * qn + d['sum_intra']
        hh = num / jnp.maximum(jnp.abs(den), d['inv_floor'])
        yield
        ms_ = jnp.mean(hh * hh, axis=-1, keepdims=True)
        yield
        op = opre_ref[r0:r1, h * C_DV:(h + 1) * C_DV]
        h_ref[r0:r1, h * C_DV:(h + 1) * C_DV] = hh * lax.rsqrt(ms_ + EPS) * gain * jax.nn.sigmoid(op)

    _interleave(output_stage(ci, h) for ci, h in items)

    for h in range(HEADS):
        cm_ref[0, h] = cms[h]
        nv_ref[0, h:h + 1, :] = nvs[h]
        m_ref[0, h:h + 1, :] = jnp.broadcast_to(ms[h], (1, 128))


def _bias_params(b_gate):
    brow = jnp.zeros((8, 128), F32).at[0, 0:2 * HEADS].set(b_gate)
    bcol = jnp.zeros((8, 128), F32).at[0:2 * HEADS, 0].set(b_gate)
    return brow, bcol


def _mix_c_prompt(qk, v, opre, gates, batch, seq, b_gate, gain):
    lb = C_SUB * CHUNK
    assert seq % lb == 0
    nc = seq // lb
    brow, bcol = _bias_params(b_gate)
    tok = lambda b, c: (b * nc + c, 0)
    const = lambda b, c: (0, 0)
    return pl.pallas_call(
        functools.partial(_mix_c_kernel, n_sub=C_SUB),
        grid=(batch, nc),
        in_specs=[pl.BlockSpec((lb, 2 * HEADS * DK), tok),
                  pl.BlockSpec((lb, HEADS * C_DV), tok),
                  pl.BlockSpec((lb, HEADS * C_DV), tok),
                  pl.BlockSpec((lb, 128), tok),
                  pl.BlockSpec((8, 128), const),
                  pl.BlockSpec((8, 128), const),
                  pl.BlockSpec((1, C_DV), const)],
        out_specs=[pl.BlockSpec((lb, HEADS * C_DV), tok),
                   pl.BlockSpec((1, HEADS, DK, C_DV), lambda b, c: (b, 0, 0, 0)),
                   pl.BlockSpec((1, HEADS, DK), lambda b, c: (b, 0, 0)),
                   pl.BlockSpec((1, HEADS, 128), lambda b, c: (b, 0, 0))],
        out_shape=[jax.ShapeDtypeStruct((batch * seq, HEADS * C_DV), F32),
                   jax.ShapeDtypeStruct((batch, HEADS, DK, C_DV), F32),
                   jax.ShapeDtypeStruct((batch, HEADS, DK), F32),
                   jax.ShapeDtypeStruct((batch, HEADS, 128), F32)],
        compiler_params=_cparams("parallel", "arbitrary"),
        name="mix_c_prompt",
    )(qk, v, opre, gates, brow, bcol, gain.reshape(1, C_DV))


def _row_select(rows, t, new, old):
    return jnp.where(rows == t, new, old)


def _step_ab_kernel(qkv_ref, rest_ref, gates_ref, cos_ref, sin_ref, wconv_ref, prow_ref, ggain_ref, rgain_ref,
                    buf_ref, sd_ref, sr_ref, mixed_ref, nbuf_ref, nsd_ref, nsr_ref):
    tb = qkv_ref.shape[0]
    u = qkv_ref[...]
    w = wconv_ref[...]
    b0 = buf_ref[:, 0:A_CONV_CH]
    b1 = buf_ref[:, A_CONV_CH:2 * A_CONV_CH]
    b2 = buf_ref[:, 2 * A_CONV_CH:3 * A_CONV_CH]
    conv = b0 * w[0:1] + b1 * w[1:2] + b2 * w[2:3] + u * w[3:4]
    nbuf_ref[:, 0:A_CONV_CH] = b1
    nbuf_ref[:, A_CONV_CH:2 * A_CONV_CH] = b2
    nbuf_ref[:, 2 * A_CONV_CH:3 * A_CONV_CH] = u
    act = _silu(conv)
    g = gates_ref[...]
    eg_all = jnp.exp(-jnp.exp(prow_ref[0:1, :]) * _softplus(g + prow_ref[1:2, :]))
    beta_all = jax.nn.sigmoid(g)
    cosf = cos_ref[...]
    sinf = sin_ref[...]
    base = HEADS * DK
    rows = lax.broadcasted_iota(jnp.int32, (tb, DK), 0)
    items = [(t, h) for t in range(tb) for h in range(HEADS)]

    qs = [_l2(act[:, h * DK:(h + 1) * DK]) * (DK ** -0.5) for h in range(HEADS)]
    ks = [_l2(act[:, (HEADS + h) * DK:(HEADS + h + 1) * DK]) for h in range(HEADS)]
    k_s = [jnp.zeros((tb, A_DV), F32) for _ in range(HEADS)]
    for t, h in items:
        k_s[h] = _row_select(rows, t, _mm(ks[h], sd_ref[t, h]), k_s[h])
    v_new = []
    for h in range(HEADS):
        v = act[:, (2 * HEADS + h) * DK:(2 * HEADS + h + 1) * DK]
        v_new.append(beta_all[:, HEADS + h:HEADS + h + 1] * (v - eg_all[:, h:h + 1] * k_s[h]))
    for t, h in items:
        nsd_ref[t, h] = (sd_ref[t, h] * eg_all[t:t + 1, h:h + 1]
                         + _mm3_tn(jnp.where(rows == t, ks[h], 0.0), v_new[h]))
    o_a = [jnp.zeros((tb, A_DV), F32) for _ in range(HEADS)]
    for t, h in items:
        o_a[h] = _row_select(rows, t, _mm(qs[h], nsd_ref[t, h]), o_a[h])
    for h in range(HEADS):
        ga = rest_ref[:, h * DK:(h + 1) * DK]
        mixed_ref[:, h * DK:(h + 1) * DK] = _rms(o_a[h], ggain_ref[...]) * _silu(ga)

    qrs, krs = [], []
    for h in range(HEADS):
        qb = rest_ref[:, base + h * DK:base + (h + 1) * DK]
        kb = rest_ref[:, 2 * base + h * DK:2 * base + (h + 1) * DK]
        qrs.append(qb * cosf + pltpu.roll(qb, DK // 2, 1) * sinf)
        krs.append((kb * cosf + pltpu.roll(kb, DK // 2, 1) * sinf) * (DK ** -0.5))
    for t, h in items:
        vb = rest_ref[:, 3 * base + h * DK:3 * base + (h + 1) * DK]
        nsr_ref[t, h] = (sr_ref[t, h] * math.exp(LOG_GAMMA[h])
                         + _mm3_tn(jnp.where(rows == t, krs[h], 0.0), vb))
    o_b = [jnp.zeros((tb, DK), F32) for _ in range(HEADS)]
    for t, h in items:
        o_b[h] = _row_select(rows, t, _mm(qrs[h], nsr_ref[t, h]), o_b[h])
    for h in range(HEADS):
        gb = rest_ref[:, 4 * base + h * DK:4 * base + (h + 1) * DK]
        mixed_ref[:, base + h * DK:base + (h + 1) * DK] = _rms(o_b[h], rgain_ref[...]) * _silu(gb)


def _mix_ab_sample(qkv, rest, gates, conv_buf, s_delta, s_ret, w_conv, a_log, dt_bias, gdn_gain, ret_gain):
    nb = qkv.shape[0]
    tb = SAMPLE_TB
    assert nb % tb == 0
    cosf, sinf = _rope_tables(PAST_LEN + jnp.arange(1))
    prow, _ = _gate_params(a_log, dt_bias)
    tok = lambda b: (b, 0)
    tok4 = lambda b: (b, 0, 0, 0)
    const = lambda b: (0, 0)
    n_buf = (CONV_W - 1) * A_CONV_CH
    mixed, nbuf, nsd, nsr = pl.pallas_call(
        _step_ab_kernel,
        grid=(nb // tb,),
        in_specs=[pl.BlockSpec((tb, A_CONV_CH), tok),
                  pl.BlockSpec((tb, rest.shape[1]), tok),
                  pl.BlockSpec((tb, 128), tok),
                  pl.BlockSpec((1, DK), const),
                  pl.BlockSpec((1, DK), const),
                  pl.BlockSpec((CONV_W, A_CONV_CH), const),
                  pl.BlockSpec((8, 128), const),
                  pl.BlockSpec((1, A_DV), const),
                  pl.BlockSpec((1, DK), const),
                  pl.BlockSpec((tb, n_buf), tok),
                  pl.BlockSpec((tb, HEADS, DK, A_DV), tok4),
                  pl.BlockSpec((tb, HEADS, DK, DK), tok4)],
        out_specs=[pl.BlockSpec((tb, 2 * HEADS * DK), tok),
                   pl.BlockSpec((tb, n_buf), tok),
                   pl.BlockSpec((tb, HEADS, DK, A_DV), tok4),
                   pl.BlockSpec((tb, HEADS, DK, DK), tok4)],
        out_shape=[jax.ShapeDtypeStruct((nb, 2 * HEADS * DK), F32),
                   jax.ShapeDtypeStruct((nb, n_buf), F32),
                   jax.ShapeDtypeStruct((nb, HEADS, DK, A_DV), F32),
                   jax.ShapeDtypeStruct((nb, HEADS, DK, DK), F32)],
        compiler_params=_cparams("parallel"),
        name="mix_ab_sample",
    )(qkv, rest, gates, cosf, sinf, w_conv, prow, gdn_gain.reshape(1, A_DV), ret_gain.reshape(1, DK),
      conv_buf.reshape(nb, n_buf), s_delta, s_ret)
    return mixed, nbuf.reshape(nb, CONV_W - 1, A_CONV_CH), nsd, nsr


def _step_c_kernel(qk_ref, v_ref, opre_ref, gates_ref, brow_ref, gain_ref, cm_ref, nv_ref, m_ref,
                   h_ref, ncm_ref, nnv_ref, nm_ref):
    tb = qk_ref.shape[0]
    g = gates_ref[...] + brow_ref[0:1, :]
    logf = _log_sigmoid(g)
    rows = lax.broadcasted_iota(jnp.int32, (tb, DK), 0)
    items = [(t, h) for t in range(tb) for h in range(HEADS)]
    qs, kws, fs, ms_new, nvs_new = [], [], [], [], []
    for h in range(HEADS):
        i_pre = g[:, h:h + 1]
        q = qk_ref[:, h * DK:(h + 1) * DK]
        k = qk_ref[:, (HEADS + h) * DK:(HEADS + h + 1) * DK] * (DK ** -0.5)
        inter = logf[:, HEADS + h:HEADS + h + 1] + m_ref[:, h:h + 1]
        m_new = jnp.maximum(inter, i_pre)
        f_state = jnp.exp(inter - m_new)
        kw = k * jnp.exp(i_pre - m_new)
        nv_new = nv_ref[:, h * DK:(h + 1) * DK] * f_state + kw
        nnv_ref[:, h * DK:(h + 1) * DK] = nv_new
        nm_ref[:, h:h + 1] = m_new
        qs.append(q)
        kws.append(kw)
        fs.append(f_state)
        ms_new.append(m_new)
        nvs_new.append(nv_new)
    for t, h in items:
        v = v_ref[:, h * C_DV:(h + 1) * C_DV]
        ncm_ref[t, h] = (cm_ref[t, h] * fs[h][t:t + 1, :]
                         + _mm3_tn(jnp.where(rows == t, kws[h], 0.0), v))
    rows_v = lax.broadcasted_iota(jnp.int32, (tb, C_DV), 0)
    nums = [jnp.zeros((tb, C_DV), F32) for _ in range(HEADS)]
    for t, h in items:
        nums[h] = _row_select(rows_v, t, _mm(qs[h], ncm_ref[t, h]), nums[h])
    for h in range(HEADS):
        den = jnp.sum(qs[h] * nvs_new[h], axis=1, keepdims=True)
        hh = nums[h] / jnp.maximum(jnp.abs(den), jnp.exp(-ms_new[h]))
        op = opre_ref[:, h * C_DV:(h + 1) * C_DV]
        h_ref[:, h * C_DV:(h + 1) * C_DV] = _rms(hh, gain_ref[...]) * jax.nn.sigmoid(op)


def _mix_c_sample(qk, v, opre, gates, s_c, s_n, s_m, b_gate, gain):
    nb = qk.shape[0]
    tb = SAMPLE_TB
    assert nb % tb == 0
    brow, _ = _bias_params(b_gate)
    tok = lambda b: (b, 0)
    tok4 = lambda b: (b, 0, 0, 0)
    const = lambda b: (0, 0)
    hh, ncm, nnv, nm = pl.pallas_call(
        _step_c_kernel,
        grid=(nb // tb,),
        in_specs=[pl.BlockSpec((tb, 2 * HEADS * DK), tok),
                  pl.BlockSpec((tb, HEADS * C_DV), tok),
                  pl.BlockSpec((tb, HEADS * C_DV), tok),
                  pl.BlockSpec((tb, 128), tok),
                  pl.BlockSpec((8, 128), const),
                  pl.BlockSpec((1, C_DV), const),
                  pl.BlockSpec((tb, HEADS, DK, C_DV), tok4),
                  pl.BlockSpec((tb, HEADS * DK), tok),
                  pl.BlockSpec((tb, HEADS), tok)],
        out_specs=[pl.BlockSpec((tb, HEADS * C_DV), tok),
                   pl.BlockSpec((tb, HEADS, DK, C_DV), tok4),
                   pl.BlockSpec((tb, HEADS * DK), tok),
                   pl.BlockSpec((tb, HEADS), tok)],
        out_shape=[jax.ShapeDtypeStruct((nb, HEADS * C_DV), F32),
                   jax.ShapeDtypeStruct((nb, HEADS, DK, C_DV), F32),
                   jax.ShapeDtypeStruct((nb, HEADS * DK), F32),
                   jax.ShapeDtypeStruct((nb, HEADS), F32)],
        compiler_params=_cparams("parallel"),
        name="mix_c_sample",
    )(qk, v, opre, gates, brow, gain.reshape(1, C_DV), s_c, s_n.reshape(nb, HEADS * DK), s_m)
    return hh, ncm, nnv.reshape(nb, HEADS, DK), nm


A_SPLITS = (A_CONV_CH, 5 * HEADS * DK, 128)
C_SPLITS = (2 * HEADS * DK, HEADS * C_DV, HEADS * C_DV, 128)


def _prep_w_in_a(w):
    n_gate = 2 * HEADS
    qkv = w[:, :A_CONV_CH]
    gates = w[:, A_CONV_CH:A_CONV_CH + n_gate]
    rest = w[:, A_CONV_CH + n_gate:]
    pad = jnp.zeros((D_MODEL, 128 - n_gate), w.dtype)
    return jnp.concatenate([qkv, rest, gates, pad], axis=1).astype(BF16)


def _prep_w_in_c(w):
    n_gate = 2 * HEADS
    main = w[:, :2 * HEADS * DK + 2 * HEADS * C_DV]
    gates = w[:, 2 * HEADS * DK + 2 * HEADS * C_DV:]
    pad = jnp.zeros((D_MODEL, 128 - n_gate), w.dtype)
    return jnp.concatenate([main, gates, pad], axis=1).astype(BF16)


def _run_group(x, prompt, states, p, tm):
    batch, seq, _ = x.shape
    h = x.reshape(batch * seq, D_MODEL)
    qkv, rest, gates = _norm_proj(h, p['norm_mix_a'][0], p['w_in_a'], A_SPLITS, tm)
    if prompt:
        mixed, conv, sd, sr = _mix_ab_prompt(qkv, rest, gates, batch, seq, p['w_conv_a'][0], p['a_log'][0],
                                             p['dt_bias'][0], p['gdn_gain'][0], p['ret_gain'][0])
    else:
        mixed, conv, sd, sr = _mix_ab_sample(qkv, rest, gates, states['conv'][0], states['delta'][0],
                                             states['ret'][0], p['w_conv_a'][0], p['a_log'][0], p['dt_bias'][0],
                                             p['gdn_gain'][0], p['ret_gain'][0])
    h = _out_mlp(h, [mixed], p['w_out_a'], p['norm_ffn'][0], p['w_up'][0], p['w_down'][0], None, tm)
    qk, v, opre, gates_c = _norm_proj(h, p['norm_mix_c'][0], p['w_in_c'], C_SPLITS, tm)
    if prompt:
        hm, cm, nv, m = _mix_c_prompt(qk, v, opre, gates_c, batch, seq, p['b_gate_c'][0], p['mlstm_gain'][0])
        m = m[:, :, 0]
    else:
        hm, cm, nv, m = _mix_c_sample(qk, v, opre, gates_c, states['c'][0], states['n'][0], states['m'][0],
                                      p['b_gate_c'][0], p['mlstm_gain'][0])
    y = _out_mlp(h, [hm], p['w_out_c'], p['norm_ffn'][1], p['w_up'][1], p['w_down'][1], p['final_gain'], tm)
    return (y.reshape(batch, seq, D_MODEL), conv[None], sd[None], sr[None], cm[None], nv[None], m[None])


def kernel(x_prompt, x_sample, state_conv_a, state_delta_a, state_ret_b, state_mlstm_C, state_mlstm_n, state_mlstm_m,
           norm_mix_a, w_in_a, w_conv_a, a_log, dt_bias, gdn_gain, ret_gain, w_out_a,
           norm_mix_c, w_in_c, b_gate_c, mlstm_gain, w_out_c, norm_ffn, w_up, w_down, final_gain):
    p = dict(norm_mix_a=norm_mix_a, w_in_a=_prep_w_in_a(w_in_a[0]), w_conv_a=w_conv_a, a_log=a_log, dt_bias=dt_bias,
             gdn_gain=gdn_gain, ret_gain=ret_gain, w_out_a=w_out_a[0].astype(BF16), norm_mix_c=norm_mix_c,
             w_in_c=_prep_w_in_c(w_in_c[0]), b_gate_c=b_gate_c, mlstm_gain=mlstm_gain,
             w_out_c=w_out_c[0].astype(BF16), norm_ffn=norm_ffn, w_up=w_up.astype(BF16),
             w_down=w_down.astype(BF16), final_gain=final_gain)
    outs_p = _run_group(x_prompt, True, None, p, 256)
    states = dict(conv=state_conv_a, delta=state_delta_a, ret=state_ret_b, c=state_mlstm_C, n=state_mlstm_n,
                  m=state_mlstm_m)
    outs_s = _run_group(x_sample, False, states, p, x_sample.shape[0] * x_sample.shape[1])
    return (outs_p[0], outs_s[0]) + outs_p[1:] + outs_s[1:]
```

```python
import functools
import math

import jax
import jax.numpy as jnp
from jax import lax
from jax.experimental import pallas as pl
from jax.experimental.pallas import tpu as pltpu

F32 = jnp.float32
BF16 = jnp.bfloat16

D_MODEL = 1024
D_FF = 4 * D_MODEL
CHUNK = 64
EPS = 1e-6
NEG = -1e30
HEADS = 4
DK = 128
A_DV = 128
C_DV = 256
CONV_W = 4
A_CONV_CH = 3 * HEADS * DK
ROPE_BASE = 10000.0
PAST_LEN = 16384
LOG_GAMMA = tuple(math.log1p(-(2.0 ** (-5.0 - h))) for h in range(HEADS))

VMEM_LIMIT_BYTES = 56 * 1024 * 1024
PROMPT_TM = 512
AB_SUB = 4
C_SUB = 4
SAMPLE_TB = 8


def _cparams(*sem):
    return pltpu.CompilerParams(dimension_semantics=sem, vmem_limit_bytes=VMEM_LIMIT_BYTES)


def _mm(a, b):
    return jnp.dot(a.astype(BF16), b.astype(BF16), preferred_element_type=F32)


def _mm_nt(a, b):
    return lax.dot_general(a.astype(BF16), b.astype(BF16), (((1,), (1,)), ((), ())), preferred_element_type=F32)


def _mm_tn(a, b):
    return lax.dot_general(a.astype(BF16), b.astype(BF16), (((0,), (0,)), ((), ())), preferred_element_type=F32)


def _split(a):
    hi = a.astype(BF16)
    lo = (a - hi.astype(F32)).astype(BF16)
    return hi, lo


def _mm3(a, b):
    ah, al = _split(a)
    bh, bl = _split(b)
    d = lambda x, y: jnp.dot(x, y, preferred_element_type=F32)
    return d(ah, bh) + (d(ah, bl) + d(al, bh))


def _mm3_tn(a, b):
    ah, al = _split(a)
    bh, bl = _split(b)
    d = lambda x, y: lax.dot_general(x, y, (((0,), (0,)), ((), ())), preferred_element_type=F32)
    return d(ah, bh) + (d(ah, bl) + d(al, bh))


def _softplus(x):
    return jnp.maximum(x, 0.0) + jnp.log1p(jnp.exp(-jnp.abs(x)))


def _log_sigmoid(x):
    return -_softplus(-x)


def _silu(x):
    return x * jax.nn.sigmoid(x)


def _rms(x, gain):
    return x * lax.rsqrt(jnp.mean(x * x, axis=-1, keepdims=True) + EPS) * gain


def _l2(t):
    return t * lax.rsqrt(jnp.sum(t * t, axis=-1, keepdims=True) + EPS)


def _interleave(gens):
    gens = list(gens)
    while gens:
        alive = []
        for gen in gens:
            try:
                next(gen)
                alive.append(gen)
            except StopIteration:
                pass
        gens = alive


def _tri_inv_many(ns):
    c = ns[0].shape[0]
    eye = (lax.broadcasted_iota(jnp.int32, (c, c), 0) == lax.broadcasted_iota(jnp.int32, (c, c), 1)).astype(F32)
    ts = [eye - n for n in ns]
    ps = list(ns)
    steps = int(math.log2(c)) - 1
    for _ in range(steps):
        ps = [_mm3(p, p) for p in ps]
        ts = [t + _mm3(t, p) for t, p in zip(ts, ps)]
    return ts


def _norm_proj_kernel(x_ref, g_ref, w_ref, *out_refs, splits):
    xn = _rms(x_ref[...], g_ref[...]).astype(BF16)
    off = 0
    for o_ref, n in zip(out_refs, splits):
        o_ref[...] = jnp.dot(xn, w_ref[:, off:off + n], preferred_element_type=F32)
        off += n


def _norm_proj(x, gain, w, splits, tm):
    t = x.shape[0]
    n = w.shape[1]
    assert sum(splits) == n and t % tm == 0
    return pl.pallas_call(
        functools.partial(_norm_proj_kernel, splits=splits),
        grid=(t // tm,),
        in_specs=[pl.BlockSpec((tm, D_MODEL), lambda i: (i, 0)),
                  pl.BlockSpec((1, D_MODEL), lambda i: (0, 0)),
                  pl.BlockSpec((D_MODEL, n), lambda i: (0, 0), pipeline_mode=pl.Buffered(1))],
        out_specs=[pl.BlockSpec((tm, s), lambda i: (i, 0)) for s in splits],
        out_shape=[jax.ShapeDtypeStruct((t, s), F32) for s in splits],
        compiler_params=_cparams("parallel"),
        name="norm_proj",
    )(x, gain.reshape(1, D_MODEL), w)


def _out_mlp_kernel(*refs, n_mix, final):
    h_ref = refs[0]
    mix_refs = refs[1:1 + n_mix]
    wout_ref, gffn_ref, wup_ref, wdown_ref = refs[1 + n_mix:5 + n_mix]
    rest = refs[5 + n_mix:]
    if final:
        gfin_ref, o_ref = rest
    else:
        (o_ref,) = rest
    h = h_ref[...]
    off = 0
    for m_ref in mix_refs:
        w = m_ref.shape[1]
        h = h + jnp.dot(m_ref[...].astype(BF16), wout_ref[off:off + w, :], preferred_element_type=F32)
        off += w
    xn = _rms(h, gffn_ref[...]).astype(BF16)
    acc = h
    step = 1024
    for j in range(D_FF // step):
        hid = jnp.dot(xn, wup_ref[:, j * step:(j + 1) * step], preferred_element_type=F32)
        hid = jnp.maximum(hid, 0.0)
        acc = acc + jnp.dot((hid * hid).astype(BF16), wdown_ref[j * step:(j + 1) * step, :],
                            preferred_element_type=F32)
    if final:
        acc = _rms(acc, gfin_ref[...])
    o_ref[...] = acc


def _out_mlp(h, mixes, w_out, g_ffn, w_up, w_down, g_final, tm):
    t = h.shape[0]
    assert t % tm == 0
    final = g_final is not None
    row = lambda i: (i, 0)
    const = lambda i: (0, 0)
    in_specs = [pl.BlockSpec((tm, D_MODEL), row)]
    in_specs += [pl.BlockSpec((tm, m.shape[1]), row) for m in mixes]
    once = pl.Buffered(1)
    in_specs += [pl.BlockSpec(w_out.shape, const, pipeline_mode=once), pl.BlockSpec((1, D_MODEL), const),
                 pl.BlockSpec(w_up.shape, const, pipeline_mode=once),
                 pl.BlockSpec(w_down.shape, const, pipeline_mode=once)]
    args = [h, *mixes, w_out, g_ffn.reshape(1, D_MODEL), w_up, w_down]
    if final:
        in_specs.append(pl.BlockSpec((1, D_MODEL), const))
        args.append(g_final.reshape(1, D_MODEL))
    return pl.pallas_call(
        functools.partial(_out_mlp_kernel, n_mix=len(mixes), final=final),
        grid=(t // tm,),
        in_specs=in_specs,
        out_specs=pl.BlockSpec((tm, D_MODEL), row),
        out_shape=jax.ShapeDtypeStruct((t, D_MODEL), F32),
        compiler_params=_cparams("parallel"),
        name="out_mlp",
    )(*args)


def _chunk_masks(c):
    ii = lax.broadcasted_iota(jnp.int32, (c, c), 0)
    jj = lax.broadcasted_iota(jnp.int32, (c, c), 1)
    return ii, jj


def _mix_ab_kernel(qkv_ref, rest_ref, gates_ref, cos_ref, sin_ref, wconv_ref, prow_ref, pcol_ref, tri_ref, spread_ref,
                   ggain_ref, rgain_ref, mixed_ref, conv_ref, sd_ref, sr_ref, xp_ref, *, n_sub):
    c = CHUNK
    lb = n_sub * c
    step = pl.program_id(1)

    @pl.when(step == 0)
    def _():
        sd_ref[...] = jnp.zeros_like(sd_ref)
        sr_ref[...] = jnp.zeros_like(sr_ref)
        xp_ref[0:8, :] = jnp.zeros((8, A_CONV_CH), F32)

    u = qkv_ref[...]
    xp_ref[8:8 + lb, :] = u
    w = wconv_ref[...]
    conv = (xp_ref[5:5 + lb, :] * w[0:1] + xp_ref[6:6 + lb, :] * w[1:2]
            + xp_ref[7:7 + lb, :] * w[2:3] + u * w[3:4])
    xp_ref[0:8, :] = u[lb - 8:lb, :]
    conv_ref[0] = u[lb - 3:lb, :]
    act = _silu(conv)

    g = gates_ref[...]
    g_t = g.T
    neg_a_row = -jnp.exp(prow_ref[0:1, :])
    dt_row = prow_ref[1:2, :]
    neg_a_col = -jnp.exp(pcol_ref[0:HEADS, 0:1])
    dt_col = pcol_ref[0:HEADS, 1:2]
    la_cols = neg_a_row * _softplus(g + dt_row)
    beta_cols = jax.nn.sigmoid(g)
    la_rows = neg_a_col * _softplus(g_t[0:HEADS, :] + dt_col)

    tri = tri_ref[...]
    la_hi, la_lo = _split(la_cols)
    g_cols = (jnp.dot(tri, la_hi, preferred_element_type=F32)
              + jnp.dot(tri, la_lo, preferred_element_type=F32))
    lr_hi, lr_lo = _split(jnp.concatenate([la_rows, jnp.zeros_like(la_rows)], axis=0))
    nt = lambda x, y: lax.dot_general(x, y, (((1,), (1,)), ((), ())), preferred_element_type=F32)
    g_rows = nt(lr_hi, tri) + nt(lr_lo, tri)
    lane = lax.broadcasted_iota(jnp.int32, (lb, 128), 1)
    x_hi, x_lo = _split(jnp.where(lane < HEADS, g_cols, beta_cols))
    spread = (jnp.dot(x_hi, spread_ref[...], preferred_element_type=F32)
              + jnp.dot(x_lo, spread_ref[...], preferred_element_type=F32))
    g_wide = spread[:, 0:HEADS * DK]
    beta_wide = spread[:, HEADS * DK:2 * HEADS * DK]
    eg_wide = jnp.exp(g_wide)

    ii, jj = _chunk_masks(c)
    causal = ii >= jj
    strict = ii > jj
    dm = (ii - jj).astype(F32)
    pos_col = lax.broadcasted_iota(jnp.int32, (c, 1), 0).astype(F32)

    ggain = ggain_ref[...]
    rgain = rgain_ref[...]
    base = HEADS * DK
    s_delta = [sd_ref[0, h] for h in range(HEADS)]
    s_ret = [sr_ref[0, h] for h in range(HEADS)]

    items = [(ci, h) for ci in range(n_sub) for h in range(HEADS)]

    rng = lambda ci: (ci * c, (ci + 1) * c)
    pre = {it: {} for it in items}
    pre_b = {it: {} for it in items}

    q_nrm, k_nrm = [], []
    for h in range(HEADS):
        q_raw = act[:, h * DK:(h + 1) * DK]
        k_raw = act[:, (HEADS + h) * DK:(HEADS + h + 1) * DK]
        q_nrm.append(_l2(q_raw) * (DK ** -0.5))
        k_nrm.append(_l2(k_raw))

    def gdn_stage(ci, h):
        d = pre[ci, h]
        r0, r1 = rng(ci)
        g_col = g_wide[r0:r1, h * DK:(h + 1) * DK]
        g_row = g_rows[h:h + 1, r0:r1]
        g_last = g_col[c - 1:c, :]
        beta = beta_wide[r0:r1, h * DK:(h + 1) * DK]
        eg = eg_wide[r0:r1, h * DK:(h + 1) * DK]
        dec_causal = jnp.exp(jnp.where(causal, g_col[:, 0:c] - g_row, NEG))
        dec_strict = jnp.where(strict, dec_causal, 0.0)
        q = q_nrm[h][r0:r1]
        k = k_nrm[h][r0:r1]
        yield
        v = act[r0:r1, (2 * HEADS + h) * DK:(2 * HEADS + h + 1) * DK]
        kb = k * beta
        prod = _mm_nt(jnp.concatenate([q, kb], axis=0), k)
        d.update(rhs=jnp.concatenate([v * beta, kb * eg], axis=1), qd=q * eg,
                 kd=k * jnp.exp(g_last - g_col), gl=jnp.exp(g_last))
        yield
        d.update(n=prod[c:2 * c] * dec_strict, qk=prod[0:c] * dec_causal)

    _interleave(gdn_stage(ci, h) for ci, h in items)
    t_invs = _tri_inv_many([pre[it]['n'] for it in items])
    for it, t_inv in zip(items, t_invs):
        pre[it]['sol'] = _mm3(t_inv, pre[it]['rhs'])

    def ret_stage(ci, h):
        d = pre_b[ci, h]
        r0, r1 = rng(ci)
        cosf = cos_ref[r0:r1, :]
        sinf = sin_ref[r0:r1, :]
        qb = rest_ref[r0:r1, base + h * DK:base + (h + 1) * DK]
        kb2 = rest_ref[r0:r1, 2 * base + h * DK:2 * base + (h + 1) * DK]
        q_rot = pltpu.roll(qb, DK // 2, 1)
        k_rot = pltpu.roll(kb2, DK // 2, 1)
        yield
        qr = qb * cosf + q_rot * sinf
        kr = (kb2 * cosf + k_rot * sinf) * (DK ** -0.5)
        lg = LOG_GAMMA[h]
        dec = jnp.exp(jnp.where(causal, dm * lg, NEG))
        qk = _mm_nt(qr, kr)
        d['qd'] = qr * jnp.exp((pos_col + 1.0) * lg)
        kd = kr * jnp.exp((float(c - 1) - pos_col) * lg)
        yield
        vb = rest_ref[r0:r1, 3 * base + h * DK:3 * base + (h + 1) * DK]
        d['intra'] = _mm(qk * dec, vb)
        d['kv'] = _mm_tn(kd, vb)

    _interleave(ret_stage(ci, h) for ci, h in items)

    for ci, h in items:
        pre_b[ci, h]['s_in'] = s_ret[h]
        s_ret[h] = s_ret[h] * math.exp(c * LOG_GAMMA[h]) + pre_b[ci, h]['kv']

    def ret_out_stage(ci, h):
        d = pre_b[ci, h]
        r0, r1 = rng(ci)
        o = _mm(d['qd'], d['s_in']) + d['intra']
        yield
        ms_ = jnp.mean(o * o, axis=-1, keepdims=True)
        yield
        gb = rest_ref[r0:r1, 4 * base + h * DK:4 * base + (h + 1) * DK]
        mixed_ref[r0:r1, base + h * DK:base + (h + 1) * DK] = o * lax.rsqrt(ms_ + EPS) * rgain * _silu(gb)

    _interleave(ret_out_stage(ci, h) for ci, h in items)

    def delta_step(ci, h):
        d = pre[ci, h]
        r = _mm(jnp.concatenate([d['sol'][:, A_DV:2 * A_DV], d['qd']], axis=0), s_delta[h])
        yield
        v_new = d['sol'][:, 0:A_DV] - r[0:c]
        d['o'] = r[c:2 * c] + _mm(d['qk'], v_new)
        s_delta[h] = s_delta[h] * d['gl'] + _mm_tn(d['kd'], v_new)

    for ci in range(n_sub):
        _interleave(delta_step(ci, h) for h in range(HEADS))

    def gdn_out_stage(ci, h):
        o = pre[ci, h]['o']
        r0, r1 = rng(ci)
        ms_ = jnp.mean(o * o, axis=-1, keepdims=True)
        yield
        ga = rest_ref[r0:r1, h * DK:(h + 1) * DK]
        mixed_ref[r0:r1, h * A_DV:(h + 1) * A_DV] = o * lax.rsqrt(ms_ + EPS) * ggain * _silu(ga)

    _interleave(gdn_out_stage(ci, h) for ci, h in items)

    for h in range(HEADS):
        sd_ref[0, h] = s_delta[h]
        sr_ref[0, h] = s_ret[h]


def _rope_tables(pos):
    half = DK // 2
    inv = ROPE_BASE ** (-jnp.arange(half, dtype=F32) / half)
    ang = pos.astype(F32)[:, None] * inv[None, :]
    cos, sin = jnp.cos(ang), jnp.sin(ang)
    return jnp.concatenate([cos, cos], axis=-1), jnp.concatenate([-sin, sin], axis=-1)


def _chunk_tri(lb):
    i = jnp.arange(lb)[:, None]
    j = jnp.arange(lb)[None, :]
    return ((i >= j) & (i // CHUNK == j // CHUNK)).astype(BF16)


def _head_spread(groups):
    n = groups * HEADS
    src = jnp.arange(128)[:, None]
    dst = jnp.arange(n * 128)[None, :] // 128
    return (src == dst).astype(BF16)


def _gate_params(a_log, dt_bias):
    prow = jnp.zeros((8, 128), F32).at[0, 0:HEADS].set(a_log).at[1, 0:HEADS].set(dt_bias)
    pcol = jnp.zeros((8, 128), F32).at[0:HEADS, 0].set(a_log).at[0:HEADS, 1].set(dt_bias)
    return prow, pcol


def _mix_ab_prompt(qkv, rest, gates, batch, seq, w_conv, a_log, dt_bias, gdn_gain, ret_gain):
    lb = AB_SUB * CHUNK
    assert seq % lb == 0
    nc = seq // lb
    cosf, sinf = _rope_tables(jnp.arange(seq))
    prow, pcol = _gate_params(a_log, dt_bias)
    tri = _chunk_tri(lb)
    spread = _head_spread(2)
    tok = lambda b, c: (b * nc + c, 0)
    const = lambda b, c: (0, 0)
    mixed, conv, sd, sr = pl.pallas_call(
        functools.partial(_mix_ab_kernel, n_sub=AB_SUB),
        grid=(batch, nc),
        in_specs=[pl.BlockSpec((lb, A_CONV_CH), tok),
                  pl.BlockSpec((lb, rest.shape[1]), tok),
                  pl.BlockSpec((lb, 128), tok),
                  pl.BlockSpec((lb, DK), lambda b, c: (c, 0)),
                  pl.BlockSpec((lb, DK), lambda b, c: (c, 0)),
                  pl.BlockSpec((CONV_W, A_CONV_CH), const),
                  pl.BlockSpec((8, 128), const),
                  pl.BlockSpec((8, 128), const),
                  pl.BlockSpec(tri.shape, const),
                  pl.BlockSpec(spread.shape, const),
                  pl.BlockSpec((1, A_DV), const),
                  pl.BlockSpec((1, DK), const)],
        out_specs=[pl.BlockSpec((lb, 2 * HEADS * DK), tok),
                   pl.BlockSpec((1, CONV_W - 1, A_CONV_CH), lambda b, c: (b, 0, 0)),
                   pl.BlockSpec((1, HEADS, DK, A_DV), lambda b, c: (b, 0, 0, 0)),
                   pl.BlockSpec((1, HEADS, DK, DK), lambda b, c: (b, 0, 0, 0))],
        out_shape=[jax.ShapeDtypeStruct((batch * seq, 2 * HEADS * DK), F32),
                   jax.ShapeDtypeStruct((batch, CONV_W - 1, A_CONV_CH), F32),
                   jax.ShapeDtypeStruct((batch, HEADS, DK, A_DV), F32),
                   jax.ShapeDtypeStruct((batch, HEADS, DK, DK), F32)],
        scratch_shapes=[pltpu.VMEM((lb + 8, A_CONV_CH), F32)],
        compiler_params=_cparams("parallel", "arbitrary"),
        name="mix_ab_prompt",
    )(qkv, rest, gates, cosf, sinf, w_conv, prow, pcol, tri, spread,
      gdn_gain.reshape(1, A_DV), ret_gain.reshape(1, DK))
    return mixed, conv, sd, sr


def _mix_c_kernel(qk_ref, v_ref, opre_ref, gates_ref, brow_ref, bcol_ref, tri_ref, spread_ref, gain_ref,
                  h_ref, cm_ref, nv_ref, m_ref, *, n_sub):
    c = CHUNK
    lb = n_sub * c
    step = pl.program_id(1)

    @pl.when(step == 0)
    def _():
        cm_ref[...] = jnp.zeros_like(cm_ref)
        nv_ref[...] = jnp.zeros_like(nv_ref)
        m_ref[...] = jnp.zeros_like(m_ref)

    items = [(ci, h) for ci in range(n_sub) for h in range(HEADS)]
    rng = lambda ci: (ci * c, (ci + 1) * c)
    pre = {it: {} for it in items}

    for ci, h in items:
        r0, r1 = rng(ci)
        q = qk_ref[r0:r1, h * DK:(h + 1) * DK]
        k = qk_ref[r0:r1, (HEADS + h) * DK:(HEADS + h + 1) * DK] * (DK ** -0.5)
        pre[ci, h].update(q=q, k=k, qk=_mm_nt(q, k))

    g = gates_ref[...] + brow_ref[0:1, :]
    g_t = gates_ref[...].T[0:2 * HEADS, :] + bcol_ref[0:2 * HEADS, 0:1]
    i_rows = g_t[0:HEADS, :]
    logf_rows = _log_sigmoid(g_t[HEADS:2 * HEADS, :])
    tri = tri_ref[...]
    lf_hi, lf_lo = _split(_log_sigmoid(g))
    b_cols = (jnp.dot(tri, lf_hi, preferred_element_type=F32)
              + jnp.dot(tri, lf_lo, preferred_element_type=F32))
    lr_hi, lr_lo = _split(jnp.concatenate([logf_rows, jnp.zeros_like(logf_rows)], axis=0))
    nt = lambda x, y: lax.dot_general(x, y, (((1,), (1,)), ((), ())), preferred_element_type=F32)
    b_rows = nt(lr_hi, tri) + nt(lr_lo, tri)
    lane = lax.broadcasted_iota(jnp.int32, (lb, 128), 1)
    x_hi, x_lo = _split(jnp.where(lane < HEADS, g, b_cols))
    spread = (jnp.dot(x_hi, spread_ref[...], preferred_element_type=F32)
              + jnp.dot(x_lo, spread_ref[...], preferred_element_type=F32))
    i_wide = spread[:, 0:HEADS * 128]
    b_wide = spread[:, HEADS * 128:2 * HEADS * 128]

    ii, jj = _chunk_masks(c)
    causal = ii >= jj
    gain = gain_ref[...]
    wide2 = lambda x: jnp.concatenate([x, x], axis=-1)

    def gates_stage(ci, h):
        d = pre[ci, h]
        r0, r1 = rng(ci)
        b_col = b_wide[r0:r1, h * 128:(h + 1) * 128]
        i_col = i_wide[r0:r1, h * 128:(h + 1) * 128]
        b_last = b_col[c - 1:c, :]
        d_log = jnp.where(causal, b_col[:, 0:c] - b_rows[h:h + 1, r0:r1] + i_rows[h:h + 1, r0:r1], NEG)
        d.update(b_col=b_col, b_last=b_last, d_log=d_log, k_log=b_last - b_col + i_col)
        yield
        d['d_max'] = jnp.max(d_log, axis=1, keepdims=True)

    _interleave(gates_stage(ci, h) for ci, h in items)

    ms = [m_ref[0, h:h + 1, :] for h in range(HEADS)]
    for ci, h in items:
        d = pre[ci, h]
        inter = d['b_col'] + ms[h]
        m_row = jnp.maximum(inter, d['d_max'])
        m_new = m_row[c - 1:c, :]
        d.update(inter=inter, m_row=m_row, m_old=ms[h], m_new=m_new)
        ms[h] = m_new

    def weights_stage(ci, h):
        d = pre[ci, h]
        r0, r1 = rng(ci)
        d['w_inter'] = jnp.exp(d['inter'] - d['m_row'])
        d['f_state'] = jnp.exp(d['b_last'] + d['m_old'] - d['m_new'])
        yield
        d['w_intra'] = jnp.exp(d['d_log'] - d['m_row'][:, 0:c]) * d['qk']
        d['kw'] = d['k'] * jnp.exp(d['k_log'] - d['m_new'])
        yield
        v = v_ref[r0:r1, h * C_DV:(h + 1) * C_DV]
        d['intra'] = _mm(d['w_intra'], v)
        d['kv'] = _mm_tn(d['kw'], v)
        yield
        d['sum_intra'] = jnp.sum(d['w_intra'], axis=1, keepdims=True)
        d['sum_kw'] = jnp.sum(d['kw'], axis=0, keepdims=True)
        d['inv_floor'] = jnp.exp(-d['m_row'])

    _interleave(weights_stage(ci, h) for ci, h in items)

    cms = [cm_ref[0, h] for h in range(HEADS)]
    nvs = [nv_ref[0, h:h + 1, :] for h in range(HEADS)]
    for ci, h in items:
        d = pre[ci, h]
        d['cm_in'], d['nv_in'] = cms[h], nvs[h]
        cms[h] = cms[h] * wide2(d['f_state']) + d['kv']
        nvs[h] = nvs[h] * d['f_state'] + d['sum_kw']

    def output_stage(ci, h):
        d = pre[ci, h]
        r0, r1 = rng(ci)
        qc = _mm(d['q'], d['cm_in'])
        qn = jnp.sum(d['q'] * d['nv_in'], axis=1, keepdims=True)
        yield
        num = wide2(d['w_inter']) * qc + d['intra']
        den = d['w_inter'] * qn + d['sum_intra']
        hh = num / wide2(jnp.maximum(jnp.abs(den), d['inv_floor']))
        yield
        ms_ = jnp.mean(hh * hh, axis=-1, keepdims=True)
        yield
        op = opre_ref[r0:r1, h * C_DV:(h + 1) * C_DV]
        h_ref[r0:r1, h * C_DV:(h + 1) * C_DV] = hh * lax.rsqrt(ms_ + EPS) * gain * jax.nn.sigmoid(op)

    _interleave(output_stage(ci, h) for ci, h in items)

    for h in range(HEADS):
        cm_ref[0, h] = cms[h]
        nv_ref[0, h:h + 1, :] = nvs[h]
        m_ref[0, h:h + 1, :] = ms[h]


def _bias_params(b_gate):
    brow = jnp.zeros((8, 128), F32).at[0, 0:2 * HEADS].set(b_gate)
    bcol = jnp.zeros((8, 128), F32).at[0:2 * HEADS, 0].set(b_gate)
    return brow, bcol


def _mix_c_prompt(qk, v, opre, gates, batch, seq, b_gate, gain):
    lb = C_SUB * CHUNK
    assert seq % lb == 0
    nc = seq // lb
    brow, bcol = _bias_params(b_gate)
    tri = _chunk_tri(lb)
    spread = _head_spread(2)
    tok = lambda b, c: (b * nc + c, 0)
    const = lambda b, c: (0, 0)
    return pl.pallas_call(
        functools.partial(_mix_c_kernel, n_sub=C_SUB),
        grid=(batch, nc),
        in_specs=[pl.BlockSpec((lb, 2 * HEADS * DK), tok),
                  pl.BlockSpec((lb, HEADS * C_DV), tok),
                  pl.BlockSpec((lb, HEADS * C_DV), tok),
                  pl.BlockSpec((lb, 128), tok),
                  pl.BlockSpec((8, 128), const),
                  pl.BlockSpec((8, 128), const),
                  pl.BlockSpec(tri.shape, const),
                  pl.BlockSpec(spread.shape, const),
                  pl.BlockSpec((1, C_DV), const)],
        out_specs=[pl.BlockSpec((lb, HEADS * C_DV), tok),
                   pl.BlockSpec((1, HEADS, DK, C_DV), lambda b, c: (b, 0, 0, 0)),
                   pl.BlockSpec((1, HEADS, DK), lambda b, c: (b, 0, 0)),
                   pl.BlockSpec((1, HEADS, 128), lambda b, c: (b, 0, 0))],
        out_shape=[jax.ShapeDtypeStruct((batch * seq, HEADS * C_DV), F32),
                   jax.ShapeDtypeStruct((batch, HEADS, DK, C_DV), F32),
                   jax.ShapeDtypeStruct((batch, HEADS, DK), F32),
                   jax.ShapeDtypeStruct((batch, HEADS, 128), F32)],
        compiler_params=_cparams("parallel", "arbitrary"),
        name="mix_c_prompt",
    )(qk, v, opre, gates, brow, bcol, tri, spread, gain.reshape(1, C_DV))


def _row_select(rows, t, new, old):
    return jnp.where(rows == t, new, old)


def _step_ab_kernel(qkv_ref, rest_ref, gates_ref, cos_ref, sin_ref, wconv_ref, prow_ref, ggain_ref, rgain_ref,
                    buf_ref, sd_ref, sr_ref, mixed_ref, nbuf_ref, nsd_ref, nsr_ref):
    tb = qkv_ref.shape[0]
    u = qkv_ref[...]
    w = wconv_ref[...]
    b0 = buf_ref[:, 0:A_CONV_CH]
    b1 = buf_ref[:, A_CONV_CH:2 * A_CONV_CH]
    b2 = buf_ref[:, 2 * A_CONV_CH:3 * A_CONV_CH]
    conv = b0 * w[0:1] + b1 * w[1:2] + b2 * w[2:3] + u * w[3:4]
    nbuf_ref[:, 0:A_CONV_CH] = b1
    nbuf_ref[:, A_CONV_CH:2 * A_CONV_CH] = b2
    nbuf_ref[:, 2 * A_CONV_CH:3 * A_CONV_CH] = u
    act = _silu(conv)
    g = gates_ref[...]
    eg_all = jnp.exp(-jnp.exp(prow_ref[0:1, :]) * _softplus(g + prow_ref[1:2, :]))
    beta_all = jax.nn.sigmoid(g)
    cosf = cos_ref[...]
    sinf = sin_ref[...]
    base = HEADS * DK
    rows = lax.broadcasted_iota(jnp.int32, (tb, DK), 0)
    items = [(t, h) for t in range(tb) for h in range(HEADS)]

    qs = [_l2(act[:, h * DK:(h + 1) * DK]) * (DK ** -0.5) for h in range(HEADS)]
    ks = [_l2(act[:, (HEADS + h) * DK:(HEADS + h + 1) * DK]) for h in range(HEADS)]
    k_s = [jnp.zeros((tb, A_DV), F32) for _ in range(HEADS)]
    for t, h in items:
        k_s[h] = _row_select(rows, t, _mm(ks[h], sd_ref[t, h]), k_s[h])
    v_new = []
    for h in range(HEADS):
        v = act[:, (2 * HEADS + h) * DK:(2 * HEADS + h + 1) * DK]
        v_new.append(beta_all[:, HEADS + h:HEADS + h + 1] * (v - eg_all[:, h:h + 1] * k_s[h]))
    for t, h in items:
        nsd_ref[t, h] = (sd_ref[t, h] * eg_all[t:t + 1, h:h + 1]
                         + _mm3_tn(jnp.where(rows == t, ks[h], 0.0), v_new[h]))
    o_a = [jnp.zeros((tb, A_DV), F32) for _ in range(HEADS)]
    for t, h in items:
        o_a[h] = _row_select(rows, t, _mm(qs[h], nsd_ref[t, h]), o_a[h])
    for h in range(HEADS):
        ga = rest_ref[:, h * DK:(h + 1) * DK]
        mixed_ref[:, h * DK:(h + 1) * DK] = _rms(o_a[h], ggain_ref[...]) * _silu(ga)

    qrs, krs = [], []
    for h in range(HEADS):
        qb = rest_ref[:, base + h * DK:base + (h + 1) * DK]
        kb = rest_ref[:, 2 * base + h * DK:2 * base + (h + 1) * DK]
        qrs.append(qb * cosf + pltpu.roll(qb, DK // 2, 1) * sinf)
        krs.append((kb * cosf + pltpu.roll(kb, DK // 2, 1) * sinf) * (DK ** -0.5))
    for t, h in items:
        vb = rest_ref[:, 3 * base + h * DK:3 * base + (h + 1) * DK]
        nsr_ref[t, h] = (sr_ref[t, h] * math.exp(LOG_GAMMA[h])
                         + _mm3_tn(jnp.where(rows == t, krs[h], 0.0), vb))
    o_b = [jnp.zeros((tb, DK), F32) for _ in range(HEADS)]
    for t, h in items:
        o_b[h] = _row_select(rows, t, _mm(qrs[h], nsr_ref[t, h]), o_b[h])
    for h in range(HEADS):
        gb = rest_ref[:, 4 * base + h * DK:4 * base + (h + 1) * DK]
        mixed_ref[:, base + h * DK:base + (h + 1) * DK] = _rms(o_b[h], rgain_ref[...]) * _silu(gb)


def _mix_ab_sample(qkv, rest, gates, conv_buf, s_delta, s_ret, w_conv, a_log, dt_bias, gdn_gain, ret_gain):
    nb = qkv.shape[0]
    tb = SAMPLE_TB
    assert nb % tb == 0
    cosf, sinf = _rope_tables(PAST_LEN + jnp.arange(1))
    prow, _ = _gate_params(a_log, dt_bias)
    tok = lambda b: (b, 0)
    tok4 = lambda b: (b, 0, 0, 0)
    const = lambda b: (0, 0)
    n_buf = (CONV_W - 1) * A_CONV_CH
    mixed, nbuf, nsd, nsr = pl.pallas_call(
        _step_ab_kernel,
        grid=(nb // tb,),
        in_specs=[pl.BlockSpec((tb, A_CONV_CH), tok),
                  pl.BlockSpec((tb, rest.shape[1]), tok),
                  pl.BlockSpec((tb, 128), tok),
                  pl.BlockSpec((1, DK), const),
                  pl.BlockSpec((1, DK), const),
                  pl.BlockSpec((CONV_W, A_CONV_CH), const),
                  pl.BlockSpec((8, 128), const),
                  pl.BlockSpec((1, A_DV), const),
                  pl.BlockSpec((1, DK), const),
                  pl.BlockSpec((tb, n_buf), tok),
                  pl.BlockSpec((tb, HEADS, DK, A_DV), tok4),
                  pl.BlockSpec((tb, HEADS, DK, DK), tok4)],
        out_specs=[pl.BlockSpec((tb, 2 * HEADS * DK), tok),
                   pl.BlockSpec((tb, n_buf), tok),
                   pl.BlockSpec((tb, HEADS, DK, A_DV), tok4),
                   pl.BlockSpec((tb, HEADS, DK, DK), tok4)],
        out_shape=[jax.ShapeDtypeStruct((nb, 2 * HEADS * DK), F32),
                   jax.ShapeDtypeStruct((nb, n_buf), F32),
                   jax.ShapeDtypeStruct((nb, HEADS, DK, A_DV), F32),
                   jax.ShapeDtypeStruct((nb, HEADS, DK, DK), F32)],
        compiler_params=_cparams("parallel"),
        name="mix_ab_sample",
    )(qkv, rest, gates, cosf, sinf, w_conv, prow, gdn_gain.reshape(1, A_DV), ret_gain.reshape(1, DK),
      conv_buf.reshape(nb, n_buf), s_delta, s_ret)
    return mixed, nbuf.reshape(nb, CONV_W - 1, A_CONV_CH), nsd, nsr


def _step_c_kernel(qk_ref, v_ref, opre_ref, gates_ref, brow_ref, gain_ref, cm_ref, nv_ref, m_ref,
                   h_ref, ncm_ref, nnv_ref, nm_ref):
    tb = qk_ref.shape[0]
    g = gates_ref[...] + brow_ref[0:1, :]
    logf = _log_sigmoid(g)
    rows = lax.broadcasted_iota(jnp.int32, (tb, DK), 0)
    items = [(t, h) for t in range(tb) for h in range(HEADS)]
    qs, kws, fs, ms_new, nvs_new = [], [], [], [], []
    for h in range(HEADS):
        i_pre = g[:, h:h + 1]
        q = qk_ref[:, h * DK:(h + 1) * DK]
        k = qk_ref[:, (HEADS + h) * DK:(HEADS + h + 1) * DK] * (DK ** -0.5)
        inter = logf[:, HEADS + h:HEADS + h + 1] + m_ref[:, h:h + 1]
        m_new = jnp.maximum(inter, i_pre)
        f_state = jnp.exp(inter - m_new)
        kw = k * jnp.exp(i_pre - m_new)
        nv_new = nv_ref[:, h * DK:(h + 1) * DK] * f_state + kw
        nnv_ref[:, h * DK:(h + 1) * DK] = nv_new
        nm_ref[:, h:h + 1] = m_new
        qs.append(q)
        kws.append(kw)
        fs.append(f_state)
        ms_new.append(m_new)
        nvs_new.append(nv_new)
    for t, h in items:
        v = v_ref[:, h * C_DV:(h + 1) * C_DV]
        ncm_ref[t, h] = (cm_ref[t, h] * fs[h][t:t + 1, :]
                         + _mm3_tn(jnp.where(rows == t, kws[h], 0.0), v))
    rows_v = lax.broadcasted_iota(jnp.int32, (tb, C_DV), 0)
    nums = [jnp.zeros((tb, C_DV), F32) for _ in range(HEADS)]
    for t, h in items:
        nums[h] = _row_select(rows_v, t, _mm(qs[h], ncm_ref[t, h]), nums[h])
    for h in range(HEADS):
        den = jnp.sum(qs[h] * nvs_new[h], axis=1, keepdims=True)
        hh = nums[h] / jnp.maximum(jnp.abs(den), jnp.exp(-ms_new[h]))
        op = opre_ref[:, h * C_DV:(h + 1) * C_DV]
        h_ref[:, h * C_DV:(h + 1) * C_DV] = _rms(hh, gain_ref[...]) * jax.nn.sigmoid(op)


def _mix_c_sample(qk, v, opre, gates, s_c, s_n, s_m, b_gate, gain):
    nb = qk.shape[0]
    tb = SAMPLE_TB
    assert nb % tb == 0
    brow, _ = _bias_params(b_gate)
    tok = lambda b: (b, 0)
    tok4 = lambda b: (b, 0, 0, 0)
    const = lambda b: (0, 0)
    hh, ncm, nnv, nm = pl.pallas_call(
        _step_c_kernel,
        grid=(nb // tb,),
        in_specs=[pl.BlockSpec((tb, 2 * HEADS * DK), tok),
                  pl.BlockSpec((tb, HEADS * C_DV), tok),
                  pl.BlockSpec((tb, HEADS * C_DV), tok),
                  pl.BlockSpec((tb, 128), tok),
                  pl.BlockSpec((8, 128), const),
                  pl.BlockSpec((1, C_DV), const),
                  pl.BlockSpec((tb, HEADS, DK, C_DV), tok4),
                  pl.BlockSpec((tb, HEADS * DK), tok),
                  pl.BlockSpec((tb, HEADS), tok)],
        out_specs=[pl.BlockSpec((tb, HEADS * C_DV), tok),
                   pl.BlockSpec((tb, HEADS, DK, C_DV), tok4),
                   pl.BlockSpec((tb, HEADS * DK), tok),
                   pl.BlockSpec((tb, HEADS), tok)],
        out_shape=[jax.ShapeDtypeStruct((nb, HEADS * C_DV), F32),
                   jax.ShapeDtypeStruct((nb, HEADS, DK, C_DV), F32),
                   jax.ShapeDtypeStruct((nb, HEADS * DK), F32),
                   jax.ShapeDtypeStruct((nb, HEADS), F32)],
        compiler_params=_cparams("parallel"),
        name="mix_c_sample",
    )(qk, v, opre, gates, brow, gain.reshape(1, C_DV), s_c, s_n.reshape(nb, HEADS * DK), s_m)
    return hh, ncm, nnv.reshape(nb, HEADS, DK), nm


A_SPLITS = (A_CONV_CH, 5 * HEADS * DK, 128)
C_SPLITS = (2 * HEADS * DK, HEADS * C_DV, HEADS * C_DV, 128)


def _prep_w_in_a(w):
    n_gate = 2 * HEADS
    qkv = w[:, :A_CONV_CH]
    gates = w[:, A_CONV_CH:A_CONV_CH + n_gate]
    rest = w[:, A_CONV_CH + n_gate:]
    pad = jnp.zeros((D_MODEL, 128 - n_gate), w.dtype)
    return jnp.concatenate([qkv, rest, gates, pad], axis=1).astype(BF16)


def _prep_w_in_c(w):
    n_gate = 2 * HEADS
    main = w[:, :2 * HEADS * DK + 2 * HEADS * C_DV]
    gates = w[:, 2 * HEADS * DK + 2 * HEADS * C_DV:]
    pad = jnp.zeros((D_MODEL, 128 - n_gate), w.dtype)
    return jnp.concatenate([main, gates, pad], axis=1).astype(BF16)


def _run_group(x, prompt, states, p, tm):
    batch, seq, _ = x.shape
    h = x.reshape(batch * seq, D_MODEL)
    qkv, rest, gates = _norm_proj(h, p['norm_mix_a'][0], p['w_in_a'], A_SPLITS, tm)
    if prompt:
        mixed, conv, sd, sr = _mix_ab_prompt(qkv, rest, gates, batch, seq, p['w_conv_a'][0], p['a_log'][0],
                                             p['dt_bias'][0], p['gdn_gain'][0], p['ret_gain'][0])
    else:
        mixed, conv, sd, sr = _mix_ab_sample(qkv, rest, gates, states['conv'][0], states['delta'][0],
                                             states['ret'][0], p['w_conv_a'][0], p['a_log'][0], p['dt_bias'][0],
                                             p['gdn_gain'][0], p['ret_gain'][0])
    h = _out_mlp(h, [mixed], p['w_out_a'], p['norm_ffn'][0], p['w_up'][0], p['w_down'][0], None, tm)
    qk, v, opre, gates_c = _norm_proj(h, p['norm_mix_c'][0], p['w_in_c'], C_SPLITS, tm)
    if prompt:
        hm, cm, nv, m = _mix_c_prompt(qk, v, opre, gates_c, batch, seq, p['b_gate_c'][0], p['mlstm_gain'][0])
        m = m[:, :, 0]
    else:
        hm, cm, nv, m = _mix_c_sample(qk, v, opre, gates_c, states['c'][0], states['n'][0], states['m'][0],
                                      p['b_gate_c'][0], p['mlstm_gain'][0])
    y = _out_mlp(h, [hm], p['w_out_c'], p['norm_ffn'][1], p['w_up'][1], p['w_down'][1], p['final_gain'], tm)
    return (y.reshape(batch, seq, D_MODEL), conv[None], sd[None], sr[None], cm[None], nv[None], m[None])


def kernel(x_prompt, x_sample, state_conv_a, state_delta_a, state_ret_b, state_mlstm_C, state_mlstm_n, state_mlstm_m,
           norm_mix_a, w_in_a, w_conv_a, a_log, dt_bias, gdn_gain, ret_gain, w_out_a,
           norm_mix_c, w_in_c, b_gate_c, mlstm_gain, w_out_c, norm_ffn, w_up, w_down, final_gain):
    p = dict(norm_mix_a=norm_mix_a, w_in_a=_prep_w_in_a(w_in_a[0]), w_conv_a=w_conv_a, a_log=a_log, dt_bias=dt_bias,
             gdn_gain=gdn_gain, ret_gain=ret_gain, w_out_a=w_out_a[0].astype(BF16), norm_mix_c=norm_mix_c,
             w_in_c=_prep_w_in_c(w_in_c[0]), b_gate_c=b_gate_c, mlstm_gain=mlstm_gain,
             w_out_c=w_out_c[0].astype(BF16), norm_ffn=norm_ffn, w_up=w_up.astype(BF16),
             w_down=w_down.astype(BF16), final_gain=final_gain)
    outs_p = _run_group(x_prompt, True, None, p, PROMPT_TM)
    states = dict(conv=state_conv_a, delta=state_delta_a, ret=state_ret_b, c=state_mlstm_C, n=state_mlstm_n,
                  m=state_mlstm_m)
    outs_s = _run_group(x_sample, False, states, p, x_sample.shape[0] * x_sample.shape[1])
    return (outs_p[0], outs_s[0]) + outs_p[1:] + outs_s[1:]
```

```python
import functools
import math

import jax
import jax.numpy as jnp
from jax import lax
from jax.experimental import pallas as pl
from jax.experimental.pallas import tpu as pltpu

F32 = jnp.float32
BF16 = jnp.bfloat16

D_MODEL = 1024
D_FF = 4 * D_MODEL
CHUNK = 64
EPS = 1e-6
NEG = -1e30
HEADS = 4
DK = 128
A_DV = 128
C_DV = 256
CONV_W = 4
A_CONV_CH = 3 * HEADS * DK
ROPE_BASE = 10000.0
PAST_LEN = 16384
LOG_GAMMA = tuple(math.log1p(-(2.0 ** (-5.0 - h))) for h in range(HEADS))

VMEM_LIMIT_BYTES = 56 * 1024 * 1024
PROMPT_TM = 512
AB_SUB = 4
C_SUB = 4
SAMPLE_TB = 8


def _cparams(*sem):
    return pltpu.CompilerParams(dimension_semantics=sem, vmem_limit_bytes=VMEM_LIMIT_BYTES)


def _mm(a, b):
    return jnp.dot(a.astype(BF16), b.astype(BF16), preferred_element_type=F32)


def _mm_nt(a, b):
    return lax.dot_general(a.astype(BF16), b.astype(BF16), (((1,), (1,)), ((), ())), preferred_element_type=F32)


def _mm_tn(a, b):
    return lax.dot_general(a.astype(BF16), b.astype(BF16), (((0,), (0,)), ((), ())), preferred_element_type=F32)


def _split(a):
    hi = a.astype(BF16)
    lo = (a - hi.astype(F32)).astype(BF16)
    return hi, lo


def _mm3(a, b):
    ah, al = _split(a)
    bh, bl = _split(b)
    d = lambda x, y: jnp.dot(x, y, preferred_element_type=F32)
    return d(ah, bh) + (d(ah, bl) + d(al, bh))


def _mm3_tn(a, b):
    ah, al = _split(a)
    bh, bl = _split(b)
    d = lambda x, y: lax.dot_general(x, y, (((0,), (0,)), ((), ())), preferred_element_type=F32)
    return d(ah, bh) + (d(ah, bl) + d(al, bh))


def _softplus(x):
    return jnp.maximum(x, 0.0) + jnp.log1p(jnp.exp(-jnp.abs(x)))


def _log_sigmoid(x):
    return -_softplus(-x)


def _silu(x):
    return x * jax.nn.sigmoid(x)


def _rms(x, gain):
    return x * lax.rsqrt(jnp.mean(x * x, axis=-1, keepdims=True) + EPS) * gain


def _l2(t):
    return t * lax.rsqrt(jnp.sum(t * t, axis=-1, keepdims=True) + EPS)


def _interleave(gens):
    gens = list(gens)
    while gens:
        alive = []
        for gen in gens:
            try:
                next(gen)
                alive.append(gen)
            except StopIteration:
                pass
        gens = alive


def _tri_inv_many(ns):
    c = ns[0].shape[0]
    eye = (lax.broadcasted_iota(jnp.int32, (c, c), 0) == lax.broadcasted_iota(jnp.int32, (c, c), 1)).astype(F32)
    ts = [eye - n for n in ns]
    ps = list(ns)
    steps = int(math.log2(c)) - 1
    for _ in range(steps):
        ps = [_mm3(p, p) for p in ps]
        ts = [t + _mm3(t, p) for t, p in zip(ts, ps)]
    return ts


def _norm_proj_kernel(x_ref, g_ref, w_ref, *out_refs, splits):
    xn = _rms(x_ref[...], g_ref[...]).astype(BF16)
    off = 0
    for o_ref, n in zip(out_refs, splits):
        o_ref[...] = jnp.dot(xn, w_ref[:, off:off + n], preferred_element_type=F32)
        off += n


def _norm_proj(x, gain, w, splits, tm):
    t = x.shape[0]
    n = w.shape[1]
    assert sum(splits) == n and t % tm == 0
    return pl.pallas_call(
        functools.partial(_norm_proj_kernel, splits=splits),
        grid=(t // tm,),
        in_specs=[pl.BlockSpec((tm, D_MODEL), lambda i: (i, 0)),
                  pl.BlockSpec((1, D_MODEL), lambda i: (0, 0)),
                  pl.BlockSpec((D_MODEL, n), lambda i: (0, 0), pipeline_mode=pl.Buffered(1))],
        out_specs=[pl.BlockSpec((tm, s), lambda i: (i, 0)) for s in splits],
        out_shape=[jax.ShapeDtypeStruct((t, s), F32) for s in splits],
        compiler_params=_cparams("parallel"),
        name="norm_proj",
    )(x, gain.reshape(1, D_MODEL), w)


def _proj_a_prompt_kernel(x_ref, g_ref, w_ref, wconv_ref, cos_ref, sin_ref,
                          qkv_ref, rest_ref, gates_ref, conv_ref, xp_ref, raw_ref, *, tiles_per_seq):
    tm = x_ref.shape[0]
    hd = HEADS * DK
    xn = _rms(x_ref[...], g_ref[...]).astype(BF16)

    @pl.when(pl.program_id(0) % tiles_per_seq == 0)
    def _():
        xp_ref[0:8, :] = jnp.zeros((8, A_CONV_CH), F32)

    w = wconv_ref[...]
    base = A_CONV_CH

    def conv_part(part, slot, r0, r1):
        c0, c1 = part * hd, (part + 1) * hd
        raw = raw_ref[slot, r0:r1, :]
        xp_ref[8 + r0:8 + r1, c0:c1] = raw
        conv = (xp_ref[5 + r0:5 + r1, c0:c1] * w[0:1, c0:c1] + xp_ref[6 + r0:6 + r1, c0:c1] * w[1:2, c0:c1]
                + xp_ref[7 + r0:7 + r1, c0:c1] * w[2:3, c0:c1] + raw * w[3:4, c0:c1])
        if r1 == tm:
            rows = r1 - r0
            xp_ref[0:8, c0:c1] = raw[rows - 8:rows, :]
            conv_ref[0, :, c0:c1] = raw[rows - 3:rows, :]
        act = _silu(conv)
        if part == 2:
            qkv_ref[r0:r1, c0:c1] = act
        else:
            scale = DK ** -0.5 if part == 0 else 1.0
            for h in range(HEADS):
                qkv_ref[r0:r1, c0 + h * DK:c0 + (h + 1) * DK] = _l2(act[:, h * DK:(h + 1) * DK]) * scale

    def rest_part(part, slot, r0, r1):
        c0, c1 = part * hd, (part + 1) * hd
        raw = raw_ref[slot, r0:r1, :]
        if part in (0, 4):
            rest_ref[r0:r1, c0:c1] = _silu(raw)
        elif part == 3:
            rest_ref[r0:r1, c0:c1] = raw
        else:
            scale = 1.0 if part == 1 else DK ** -0.5
            for h in range(HEADS):
                t = raw[:, h * DK:(h + 1) * DK]
                rest_ref[r0:r1, c0 + h * DK:c0 + (h + 1) * DK] = (
                    (t * cos_ref[r0:r1, :] + pltpu.roll(t, DK // 2, 1) * sin_ref[r0:r1, :]) * scale)

    def gates_part(_, slot, r0, r1):
        gates_ref[r0:r1, :] = raw_ref[slot, r0:r1, 0:128]

    groups = [(0, hd, functools.partial(conv_part, 0)),
              (base + 3 * hd, hd, functools.partial(rest_part, 3)),
              (hd, hd, functools.partial(conv_part, 1)),
              (base, hd, functools.partial(rest_part, 0)),
              (2 * hd, hd, functools.partial(conv_part, 2)),
              (base + hd, hd, functools.partial(rest_part, 1)),
              (base + 5 * hd, 128, functools.partial(gates_part, 0)),
              (base + 2 * hd, hd, functools.partial(rest_part, 2)),
              (base + 4 * hd, hd, functools.partial(rest_part, 4))]

    def project(n):
        col0, width, _ = groups[n]
        for j in range(0, width, 256):
            wj = min(256, width - j)
            raw_ref[n % 2, :, j:j + wj] = jnp.dot(xn, w_ref[:, col0 + j:col0 + j + wj],
                                                  preferred_element_type=F32)
            yield

    def epilogue(n):
        for r0 in range(0, tm, 128):
            groups[n][2](n % 2, r0, min(r0 + 128, tm))
            yield

    _interleave([project(0)])
    for n in range(len(groups)):
        _interleave([epilogue(n)] + ([project(n + 1)] if n + 1 < len(groups) else []))


def _proj_a_prompt(x, gain, w, w_conv, batch, seq, tm):
    t = x.shape[0]
    assert t % tm == 0 and seq % tm == 0
    tiles_per_seq = seq // tm
    cosf, sinf = _rope_tables(jnp.arange(seq))
    row = lambda i: (i, 0)
    const = lambda i: (0, 0)
    pos = lambda i: (i % tiles_per_seq, 0)
    n_rest = 5 * HEADS * DK
    return pl.pallas_call(
        functools.partial(_proj_a_prompt_kernel, tiles_per_seq=tiles_per_seq),
        grid=(t // tm,),
        in_specs=[pl.BlockSpec((tm, D_MODEL), row),
                  pl.BlockSpec((1, D_MODEL), const),
                  pl.BlockSpec(w.shape, const, pipeline_mode=pl.Buffered(1)),
                  pl.BlockSpec((CONV_W, A_CONV_CH), const),
                  pl.BlockSpec((tm, DK), pos),
                  pl.BlockSpec((tm, DK), pos)],
        out_specs=[pl.BlockSpec((tm, A_CONV_CH), row),
                   pl.BlockSpec((tm, n_rest), row),
                   pl.BlockSpec((tm, 128), row),
                   pl.BlockSpec((1, CONV_W - 1, A_CONV_CH), lambda i: (i // tiles_per_seq, 0, 0))],
        out_shape=[jax.ShapeDtypeStruct((t, A_CONV_CH), F32),
                   jax.ShapeDtypeStruct((t, n_rest), F32),
                   jax.ShapeDtypeStruct((t, 128), F32),
                   jax.ShapeDtypeStruct((batch, CONV_W - 1, A_CONV_CH), F32)],
        scratch_shapes=[pltpu.VMEM((tm + 8, A_CONV_CH), F32), pltpu.VMEM((2, tm, HEADS * DK), F32)],
        compiler_params=_cparams("arbitrary"),
        name="proj_a_prompt",
    )(x, gain.reshape(1, D_MODEL), w, w_conv, cosf, sinf)


def _out_mlp_kernel(*refs, n_mix, final):
    h_ref = refs[0]
    mix_refs = refs[1:1 + n_mix]
    wout_ref, gffn_ref, wup_ref, wdown_ref = refs[1 + n_mix:5 + n_mix]
    rest = refs[5 + n_mix:]
    if final:
        gfin_ref, o_ref = rest
    else:
        (o_ref,) = rest
    h = h_ref[...]
    off = 0
    for m_ref in mix_refs:
        w = m_ref.shape[1]
        h = h + jnp.dot(m_ref[...].astype(BF16), wout_ref[off:off + w, :], preferred_element_type=F32)
        off += w
    xn = _rms(h, gffn_ref[...]).astype(BF16)
    acc = h
    step = 1024
    for j in range(D_FF // step):
        hid = jnp.dot(xn, wup_ref[:, j * step:(j + 1) * step], preferred_element_type=F32)
        hid = jnp.maximum(hid, 0.0)
        acc = acc + jnp.dot((hid * hid).astype(BF16), wdown_ref[j * step:(j + 1) * step, :],
                            preferred_element_type=F32)
    if final:
        acc = _rms(acc, gfin_ref[...])
    o_ref[...] = acc


def _out_mlp(h, mixes, w_out, g_ffn, w_up, w_down, g_final, tm):
    t = h.shape[0]
    assert t % tm == 0
    final = g_final is not None
    row = lambda i: (i, 0)
    const = lambda i: (0, 0)
    in_specs = [pl.BlockSpec((tm, D_MODEL), row)]
    in_specs += [pl.BlockSpec((tm, m.shape[1]), row) for m in mixes]
    once = pl.Buffered(1)
    in_specs += [pl.BlockSpec(w_out.shape, const, pipeline_mode=once), pl.BlockSpec((1, D_MODEL), const),
                 pl.BlockSpec(w_up.shape, const, pipeline_mode=once),
                 pl.BlockSpec(w_down.shape, const, pipeline_mode=once)]
    args = [h, *mixes, w_out, g_ffn.reshape(1, D_MODEL), w_up, w_down]
    if final:
        in_specs.append(pl.BlockSpec((1, D_MODEL), const))
        args.append(g_final.reshape(1, D_MODEL))
    return pl.pallas_call(
        functools.partial(_out_mlp_kernel, n_mix=len(mixes), final=final),
        grid=(t // tm,),
        in_specs=in_specs,
        out_specs=pl.BlockSpec((tm, D_MODEL), row),
        out_shape=jax.ShapeDtypeStruct((t, D_MODEL), F32),
        compiler_params=_cparams("parallel"),
        name="out_mlp",
    )(*args)


def _chunk_masks(c):
    ii = lax.broadcasted_iota(jnp.int32, (c, c), 0)
    jj = lax.broadcasted_iota(jnp.int32, (c, c), 1)
    return ii, jj


def _mix_ab_kernel(qkv_ref, rest_ref, gates_ref, prow_ref, pcol_ref, tri_ref, spread_ref,
                   ggain_ref, rgain_ref, mixed_ref, sd_ref, sr_ref, *, n_sub):
    c = CHUNK
    lb = n_sub * c
    step = pl.program_id(1)

    @pl.when(step == 0)
    def _():
        sd_ref[...] = jnp.zeros_like(sd_ref)
        sr_ref[...] = jnp.zeros_like(sr_ref)

    g = gates_ref[...]
    g_t = g.T
    neg_a_row = -jnp.exp(prow_ref[0:1, :])
    dt_row = prow_ref[1:2, :]
    neg_a_col = -jnp.exp(pcol_ref[0:HEADS, 0:1])
    dt_col = pcol_ref[0:HEADS, 1:2]
    la_cols = neg_a_row * _softplus(g + dt_row)
    beta_cols = jax.nn.sigmoid(g)
    la_rows = neg_a_col * _softplus(g_t[0:HEADS, :] + dt_col)

    tri = tri_ref[...]
    la_hi, la_lo = _split(la_cols)
    g_cols = (jnp.dot(tri, la_hi, preferred_element_type=F32)
              + jnp.dot(tri, la_lo, preferred_element_type=F32))
    lr_hi, lr_lo = _split(jnp.concatenate([la_rows, jnp.zeros_like(la_rows)], axis=0))
    nt = lambda x, y: lax.dot_general(x, y, (((1,), (1,)), ((), ())), preferred_element_type=F32)
    g_rows = nt(lr_hi, tri) + nt(lr_lo, tri)
    lane = lax.broadcasted_iota(jnp.int32, (lb, 128), 1)
    x_hi, x_lo = _split(jnp.where(lane < HEADS, g_cols, beta_cols))
    spread = (jnp.dot(x_hi, spread_ref[...], preferred_element_type=F32)
              + jnp.dot(x_lo, spread_ref[...], preferred_element_type=F32))
    g_wide = spread[:, 0:HEADS * DK]
    beta_wide = spread[:, HEADS * DK:2 * HEADS * DK]
    eg_wide = jnp.exp(g_wide)

    ii, jj = _chunk_masks(c)
    causal = ii >= jj
    strict = ii > jj
    dm = (ii - jj).astype(F32)
    pos_col = lax.broadcasted_iota(jnp.int32, (c, 1), 0).astype(F32)

    ggain = ggain_ref[...]
    rgain = rgain_ref[...]
    base = HEADS * DK
    s_delta = [sd_ref[0, h] for h in range(HEADS)]
    s_ret = [sr_ref[0, h] for h in range(HEADS)]

    items = [(ci, h) for ci in range(n_sub) for h in range(HEADS)]

    rng = lambda ci: (ci * c, (ci + 1) * c)
    pre = {it: {} for it in items}
    pre_b = {it: {} for it in items}

    def gdn_stage(ci, h):
        d = pre[ci, h]
        r0, r1 = rng(ci)
        g_col = g_wide[r0:r1, h * DK:(h + 1) * DK]
        g_row = g_rows[h:h + 1, r0:r1]
        g_last = g_col[c - 1:c, :]
        beta = beta_wide[r0:r1, h * DK:(h + 1) * DK]
        eg = eg_wide[r0:r1, h * DK:(h + 1) * DK]
        dec_causal = jnp.exp(jnp.where(causal, g_col[:, 0:c] - g_row, NEG))
        dec_strict = jnp.where(strict, dec_causal, 0.0)
        q = qkv_ref[r0:r1, h * DK:(h + 1) * DK]
        k = qkv_ref[r0:r1, (HEADS + h) * DK:(HEADS + h + 1) * DK]
        yield
        v = qkv_ref[r0:r1, (2 * HEADS + h) * DK:(2 * HEADS + h + 1) * DK]
        kb = k * beta
        prod = _mm_nt(jnp.concatenate([q, kb], axis=0), k)
        d.update(rhs=jnp.concatenate([v * beta, kb * eg], axis=1), qd=q * eg,
                 kd=k * jnp.exp(g_last - g_col), gl=jnp.exp(g_last))
        yield
        d.update(n=prod[c:2 * c] * dec_strict, qk=prod[0:c] * dec_causal)

    _interleave(gdn_stage(ci, h) for ci, h in items)
    t_invs = _tri_inv_many([pre[it]['n'] for it in items])
    for it, t_inv in zip(items, t_invs):
        pre[it]['sol'] = _mm3(t_inv, pre[it]['rhs'])

    def ret_stage(ci, h):
        d = pre_b[ci, h]
        r0, r1 = rng(ci)
        qr = rest_ref[r0:r1, base + h * DK:base + (h + 1) * DK]
        kr = rest_ref[r0:r1, 2 * base + h * DK:2 * base + (h + 1) * DK]
        lg = LOG_GAMMA[h]
        dec = jnp.exp(jnp.where(causal, dm * lg, NEG))
        qk = _mm_nt(qr, kr)
        d['qd'] = qr * jnp.exp((pos_col + 1.0) * lg)
        kd = kr * jnp.exp((float(c - 1) - pos_col) * lg)
        yield
        vb = rest_ref[r0:r1, 3 * base + h * DK:3 * base + (h + 1) * DK]
        d['intra'] = _mm(qk * dec, vb)
        d['kv'] = _mm_tn(kd, vb)

    _interleave(ret_stage(ci, h) for ci, h in items)

    for ci, h in items:
        pre_b[ci, h]['s_in'] = s_ret[h]
        s_ret[h] = s_ret[h] * math.exp(c * LOG_GAMMA[h]) + pre_b[ci, h]['kv']

    def ret_out_stage(ci, h):
        d = pre_b[ci, h]
        r0, r1 = rng(ci)
        o = _mm(d['qd'], d['s_in']) + d['intra']
        yield
        ms_ = jnp.mean(o * o, axis=-1, keepdims=True)
        yield
        gb = rest_ref[r0:r1, 4 * base + h * DK:4 * base + (h + 1) * DK]
        mixed_ref[r0:r1, base + h * DK:base + (h + 1) * DK] = o * lax.rsqrt(ms_ + EPS) * rgain * gb

    _interleave(ret_out_stage(ci, h) for ci, h in items)

    def delta_step(ci, h):
        d = pre[ci, h]
        r = _mm(jnp.concatenate([d['sol'][:, A_DV:2 * A_DV], d['qd']], axis=0), s_delta[h])
        yield
        v_new = d['sol'][:, 0:A_DV] - r[0:c]
        d['o'] = r[c:2 * c] + _mm(d['qk'], v_new)
        s_delta[h] = s_delta[h] * d['gl'] + _mm_tn(d['kd'], v_new)

    for ci in range(n_sub):
        _interleave(delta_step(ci, h) for h in range(HEADS))

    def gdn_out_stage(ci, h):
        o = pre[ci, h]['o']
        r0, r1 = rng(ci)
        ms_ = jnp.mean(o * o, axis=-1, keepdims=True)
        yield
        ga = rest_ref[r0:r1, h * DK:(h + 1) * DK]
        mixed_ref[r0:r1, h * A_DV:(h + 1) * A_DV] = o * lax.rsqrt(ms_ + EPS) * ggain * ga

    _interleave(gdn_out_stage(ci, h) for ci, h in items)

    for h in range(HEADS):
        sd_ref[0, h] = s_delta[h]
        sr_ref[0, h] = s_ret[h]


def _rope_tables(pos):
    half = DK // 2
    inv = ROPE_BASE ** (-jnp.arange(half, dtype=F32) / half)
    ang = pos.astype(F32)[:, None] * inv[None, :]
    cos, sin = jnp.cos(ang), jnp.sin(ang)
    return jnp.concatenate([cos, cos], axis=-1), jnp.concatenate([-sin, sin], axis=-1)


def _chunk_tri(lb):
    i = jnp.arange(lb)[:, None]
    j = jnp.arange(lb)[None, :]
    return ((i >= j) & (i // CHUNK == j // CHUNK)).astype(BF16)


def _head_spread(groups):
    n = groups * HEADS
    src = jnp.arange(128)[:, None]
    dst = jnp.arange(n * 128)[None, :] // 128
    return (src == dst).astype(BF16)


def _gate_params(a_log, dt_bias):
    prow = jnp.zeros((8, 128), F32).at[0, 0:HEADS].set(a_log).at[1, 0:HEADS].set(dt_bias)
    pcol = jnp.zeros((8, 128), F32).at[0:HEADS, 0].set(a_log).at[0:HEADS, 1].set(dt_bias)
    return prow, pcol


def _mix_ab_prompt(qkv, rest, gates, batch, seq, a_log, dt_bias, gdn_gain, ret_gain):
    lb = AB_SUB * CHUNK
    assert seq % lb == 0
    nc = seq // lb
    prow, pcol = _gate_params(a_log, dt_bias)
    tri = _chunk_tri(lb)
    spread = _head_spread(2)
    tok = lambda b, c: (b * nc + c, 0)
    const = lambda b, c: (0, 0)
    mixed, sd, sr = pl.pallas_call(
        functools.partial(_mix_ab_kernel, n_sub=AB_SUB),
        grid=(batch, nc),
        in_specs=[pl.BlockSpec((lb, A_CONV_CH), tok),
                  pl.BlockSpec((lb, rest.shape[1]), tok),
                  pl.BlockSpec((lb, 128), tok),
                  pl.BlockSpec((8, 128), const),
                  pl.BlockSpec((8, 128), const),
                  pl.BlockSpec(tri.shape, const),
                  pl.BlockSpec(spread.shape, const),
                  pl.BlockSpec((1, A_DV), const),
                  pl.BlockSpec((1, DK), const)],
        out_specs=[pl.BlockSpec((lb, 2 * HEADS * DK), tok),
                   pl.BlockSpec((1, HEADS, DK, A_DV), lambda b, c: (b, 0, 0, 0)),
                   pl.BlockSpec((1, HEADS, DK, DK), lambda b, c: (b, 0, 0, 0))],
        out_shape=[jax.ShapeDtypeStruct((batch * seq, 2 * HEADS * DK), F32),
                   jax.ShapeDtypeStruct((batch, HEADS, DK, A_DV), F32),
                   jax.ShapeDtypeStruct((batch, HEADS, DK, DK), F32)],
        compiler_params=_cparams("parallel", "arbitrary"),
        name="mix_ab_prompt",
    )(qkv, rest, gates, prow, pcol, tri, spread, gdn_gain.reshape(1, A_DV), ret_gain.reshape(1, DK))
    return mixed, sd, sr


def _mix_c_kernel(qk_ref, v_ref, opre_ref, gates_ref, brow_ref, bcol_ref, tri_ref, spread_ref, gain_ref,
                  h_ref, cm_ref, nv_ref, m_ref, *, n_sub):
    c = CHUNK
    lb = n_sub * c
    step = pl.program_id(1)

    @pl.when(step == 0)
    def _():
        cm_ref[...] = jnp.zeros_like(cm_ref)
        nv_ref[...] = jnp.zeros_like(nv_ref)
        m_ref[...] = jnp.zeros_like(m_ref)

    items = [(ci, h) for ci in range(n_sub) for h in range(HEADS)]
    rng = lambda ci: (ci * c, (ci + 1) * c)
    pre = {it: {} for it in items}

    for ci, h in items:
        r0, r1 = rng(ci)
        q = qk_ref[r0:r1, h * DK:(h + 1) * DK]
        k = qk_ref[r0:r1, (HEADS + h) * DK:(HEADS + h + 1) * DK] * (DK ** -0.5)
        pre[ci, h].update(q=q, k=k, qk=_mm_nt(q, k))

    g = gates_ref[...] + brow_ref[0:1, :]
    g_t = gates_ref[...].T[0:2 * HEADS, :] + bcol_ref[0:2 * HEADS, 0:1]
    i_rows = g_t[0:HEADS, :]
    logf_rows = _log_sigmoid(g_t[HEADS:2 * HEADS, :])
    tri = tri_ref[...]
    lf_hi, lf_lo = _split(_log_sigmoid(g))
    b_cols = (jnp.dot(tri, lf_hi, preferred_element_type=F32)
              + jnp.dot(tri, lf_lo, preferred_element_type=F32))
    lr_hi, lr_lo = _split(jnp.concatenate([logf_rows, jnp.zeros_like(logf_rows)], axis=0))
    nt = lambda x, y: lax.dot_general(x, y, (((1,), (1,)), ((), ())), preferred_element_type=F32)
    b_rows = nt(lr_hi, tri) + nt(lr_lo, tri)
    lane = lax.broadcasted_iota(jnp.int32, (lb, 128), 1)
    x_hi, x_lo = _split(jnp.where(lane < HEADS, g, b_cols))
    spread = (jnp.dot(x_hi, spread_ref[...], preferred_element_type=F32)
              + jnp.dot(x_lo, spread_ref[...], preferred_element_type=F32))
    i_wide = spread[:, 0:HEADS * 128]
    b_wide = spread[:, HEADS * 128:2 * HEADS * 128]

    ii, jj = _chunk_masks(c)
    causal = ii >= jj
    gain = gain_ref[...]
    wide2 = lambda x: jnp.concatenate([x, x], axis=-1)

    def gates_stage(ci, h):
        d = pre[ci, h]
        r0, r1 = rng(ci)
        b_col = b_wide[r0:r1, h * 128:(h + 1) * 128]
        i_col = i_wide[r0:r1, h * 128:(h + 1) * 128]
        b_last = b_col[c - 1:c, :]
        d_log = jnp.where(causal, b_col[:, 0:c] - b_rows[h:h + 1, r0:r1] + i_rows[h:h + 1, r0:r1], NEG)
        d.update(b_col=b_col, b_last=b_last, d_log=d_log, k_log=b_last - b_col + i_col)
        yield
        d['d_max'] = jnp.max(d_log, axis=1, keepdims=True)

    _interleave(gates_stage(ci, h) for ci, h in items)

    ms = [m_ref[0, h:h + 1, :] for h in range(HEADS)]
    for ci, h in items:
        d = pre[ci, h]
        inter = d['b_col'] + ms[h]
        m_row = jnp.maximum(inter, d['d_max'])
        m_new = m_row[c - 1:c, :]
        d.update(inter=inter, m_row=m_row, m_old=ms[h], m_new=m_new)
        ms[h] = m_new

    def weights_stage(ci, h):
        d = pre[ci, h]
        r0, r1 = rng(ci)
        d['w_inter'] = jnp.exp(d['inter'] - d['m_row'])
        d['f_state'] = jnp.exp(d['b_last'] + d['m_old'] - d['m_new'])
        yield
        d['w_intra'] = jnp.exp(d['d_log'] - d['m_row'][:, 0:c]) * d['qk']
        d['kw'] = d['k'] * jnp.exp(d['k_log'] - d['m_new'])
        yield
        v = v_ref[r0:r1, h * C_DV:(h + 1) * C_DV]
        d['intra'] = _mm(d['w_intra'], v)
        d['kv'] = _mm_tn(d['kw'], v)
        yield
        d['sum_intra'] = jnp.sum(d['w_intra'], axis=1, keepdims=True)
        d['sum_kw'] = jnp.sum(d['kw'], axis=0, keepdims=True)
        d['inv_floor'] = jnp.exp(-d['m_row'])

    _interleave(weights_stage(ci, h) for ci, h in items)

    cms = [cm_ref[0, h] for h in range(HEADS)]
    nvs = [nv_ref[0, h:h + 1, :] for h in range(HEADS)]
    for ci, h in items:
        d = pre[ci, h]
        d['cm_in'], d['nv_in'] = cms[h], nvs[h]
        cms[h] = cms[h] * wide2(d['f_state']) + d['kv']
        nvs[h] = nvs[h] * d['f_state'] + d['sum_kw']

    def output_stage(ci, h):
        d = pre[ci, h]
        r0, r1 = rng(ci)
        qc = _mm(d['q'], d['cm_in'])
        qn = jnp.sum(d['q'] * d['nv_in'], axis=1, keepdims=True)
        yield
        num = wide2(d['w_inter']) * qc + d['intra']
        den = d['w_inter'] * qn + d['sum_intra']
        hh = num / wide2(jnp.maximum(jnp.abs(den), d['inv_floor']))
        yield
        ms_ = jnp.mean(hh * hh, axis=-1, keepdims=True)
        yield
        op = opre_ref[r0:r1, h * C_DV:(h + 1) * C_DV]
        h_ref[r0:r1, h * C_DV:(h + 1) * C_DV] = hh * lax.rsqrt(ms_ + EPS) * gain * jax.nn.sigmoid(op)

    _interleave(output_stage(ci, h) for ci, h in items)

    for h in range(HEADS):
        cm_ref[0, h] = cms[h]
        nv_ref[0, h:h + 1, :] = nvs[h]
        m_ref[0, h:h + 1, :] = ms[h]


def _bias_params(b_gate):
    brow = jnp.zeros((8, 128), F32).at[0, 0:2 * HEADS].set(b_gate)
    bcol = jnp.zeros((8, 128), F32).at[0:2 * HEADS, 0].set(b_gate)
    return brow, bcol


def _mix_c_prompt(qk, v, opre, gates, batch, seq, b_gate, gain):
    lb = C_SUB * CHUNK
    assert seq % lb == 0
    nc = seq // lb
    brow, bcol = _bias_params(b_gate)
    tri = _chunk_tri(lb)
    spread = _head_spread(2)
    tok = lambda b, c: (b * nc + c, 0)
    const = lambda b, c: (0, 0)
    return pl.pallas_call(
        functools.partial(_mix_c_kernel, n_sub=C_SUB),
        grid=(batch, nc),
        in_specs=[pl.BlockSpec((lb, 2 * HEADS * DK), tok),
                  pl.BlockSpec((lb, HEADS * C_DV), tok),
                  pl.BlockSpec((lb, HEADS * C_DV), tok),
                  pl.BlockSpec((lb, 128), tok),
                  pl.BlockSpec((8, 128), const),
                  pl.BlockSpec((8, 128), const),
                  pl.BlockSpec(tri.shape, const),
                  pl.BlockSpec(spread.shape, const),
                  pl.BlockSpec((1, C_DV), const)],
        out_specs=[pl.BlockSpec((lb, HEADS * C_DV), tok),
                   pl.BlockSpec((1, HEADS, DK, C_DV), lambda b, c: (b, 0, 0, 0)),
                   pl.BlockSpec((1, HEADS, DK), lambda b, c: (b, 0, 0)),
                   pl.BlockSpec((1, HEADS, 128), lambda b, c: (b, 0, 0))],
        out_shape=[jax.ShapeDtypeStruct((batch * seq, HEADS * C_DV), F32),
                   jax.ShapeDtypeStruct((batch, HEADS, DK, C_DV), F32),
                   jax.ShapeDtypeStruct((batch, HEADS, DK), F32),
                   jax.ShapeDtypeStruct((batch, HEADS, 128), F32)],
        compiler_params=_cparams("parallel", "arbitrary"),
        name="mix_c_prompt",
    )(qk, v, opre, gates, brow, bcol, tri, spread, gain.reshape(1, C_DV))


def _row_select(rows, t, new, old):
    return jnp.where(rows == t, new, old)


def _step_ab_kernel(qkv_ref, rest_ref, gates_ref, cos_ref, sin_ref, wconv_ref, prow_ref, ggain_ref, rgain_ref,
                    buf_ref, sd_ref, sr_ref, mixed_ref, nbuf_ref, nsd_ref, nsr_ref):
    tb = qkv_ref.shape[0]
    u = qkv_ref[...]
    w = wconv_ref[...]
    b0 = buf_ref[:, 0:A_CONV_CH]
    b1 = buf_ref[:, A_CONV_CH:2 * A_CONV_CH]
    b2 = buf_ref[:, 2 * A_CONV_CH:3 * A_CONV_CH]
    conv = b0 * w[0:1] + b1 * w[1:2] + b2 * w[2:3] + u * w[3:4]
    nbuf_ref[:, 0:A_CONV_CH] = b1
    nbuf_ref[:, A_CONV_CH:2 * A_CONV_CH] = b2
    nbuf_ref[:, 2 * A_CONV_CH:3 * A_CONV_CH] = u
    act = _silu(conv)
    g = gates_ref[...]
    eg_all = jnp.exp(-jnp.exp(prow_ref[0:1, :]) * _softplus(g + prow_ref[1:2, :]))
    beta_all = jax.nn.sigmoid(g)
    cosf = cos_ref[...]
    sinf = sin_ref[...]
    base = HEADS * DK
    rows = lax.broadcasted_iota(jnp.int32, (tb, DK), 0)
    items = [(t, h) for t in range(tb) for h in range(HEADS)]

    qs = [_l2(act[:, h * DK:(h + 1) * DK]) * (DK ** -0.5) for h in range(HEADS)]
    ks = [_l2(act[:, (HEADS + h) * DK:(HEADS + h + 1) * DK]) for h in range(HEADS)]
    k_s = [jnp.zeros((tb, A_DV), F32) for _ in range(HEADS)]
    for t, h in items:
        k_s[h] = _row_select(rows, t, _mm(ks[h], sd_ref[t, h]), k_s[h])
    v_new = []
    for h in range(HEADS):
        v = act[:, (2 * HEADS + h) * DK:(2 * HEADS + h + 1) * DK]
        v_new.append(beta_all[:, HEADS + h:HEADS + h + 1] * (v - eg_all[:, h:h + 1] * k_s[h]))
    for t, h in items:
        nsd_ref[t, h] = (sd_ref[t, h] * eg_all[t:t + 1, h:h + 1]
                         + _mm3_tn(jnp.where(rows == t, ks[h], 0.0), v_new[h]))
    o_a = [jnp.zeros((tb, A_DV), F32) for _ in range(HEADS)]
    for t, h in items:
        o_a[h] = _row_select(rows, t, _mm(qs[h], nsd_ref[t, h]), o_a[h])
    for h in range(HEADS):
        ga = rest_ref[:, h * DK:(h + 1) * DK]
        mixed_ref[:, h * DK:(h + 1) * DK] = _rms(o_a[h], ggain_ref[...]) * _silu(ga)

    qrs, krs = [], []
    for h in range(HEADS):
        qb = rest_ref[:, base + h * DK:base + (h + 1) * DK]
        kb = rest_ref[:, 2 * base + h * DK:2 * base + (h + 1) * DK]
        qrs.append(qb * cosf + pltpu.roll(qb, DK // 2, 1) * sinf)
        krs.append((kb * cosf + pltpu.roll(kb, DK // 2, 1) * sinf) * (DK ** -0.5))
    for t, h in items:
        vb = rest_ref[:, 3 * base + h * DK:3 * base + (h + 1) * DK]
        nsr_ref[t, h] = (sr_ref[t, h] * math.exp(LOG_GAMMA[h])
                         + _mm3_tn(jnp.where(rows == t, krs[h], 0.0), vb))
    o_b = [jnp.zeros((tb, DK), F32) for _ in range(HEADS)]
    for t, h in items:
        o_b[h] = _row_select(rows, t, _mm(qrs[h], nsr_ref[t, h]), o_b[h])
    for h in range(HEADS):
        gb = rest_ref[:, 4 * base + h * DK:4 * base + (h + 1) * DK]
        mixed_ref[:, base + h * DK:base + (h + 1) * DK] = _rms(o_b[h], rgain_ref[...]) * _silu(gb)


def _mix_ab_sample(qkv, rest, gates, conv_buf, s_delta, s_ret, w_conv, a_log, dt_bias, gdn_gain, ret_gain):
    nb = qkv.shape[0]
    tb = SAMPLE_TB
    assert nb % tb == 0
    cosf, sinf = _rope_tables(PAST_LEN + jnp.arange(1))
    prow, _ = _gate_params(a_log, dt_bias)
    tok = lambda b: (b, 0)
    tok4 = lambda b: (b, 0, 0, 0)
    const = lambda b: (0, 0)
    n_buf = (CONV_W - 1) * A_CONV_CH
    mixed, nbuf, nsd, nsr = pl.pallas_call(
        _step_ab_kernel,
        grid=(nb // tb,),
        in_specs=[pl.BlockSpec((tb, A_CONV_CH), tok),
                  pl.BlockSpec((tb, rest.shape[1]), tok),
                  pl.BlockSpec((tb, 128), tok),
                  pl.BlockSpec((1, DK), const),
                  pl.BlockSpec((1, DK), const),
                  pl.BlockSpec((CONV_W, A_CONV_CH), const),
                  pl.BlockSpec((8, 128), const),
                  pl.BlockSpec((1, A_DV), const),
                  pl.BlockSpec((1, DK), const),
                  pl.BlockSpec((tb, n_buf), tok),
                  pl.BlockSpec((tb, HEADS, DK, A_DV), tok4),
                  pl.BlockSpec((tb, HEADS, DK, DK), tok4)],
        out_specs=[pl.BlockSpec((tb, 2 * HEADS * DK), tok),
                   pl.BlockSpec((tb, n_buf), tok),
                   pl.BlockSpec((tb, HEADS, DK, A_DV), tok4),
                   pl.BlockSpec((tb, HEADS, DK, DK), tok4)],
        out_shape=[jax.ShapeDtypeStruct((nb, 2 * HEADS * DK), F32),
                   jax.ShapeDtypeStruct((nb, n_buf), F32),
                   jax.ShapeDtypeStruct((nb, HEADS, DK, A_DV), F32),
                   jax.ShapeDtypeStruct((nb, HEADS, DK, DK), F32)],
        compiler_params=_cparams("parallel"),
        name="mix_ab_sample",
    )(qkv, rest, gates, cosf, sinf, w_conv, prow, gdn_gain.reshape(1, A_DV), ret_gain.reshape(1, DK),
      conv_buf.reshape(nb, n_buf), s_delta, s_ret)
    return mixed, nbuf.reshape(nb, CONV_W - 1, A_CONV_CH), nsd, nsr


def _step_c_kernel(qk_ref, v_ref, opre_ref, gates_ref, brow_ref, gain_ref, cm_ref, nv_ref, m_ref,
                   h_ref, ncm_ref, nnv_ref, nm_ref):
    tb = qk_ref.shape[0]
    g = gates_ref[...] + brow_ref[0:1, :]
    logf = _log_sigmoid(g)
    rows = lax.broadcasted_iota(jnp.int32, (tb, DK), 0)
    items = [(t, h) for t in range(tb) for h in range(HEADS)]
    qs, kws, fs, ms_new, nvs_new = [], [], [], [], []
    for h in range(HEADS):
        i_pre = g[:, h:h + 1]
        q = qk_ref[:, h * DK:(h + 1) * DK]
        k = qk_ref[:, (HEADS + h) * DK:(HEADS + h + 1) * DK] * (DK ** -0.5)
        inter = logf[:, HEADS + h:HEADS + h + 1] + m_ref[:, h:h + 1]
        m_new = jnp.maximum(inter, i_pre)
        f_state = jnp.exp(inter - m_new)
        kw = k * jnp.exp(i_pre - m_new)
        nv_new = nv_ref[:, h * DK:(h + 1) * DK] * f_state + kw
        nnv_ref[:, h * DK:(h + 1) * DK] = nv_new
        nm_ref[:, h:h + 1] = m_new
        qs.append(q)
        kws.append(kw)
        fs.append(f_state)
        ms_new.append(m_new)
        nvs_new.append(nv_new)
    for t, h in items:
        v = v_ref[:, h * C_DV:(h + 1) * C_DV]
        ncm_ref[t, h] = (cm_ref[t, h] * fs[h][t:t + 1, :]
                         + _mm3_tn(jnp.where(rows == t, kws[h], 0.0), v))
    rows_v = lax.broadcasted_iota(jnp.int32, (tb, C_DV), 0)
    nums = [jnp.zeros((tb, C_DV), F32) for _ in range(HEADS)]
    for t, h in items:
        nums[h] = _row_select(rows_v, t, _mm(qs[h], ncm_ref[t, h]), nums[h])
    for h in range(HEADS):
        den = jnp.sum(qs[h] * nvs_new[h], axis=1, keepdims=True)
        hh = nums[h] / jnp.maximum(jnp.abs(den), jnp.exp(-ms_new[h]))
        op = opre_ref[:, h * C_DV:(h + 1) * C_DV]
        h_ref[:, h * C_DV:(h + 1) * C_DV] = _rms(hh, gain_ref[...]) * jax.nn.sigmoid(op)


def _mix_c_sample(qk, v, opre, gates, s_c, s_n, s_m, b_gate, gain):
    nb = qk.shape[0]
    tb = SAMPLE_TB
    assert nb % tb == 0
    brow, _ = _bias_params(b_gate)
    tok = lambda b: (b, 0)
    tok4 = lambda b: (b, 0, 0, 0)
    const = lambda b: (0, 0)
    hh, ncm, nnv, nm = pl.pallas_call(
        _step_c_kernel,
        grid=(nb // tb,),
        in_specs=[pl.BlockSpec((tb, 2 * HEADS * DK), tok),
                  pl.BlockSpec((tb, HEADS * C_DV), tok),
                  pl.BlockSpec((tb, HEADS * C_DV), tok),
                  pl.BlockSpec((tb, 128), tok),
                  pl.BlockSpec((8, 128), const),
                  pl.BlockSpec((1, C_DV), const),
                  pl.BlockSpec((tb, HEADS, DK, C_DV), tok4),
                  pl.BlockSpec((tb, HEADS * DK), tok),
                  pl.BlockSpec((tb, HEADS), tok)],
        out_specs=[pl.BlockSpec((tb, HEADS * C_DV), tok),
                   pl.BlockSpec((tb, HEADS, DK, C_DV), tok4),
                   pl.BlockSpec((tb, HEADS * DK), tok),
                   pl.BlockSpec((tb, HEADS), tok)],
        out_shape=[jax.ShapeDtypeStruct((nb, HEADS * C_DV), F32),
                   jax.ShapeDtypeStruct((nb, HEADS, DK, C_DV), F32),
                   jax.ShapeDtypeStruct((nb, HEADS * DK), F32),
                   jax.ShapeDtypeStruct((nb, HEADS), F32)],
        compiler_params=_cparams("parallel"),
        name="mix_c_sample",
    )(qk, v, opre, gates, brow, gain.reshape(1, C_DV), s_c, s_n.reshape(nb, HEADS * DK), s_m)
    return hh, ncm, nnv.reshape(nb, HEADS, DK), nm


A_SPLITS = (A_CONV_CH, 5 * HEADS * DK, 128)
C_SPLITS = (2 * HEADS * DK, HEADS * C_DV, HEADS * C_DV, 128)


def _prep_w_in_a(w):
    n_gate = 2 * HEADS
    qkv = w[:, :A_CONV_CH]
    gates = w[:, A_CONV_CH:A_CONV_CH + n_gate]
    rest = w[:, A_CONV_CH + n_gate:]
    pad = jnp.zeros((D_MODEL, 128 - n_gate), w.dtype)
    return jnp.concatenate([qkv, rest, gates, pad], axis=1).astype(BF16)


def _prep_w_in_c(w):
    n_gate = 2 * HEADS
    main = w[:, :2 * HEADS * DK + 2 * HEADS * C_DV]
    gates = w[:, 2 * HEADS * DK + 2 * HEADS * C_DV:]
    pad = jnp.zeros((D_MODEL, 128 - n_gate), w.dtype)
    return jnp.concatenate([main, gates, pad], axis=1).astype(BF16)


def _run_group(x, prompt, states, p, tm):
    batch, seq, _ = x.shape
    h = x.reshape(batch * seq, D_MODEL)
    if prompt:
        qkv, rest, gates, conv = _proj_a_prompt(h, p['norm_mix_a'][0], p['w_in_a'], p['w_conv_a'][0], batch, seq, tm)
        mixed, sd, sr = _mix_ab_prompt(qkv, rest, gates, batch, seq, p['a_log'][0], p['dt_bias'][0],
                                       p['gdn_gain'][0], p['ret_gain'][0])
    else:
        qkv, rest, gates = _norm_proj(h, p['norm_mix_a'][0], p['w_in_a'], A_SPLITS, tm)
        mixed, conv, sd, sr = _mix_ab_sample(qkv, rest, gates, states['conv'][0], states['delta'][0],
                                             states['ret'][0], p['w_conv_a'][0], p['a_log'][0], p['dt_bias'][0],
                                             p['gdn_gain'][0], p['ret_gain'][0])
    h = _out_mlp(h, [mixed], p['w_out_a'], p['norm_ffn'][0], p['w_up'][0], p['w_down'][0], None, tm)
    qk, v, opre, gates_c = _norm_proj(h, p['norm_mix_c'][0], p['w_in_c'], C_SPLITS, tm)
    if prompt:
        hm, cm, nv, m = _mix_c_prompt(qk, v, opre, gates_c, batch, seq, p['b_gate_c'][0], p['mlstm_gain'][0])
        m = m[:, :, 0]
    else:
        hm, cm, nv, m = _mix_c_sample(qk, v, opre, gates_c, states['c'][0], states['n'][0], states['m'][0],
                                      p['b_gate_c'][0], p['mlstm_gain'][0])
    y = _out_mlp(h, [hm], p['w_out_c'], p['norm_ffn'][1], p['w_up'][1], p['w_down'][1], p['final_gain'], tm)
    return (y.reshape(batch, seq, D_MODEL), conv[None], sd[None], sr[None], cm[None], nv[None], m[None])


def kernel(x_prompt, x_sample, state_conv_a, state_delta_a, state_ret_b, state_mlstm_C, state_mlstm_n, state_mlstm_m,
           norm_mix_a, w_in_a, w_conv_a, a_log, dt_bias, gdn_gain, ret_gain, w_out_a,
           norm_mix_c, w_in_c, b_gate_c, mlstm_gain, w_out_c, norm_ffn, w_up, w_down, final_gain):
    p = dict(norm_mix_a=norm_mix_a, w_in_a=_prep_w_in_a(w_in_a[0]), w_conv_a=w_conv_a, a_log=a_log, dt_bias=dt_bias,
             gdn_gain=gdn_gain, ret_gain=ret_gain, w_out_a=w_out_a[0].astype(BF16), norm_mix_c=norm_mix_c,
             w_in_c=_prep_w_in_c(w_in_c[0]), b_gate_c=b_gate_c, mlstm_gain=mlstm_gain,
             w_out_c=w_out_c[0].astype(BF16), norm_ffn=norm_ffn, w_up=w_up.astype(BF16),
             w_down=w_down.astype(BF16), final_gain=final_gain)
    outs_p = _run_group(x_prompt, True, None, p, PROMPT_TM)
    states = dict(conv=state_conv_a, delta=state_delta_a, ret=state_ret_b, c=state_mlstm_C, n=state_mlstm_n,
                  m=state_mlstm_m)
    outs_s = _run_group(x_sample, False, states, p, x_sample.shape[0] * x_sample.shape[1])
    return (outs_p[0], outs_s[0]) + outs_p[1:] + outs_s[1:]
```

```python
import functools
import math

import jax
import jax.numpy as jnp
from jax import lax
from jax.experimental import pallas as pl
from jax.experimental.pallas import tpu as pltpu

F32 = jnp.float32
BF16 = jnp.bfloat16

D_MODEL = 1024
D_FF = 4 * D_MODEL
CHUNK = 64
EPS = 1e-6
NEG = -1e30
HEADS = 4
DK = 128
A_DV = 128
C_DV = 256
CONV_W = 4
A_CONV_CH = 3 * HEADS * DK
ROPE_BASE = 10000.0
PAST_LEN = 16384
LOG_GAMMA = tuple(math.log1p(-(2.0 ** (-5.0 - h))) for h in range(HEADS))

VMEM_LIMIT_BYTES = 56 * 1024 * 1024
PROMPT_TM = 512
AB_SUB = 4
C_SUB = 4
SAMPLE_TB = 8


def _cparams(*sem):
    return pltpu.CompilerParams(dimension_semantics=sem, vmem_limit_bytes=VMEM_LIMIT_BYTES)


def _mm(a, b):
    return jnp.dot(a.astype(BF16), b.astype(BF16), preferred_element_type=F32)


def _mm_nt(a, b):
    return lax.dot_general(a.astype(BF16), b.astype(BF16), (((1,), (1,)), ((), ())), preferred_element_type=F32)


def _mm_tn(a, b):
    return lax.dot_general(a.astype(BF16), b.astype(BF16), (((0,), (0,)), ((), ())), preferred_element_type=F32)


def _split(a):
    hi = a.astype(BF16)
    lo = (a - hi.astype(F32)).astype(BF16)
    return hi, lo


def _mm3(a, b):
    ah, al = _split(a)
    bh, bl = _split(b)
    d = lambda x, y: jnp.dot(x, y, preferred_element_type=F32)
    return d(ah, bh) + (d(ah, bl) + d(al, bh))


def _mm3_tn(a, b):
    ah, al = _split(a)
    bh, bl = _split(b)
    d = lambda x, y: lax.dot_general(x, y, (((0,), (0,)), ((), ())), preferred_element_type=F32)
    return d(ah, bh) + (d(ah, bl) + d(al, bh))


def _softplus(x):
    return jnp.maximum(x, 0.0) + jnp.log1p(jnp.exp(-jnp.abs(x)))


def _log_sigmoid(x):
    return -_softplus(-x)


def _silu(x):
    return x * jax.nn.sigmoid(x)


def _rms(x, gain):
    return x * lax.rsqrt(jnp.mean(x * x, axis=-1, keepdims=True) + EPS) * gain


def _l2(t):
    return t * lax.rsqrt(jnp.sum(t * t, axis=-1, keepdims=True) + EPS)


def _interleave(gens):
    gens = list(gens)
    while gens:
        alive = []
        for gen in gens:
            try:
                next(gen)
                alive.append(gen)
            except StopIteration:
                pass
        gens = alive


def _tri_inv_many(ns):
    c = ns[0].shape[0]
    eye = (lax.broadcasted_iota(jnp.int32, (c, c), 0) == lax.broadcasted_iota(jnp.int32, (c, c), 1)).astype(F32)
    ts = [eye - n for n in ns]
    ps = list(ns)
    steps = int(math.log2(c)) - 1
    for _ in range(steps):
        ps = [_mm3(p, p) for p in ps]
        ts = [t + _mm3(t, p) for t, p in zip(ts, ps)]
    return ts


def _project_rows(x_ref, g_ref, w_refs, out_refs, splits):
    xn = _rms(x_ref[...], g_ref[...]).astype(BF16)
    o = 0
    for w_ref, widths in zip(w_refs, splits):
        off = 0
        for n in widths:
            out_refs[o][...] = jnp.dot(xn, w_ref[:, off:off + n], preferred_element_type=F32)
            off += n
            o += 1


def _norm_proj_kernel(x_ref, xs_ref, g_ref, *refs, splits):
    n_out = sum(len(s) for s in splits)
    w_refs = refs[:len(splits)]
    out_refs = refs[len(splits):len(splits) + n_out]
    outs_refs = refs[len(splits) + n_out:]
    _project_rows(x_ref, g_ref, w_refs, out_refs, splits)

    @pl.when(pl.program_id(0) == 0)
    def _():
        _project_rows(xs_ref, g_ref, w_refs, outs_refs, splits)


def _norm_proj(x, x_s, gain, weights, splits, tm):
    t, ts = x.shape[0], x_s.shape[0]
    assert t % tm == 0 and all(sum(s) == w.shape[1] for w, s in zip(weights, splits))
    widths = [n for s in splits for n in s]
    row = lambda i: (i, 0)
    const = lambda i: (0, 0)
    outs = pl.pallas_call(
        functools.partial(_norm_proj_kernel, splits=splits),
        grid=(t // tm,),
        in_specs=[pl.BlockSpec((tm, D_MODEL), row), pl.BlockSpec((ts, D_MODEL), const),
                  pl.BlockSpec((1, D_MODEL), const)]
                 + [pl.BlockSpec(w.shape, const, pipeline_mode=pl.Buffered(1)) for w in weights],
        out_specs=[pl.BlockSpec((tm, n), row) for n in widths] + [pl.BlockSpec((ts, n), const) for n in widths],
        out_shape=[jax.ShapeDtypeStruct((t, n), F32) for n in widths]
                  + [jax.ShapeDtypeStruct((ts, n), F32) for n in widths],
        compiler_params=_cparams("arbitrary"),
        name="norm_proj",
    )(x, x_s, gain.reshape(1, D_MODEL), *weights)
    return outs[:len(widths)], outs[len(widths):]


def _proj_a_prompt_kernel(x_ref, xs_ref, g_ref, wqkv_ref, wrest_ref, wgate_ref, wconv_ref, cos_ref, sin_ref,
                          qkv_ref, rest_ref, gates_ref, conv_ref, qkvs_ref, rests_ref, gatess_ref,
                          xp_ref, raw_ref, *, tiles_per_seq):
    tm = x_ref.shape[0]
    hd = HEADS * DK

    @pl.when(pl.program_id(0) == 0)
    def _():
        _project_rows(xs_ref, g_ref, (wqkv_ref, wrest_ref, wgate_ref), (qkvs_ref, rests_ref, gatess_ref), A_SPLITS)

    xn = _rms(x_ref[...], g_ref[...]).astype(BF16)

    @pl.when(pl.program_id(0) % tiles_per_seq == 0)
    def _():
        xp_ref[0:8, :] = jnp.zeros((8, A_CONV_CH), F32)

    w = wconv_ref[...]
    base = A_CONV_CH

    def conv_part(part, slot, r0, r1):
        c0, c1 = part * hd, (part + 1) * hd
        raw = raw_ref[slot, r0:r1, :]
        xp_ref[8 + r0:8 + r1, c0:c1] = raw
        conv = (xp_ref[5 + r0:5 + r1, c0:c1] * w[0:1, c0:c1] + xp_ref[6 + r0:6 + r1, c0:c1] * w[1:2, c0:c1]
                + xp_ref[7 + r0:7 + r1, c0:c1] * w[2:3, c0:c1] + raw * w[3:4, c0:c1])
        if r1 == tm:
            rows = r1 - r0
            xp_ref[0:8, c0:c1] = raw[rows - 8:rows, :]
            conv_ref[0, :, c0:c1] = raw[rows - 3:rows, :]
        act = _silu(conv)
        if part == 2:
            qkv_ref[r0:r1, c0:c1] = act
        else:
            scale = DK ** -0.5 if part == 0 else 1.0
            for h in range(HEADS):
                qkv_ref[r0:r1, c0 + h * DK:c0 + (h + 1) * DK] = _l2(act[:, h * DK:(h + 1) * DK]) * scale

    def rest_part(part, slot, r0, r1):
        c0, c1 = part * hd, (part + 1) * hd
        raw = raw_ref[slot, r0:r1, :]
        if part in (0, 4):
            rest_ref[r0:r1, c0:c1] = _silu(raw)
        elif part == 3:
            rest_ref[r0:r1, c0:c1] = raw
        else:
            scale = 1.0 if part == 1 else DK ** -0.5
            for h in range(HEADS):
                t = raw[:, h * DK:(h + 1) * DK]
                rest_ref[r0:r1, c0 + h * DK:c0 + (h + 1) * DK] = (
                    (t * cos_ref[r0:r1, :] + pltpu.roll(t, DK // 2, 1) * sin_ref[r0:r1, :]) * scale)

    def gates_part(_, slot, r0, r1):
        gates_ref[r0:r1, :] = raw_ref[slot, r0:r1, 0:128]

    groups = [(wqkv_ref, 0, hd, functools.partial(conv_part, 0)),
              (wrest_ref, 3 * hd, hd, functools.partial(rest_part, 3)),
              (wqkv_ref, hd, hd, functools.partial(conv_part, 1)),
              (wrest_ref, 0, hd, functools.partial(rest_part, 0)),
              (wqkv_ref, 2 * hd, hd, functools.partial(conv_part, 2)),
              (wrest_ref, hd, hd, functools.partial(rest_part, 1)),
              (wgate_ref, 0, 128, functools.partial(gates_part, 0)),
              (wrest_ref, 2 * hd, hd, functools.partial(rest_part, 2)),
              (wrest_ref, 4 * hd, hd, functools.partial(rest_part, 4))]

    def project(n):
        w_ref, col0, width, _ = groups[n]
        for j in range(0, width, 256):
            wj = min(256, width - j)
            raw_ref[n % 2, :, j:j + wj] = jnp.dot(xn, w_ref[:, col0 + j:col0 + j + wj],
                                                  preferred_element_type=F32)
            yield

    def epilogue(n):
        for r0 in range(0, tm, 128):
            groups[n][3](n % 2, r0, min(r0 + 128, tm))
            yield

    _interleave([project(0)])
    for n in range(len(groups)):
        _interleave([epilogue(n)] + ([project(n + 1)] if n + 1 < len(groups) else []))


def _proj_a_prompt(x, x_s, gain, weights, w_conv, batch, seq, tm):
    t, ts = x.shape[0], x_s.shape[0]
    assert t % tm == 0 and seq % tm == 0
    tiles_per_seq = seq // tm
    cosf, sinf = _rope_tables(jnp.arange(seq))
    row = lambda i: (i, 0)
    const = lambda i: (0, 0)
    pos = lambda i: (i % tiles_per_seq, 0)
    n_rest = 5 * HEADS * DK
    outs = pl.pallas_call(
        functools.partial(_proj_a_prompt_kernel, tiles_per_seq=tiles_per_seq),
        grid=(t // tm,),
        in_specs=[pl.BlockSpec((tm, D_MODEL), row),
                  pl.BlockSpec((ts, D_MODEL), const),
                  pl.BlockSpec((1, D_MODEL), const)]
                 + [pl.BlockSpec(w.shape, const, pipeline_mode=pl.Buffered(1)) for w in weights]
                 + [pl.BlockSpec((CONV_W, A_CONV_CH), const),
                    pl.BlockSpec((tm, DK), pos),
                    pl.BlockSpec((tm, DK), pos)],
        out_specs=[pl.BlockSpec((tm, A_CONV_CH), row),
                   pl.BlockSpec((tm, n_rest), row),
                   pl.BlockSpec((tm, 128), row),
                   pl.BlockSpec((1, CONV_W - 1, A_CONV_CH), lambda i: (i // tiles_per_seq, 0, 0)),
                   pl.BlockSpec((ts, A_CONV_CH), const),
                   pl.BlockSpec((ts, n_rest), const),
                   pl.BlockSpec((ts, 128), const)],
        out_shape=[jax.ShapeDtypeStruct((t, A_CONV_CH), F32),
                   jax.ShapeDtypeStruct((t, n_rest), F32),
                   jax.ShapeDtypeStruct((t, 128), F32),
                   jax.ShapeDtypeStruct((batch, CONV_W - 1, A_CONV_CH), F32),
                   jax.ShapeDtypeStruct((ts, A_CONV_CH), F32),
                   jax.ShapeDtypeStruct((ts, n_rest), F32),
                   jax.ShapeDtypeStruct((ts, 128), F32)],
        scratch_shapes=[pltpu.VMEM((tm + 8, A_CONV_CH), F32), pltpu.VMEM((2, tm, HEADS * DK), F32)],
        compiler_params=_cparams("arbitrary"),
        name="proj_a_prompt",
    )(x, x_s, gain.reshape(1, D_MODEL), *weights, w_conv, cosf, sinf)
    return outs[:4], outs[4:]


def _out_mlp_kernel(h_ref, m_ref, hs_ref, ms_ref, wout_ref, gffn_ref, wup_ref, wdown_ref, *rest, final):
    if final:
        gfin_ref, o_ref, os_ref = rest
    else:
        o_ref, os_ref = rest

    def block(h_r, m_r, o_r):
        h = h_r[...] + jnp.dot(m_r[...].astype(BF16), wout_ref[...], preferred_element_type=F32)
        xn = _rms(h, gffn_ref[...]).astype(BF16)
        acc = h
        step = 1024
        for j in range(D_FF // step):
            hid = jnp.dot(xn, wup_ref[:, j * step:(j + 1) * step], preferred_element_type=F32)
            hid = jnp.maximum(hid, 0.0)
            acc = acc + jnp.dot((hid * hid).astype(BF16), wdown_ref[j * step:(j + 1) * step, :],
                                preferred_element_type=F32)
        if final:
            acc = _rms(acc, gfin_ref[...])
        o_r[...] = acc

    block(h_ref, m_ref, o_ref)

    @pl.when(pl.program_id(0) == 0)
    def _():
        block(hs_ref, ms_ref, os_ref)


def _out_mlp(h, mix, h_s, mix_s, w_out, g_ffn, w_up, w_down, g_final, tm):
    t, ts = h.shape[0], h_s.shape[0]
    assert t % tm == 0
    final = g_final is not None
    row = lambda i: (i, 0)
    const = lambda i: (0, 0)
    once = pl.Buffered(1)
    in_specs = [pl.BlockSpec((tm, D_MODEL), row), pl.BlockSpec((tm, mix.shape[1]), row),
                pl.BlockSpec((ts, D_MODEL), const), pl.BlockSpec((ts, mix_s.shape[1]), const),
                pl.BlockSpec(w_out.shape, const, pipeline_mode=once), pl.BlockSpec((1, D_MODEL), const),
                pl.BlockSpec(w_up.shape, const, pipeline_mode=once),
                pl.BlockSpec(w_down.shape, const, pipeline_mode=once)]
    args = [h, mix, h_s, mix_s, w_out, g_ffn.reshape(1, D_MODEL), w_up, w_down]
    if final:
        in_specs.append(pl.BlockSpec((1, D_MODEL), const))
        args.append(g_final.reshape(1, D_MODEL))
    return pl.pallas_call(
        functools.partial(_out_mlp_kernel, final=final),
        grid=(t // tm,),
        in_specs=in_specs,
        out_specs=[pl.BlockSpec((tm, D_MODEL), row), pl.BlockSpec((ts, D_MODEL), const)],
        out_shape=[jax.ShapeDtypeStruct((t, D_MODEL), F32), jax.ShapeDtypeStruct((ts, D_MODEL), F32)],
        compiler_params=_cparams("arbitrary"),
        name="out_mlp",
    )(*args)


def _chunk_masks(c):
    ii = lax.broadcasted_iota(jnp.int32, (c, c), 0)
    jj = lax.broadcasted_iota(jnp.int32, (c, c), 1)
    return ii, jj


def _mix_ab_kernel(qkv_ref, rest_ref, gates_ref, prow_ref, pcol_ref, tri_ref, spread_ref,
                   ggain_ref, rgain_ref, mixed_ref, sd_ref, sr_ref, *, n_sub):
    c = CHUNK
    lb = n_sub * c
    step = pl.program_id(1)

    @pl.when(step == 0)
    def _():
        sd_ref[...] = jnp.zeros_like(sd_ref)
        sr_ref[...] = jnp.zeros_like(sr_ref)

    g = gates_ref[...]
    g_t = g.T
    neg_a_row = -jnp.exp(prow_ref[0:1, :])
    dt_row = prow_ref[1:2, :]
    neg_a_col = -jnp.exp(pcol_ref[0:HEADS, 0:1])
    dt_col = pcol_ref[0:HEADS, 1:2]
    la_cols = neg_a_row * _softplus(g + dt_row)
    beta_cols = jax.nn.sigmoid(g)
    la_rows = neg_a_col * _softplus(g_t[0:HEADS, :] + dt_col)

    tri = tri_ref[...]
    la_hi, la_lo = _split(la_cols)
    g_cols = (jnp.dot(tri, la_hi, preferred_element_type=F32)
              + jnp.dot(tri, la_lo, preferred_element_type=F32))
    lr_hi, lr_lo = _split(jnp.concatenate([la_rows, jnp.zeros_like(la_rows)], axis=0))
    nt = lambda x, y: lax.dot_general(x, y, (((1,), (1,)), ((), ())), preferred_element_type=F32)
    g_rows = nt(lr_hi, tri) + nt(lr_lo, tri)
    lane = lax.broadcasted_iota(jnp.int32, (lb, 128), 1)
    x_hi, x_lo = _split(jnp.where(lane < HEADS, g_cols, beta_cols))
    spread = (jnp.dot(x_hi, spread_ref[...], preferred_element_type=F32)
              + jnp.dot(x_lo, spread_ref[...], preferred_element_type=F32))
    g_wide = spread[:, 0:HEADS * DK]
    beta_wide = spread[:, HEADS * DK:2 * HEADS * DK]
    eg_wide = jnp.exp(g_wide)

    ii, jj = _chunk_masks(c)
    causal = ii >= jj
    strict = ii > jj
    dm = (ii - jj).astype(F32)
    pos_col = lax.broadcasted_iota(jnp.int32, (c, 1), 0).astype(F32)

    ggain = ggain_ref[...]
    rgain = rgain_ref[...]
    base = HEADS * DK
    s_delta = [sd_ref[0, h] for h in range(HEADS)]
    s_ret = [sr_ref[0, h] for h in range(HEADS)]

    items = [(ci, h) for ci in range(n_sub) for h in range(HEADS)]

    rng = lambda ci: (ci * c, (ci + 1) * c)
    pre = {it: {} for it in items}
    pre_b = {it: {} for it in items}

    def gdn_stage(ci, h):
        d = pre[ci, h]
        r0, r1 = rng(ci)
        g_col = g_wide[r0:r1, h * DK:(h + 1) * DK]
        g_row = g_rows[h:h + 1, r0:r1]
        g_last = g_col[c - 1:c, :]
        beta = beta_wide[r0:r1, h * DK:(h + 1) * DK]
        eg = eg_wide[r0:r1, h * DK:(h + 1) * DK]
        dec_causal = jnp.exp(jnp.where(causal, g_col[:, 0:c] - g_row, NEG))
        dec_strict = jnp.where(strict, dec_causal, 0.0)
        q = qkv_ref[r0:r1, h * DK:(h + 1) * DK]
        k = qkv_ref[r0:r1, (HEADS + h) * DK:(HEADS + h + 1) * DK]
        yield
        v = qkv_ref[r0:r1, (2 * HEADS + h) * DK:(2 * HEADS + h + 1) * DK]
        kb = k * beta
        prod = _mm_nt(jnp.concatenate([q, kb], axis=0), k)
        d.update(rhs=jnp.concatenate([v * beta, kb * eg], axis=1), qd=q * eg,
                 kd=k * jnp.exp(g_last - g_col), gl=jnp.exp(g_last))
        yield
        d.update(n=prod[c:2 * c] * dec_strict, qk=prod[0:c] * dec_causal)

    _interleave(gdn_stage(ci, h) for ci, h in items)
    t_invs = _tri_inv_many([pre[it]['n'] for it in items])
    for it, t_inv in zip(items, t_invs):
        pre[it]['sol'] = _mm3(t_inv, pre[it]['rhs'])

    def ret_stage(ci, h):
        d = pre_b[ci, h]
        r0, r1 = rng(ci)
        qr = rest_ref[r0:r1, base + h * DK:base + (h + 1) * DK]
        kr = rest_ref[r0:r1, 2 * base + h * DK:2 * base + (h + 1) * DK]
        lg = LOG_GAMMA[h]
        dec = jnp.exp(jnp.where(causal, dm * lg, NEG))
        qk = _mm_nt(qr, kr)
        d['qd'] = qr * jnp.exp((pos_col + 1.0) * lg)
        kd = kr * jnp.exp((float(c - 1) - pos_col) * lg)
        yield
        vb = rest_ref[r0:r1, 3 * base + h * DK:3 * base + (h + 1) * DK]
        d['intra'] = _mm(qk * dec, vb)
        d['kv'] = _mm_tn(kd, vb)

    _interleave(ret_stage(ci, h) for ci, h in items)

    for ci, h in items:
        pre_b[ci, h]['s_in'] = s_ret[h]
        s_ret[h] = s_ret[h] * math.exp(c * LOG_GAMMA[h]) + pre_b[ci, h]['kv']

    def ret_out_stage(ci, h):
        d = pre_b[ci, h]
        r0, r1 = rng(ci)
        o = _mm(d['qd'], d['s_in']) + d['intra']
        yield
        ms_ = jnp.mean(o * o, axis=-1, keepdims=True)
        yield
        gb = rest_ref[r0:r1, 4 * base + h * DK:4 * base + (h + 1) * DK]
        mixed_ref[r0:r1, base + h * DK:base + (h + 1) * DK] = o * lax.rsqrt(ms_ + EPS) * rgain * gb

    _interleave(ret_out_stage(ci, h) for ci, h in items)

    def delta_step(ci, h):
        d = pre[ci, h]
        r = _mm(jnp.concatenate([d['sol'][:, A_DV:2 * A_DV], d['qd']], axis=0), s_delta[h])
        yield
        v_new = d['sol'][:, 0:A_DV] - r[0:c]
        d['o'] = r[c:2 * c] + _mm(d['qk'], v_new)
        s_delta[h] = s_delta[h] * d['gl'] + _mm_tn(d['kd'], v_new)

    for ci in range(n_sub):
        _interleave(delta_step(ci, h) for h in range(HEADS))

    def gdn_out_stage(ci, h):
        o = pre[ci, h]['o']
        r0, r1 = rng(ci)
        ms_ = jnp.mean(o * o, axis=-1, keepdims=True)
        yield
        ga = rest_ref[r0:r1, h * DK:(h + 1) * DK]
        mixed_ref[r0:r1, h * A_DV:(h + 1) * A_DV] = o * lax.rsqrt(ms_ + EPS) * ggain * ga

    _interleave(gdn_out_stage(ci, h) for ci, h in items)

    for h in range(HEADS):
        sd_ref[0, h] = s_delta[h]
        sr_ref[0, h] = s_ret[h]


def _rope_tables(pos):
    half = DK // 2
    inv = ROPE_BASE ** (-jnp.arange(half, dtype=F32) / half)
    ang = pos.astype(F32)[:, None] * inv[None, :]
    cos, sin = jnp.cos(ang), jnp.sin(ang)
    return jnp.concatenate([cos, cos], axis=-1), jnp.concatenate([-sin, sin], axis=-1)


def _chunk_tri(lb):
    i = jnp.arange(lb)[:, None]
    j = jnp.arange(lb)[None, :]
    return ((i >= j) & (i // CHUNK == j // CHUNK)).astype(BF16)


def _head_spread(groups):
    n = groups * HEADS
    src = jnp.arange(128)[:, None]
    dst = jnp.arange(n * 128)[None, :] // 128
    return (src == dst).astype(BF16)


def _gate_params(a_log, dt_bias):
    prow = jnp.zeros((8, 128), F32).at[0, 0:HEADS].set(a_log).at[1, 0:HEADS].set(dt_bias)
    pcol = jnp.zeros((8, 128), F32).at[0:HEADS, 0].set(a_log).at[0:HEADS, 1].set(dt_bias)
    return prow, pcol


def _mix_ab_prompt(qkv, rest, gates, batch, seq, a_log, dt_bias, gdn_gain, ret_gain):
    lb = AB_SUB * CHUNK
    assert seq % lb == 0
    nc = seq // lb
    prow, pcol = _gate_params(a_log, dt_bias)
    tri = _chunk_tri(lb)
    spread = _head_spread(2)
    tok = lambda b, c: (b * nc + c, 0)
    const = lambda b, c: (0, 0)
    mixed, sd, sr = pl.pallas_call(
        functools.partial(_mix_ab_kernel, n_sub=AB_SUB),
        grid=(batch, nc),
        in_specs=[pl.BlockSpec((lb, A_CONV_CH), tok),
                  pl.BlockSpec((lb, rest.shape[1]), tok),
                  pl.BlockSpec((lb, 128), tok),
                  pl.BlockSpec((8, 128), const),
                  pl.BlockSpec((8, 128), const),
                  pl.BlockSpec(tri.shape, const),
                  pl.BlockSpec(spread.shape, const),
                  pl.BlockSpec((1, A_DV), const),
                  pl.BlockSpec((1, DK), const)],
        out_specs=[pl.BlockSpec((lb, 2 * HEADS * DK), tok),
                   pl.BlockSpec((1, HEADS, DK, A_DV), lambda b, c: (b, 0, 0, 0)),
                   pl.BlockSpec((1, HEADS, DK, DK), lambda b, c: (b, 0, 0, 0))],
        out_shape=[jax.ShapeDtypeStruct((batch * seq, 2 * HEADS * DK), F32),
                   jax.ShapeDtypeStruct((batch, HEADS, DK, A_DV), F32),
                   jax.ShapeDtypeStruct((batch, HEADS, DK, DK), F32)],
        compiler_params=_cparams("parallel", "arbitrary"),
        name="mix_ab_prompt",
    )(qkv, rest, gates, prow, pcol, tri, spread, gdn_gain.reshape(1, A_DV), ret_gain.reshape(1, DK))
    return mixed, sd, sr


def _mix_c_kernel(qk_ref, v_ref, opre_ref, gates_ref, brow_ref, bcol_ref, tri_ref, spread_ref, gain_ref,
                  h_ref, cm_ref, nv_ref, m_ref, *, n_sub):
    c = CHUNK
    lb = n_sub * c
    step = pl.program_id(1)

    @pl.when(step == 0)
    def _():
        cm_ref[...] = jnp.zeros_like(cm_ref)
        nv_ref[...] = jnp.zeros_like(nv_ref)
        m_ref[...] = jnp.zeros_like(m_ref)

    items = [(ci, h) for ci in range(n_sub) for h in range(HEADS)]
    rng = lambda ci: (ci * c, (ci + 1) * c)
    pre = {it: {} for it in items}

    for ci, h in items:
        r0, r1 = rng(ci)
        q = qk_ref[r0:r1, h * DK:(h + 1) * DK]
        k = qk_ref[r0:r1, (HEADS + h) * DK:(HEADS + h + 1) * DK] * (DK ** -0.5)
        pre[ci, h].update(q=q, k=k, qk=_mm_nt(q, k))

    g = gates_ref[...] + brow_ref[0:1, :]
    g_t = gates_ref[...].T[0:2 * HEADS, :] + bcol_ref[0:2 * HEADS, 0:1]
    i_rows = g_t[0:HEADS, :]
    logf_rows = _log_sigmoid(g_t[HEADS:2 * HEADS, :])
    tri = tri_ref[...]
    lf_hi, lf_lo = _split(_log_sigmoid(g))
    b_cols = (jnp.dot(tri, lf_hi, preferred_element_type=F32)
              + jnp.dot(tri, lf_lo, preferred_element_type=F32))
    lr_hi, lr_lo = _split(jnp.concatenate([logf_rows, jnp.zeros_like(logf_rows)], axis=0))
    nt = lambda x, y: lax.dot_general(x, y, (((1,), (1,)), ((), ())), preferred_element_type=F32)
    b_rows = nt(lr_hi, tri) + nt(lr_lo, tri)
    lane = lax.broadcasted_iota(jnp.int32, (lb, 128), 1)
    x_hi, x_lo = _split(jnp.where(lane < HEADS, g, b_cols))
    spread = (jnp.dot(x_hi, spread_ref[...], preferred_element_type=F32)
              + jnp.dot(x_lo, spread_ref[...], preferred_element_type=F32))
    i_wide = spread[:, 0:HEADS * 128]
    b_wide = spread[:, HEADS * 128:2 * HEADS * 128]

    ii, jj = _chunk_masks(c)
    causal = ii >= jj
    gain = gain_ref[...]
    wide2 = lambda x: jnp.concatenate([x, x], axis=-1)

    def gates_stage(ci, h):
        d = pre[ci, h]
        r0, r1 = rng(ci)
        b_col = b_wide[r0:r1, h * 128:(h + 1) * 128]
        i_col = i_wide[r0:r1, h * 128:(h + 1) * 128]
        b_last = b_col[c - 1:c, :]
        d_log = jnp.where(causal, b_col[:, 0:c] - b_rows[h:h + 1, r0:r1] + i_rows[h:h + 1, r0:r1], NEG)
        d.update(b_col=b_col, b_last=b_last, d_log=d_log, k_log=b_last - b_col + i_col)
        yield
        d['d_max'] = jnp.max(d_log, axis=1, keepdims=True)

    _interleave(gates_stage(ci, h) for ci, h in items)

    ms = [m_ref[0, h:h + 1, :] for h in range(HEADS)]
    for ci, h in items:
        d = pre[ci, h]
        inter = d['b_col'] + ms[h]
        m_row = jnp.maximum(inter, d['d_max'])
        m_new = m_row[c - 1:c, :]
        d.update(inter=inter, m_row=m_row, m_old=ms[h], m_new=m_new)
        ms[h] = m_new

    def weights_stage(ci, h):
        d = pre[ci, h]
        r0, r1 = rng(ci)
        d['w_inter'] = jnp.exp(d['inter'] - d['m_row'])
        d['f_state'] = jnp.exp(d['b_last'] + d['m_old'] - d['m_new'])
        yield
        d['w_intra'] = jnp.exp(d['d_log'] - d['m_row'][:, 0:c]) * d['qk']
        d['kw'] = d['k'] * jnp.exp(d['k_log'] - d['m_new'])
        yield
        v = v_ref[r0:r1, h * C_DV:(h + 1) * C_DV]
        d['intra'] = _mm(d['w_intra'], v)
        d['kv'] = _mm_tn(d['kw'], v)
        yield
        d['sum_intra'] = jnp.sum(d['w_intra'], axis=1, keepdims=True)
        d['sum_kw'] = jnp.sum(d['kw'], axis=0, keepdims=True)
        d['inv_floor'] = jnp.exp(-d['m_row'])

    _interleave(weights_stage(ci, h) for ci, h in items)

    cms = [cm_ref[0, h] for h in range(HEADS)]
    nvs = [nv_ref[0, h:h + 1, :] for h in range(HEADS)]
    for ci, h in items:
        d = pre[ci, h]
        d['cm_in'], d['nv_in'] = cms[h], nvs[h]
        cms[h] = cms[h] * wide2(d['f_state']) + d['kv']
        nvs[h] = nvs[h] * d['f_state'] + d['sum_kw']

    def output_stage(ci, h):
        d = pre[ci, h]
        r0, r1 = rng(ci)
        qc = _mm(d['q'], d['cm_in'])
        qn = jnp.sum(d['q'] * d['nv_in'], axis=1, keepdims=True)
        yield
        num = wide2(d['w_inter']) * qc + d['intra']
        den = d['w_inter'] * qn + d['sum_intra']
        hh = num / wide2(jnp.maximum(jnp.abs(den), d['inv_floor']))
        yield
        ms_ = jnp.mean(hh * hh, axis=-1, keepdims=True)
        yield
        op = opre_ref[r0:r1, h * C_DV:(h + 1) * C_DV]
        h_ref[r0:r1, h * C_DV:(h + 1) * C_DV] = hh * lax.rsqrt(ms_ + EPS) * gain * jax.nn.sigmoid(op)

    _interleave(output_stage(ci, h) for ci, h in items)

    for h in range(HEADS):
        cm_ref[0, h] = cms[h]
        nv_ref[0, h:h + 1, :] = nvs[h]
        m_ref[0, h:h + 1, :] = ms[h]


def _bias_params(b_gate):
    brow = jnp.zeros((8, 128), F32).at[0, 0:2 * HEADS].set(b_gate)
    bcol = jnp.zeros((8, 128), F32).at[0:2 * HEADS, 0].set(b_gate)
    return brow, bcol


def _mix_c_prompt(qk, v, opre, gates, batch, seq, b_gate, gain):
    lb = C_SUB * CHUNK
    assert seq % lb == 0
    nc = seq // lb
    brow, bcol = _bias_params(b_gate)
    tri = _chunk_tri(lb)
    spread = _head_spread(2)
    tok = lambda b, c: (b * nc + c, 0)
    const = lambda b, c: (0, 0)
    return pl.pallas_call(
        functools.partial(_mix_c_kernel, n_sub=C_SUB),
        grid=(batch, nc),
        in_specs=[pl.BlockSpec((lb, 2 * HEADS * DK), tok),
                  pl.BlockSpec((lb, HEADS * C_DV), tok),
                  pl.BlockSpec((lb, HEADS * C_DV), tok),
                  pl.BlockSpec((lb, 128), tok),
                  pl.BlockSpec((8, 128), const),
                  pl.BlockSpec((8, 128), const),
                  pl.BlockSpec(tri.shape, const),
                  pl.BlockSpec(spread.shape, const),
                  pl.BlockSpec((1, C_DV), const)],
        out_specs=[pl.BlockSpec((lb, HEADS * C_DV), tok),
                   pl.BlockSpec((1, HEADS, DK, C_DV), lambda b, c: (b, 0, 0, 0)),
                   pl.BlockSpec((1, HEADS, DK), lambda b, c: (b, 0, 0)),
                   pl.BlockSpec((1, HEADS, 128), lambda b, c: (b, 0, 0))],
        out_shape=[jax.ShapeDtypeStruct((batch * seq, HEADS * C_DV), F32),
                   jax.ShapeDtypeStruct((batch, HEADS, DK, C_DV), F32),
                   jax.ShapeDtypeStruct((batch, HEADS, DK), F32),
                   jax.ShapeDtypeStruct((batch, HEADS, 128), F32)],
        compiler_params=_cparams("parallel", "arbitrary"),
        name="mix_c_prompt",
    )(qk, v, opre, gates, brow, bcol, tri, spread, gain.reshape(1, C_DV))


def _row_select(rows, t, new, old):
    return jnp.where(rows == t, new, old)


def _step_ab_kernel(qkv_ref, rest_ref, gates_ref, cos_ref, sin_ref, wconv_ref, prow_ref, ggain_ref, rgain_ref,
                    buf_ref, sd_ref, sr_ref, mixed_ref, nbuf_ref, nsd_ref, nsr_ref):
    tb = qkv_ref.shape[0]
    u = qkv_ref[...]
    w = wconv_ref[...]
    b0 = buf_ref[:, 0:A_CONV_CH]
    b1 = buf_ref[:, A_CONV_CH:2 * A_CONV_CH]
    b2 = buf_ref[:, 2 * A_CONV_CH:3 * A_CONV_CH]
    conv = b0 * w[0:1] + b1 * w[1:2] + b2 * w[2:3] + u * w[3:4]
    nbuf_ref[:, 0:A_CONV_CH] = b1
    nbuf_ref[:, A_CONV_CH:2 * A_CONV_CH] = b2
    nbuf_ref[:, 2 * A_CONV_CH:3 * A_CONV_CH] = u
    act = _silu(conv)
    g = gates_ref[...]
    eg_all = jnp.exp(-jnp.exp(prow_ref[0:1, :]) * _softplus(g + prow_ref[1:2, :]))
    beta_all = jax.nn.sigmoid(g)
    cosf = cos_ref[...]
    sinf = sin_ref[...]
    base = HEADS * DK
    rows = lax.broadcasted_iota(jnp.int32, (tb, DK), 0)
    items = [(t, h) for t in range(tb) for h in range(HEADS)]

    qs = [_l2(act[:, h * DK:(h + 1) * DK]) * (DK ** -0.5) for h in range(HEADS)]
    ks = [_l2(act[:, (HEADS + h) * DK:(HEADS + h + 1) * DK]) for h in range(HEADS)]
    k_s = [jnp.zeros((tb, A_DV), F32) for _ in range(HEADS)]
    for t, h in items:
        k_s[h] = _row_select(rows, t, _mm(ks[h], sd_ref[t, h]), k_s[h])
    v_new = []
    for h in range(HEADS):
        v = act[:, (2 * HEADS + h) * DK:(2 * HEADS + h + 1) * DK]
        v_new.append(beta_all[:, HEADS + h:HEADS + h + 1] * (v - eg_all[:, h:h + 1] * k_s[h]))
    for t, h in items:
        nsd_ref[t, h] = (sd_ref[t, h] * eg_all[t:t + 1, h:h + 1]
                         + _mm3_tn(jnp.where(rows == t, ks[h], 0.0), v_new[h]))
    o_a = [jnp.zeros((tb, A_DV), F32) for _ in range(HEADS)]
    for t, h in items:
        o_a[h] = _row_select(rows, t, _mm(qs[h], nsd_ref[t, h]), o_a[h])
    for h in range(HEADS):
        ga = rest_ref[:, h * DK:(h + 1) * DK]
        mixed_ref[:, h * DK:(h + 1) * DK] = _rms(o_a[h], ggain_ref[...]) * _silu(ga)

    qrs, krs = [], []
    for h in range(HEADS):
        qb = rest_ref[:, base + h * DK:base + (h + 1) * DK]
        kb = rest_ref[:, 2 * base + h * DK:2 * base + (h + 1) * DK]
        qrs.append(qb * cosf + pltpu.roll(qb, DK // 2, 1) * sinf)
        krs.append((kb * cosf + pltpu.roll(kb, DK // 2, 1) * sinf) * (DK ** -0.5))
    for t, h in items:
        vb = rest_ref[:, 3 * base + h * DK:3 * base + (h + 1) * DK]
        nsr_ref[t, h] = (sr_ref[t, h] * math.exp(LOG_GAMMA[h])
                         + _mm3_tn(jnp.where(rows == t, krs[h], 0.0), vb))
    o_b = [jnp.zeros((tb, DK), F32) for _ in range(HEADS)]
    for t, h in items:
        o_b[h] = _row_select(rows, t, _mm(qrs[h], nsr_ref[t, h]), o_b[h])
    for h in range(HEADS):
        gb = rest_ref[:, 4 * base + h * DK:4 * base + (h + 1) * DK]
        mixed_ref[:, base + h * DK:base + (h + 1) * DK] = _rms(o_b[h], rgain_ref[...]) * _silu(gb)


def _mix_ab_sample(qkv, rest, gates, conv_buf, s_delta, s_ret, w_conv, a_log, dt_bias, gdn_gain, ret_gain):
    nb = qkv.shape[0]
    tb = SAMPLE_TB
    assert nb % tb == 0
    cosf, sinf = _rope_tables(PAST_LEN + jnp.arange(1))
    prow, _ = _gate_params(a_log, dt_bias)
    tok = lambda b: (b, 0)
    tok4 = lambda b: (b, 0, 0, 0)
    const = lambda b: (0, 0)
    n_buf = (CONV_W - 1) * A_CONV_CH
    mixed, nbuf, nsd, nsr = pl.pallas_call(
        _step_ab_kernel,
        grid=(nb // tb,),
        in_specs=[pl.BlockSpec((tb, A_CONV_CH), tok),
                  pl.BlockSpec((tb, rest.shape[1]), tok),
                  pl.BlockSpec((tb, 128), tok),
                  pl.BlockSpec((1, DK), const),
                  pl.BlockSpec((1, DK), const),
                  pl.BlockSpec((CONV_W, A_CONV_CH), const),
                  pl.BlockSpec((8, 128), const),
                  pl.BlockSpec((1, A_DV), const),
                  pl.BlockSpec((1, DK), const),
                  pl.BlockSpec((tb, n_buf), tok),
                  pl.BlockSpec((tb, HEADS, DK, A_DV), tok4),
                  pl.BlockSpec((tb, HEADS, DK, DK), tok4)],
        out_specs=[pl.BlockSpec((tb, 2 * HEADS * DK), tok),
                   pl.BlockSpec((tb, n_buf), tok),
                   pl.BlockSpec((tb, HEADS, DK, A_DV), tok4),
                   pl.BlockSpec((tb, HEADS, DK, DK), tok4)],
        out_shape=[jax.ShapeDtypeStruct((nb, 2 * HEADS * DK), F32),
                   jax.ShapeDtypeStruct((nb, n_buf), F32),
                   jax.ShapeDtypeStruct((nb, HEADS, DK, A_DV), F32),
                   jax.ShapeDtypeStruct((nb, HEADS, DK, DK), F32)],
        compiler_params=_cparams("parallel"),
        name="mix_ab_sample",
    )(qkv, rest, gates, cosf, sinf, w_conv, prow, gdn_gain.reshape(1, A_DV), ret_gain.reshape(1, DK),
      conv_buf.reshape(nb, n_buf), s_delta, s_ret)
    return mixed, nbuf.reshape(nb, CONV_W - 1, A_CONV_CH), nsd, nsr


def _step_c_kernel(qk_ref, v_ref, opre_ref, gates_ref, brow_ref, gain_ref, cm_ref, nv_ref, m_ref,
                   h_ref, ncm_ref, nnv_ref, nm_ref):
    tb = qk_ref.shape[0]
    g = gates_ref[...] + brow_ref[0:1, :]
    logf = _log_sigmoid(g)
    rows = lax.broadcasted_iota(jnp.int32, (tb, DK), 0)
    items = [(t, h) for t in range(tb) for h in range(HEADS)]
    qs, kws, fs, ms_new, nvs_new = [], [], [], [], []
    for h in range(HEADS):
        i_pre = g[:, h:h + 1]
        q = qk_ref[:, h * DK:(h + 1) * DK]
        k = qk_ref[:, (HEADS + h) * DK:(HEADS + h + 1) * DK] * (DK ** -0.5)
        inter = logf[:, HEADS + h:HEADS + h + 1] + m_ref[:, h:h + 1]
        m_new = jnp.maximum(inter, i_pre)
        f_state = jnp.exp(inter - m_new)
        kw = k * jnp.exp(i_pre - m_new)
        nv_new = nv_ref[:, h * DK:(h + 1) * DK] * f_state + kw
        nnv_ref[:, h * DK:(h + 1) * DK] = nv_new
        nm_ref[:, h:h + 1] = m_new
        qs.append(q)
        kws.append(kw)
        fs.append(f_state)
        ms_new.append(m_new)
        nvs_new.append(nv_new)
    for t, h in items:
        v = v_ref[:, h * C_DV:(h + 1) * C_DV]
        ncm_ref[t, h] = (cm_ref[t, h] * fs[h][t:t + 1, :]
                         + _mm3_tn(jnp.where(rows == t, kws[h], 0.0), v))
    rows_v = lax.broadcasted_iota(jnp.int32, (tb, C_DV), 0)
    nums = [jnp.zeros((tb, C_DV), F32) for _ in range(HEADS)]
    for t, h in items:
        nums[h] = _row_select(rows_v, t, _mm(qs[h], ncm_ref[t, h]), nums[h])
    for h in range(HEADS):
        den = jnp.sum(qs[h] * nvs_new[h], axis=1, keepdims=True)
        hh = nums[h] / jnp.maximum(jnp.abs(den), jnp.exp(-ms_new[h]))
        op = opre_ref[:, h * C_DV:(h + 1) * C_DV]
        h_ref[:, h * C_DV:(h + 1) * C_DV] = _rms(hh, gain_ref[...]) * jax.nn.sigmoid(op)


def _mix_c_sample(qk, v, opre, gates, s_c, s_n, s_m, b_gate, gain):
    nb = qk.shape[0]
    tb = SAMPLE_TB
    assert nb % tb == 0
    brow, _ = _bias_params(b_gate)
    tok = lambda b: (b, 0)
    tok4 = lambda b: (b, 0, 0, 0)
    const = lambda b: (0, 0)
    hh, ncm, nnv, nm = pl.pallas_call(
        _step_c_kernel,
        grid=(nb // tb,),
        in_specs=[pl.BlockSpec((tb, 2 * HEADS * DK), tok),
                  pl.BlockSpec((tb, HEADS * C_DV), tok),
                  pl.BlockSpec((tb, HEADS * C_DV), tok),
                  pl.BlockSpec((tb, 128), tok),
                  pl.BlockSpec((8, 128), const),
                  pl.BlockSpec((1, C_DV), const),
                  pl.BlockSpec((tb, HEADS, DK, C_DV), tok4),
                  pl.BlockSpec((tb, HEADS * DK), tok),
                  pl.BlockSpec((tb, HEADS), tok)],
        out_specs=[pl.BlockSpec((tb, HEADS * C_DV), tok),
                   pl.BlockSpec((tb, HEADS, DK, C_DV), tok4),
                   pl.BlockSpec((tb, HEADS * DK), tok),
                   pl.BlockSpec((tb, HEADS), tok)],
        out_shape=[jax.ShapeDtypeStruct((nb, HEADS * C_DV), F32),
                   jax.ShapeDtypeStruct((nb, HEADS, DK, C_DV), F32),
                   jax.ShapeDtypeStruct((nb, HEADS * DK), F32),
                   jax.ShapeDtypeStruct((nb, HEADS), F32)],
        compiler_params=_cparams("parallel"),
        name="mix_c_sample",
    )(qk, v, opre, gates, brow, gain.reshape(1, C_DV), s_c, s_n.reshape(nb, HEADS * DK), s_m)
    return hh, ncm, nnv.reshape(nb, HEADS, DK), nm


A_SPLITS = ((A_CONV_CH,), (5 * HEADS * DK,), (128,))
C_SPLITS = ((2 * HEADS * DK, HEADS * C_DV, HEADS * C_DV), (128,))


def _gate_columns(w_gates):
    return jnp.pad(w_gates, ((0, 0), (0, 128 - w_gates.shape[1]))).astype(BF16)


def _prep_w_in_a(w):
    n_gate = 2 * HEADS
    return (w[:, :A_CONV_CH].astype(BF16), w[:, A_CONV_CH + n_gate:].astype(BF16),
            _gate_columns(w[:, A_CONV_CH:A_CONV_CH + n_gate]))


def _prep_w_in_c(w):
    n_main = 2 * HEADS * DK + 2 * HEADS * C_DV
    return (w[:, :n_main].astype(BF16), _gate_columns(w[:, n_main:]))


def kernel(x_prompt, x_sample, state_conv_a, state_delta_a, state_ret_b, state_mlstm_C, state_mlstm_n, state_mlstm_m,
           norm_mix_a, w_in_a, w_conv_a, a_log, dt_bias, gdn_gain, ret_gain, w_out_a,
           norm_mix_c, w_in_c, b_gate_c, mlstm_gain, w_out_c, norm_ffn, w_up, w_down, final_gain):
    batch, seq, _ = x_prompt.shape
    n_s = x_sample.shape[0] * x_sample.shape[1]
    tm = PROMPT_TM
    w_a = _prep_w_in_a(w_in_a[0])
    w_c = _prep_w_in_c(w_in_c[0])
    w_ups = [w_up[i].astype(BF16) for i in range(w_up.shape[0])]
    w_downs = [w_down[i].astype(BF16) for i in range(w_down.shape[0])]
    h = x_prompt.reshape(batch * seq, D_MODEL)
    h_s = x_sample.reshape(n_s, D_MODEL)

    (qkv, rest, gates, conv), (qkv_s, rest_s, gates_s) = _proj_a_prompt(
        h, h_s, norm_mix_a[0], w_a, w_conv_a[0], batch, seq, tm)
    mixed, sd, sr = _mix_ab_prompt(qkv, rest, gates, batch, seq, a_log[0], dt_bias[0], gdn_gain[0], ret_gain[0])
    mixed_s, conv_s, sd_s, sr_s = _mix_ab_sample(qkv_s, rest_s, gates_s, state_conv_a[0], state_delta_a[0],
                                                 state_ret_b[0], w_conv_a[0], a_log[0], dt_bias[0], gdn_gain[0],
                                                 ret_gain[0])
    h, h_s = _out_mlp(h, mixed, h_s, mixed_s, w_out_a[0].astype(BF16), norm_ffn[0], w_ups[0], w_downs[0], None, tm)

    (qk, v, opre, gates_c), (qk_s, v_s, opre_s, gates_cs) = _norm_proj(h, h_s, norm_mix_c[0], w_c, C_SPLITS, tm)
    hm, cm, nv, m = _mix_c_prompt(qk, v, opre, gates_c, batch, seq, b_gate_c[0], mlstm_gain[0])
    hm_s, cm_s, nv_s, m_s = _mix_c_sample(qk_s, v_s, opre_s, gates_cs, state_mlstm_C[0], state_mlstm_n[0],
                                          state_mlstm_m[0], b_gate_c[0], mlstm_gain[0])
    y, y_s = _out_mlp(h, hm, h_s, hm_s, w_out_c[0].astype(BF16), norm_ffn[1], w_ups[1], w_downs[1], final_gain, tm)

    return (y.reshape(x_prompt.shape), y_s.reshape(x_sample.shape),
            conv[None], sd[None], sr[None], cm[None], nv[None], m[:, :, 0][None],
            conv_s[None], sd_s[None], sr_s[None], cm_s[None], nv_s[None], m_s[None])
```

```python
import functools
import math

import jax
import jax.numpy as jnp
from jax import lax
from jax.experimental import pallas as pl
from jax.experimental.pallas import tpu as pltpu

F32 = jnp.float32
BF16 = jnp.bfloat16

D_MODEL = 1024
D_FF = 4 * D_MODEL
CHUNK = 64
EPS = 1e-6
NEG = -1e30
HEADS = 4
DK = 128
A_DV = 128
C_DV = 256
CONV_W = 4
A_CONV_CH = 3 * HEADS * DK
ROPE_BASE = 10000.0
PAST_LEN = 16384
LOG_GAMMA = tuple(math.log1p(-(2.0 ** (-5.0 - h))) for h in range(HEADS))

VMEM_LIMIT_BYTES = 56 * 1024 * 1024
PROMPT_TM = 512
AB_SUB = 4
C_SUB = 4
SAMPLE_TB = 8


def _cparams(*sem):
    return pltpu.CompilerParams(dimension_semantics=sem, vmem_limit_bytes=VMEM_LIMIT_BYTES)


def _mm(a, b):
    return jnp.dot(a.astype(BF16), b.astype(BF16), preferred_element_type=F32)


def _mm_nt(a, b):
    return lax.dot_general(a.astype(BF16), b.astype(BF16), (((1,), (1,)), ((), ())), preferred_element_type=F32)


def _mm_tn(a, b):
    return lax.dot_general(a.astype(BF16), b.astype(BF16), (((0,), (0,)), ((), ())), preferred_element_type=F32)


def _split(a):
    hi = a.astype(BF16)
    lo = (a - hi.astype(F32)).astype(BF16)
    return hi, lo


def _mm3(a, b):
    ah, al = _split(a)
    bh, bl = _split(b)
    d = lambda x, y: jnp.dot(x, y, preferred_element_type=F32)
    return d(ah, bh) + (d(ah, bl) + d(al, bh))


def _mm3_tn(a, b):
    ah, al = _split(a)
    bh, bl = _split(b)
    d = lambda x, y: lax.dot_general(x, y, (((0,), (0,)), ((), ())), preferred_element_type=F32)
    return d(ah, bh) + (d(ah, bl) + d(al, bh))


def _softplus(x):
    return jnp.maximum(x, 0.0) + jnp.log1p(jnp.exp(-jnp.abs(x)))


def _log_sigmoid(x):
    return -_softplus(-x)


def _silu(x):
    return x * jax.nn.sigmoid(x)


def _rms(x, gain):
    return x * lax.rsqrt(jnp.mean(x * x, axis=-1, keepdims=True) + EPS) * gain


def _l2(t):
    return t * lax.rsqrt(jnp.sum(t * t, axis=-1, keepdims=True) + EPS)


def _interleave(gens):
    gens = list(gens)
    while gens:
        alive = []
        for gen in gens:
            try:
                next(gen)
                alive.append(gen)
            except StopIteration:
                pass
        gens = alive


def _tri_inv_many(ns):
    c = ns[0].shape[0]
    ii = lax.broadcasted_iota(jnp.int32, (c, c), 0)
    jj = lax.broadcasted_iota(jnp.int32, (c, c), 1)
    eye = (ii == jj).astype(F32)
    ts = [eye - jnp.where((ii >> 1) == (jj >> 1), n, 0.0) for n in ns]
    for lvl in range(1, int(math.log2(c))):
        off = ((ii >> (lvl + 1)) == (jj >> (lvl + 1))) & ((ii >> lvl) != (jj >> lvl))
        ys = [_mm(jnp.where(off, n, 0.0), t) for n, t in zip(ns, ts)]
        ts = [t - _mm(t, y) for t, y in zip(ts, ys)]
    return [t - eye for t in ts]


def _project_rows(x_ref, g_ref, w_refs, out_refs, splits):
    xn = _rms(x_ref[...], g_ref[...]).astype(BF16)
    o = 0
    for w_ref, widths in zip(w_refs, splits):
        off = 0
        for n in widths:
            out_refs[o][...] = jnp.dot(xn, w_ref[:, off:off + n], preferred_element_type=F32)
            off += n
            o += 1


def _norm_proj_kernel(x_ref, xs_ref, g_ref, *refs, splits):
    n_out = sum(len(s) for s in splits)
    w_refs = refs[:len(splits)]
    out_refs = refs[len(splits):len(splits) + n_out]
    outs_refs = refs[len(splits) + n_out:]
    _project_rows(x_ref, g_ref, w_refs, out_refs, splits)

    @pl.when(pl.program_id(0) == 0)
    def _():
        _project_rows(xs_ref, g_ref, w_refs, outs_refs, splits)


def _norm_proj(x, x_s, gain, weights, splits, tm):
    t, ts = x.shape[0], x_s.shape[0]
    assert t % tm == 0 and all(sum(s) == w.shape[1] for w, s in zip(weights, splits))
    widths = [n for s in splits for n in s]
    row = lambda i: (i, 0)
    const = lambda i: (0, 0)
    outs = pl.pallas_call(
        functools.partial(_norm_proj_kernel, splits=splits),
        grid=(t // tm,),
        in_specs=[pl.BlockSpec((tm, D_MODEL), row), pl.BlockSpec((ts, D_MODEL), const),
                  pl.BlockSpec((1, D_MODEL), const)]
                 + [pl.BlockSpec(w.shape, const, pipeline_mode=pl.Buffered(1)) for w in weights],
        out_specs=[pl.BlockSpec((tm, n), row) for n in widths] + [pl.BlockSpec((ts, n), const) for n in widths],
        out_shape=[jax.ShapeDtypeStruct((t, n), F32) for n in widths]
                  + [jax.ShapeDtypeStruct((ts, n), F32) for n in widths],
        compiler_params=_cparams("arbitrary"),
        name="norm_proj",
    )(x, x_s, gain.reshape(1, D_MODEL), *weights)
    return outs[:len(widths)], outs[len(widths):]


def _proj_a_prompt_kernel(x_ref, xs_ref, g_ref, wqkv_ref, wrest_ref, wgate_ref, wconv_ref, cos_ref, sin_ref,
                          qkv_ref, rest_ref, gates_ref, conv_ref, qkvs_ref, rests_ref, gatess_ref,
                          xp_ref, raw_ref, *, tiles_per_seq):
    tm = x_ref.shape[0]
    hd = HEADS * DK

    @pl.when(pl.program_id(0) == 0)
    def _():
        _project_rows(xs_ref, g_ref, (wqkv_ref, wrest_ref, wgate_ref), (qkvs_ref, rests_ref, gatess_ref), A_SPLITS)

    xn = _rms(x_ref[...], g_ref[...]).astype(BF16)

    @pl.when(pl.program_id(0) % tiles_per_seq == 0)
    def _():
        xp_ref[0:8, :] = jnp.zeros((8, A_CONV_CH), F32)

    w = wconv_ref[...]
    base = A_CONV_CH

    def conv_part(part, slot, r0, r1):
        c0, c1 = part * hd, (part + 1) * hd
        raw = raw_ref[slot, r0:r1, :]
        xp_ref[8 + r0:8 + r1, c0:c1] = raw
        conv = (xp_ref[5 + r0:5 + r1, c0:c1] * w[0:1, c0:c1] + xp_ref[6 + r0:6 + r1, c0:c1] * w[1:2, c0:c1]
                + xp_ref[7 + r0:7 + r1, c0:c1] * w[2:3, c0:c1] + raw * w[3:4, c0:c1])
        if r1 == tm:
            rows = r1 - r0
            xp_ref[0:8, c0:c1] = raw[rows - 8:rows, :]
            conv_ref[0, :, c0:c1] = raw[rows - 3:rows, :]
        act = _silu(conv)
        if part == 2:
            qkv_ref[r0:r1, c0:c1] = act
        else:
            scale = DK ** -0.5 if part == 0 else 1.0
            for h in range(HEADS):
                qkv_ref[r0:r1, c0 + h * DK:c0 + (h + 1) * DK] = _l2(act[:, h * DK:(h + 1) * DK]) * scale

    def rest_part(part, slot, r0, r1):
        c0, c1 = part * hd, (part + 1) * hd
        raw = raw_ref[slot, r0:r1, :]
        if part in (0, 4):
            rest_ref[r0:r1, c0:c1] = _silu(raw)
        elif part == 3:
            rest_ref[r0:r1, c0:c1] = raw
        else:
            scale = 1.0 if part == 1 else DK ** -0.5
            for h in range(HEADS):
                t = raw[:, h * DK:(h + 1) * DK]
                rest_ref[r0:r1, c0 + h * DK:c0 + (h + 1) * DK] = (
                    (t * cos_ref[r0:r1, :] + pltpu.roll(t, DK // 2, 1) * sin_ref[r0:r1, :]) * scale)

    def gates_part(_, slot, r0, r1):
        gates_ref[r0:r1, :] = raw_ref[slot, r0:r1, 0:128]

    groups = [(wqkv_ref, 0, hd, functools.partial(conv_part, 0)),
              (wrest_ref, 3 * hd, hd, functools.partial(rest_part, 3)),
              (wqkv_ref, hd, hd, functools.partial(conv_part, 1)),
              (wrest_ref, 0, hd, functools.partial(rest_part, 0)),
              (wqkv_ref, 2 * hd, hd, functools.partial(conv_part, 2)),
              (wrest_ref, hd, hd, functools.partial(rest_part, 1)),
              (wgate_ref, 0, 128, functools.partial(gates_part, 0)),
              (wrest_ref, 2 * hd, hd, functools.partial(rest_part, 2)),
              (wrest_ref, 4 * hd, hd, functools.partial(rest_part, 4))]

    def project(n):
        w_ref, col0, width, _ = groups[n]
        for j in range(0, width, 256):
            wj = min(256, width - j)
            raw_ref[n % 2, :, j:j + wj] = jnp.dot(xn, w_ref[:, col0 + j:col0 + j + wj],
                                                  preferred_element_type=F32)
            yield

    def epilogue(n):
        for r0 in range(0, tm, 128):
            groups[n][3](n % 2, r0, min(r0 + 128, tm))
            yield

    _interleave([project(0)])
    for n in range(len(groups)):
        _interleave([epilogue(n)] + ([project(n + 1)] if n + 1 < len(groups) else []))


def _proj_a_prompt(x, x_s, gain, weights, w_conv, batch, seq, tm):
    t, ts = x.shape[0], x_s.shape[0]
    assert t % tm == 0 and seq % tm == 0
    tiles_per_seq = seq // tm
    cosf, sinf = _rope_tables(jnp.arange(seq))
    row = lambda i: (i, 0)
    const = lambda i: (0, 0)
    pos = lambda i: (i % tiles_per_seq, 0)
    n_rest = 5 * HEADS * DK
    outs = pl.pallas_call(
        functools.partial(_proj_a_prompt_kernel, tiles_per_seq=tiles_per_seq),
        grid=(t // tm,),
        in_specs=[pl.BlockSpec((tm, D_MODEL), row),
                  pl.BlockSpec((ts, D_MODEL), const),
                  pl.BlockSpec((1, D_MODEL), const)]
                 + [pl.BlockSpec(w.shape, const, pipeline_mode=pl.Buffered(1)) for w in weights]
                 + [pl.BlockSpec((CONV_W, A_CONV_CH), const),
                    pl.BlockSpec((tm, DK), pos),
                    pl.BlockSpec((tm, DK), pos)],
        out_specs=[pl.BlockSpec((tm, A_CONV_CH), row),
                   pl.BlockSpec((tm, n_rest), row),
                   pl.BlockSpec((tm, 128), row),
                   pl.BlockSpec((1, CONV_W - 1, A_CONV_CH), lambda i: (i // tiles_per_seq, 0, 0)),
                   pl.BlockSpec((ts, A_CONV_CH), const),
                   pl.BlockSpec((ts, n_rest), const),
                   pl.BlockSpec((ts, 128), const)],
        out_shape=[jax.ShapeDtypeStruct((t, A_CONV_CH), F32),
                   jax.ShapeDtypeStruct((t, n_rest), F32),
                   jax.ShapeDtypeStruct((t, 128), F32),
                   jax.ShapeDtypeStruct((batch, CONV_W - 1, A_CONV_CH), F32),
                   jax.ShapeDtypeStruct((ts, A_CONV_CH), F32),
                   jax.ShapeDtypeStruct((ts, n_rest), F32),
                   jax.ShapeDtypeStruct((ts, 128), F32)],
        scratch_shapes=[pltpu.VMEM((tm + 8, A_CONV_CH), F32), pltpu.VMEM((2, tm, HEADS * DK), F32)],
        compiler_params=_cparams("arbitrary"),
        name="proj_a_prompt",
    )(x, x_s, gain.reshape(1, D_MODEL), *weights, w_conv, cosf, sinf)
    return outs[:4], outs[4:]


def _out_mlp_kernel(h_ref, m_ref, hs_ref, ms_ref, wout_ref, gffn_ref, wup_ref, wdown_ref, *rest, final):
    if final:
        gfin_ref, o_ref, os_ref = rest
    else:
        o_ref, os_ref = rest

    def block(h_r, m_r, o_r):
        h = h_r[...] + jnp.dot(m_r[...].astype(BF16), wout_ref[...], preferred_element_type=F32)
        xn = _rms(h, gffn_ref[...]).astype(BF16)
        acc = h
        step = 1024
        for j in range(D_FF // step):
            hid = jnp.dot(xn, wup_ref[:, j * step:(j + 1) * step], preferred_element_type=F32)
            hid = jnp.maximum(hid, 0.0)
            acc = acc + jnp.dot((hid * hid).astype(BF16), wdown_ref[j * step:(j + 1) * step, :],
                                preferred_element_type=F32)
        if final:
            acc = _rms(acc, gfin_ref[...])
        o_r[...] = acc

    block(h_ref, m_ref, o_ref)

    @pl.when(pl.program_id(0) == 0)
    def _():
        block(hs_ref, ms_ref, os_ref)


def _out_mlp(h, mix, h_s, mix_s, w_out, g_ffn, w_up, w_down, layer, g_final, tm):
    t, ts = h.shape[0], h_s.shape[0]
    assert t % tm == 0
    final = g_final is not None
    row = lambda i: (i, 0)
    const = lambda i: (0, 0)
    this_layer = lambda i: (layer, 0, 0)
    once = pl.Buffered(1)
    in_specs = [pl.BlockSpec((tm, D_MODEL), row), pl.BlockSpec((tm, mix.shape[1]), row),
                pl.BlockSpec((ts, D_MODEL), const), pl.BlockSpec((ts, mix_s.shape[1]), const),
                pl.BlockSpec(w_out.shape, const, pipeline_mode=once), pl.BlockSpec((1, D_MODEL), const),
                pl.BlockSpec((None,) + w_up.shape[1:], this_layer, pipeline_mode=once),
                pl.BlockSpec((None,) + w_down.shape[1:], this_layer, pipeline_mode=once)]
    args = [h, mix, h_s, mix_s, w_out, g_ffn.reshape(1, D_MODEL), w_up, w_down]
    if final:
        in_specs.append(pl.BlockSpec((1, D_MODEL), const))
        args.append(g_final.reshape(1, D_MODEL))
    return pl.pallas_call(
        functools.partial(_out_mlp_kernel, final=final),
        grid=(t // tm,),
        in_specs=in_specs,
        out_specs=[pl.BlockSpec((tm, D_MODEL), row), pl.BlockSpec((ts, D_MODEL), const)],
        out_shape=[jax.ShapeDtypeStruct((t, D_MODEL), F32), jax.ShapeDtypeStruct((ts, D_MODEL), F32)],
        compiler_params=_cparams("arbitrary"),
        name="out_mlp",
    )(*args)


def _chunk_masks(c):
    ii = lax.broadcasted_iota(jnp.int32, (c, c), 0)
    jj = lax.broadcasted_iota(jnp.int32, (c, c), 1)
    return ii, jj


def _mix_ab_kernel(qkv_ref, rest_ref, gates_ref, prow_ref, pcol_ref, tri_ref, spread_ref,
                   ggain_ref, rgain_ref, mixed_ref, sd_ref, sr_ref, *, n_sub):
    c = CHUNK
    lb = n_sub * c
    step = pl.program_id(1)

    @pl.when(step == 0)
    def _():
        sd_ref[...] = jnp.zeros_like(sd_ref)
        sr_ref[...] = jnp.zeros_like(sr_ref)

    g = gates_ref[...]
    g_t = g.T
    neg_a_row = -jnp.exp(prow_ref[0:1, :])
    dt_row = prow_ref[1:2, :]
    neg_a_col = -jnp.exp(pcol_ref[0:HEADS, 0:1])
    dt_col = pcol_ref[0:HEADS, 1:2]
    la_cols = neg_a_row * _softplus(g + dt_row)
    beta_cols = jax.nn.sigmoid(g)
    la_rows = neg_a_col * _softplus(g_t[0:HEADS, :] + dt_col)

    tri = tri_ref[...]
    la_hi, la_lo = _split(la_cols)
    g_cols = (jnp.dot(tri, la_hi, preferred_element_type=F32)
              + jnp.dot(tri, la_lo, preferred_element_type=F32))
    lr_hi, lr_lo = _split(jnp.concatenate([la_rows, jnp.zeros_like(la_rows)], axis=0))
    nt = lambda x, y: lax.dot_general(x, y, (((1,), (1,)), ((), ())), preferred_element_type=F32)
    g_rows = nt(lr_hi, tri) + nt(lr_lo, tri)
    lane = lax.broadcasted_iota(jnp.int32, (lb, 128), 1)
    x_hi, x_lo = _split(jnp.where(lane < HEADS, g_cols, beta_cols))
    spread = (jnp.dot(x_hi, spread_ref[...], preferred_element_type=F32)
              + jnp.dot(x_lo, spread_ref[...], preferred_element_type=F32))
    g_wide = spread[:, 0:HEADS * DK]
    beta_wide = spread[:, HEADS * DK:2 * HEADS * DK]
    eg_wide = jnp.exp(g_wide)

    ii, jj = _chunk_masks(c)
    causal = ii >= jj
    strict = ii > jj
    dm = (ii - jj).astype(F32)
    pos_col = lax.broadcasted_iota(jnp.int32, (c, 1), 0).astype(F32)

    ggain = ggain_ref[...]
    rgain = rgain_ref[...]
    base = HEADS * DK
    s_delta = [sd_ref[0, h] for h in range(HEADS)]
    s_ret = [sr_ref[0, h] for h in range(HEADS)]

    items = [(ci, h) for ci in range(n_sub) for h in range(HEADS)]

    rng = lambda ci: (ci * c, (ci + 1) * c)
    pre = {it: {} for it in items}
    pre_b = {it: {} for it in items}

    def gdn_stage(ci, h):
        d = pre[ci, h]
        r0, r1 = rng(ci)
        g_col = g_wide[r0:r1, h * DK:(h + 1) * DK]
        g_row = g_rows[h:h + 1, r0:r1]
        g_last = g_col[c - 1:c, :]
        beta = beta_wide[r0:r1, h * DK:(h + 1) * DK]
        eg = eg_wide[r0:r1, h * DK:(h + 1) * DK]
        dec_causal = jnp.exp(jnp.where(causal, g_col[:, 0:c] - g_row, NEG))
        dec_strict = jnp.where(strict, dec_causal, 0.0)
        q = qkv_ref[r0:r1, h * DK:(h + 1) * DK]
        k = qkv_ref[r0:r1, (HEADS + h) * DK:(HEADS + h + 1) * DK]
        yield
        v = qkv_ref[r0:r1, (2 * HEADS + h) * DK:(2 * HEADS + h + 1) * DK]
        kb = k * beta
        prod = _mm_nt(jnp.concatenate([q, kb], axis=0), k)
        d.update(rhs=jnp.concatenate([v * beta, kb * eg], axis=1), qd=q * eg,
                 kd=k * jnp.exp(g_last - g_col), gl=jnp.exp(g_last))
        yield
        d.update(n=prod[c:2 * c] * dec_strict, qk=prod[0:c] * dec_causal)

    _interleave(gdn_stage(ci, h) for ci, h in items)
    t_offs = _tri_inv_many([pre[it]['n'] for it in items])
    for it, t_off in zip(items, t_offs):
        pre[it]['sol'] = pre[it]['rhs'] + _mm(t_off, pre[it]['rhs'])

    def ret_stage(ci, h):
        d = pre_b[ci, h]
        r0, r1 = rng(ci)
        qr = rest_ref[r0:r1, base + h * DK:base + (h + 1) * DK]
        kr = rest_ref[r0:r1, 2 * base + h * DK:2 * base + (h + 1) * DK]
        lg = LOG_GAMMA[h]
        dec = jnp.exp(jnp.where(causal, dm * lg, NEG))
        qk = _mm_nt(qr, kr)
        d['qd'] = qr * jnp.exp((pos_col + 1.0) * lg)
        kd = kr * jnp.exp((float(c - 1) - pos_col) * lg)
        yield
        vb = rest_ref[r0:r1, 3 * base + h * DK:3 * base + (h + 1) * DK]
        d['intra'] = _mm(qk * dec, vb)
        d['kv'] = _mm_tn(kd, vb)

    _interleave(ret_stage(ci, h) for ci, h in items)

    for ci, h in items:
        pre_b[ci, h]['s_in'] = s_ret[h]
        s_ret[h] = s_ret[h] * math.exp(c * LOG_GAMMA[h]) + pre_b[ci, h]['kv']

    def ret_out_stage(ci, h):
        d = pre_b[ci, h]
        r0, r1 = rng(ci)
        o = _mm(d['qd'], d['s_in']) + d['intra']
        yield
        ms_ = jnp.mean(o * o, axis=-1, keepdims=True)
        yield
        gb = rest_ref[r0:r1, 4 * base + h * DK:4 * base + (h + 1) * DK]
        mixed_ref[r0:r1, base + h * DK:base + (h + 1) * DK] = o * lax.rsqrt(ms_ + EPS) * rgain * gb

    _interleave(ret_out_stage(ci, h) for ci, h in items)

    def delta_step(ci, h):
        d = pre[ci, h]
        r = _mm(jnp.concatenate([d['sol'][:, A_DV:2 * A_DV], d['qd']], axis=0), s_delta[h])
        yield
        v_new = d['sol'][:, 0:A_DV] - r[0:c]
        d['o'] = r[c:2 * c] + _mm(d['qk'], v_new)
        s_delta[h] = s_delta[h] * d['gl'] + _mm_tn(d['kd'], v_new)

    for ci in range(n_sub):
        _interleave(delta_step(ci, h) for h in range(HEADS))

    def gdn_out_stage(ci, h):
        o = pre[ci, h]['o']
        r0, r1 = rng(ci)
        ms_ = jnp.mean(o * o, axis=-1, keepdims=True)
        yield
        ga = rest_ref[r0:r1, h * DK:(h + 1) * DK]
        mixed_ref[r0:r1, h * A_DV:(h + 1) * A_DV] = o * lax.rsqrt(ms_ + EPS) * ggain * ga

    _interleave(gdn_out_stage(ci, h) for ci, h in items)

    for h in range(HEADS):
        sd_ref[0, h] = s_delta[h]
        sr_ref[0, h] = s_ret[h]


def _rope_tables(pos):
    half = DK // 2
    inv = ROPE_BASE ** (-jnp.arange(half, dtype=F32) / half)
    ang = pos.astype(F32)[:, None] * inv[None, :]
    cos, sin = jnp.cos(ang), jnp.sin(ang)
    return jnp.concatenate([cos, cos], axis=-1), jnp.concatenate([-sin, sin], axis=-1)


def _chunk_tri(lb):
    i = jnp.arange(lb)[:, None]
    j = jnp.arange(lb)[None, :]
    return ((i >= j) & (i // CHUNK == j // CHUNK)).astype(BF16)


def _head_spread(groups):
    n = groups * HEADS
    src = jnp.arange(128)[:, None]
    dst = jnp.arange(n * 128)[None, :] // 128
    return (src == dst).astype(BF16)


def _gate_params(a_log, dt_bias):
    prow = jnp.zeros((8, 128), F32).at[0, 0:HEADS].set(a_log).at[1, 0:HEADS].set(dt_bias)
    pcol = jnp.zeros((8, 128), F32).at[0:HEADS, 0].set(a_log).at[0:HEADS, 1].set(dt_bias)
    return prow, pcol


def _mix_ab_prompt(qkv, rest, gates, batch, seq, a_log, dt_bias, gdn_gain, ret_gain):
    lb = AB_SUB * CHUNK
    assert seq % lb == 0
    nc = seq // lb
    prow, pcol = _gate_params(a_log, dt_bias)
    tri = _chunk_tri(lb)
    spread = _head_spread(2)
    tok = lambda b, c: (b * nc + c, 0)
    const = lambda b, c: (0, 0)
    mixed, sd, sr = pl.pallas_call(
        functools.partial(_mix_ab_kernel, n_sub=AB_SUB),
        grid=(batch, nc),
        in_specs=[pl.BlockSpec((lb, A_CONV_CH), tok),
                  pl.BlockSpec((lb, rest.shape[1]), tok),
                  pl.BlockSpec((lb, 128), tok),
                  pl.BlockSpec((8, 128), const),
                  pl.BlockSpec((8, 128), const),
                  pl.BlockSpec(tri.shape, const),
                  pl.BlockSpec(spread.shape, const),
                  pl.BlockSpec((1, A_DV), const),
                  pl.BlockSpec((1, DK), const)],
        out_specs=[pl.BlockSpec((lb, 2 * HEADS * DK), tok),
                   pl.BlockSpec((1, HEADS, DK, A_DV), lambda b, c: (b, 0, 0, 0)),
                   pl.BlockSpec((1, HEADS, DK, DK), lambda b, c: (b, 0, 0, 0))],
        out_shape=[jax.ShapeDtypeStruct((batch * seq, 2 * HEADS * DK), F32),
                   jax.ShapeDtypeStruct((batch, HEADS, DK, A_DV), F32),
                   jax.ShapeDtypeStruct((batch, HEADS, DK, DK), F32)],
        compiler_params=_cparams("parallel", "arbitrary"),
        name="mix_ab_prompt",
    )(qkv, rest, gates, prow, pcol, tri, spread, gdn_gain.reshape(1, A_DV), ret_gain.reshape(1, DK))
    return mixed, sd, sr


def _mix_c_kernel(qk_ref, v_ref, opre_ref, gates_ref, brow_ref, bcol_ref, tri_ref, spread_ref, gain_ref,
                  h_ref, cm_ref, nv_ref, m_ref, *, n_sub):
    c = CHUNK
    lb = n_sub * c
    step = pl.program_id(1)

    @pl.when(step == 0)
    def _():
        cm_ref[...] = jnp.zeros_like(cm_ref)
        nv_ref[...] = jnp.zeros_like(nv_ref)
        m_ref[...] = jnp.zeros_like(m_ref)

    items = [(ci, h) for ci in range(n_sub) for h in range(HEADS)]
    rng = lambda ci: (ci * c, (ci + 1) * c)
    pre = {it: {} for it in items}

    for ci, h in items:
        r0, r1 = rng(ci)
        q = qk_ref[r0:r1, h * DK:(h + 1) * DK]
        k = qk_ref[r0:r1, (HEADS + h) * DK:(HEADS + h + 1) * DK] * (DK ** -0.5)
        pre[ci, h].update(q=q, k=k, qk=_mm_nt(q, k))

    g = gates_ref[...] + brow_ref[0:1, :]
    g_t = gates_ref[...].T[0:2 * HEADS, :] + bcol_ref[0:2 * HEADS, 0:1]
    i_rows = g_t[0:HEADS, :]
    logf_rows = _log_sigmoid(g_t[HEADS:2 * HEADS, :])
    tri = tri_ref[...]
    lf_hi, lf_lo = _split(_log_sigmoid(g))
    b_cols = (jnp.dot(tri, lf_hi, preferred_element_type=F32)
              + jnp.dot(tri, lf_lo, preferred_element_type=F32))
    lr_hi, lr_lo = _split(jnp.concatenate([logf_rows, jnp.zeros_like(logf_rows)], axis=0))
    nt = lambda x, y: lax.dot_general(x, y, (((1,), (1,)), ((), ())), preferred_element_type=F32)
    b_rows = nt(lr_hi, tri) + nt(lr_lo, tri)
    lane = lax.broadcasted_iota(jnp.int32, (lb, 128), 1)
    x_hi, x_lo = _split(jnp.where(lane < HEADS, g, b_cols))
    spread = (jnp.dot(x_hi, spread_ref[...], preferred_element_type=F32)
              + jnp.dot(x_lo, spread_ref[...], preferred_element_type=F32))
    i_wide = spread[:, 0:HEADS * 128]
    b_wide = spread[:, HEADS * 128:2 * HEADS * 128]

    ii, jj = _chunk_masks(c)
    causal = ii >= jj
    gain = gain_ref[...]
    wide2 = lambda x: jnp.concatenate([x, x], axis=-1)

    def gates_stage(ci, h):
        d = pre[ci, h]
        r0, r1 = rng(ci)
        b_col = b_wide[r0:r1, h * 128:(h + 1) * 128]
        i_col = i_wide[r0:r1, h * 128:(h + 1) * 128]
        b_last = b_col[c - 1:c, :]
        d_log = jnp.where(causal, b_col[:, 0:c] - b_rows[h:h + 1, r0:r1] + i_rows[h:h + 1, r0:r1], NEG)
        d.update(b_col=b_col, b_last=b_last, d_log=d_log, k_log=b_last - b_col + i_col)
        yield
        d['d_max'] = jnp.max(d_log, axis=1, keepdims=True)

    _interleave(gates_stage(ci, h) for ci, h in items)

    ms = [m_ref[0, h:h + 1, :] for h in range(HEADS)]
    for ci, h in items:
        d = pre[ci, h]
        inter = d['b_col'] + ms[h]
        m_row = jnp.maximum(inter, d['d_max'])
        m_new = m_row[c - 1:c, :]
        d.update(inter=inter, m_row=m_row, m_old=ms[h], m_new=m_new)
        ms[h] = m_new

    def weights_stage(ci, h):
        d = pre[ci, h]
        r0, r1 = rng(ci)
        d['w_inter'] = jnp.exp(d['inter'] - d['m_row'])
        d['f_state'] = jnp.exp(d['b_last'] + d['m_old'] - d['m_new'])
        yield
        d['w_intra'] = jnp.exp(d['d_log'] - d['m_row'][:, 0:c]) * d['qk']
        d['kw'] = d['k'] * jnp.exp(d['k_log'] - d['m_new'])
        yield
        v = v_ref[r0:r1, h * C_DV:(h + 1) * C_DV]
        d['intra'] = _mm(d['w_intra'], v)
        d['kv'] = _mm_tn(d['kw'], v)
        yield
        d['sum_intra'] = jnp.sum(d['w_intra'], axis=1, keepdims=True)
        d['sum_kw'] = jnp.sum(d['kw'], axis=0, keepdims=True)
        d['inv_floor'] = jnp.exp(-d['m_row'])

    _interleave(weights_stage(ci, h) for ci, h in items)

    cms = [cm_ref[0, h] for h in range(HEADS)]
    nvs = [nv_ref[0, h:h + 1, :] for h in range(HEADS)]
    for ci, h in items:
        d = pre[ci, h]
        d['cm_in'], d['nv_in'] = cms[h], nvs[h]
        cms[h] = cms[h] * wide2(d['f_state']) + d['kv']
        nvs[h] = nvs[h] * d['f_state'] + d['sum_kw']

    def output_stage(ci, h):
        d = pre[ci, h]
        r0, r1 = rng(ci)
        qc = _mm(d['q'], d['cm_in'])
        qn = jnp.sum(d['q'] * d['nv_in'], axis=1, keepdims=True)
        yield
        num = wide2(d['w_inter']) * qc + d['intra']
        den = d['w_inter'] * qn + d['sum_intra']
        hh = num / wide2(jnp.maximum(jnp.abs(den), d['inv_floor']))
        yield
        ms_ = jnp.mean(hh * hh, axis=-1, keepdims=True)
        yield
        op = opre_ref[r0:r1, h * C_DV:(h + 1) * C_DV]
        h_ref[r0:r1, h * C_DV:(h + 1) * C_DV] = hh * lax.rsqrt(ms_ + EPS) * gain * jax.nn.sigmoid(op)

    _interleave(output_stage(ci, h) for ci, h in items)

    for h in range(HEADS):
        cm_ref[0, h] = cms[h]
        nv_ref[0, h:h + 1, :] = nvs[h]
        m_ref[0, h:h + 1, :] = ms[h]


def _bias_params(b_gate):
    brow = jnp.zeros((8, 128), F32).at[0, 0:2 * HEADS].set(b_gate)
    bcol = jnp.zeros((8, 128), F32).at[0:2 * HEADS, 0].set(b_gate)
    return brow, bcol


def _mix_c_prompt(qk, v, opre, gates, batch, seq, b_gate, gain):
    lb = C_SUB * CHUNK
    assert seq % lb == 0
    nc = seq // lb
    brow, bcol = _bias_params(b_gate)
    tri = _chunk_tri(lb)
    spread = _head_spread(2)
    tok = lambda b, c: (b * nc + c, 0)
    const = lambda b, c: (0, 0)
    return pl.pallas_call(
        functools.partial(_mix_c_kernel, n_sub=C_SUB),
        grid=(batch, nc),
        in_specs=[pl.BlockSpec((lb, 2 * HEADS * DK), tok),
                  pl.BlockSpec((lb, HEADS * C_DV), tok),
                  pl.BlockSpec((lb, HEADS * C_DV), tok),
                  pl.BlockSpec((lb, 128), tok),
                  pl.BlockSpec((8, 128), const),
                  pl.BlockSpec((8, 128), const),
                  pl.BlockSpec(tri.shape, const),
                  pl.BlockSpec(spread.shape, const),
                  pl.BlockSpec((1, C_DV), const)],
        out_specs=[pl.BlockSpec((lb, HEADS * C_DV), tok),
                   pl.BlockSpec((1, HEADS, DK, C_DV), lambda b, c: (b, 0, 0, 0)),
                   pl.BlockSpec((1, HEADS, DK), lambda b, c: (b, 0, 0)),
                   pl.BlockSpec((1, HEADS, 128), lambda b, c: (b, 0, 0))],
        out_shape=[jax.ShapeDtypeStruct((batch * seq, HEADS * C_DV), F32),
                   jax.ShapeDtypeStruct((batch, HEADS, DK, C_DV), F32),
                   jax.ShapeDtypeStruct((batch, HEADS, DK), F32),
                   jax.ShapeDtypeStruct((batch, HEADS, 128), F32)],
        compiler_params=_cparams("parallel", "arbitrary"),
        name="mix_c_prompt",
    )(qk, v, opre, gates, brow, bcol, tri, spread, gain.reshape(1, C_DV))


def _row_select(rows, t, new, old):
    return jnp.where(rows == t, new, old)


def _step_ab_kernel(qkv_ref, rest_ref, gates_ref, cos_ref, sin_ref, wconv_ref, prow_ref, ggain_ref, rgain_ref,
                    buf_ref, sd_ref, sr_ref, mixed_ref, nbuf_ref, nsd_ref, nsr_ref):
    tb = qkv_ref.shape[0]
    u = qkv_ref[...]
    w = wconv_ref[...]
    b0 = buf_ref[:, 0:A_CONV_CH]
    b1 = buf_ref[:, A_CONV_CH:2 * A_CONV_CH]
    b2 = buf_ref[:, 2 * A_CONV_CH:3 * A_CONV_CH]
    conv = b0 * w[0:1] + b1 * w[1:2] + b2 * w[2:3] + u * w[3:4]
    nbuf_ref[:, 0:A_CONV_CH] = b1
    nbuf_ref[:, A_CONV_CH:2 * A_CONV_CH] = b2
    nbuf_ref[:, 2 * A_CONV_CH:3 * A_CONV_CH] = u
    act = _silu(conv)
    g = gates_ref[...]
    eg_all = jnp.exp(-jnp.exp(prow_ref[0:1, :]) * _softplus(g + prow_ref[1:2, :]))
    beta_all = jax.nn.sigmoid(g)
    cosf = cos_ref[...]
    sinf = sin_ref[...]
    base = HEADS * DK
    rows = lax.broadcasted_iota(jnp.int32, (tb, DK), 0)
    items = [(t, h) for t in range(tb) for h in range(HEADS)]

    qs = [_l2(act[:, h * DK:(h + 1) * DK]) * (DK ** -0.5) for h in range(HEADS)]
    ks = [_l2(act[:, (HEADS + h) * DK:(HEADS + h + 1) * DK]) for h in range(HEADS)]
    k_s = [jnp.zeros((tb, A_DV), F32) for _ in range(HEADS)]
    for t, h in items:
        k_s[h] = _row_select(rows, t, _mm(ks[h], sd_ref[t, h]), k_s[h])
    v_new = []
    for h in range(HEADS):
        v = act[:, (2 * HEADS + h) * DK:(2 * HEADS + h + 1) * DK]
        v_new.append(beta_all[:, HEADS + h:HEADS + h + 1] * (v - eg_all[:, h:h + 1] * k_s[h]))
    for t, h in items:
        nsd_ref[t, h] = (sd_ref[t, h] * eg_all[t:t + 1, h:h + 1]
                         + _mm3_tn(jnp.where(rows == t, ks[h], 0.0), v_new[h]))
    o_a = [jnp.zeros((tb, A_DV), F32) for _ in range(HEADS)]
    for t, h in items:
        o_a[h] = _row_select(rows, t, _mm(qs[h], nsd_ref[t, h]), o_a[h])
    for h in range(HEADS):
        ga = rest_ref[:, h * DK:(h + 1) * DK]
        mixed_ref[:, h * DK:(h + 1) * DK] = _rms(o_a[h], ggain_ref[...]) * _silu(ga)

    qrs, krs = [], []
    for h in range(HEADS):
        qb = rest_ref[:, base + h * DK:base + (h + 1) * DK]
        kb = rest_ref[:, 2 * base + h * DK:2 * base + (h + 1) * DK]
        qrs.append(qb * cosf + pltpu.roll(qb, DK // 2, 1) * sinf)
        krs.append((kb * cosf + pltpu.roll(kb, DK // 2, 1) * sinf) * (DK ** -0.5))
    for t, h in items:
        vb = rest_ref[:, 3 * base + h * DK:3 * base + (h + 1) * DK]
        nsr_ref[t, h] = (sr_ref[t, h] * math.exp(LOG_GAMMA[h])
                         + _mm3_tn(jnp.where(rows == t, krs[h], 0.0), vb))
    o_b = [jnp.zeros((tb, DK), F32) for _ in range(HEADS)]
    for t, h in items:
        o_b[h] = _row_select(rows, t, _mm(qrs[h], nsr_ref[t, h]), o_b[h])
    for h in range(HEADS):
        gb = rest_ref[:, 4 * base + h * DK:4 * base + (h + 1) * DK]
        mixed_ref[:, base + h * DK:base + (h + 1) * DK] = _rms(o_b[h], rgain_ref[...]) * _silu(gb)


def _mix_ab_sample(qkv, rest, gates, conv_buf, s_delta, s_ret, w_conv, a_log, dt_bias, gdn_gain, ret_gain):
    nb = qkv.shape[0]
    tb = SAMPLE_TB
    assert nb % tb == 0
    cosf, sinf = _rope_tables(PAST_LEN + jnp.arange(1))
    prow, _ = _gate_params(a_log, dt_bias)
    tok = lambda b: (b, 0)
    tok4 = lambda b: (b, 0, 0, 0)
    const = lambda b: (0, 0)
    n_buf = (CONV_W - 1) * A_CONV_CH
    mixed, nbuf, nsd, nsr = pl.pallas_call(
        _step_ab_kernel,
        grid=(nb // tb,),
        in_specs=[pl.BlockSpec((tb, A_CONV_CH), tok),
                  pl.BlockSpec((tb, rest.shape[1]), tok),
                  pl.BlockSpec((tb, 128), tok),
                  pl.BlockSpec((1, DK), const),
                  pl.BlockSpec((1, DK), const),
                  pl.BlockSpec((CONV_W, A_CONV_CH), const),
                  pl.BlockSpec((8, 128), const),
                  pl.BlockSpec((1, A_DV), const),
                  pl.BlockSpec((1, DK), const),
                  pl.BlockSpec((tb, n_buf), tok),
                  pl.BlockSpec((tb, HEADS, DK, A_DV), tok4),
                  pl.BlockSpec((tb, HEADS, DK, DK), tok4)],
        out_specs=[pl.BlockSpec((tb, 2 * HEADS * DK), tok),
                   pl.BlockSpec((tb, n_buf), tok),
                   pl.BlockSpec((tb, HEADS, DK, A_DV), tok4),
                   pl.BlockSpec((tb, HEADS, DK, DK), tok4)],
        out_shape=[jax.ShapeDtypeStruct((nb, 2 * HEADS * DK), F32),
                   jax.ShapeDtypeStruct((nb, n_buf), F32),
                   jax.ShapeDtypeStruct((nb, HEADS, DK, A_DV), F32),
                   jax.ShapeDtypeStruct((nb, HEADS, DK, DK), F32)],
        compiler_params=_cparams("parallel"),
        name="mix_ab_sample",
    )(qkv, rest, gates, cosf, sinf, w_conv, prow, gdn_gain.reshape(1, A_DV), ret_gain.reshape(1, DK),
      conv_buf.reshape(nb, n_buf), s_delta, s_ret)
    return mixed, nbuf.reshape(nb, CONV_W - 1, A_CONV_CH), nsd, nsr


def _step_c_kernel(qk_ref, v_ref, opre_ref, gates_ref, brow_ref, gain_ref, cm_ref, nv_ref, m_ref,
                   h_ref, ncm_ref, nnv_ref, nm_ref):
    tb = qk_ref.shape[0]
    g = gates_ref[...] + brow_ref[0:1, :]
    logf = _log_sigmoid(g)
    rows = lax.broadcasted_iota(jnp.int32, (tb, DK), 0)
    items = [(t, h) for t in range(tb) for h in range(HEADS)]
    qs, kws, fs, ms_new, nvs_new = [], [], [], [], []
    for h in range(HEADS):
        i_pre = g[:, h:h + 1]
        q = qk_ref[:, h * DK:(h + 1) * DK]
        k = qk_ref[:, (HEADS + h) * DK:(HEADS + h + 1) * DK] * (DK ** -0.5)
        inter = logf[:, HEADS + h:HEADS + h + 1] + m_ref[:, h:h + 1]
        m_new = jnp.maximum(inter, i_pre)
        f_state = jnp.exp(inter - m_new)
        kw = k * jnp.exp(i_pre - m_new)
        nv_new = nv_ref[:, h * DK:(h + 1) * DK] * f_state + kw
        nnv_ref[:, h * DK:(h + 1) * DK] = nv_new
        nm_ref[:, h:h + 1] = m_new
        qs.append(q)
        kws.append(kw)
        fs.append(f_state)
        ms_new.append(m_new)
        nvs_new.append(nv_new)
    for t, h in items:
        v = v_ref[:, h * C_DV:(h + 1) * C_DV]
        ncm_ref[t, h] = (cm_ref[t, h] * fs[h][t:t + 1, :]
                         + _mm3_tn(jnp.where(rows == t, kws[h], 0.0), v))
    rows_v = lax.broadcasted_iota(jnp.int32, (tb, C_DV), 0)
    nums = [jnp.zeros((tb, C_DV), F32) for _ in range(HEADS)]
    for t, h in items:
        nums[h] = _row_select(rows_v, t, _mm(qs[h], ncm_ref[t, h]), nums[h])
    for h in range(HEADS):
        den = jnp.sum(qs[h] * nvs_new[h], axis=1, keepdims=True)
        hh = nums[h] / jnp.maximum(jnp.abs(den), jnp.exp(-ms_new[h]))
        op = opre_ref[:, h * C_DV:(h + 1) * C_DV]
        h_ref[:, h * C_DV:(h + 1) * C_DV] = _rms(hh, gain_ref[...]) * jax.nn.sigmoid(op)


def _mix_c_sample(qk, v, opre, gates, s_c, s_n, s_m, b_gate, gain):
    nb = qk.shape[0]
    tb = SAMPLE_TB
    assert nb % tb == 0
    brow, _ = _bias_params(b_gate)
    tok = lambda b: (b, 0)
    tok4 = lambda b: (b, 0, 0, 0)
    const = lambda b: (0, 0)
    hh, ncm, nnv, nm = pl.pallas_call(
        _step_c_kernel,
        grid=(nb // tb,),
        in_specs=[pl.BlockSpec((tb, 2 * HEADS * DK), tok),
                  pl.BlockSpec((tb, HEADS * C_DV), tok),
                  pl.BlockSpec((tb, HEADS * C_DV), tok),
                  pl.BlockSpec((tb, 128), tok),
                  pl.BlockSpec((8, 128), const),
                  pl.BlockSpec((1, C_DV), const),
                  pl.BlockSpec((tb, HEADS, DK, C_DV), tok4),
                  pl.BlockSpec((tb, HEADS * DK), tok),
                  pl.BlockSpec((tb, HEADS), tok)],
        out_specs=[pl.BlockSpec((tb, HEADS * C_DV), tok),
                   pl.BlockSpec((tb, HEADS, DK, C_DV), tok4),
                   pl.BlockSpec((tb, HEADS * DK), tok),
                   pl.BlockSpec((tb, HEADS), tok)],
        out_shape=[jax.ShapeDtypeStruct((nb, HEADS * C_DV), F32),
                   jax.ShapeDtypeStruct((nb, HEADS, DK, C_DV), F32),
                   jax.ShapeDtypeStruct((nb, HEADS * DK), F32),
                   jax.ShapeDtypeStruct((nb, HEADS), F32)],
        compiler_params=_cparams("parallel"),
        name="mix_c_sample",
    )(qk, v, opre, gates, brow, gain.reshape(1, C_DV), s_c, s_n.reshape(nb, HEADS * DK), s_m)
    return hh, ncm, nnv.reshape(nb, HEADS, DK), nm


A_SPLITS = ((A_CONV_CH,), (5 * HEADS * DK,), (128,))
C_SPLITS = ((2 * HEADS * DK, HEADS * C_DV, HEADS * C_DV), (128,))


def _gate_columns(w_gates):
    return jnp.pad(w_gates, ((0, 0), (0, 128 - w_gates.shape[1]))).astype(BF16)


def _prep_w_in_a(w):
    n_gate = 2 * HEADS
    return (w[:, :A_CONV_CH].astype(BF16), w[:, A_CONV_CH + n_gate:].astype(BF16),
            _gate_columns(w[:, A_CONV_CH:A_CONV_CH + n_gate]))


def _prep_w_in_c(w):
    n_main = 2 * HEADS * DK + 2 * HEADS * C_DV
    return (w[:, :n_main].astype(BF16), _gate_columns(w[:, n_main:]))


def kernel(x_prompt, x_sample, state_conv_a, state_delta_a, state_ret_b, state_mlstm_C, state_mlstm_n, state_mlstm_m,
           norm_mix_a, w_in_a, w_conv_a, a_log, dt_bias, gdn_gain, ret_gain, w_out_a,
           norm_mix_c, w_in_c, b_gate_c, mlstm_gain, w_out_c, norm_ffn, w_up, w_down, final_gain):
    batch, seq, _ = x_prompt.shape
    n_s = x_sample.shape[0] * x_sample.shape[1]
    tm = PROMPT_TM
    w_a = _prep_w_in_a(w_in_a[0])
    w_c = _prep_w_in_c(w_in_c[0])
    w_ups = w_up.astype(BF16)
    w_downs = w_down.astype(BF16)
    h = x_prompt.reshape(batch * seq, D_MODEL)
    h_s = x_sample.reshape(n_s, D_MODEL)

    (qkv, rest, gates, conv), (qkv_s, rest_s, gates_s) = _proj_a_prompt(
        h, h_s, norm_mix_a[0], w_a, w_conv_a[0], batch, seq, tm)
    mixed, sd, sr = _mix_ab_prompt(qkv, rest, gates, batch, seq, a_log[0], dt_bias[0], gdn_gain[0], ret_gain[0])
    mixed_s, conv_s, sd_s, sr_s = _mix_ab_sample(qkv_s, rest_s, gates_s, state_conv_a[0], state_delta_a[0],
                                                 state_ret_b[0], w_conv_a[0], a_log[0], dt_bias[0], gdn_gain[0],
                                                 ret_gain[0])
    h, h_s = _out_mlp(h, mixed, h_s, mixed_s, w_out_a[0].astype(BF16), norm_ffn[0], w_ups, w_downs, 0, None, tm)

    (qk, v, opre, gates_c), (qk_s, v_s, opre_s, gates_cs) = _norm_proj(h, h_s, norm_mix_c[0], w_c, C_SPLITS, tm)
    hm, cm, nv, m = _mix_c_prompt(qk, v, opre, gates_c, batch, seq, b_gate_c[0], mlstm_gain[0])
    hm_s, cm_s, nv_s, m_s = _mix_c_sample(qk_s, v_s, opre_s, gates_cs, state_mlstm_C[0], state_mlstm_n[0],
                                          state_mlstm_m[0], b_gate_c[0], mlstm_gain[0])
    y, y_s = _out_mlp(h, hm, h_s, hm_s, w_out_c[0].astype(BF16), norm_ffn[1], w_ups, w_downs, 1, final_gain, tm)

    return (y.reshape(x_prompt.shape), y_s.reshape(x_sample.shape),
            conv[None], sd[None], sr[None], cm[None], nv[None], m[:, :, 0][None],
            conv_s[None], sd_s[None], sr_s[None], cm_s[None], nv_s[None], m_s[None])
```

```python
import functools
import math

import jax
import jax.numpy as jnp
from jax import lax
from jax.experimental import pallas as pl
from jax.experimental.pallas import tpu as pltpu

F32 = jnp.float32
BF16 = jnp.bfloat16

D_MODEL = 1024
D_FF = 4 * D_MODEL
CHUNK = 64
EPS = 1e-6
NEG = -1e30
HEADS = 4
DK = 128
A_DV = 128
C_DV = 256
CONV_W = 4
A_CONV_CH = 3 * HEADS * DK
ROPE_BASE = 10000.0
PAST_LEN = 16384
LOG_GAMMA = tuple(math.log1p(-(2.0 ** (-5.0 - h))) for h in range(HEADS))

VMEM_LIMIT_BYTES = 56 * 1024 * 1024
PROMPT_TM = 512
AB_SUB = 4
C_SUB = 4
SAMPLE_TB = 8


def _cparams(*sem):
    return pltpu.CompilerParams(dimension_semantics=sem, vmem_limit_bytes=VMEM_LIMIT_BYTES)


def _mm(a, b):
    return jnp.dot(a.astype(BF16), b.astype(BF16), preferred_element_type=F32)


def _mm_nt(a, b):
    return lax.dot_general(a.astype(BF16), b.astype(BF16), (((1,), (1,)), ((), ())), preferred_element_type=F32)


def _mm_tn(a, b):
    return lax.dot_general(a.astype(BF16), b.astype(BF16), (((0,), (0,)), ((), ())), preferred_element_type=F32)


def _split(a):
    hi = a.astype(BF16)
    lo = (a - hi.astype(F32)).astype(BF16)
    return hi, lo


def _mm3(a, b):
    ah, al = _split(a)
    bh, bl = _split(b)
    d = lambda x, y: jnp.dot(x, y, preferred_element_type=F32)
    return d(ah, bh) + (d(ah, bl) + d(al, bh))


def _mm3_tn(a, b):
    ah, al = _split(a)
    bh, bl = _split(b)
    d = lambda x, y: lax.dot_general(x, y, (((0,), (0,)), ((), ())), preferred_element_type=F32)
    return d(ah, bh) + (d(ah, bl) + d(al, bh))


def _softplus(x):
    return jnp.maximum(x, 0.0) + jnp.log1p(jnp.exp(-jnp.abs(x)))


def _log_sigmoid(x):
    return -_softplus(-x)


def _silu(x):
    return x * jax.nn.sigmoid(x)


def _rms(x, gain):
    return x * lax.rsqrt(jnp.mean(x * x, axis=-1, keepdims=True) + EPS) * gain


def _l2(t):
    return t * lax.rsqrt(jnp.sum(t * t, axis=-1, keepdims=True) + EPS)


def _interleave(gens):
    gens = list(gens)
    while gens:
        alive = []
        for gen in gens:
            try:
                next(gen)
                alive.append(gen)
            except StopIteration:
                pass
        gens = alive


def _lockstep(gens):
    gens = list(gens)
    while gens:
        alive = []
        for gen in gens:
            try:
                next(gen)
                alive.append(gen)
            except StopIteration:
                pass
        gens = alive
        if gens:
            yield


def _head_block_rows(x):
    c, n = x.shape
    t = n // HEADS
    z = jnp.zeros((c, t), x.dtype)
    return jnp.concatenate(
        [jnp.concatenate([x[:, h * t:(h + 1) * t] if g == h else z for g in range(HEADS)], axis=1)
         for h in range(HEADS)], axis=0)


def _head_block_diag(y):
    c, n = y.shape
    per_tile = 128 // c
    assert n == HEADS * c and 128 % c == 0 and HEADS % per_tile == 0
    lane = lax.broadcasted_iota(jnp.int32, (c, 128), 1)
    z = jnp.zeros((c, 128), y.dtype)
    blocks = []
    for h in range(HEADS):
        t = h // per_tile
        lo = (h % per_tile) * c
        kept = jnp.where((lane >= lo) & (lane < lo + c), y[:, t * 128:(t + 1) * 128], z)
        blocks.append(jnp.concatenate([kept if g == t else z for g in range(n // 128)], axis=1))
    return jnp.concatenate(blocks, axis=0)


def _wide_mm(x, y):
    return jnp.dot(x.astype(BF16), _head_block_diag(y.astype(BF16)), preferred_element_type=F32)


def _tri_inv_wide(ns):
    c, wd = ns[0].shape
    ii = lax.broadcasted_iota(jnp.int32, (c, wd), 0)
    jl = lax.broadcasted_iota(jnp.int32, (c, wd), 1) & (c - 1)
    eye = (ii == jl).astype(F32)
    ts = [eye - jnp.where((ii >> 1) == (jl >> 1), n, 0.0) for n in ns]
    for lvl in range(1, int(math.log2(c))):
        off = ((ii >> (lvl + 1)) == (jl >> (lvl + 1))) & ((ii >> lvl) != (jl >> lvl))
        ys = [_wide_mm(jnp.where(off, n, 0.0), t) for n, t in zip(ns, ts)]
        yield
        ts = [t - _wide_mm(t, y) for t, y in zip(ts, ys)]
        yield
    return [t - eye for t in ts]


def _project_rows(x_ref, g_ref, w_refs, out_refs, splits):
    xn = _rms(x_ref[...], g_ref[...]).astype(BF16)
    o = 0
    for w_ref, widths in zip(w_refs, splits):
        off = 0
        for n in widths:
            out_refs[o][...] = jnp.dot(xn, w_ref[:, off:off + n], preferred_element_type=F32)
            off += n
            o += 1


def _norm_proj_kernel(x_ref, xs_ref, g_ref, *refs, splits):
    n_out = sum(len(s) for s in splits)
    w_refs = refs[:len(splits)]
    out_refs = refs[len(splits):len(splits) + n_out]
    outs_refs = refs[len(splits) + n_out:]
    _project_rows(x_ref, g_ref, w_refs, out_refs, splits)

    @pl.when(pl.program_id(0) == 0)
    def _():
        _project_rows(xs_ref, g_ref, w_refs, outs_refs, splits)


def _norm_proj(x, x_s, gain, weights, splits, tm):
    t, ts = x.shape[0], x_s.shape[0]
    assert t % tm == 0 and all(sum(s) == w.shape[1] for w, s in zip(weights, splits))
    widths = [n for s in splits for n in s]
    row = lambda i: (i, 0)
    const = lambda i: (0, 0)
    outs = pl.pallas_call(
        functools.partial(_norm_proj_kernel, splits=splits),
        grid=(t // tm,),
        in_specs=[pl.BlockSpec((tm, D_MODEL), row), pl.BlockSpec((ts, D_MODEL), const),
                  pl.BlockSpec((1, D_MODEL), const)]
                 + [pl.BlockSpec(w.shape, const, pipeline_mode=pl.Buffered(1)) for w in weights],
        out_specs=[pl.BlockSpec((tm, n), row) for n in widths] + [pl.BlockSpec((ts, n), const) for n in widths],
        out_shape=[jax.ShapeDtypeStruct((t, n), F32) for n in widths]
                  + [jax.ShapeDtypeStruct((ts, n), F32) for n in widths],
        compiler_params=_cparams("arbitrary"),
        name="norm_proj",
    )(x, x_s, gain.reshape(1, D_MODEL), *weights)
    return outs[:len(widths)], outs[len(widths):]


def _proj_a_prompt_kernel(x_ref, xs_ref, g_ref, wqkv_ref, wrest_ref, wgate_ref, wconv_ref, cos_ref, sin_ref,
                          qkv_ref, rest_ref, gates_ref, conv_ref, qkvs_ref, rests_ref, gatess_ref,
                          xp_ref, raw_ref, *, tiles_per_seq):
    tm = x_ref.shape[0]
    hd = HEADS * DK

    @pl.when(pl.program_id(0) == 0)
    def _():
        _project_rows(xs_ref, g_ref, (wqkv_ref, wrest_ref, wgate_ref), (qkvs_ref, rests_ref, gatess_ref), A_SPLITS)

    xn = _rms(x_ref[...], g_ref[...]).astype(BF16)

    @pl.when(pl.program_id(0) % tiles_per_seq == 0)
    def _():
        xp_ref[0:8, :] = jnp.zeros((8, A_CONV_CH), F32)

    w = wconv_ref[...]
    base = A_CONV_CH

    def conv_part(part, slot, r0, r1):
        c0, c1 = part * hd, (part + 1) * hd
        raw = raw_ref[slot, r0:r1, :]
        xp_ref[8 + r0:8 + r1, c0:c1] = raw
        conv = (xp_ref[5 + r0:5 + r1, c0:c1] * w[0:1, c0:c1] + xp_ref[6 + r0:6 + r1, c0:c1] * w[1:2, c0:c1]
                + xp_ref[7 + r0:7 + r1, c0:c1] * w[2:3, c0:c1] + raw * w[3:4, c0:c1])
        if r1 == tm:
            rows = r1 - r0
            xp_ref[0:8, c0:c1] = raw[rows - 8:rows, :]
            conv_ref[0, :, c0:c1] = raw[rows - 3:rows, :]
        act = _silu(conv)
        if part == 2:
            qkv_ref[r0:r1, c0:c1] = act
        else:
            scale = DK ** -0.5 if part == 0 else 1.0
            for h in range(HEADS):
                qkv_ref[r0:r1, c0 + h * DK:c0 + (h + 1) * DK] = _l2(act[:, h * DK:(h + 1) * DK]) * scale

    def rest_part(part, slot, r0, r1):
        c0, c1 = part * hd, (part + 1) * hd
        raw = raw_ref[slot, r0:r1, :]
        if part in (0, 4):
            rest_ref[r0:r1, c0:c1] = _silu(raw)
        elif part == 3:
            rest_ref[r0:r1, c0:c1] = raw
        else:
            scale = 1.0 if part == 1 else DK ** -0.5
            for h in range(HEADS):
                t = raw[:, h * DK:(h + 1) * DK]
                rest_ref[r0:r1, c0 + h * DK:c0 + (h + 1) * DK] = (
                    (t * cos_ref[r0:r1, :] + pltpu.roll(t, DK // 2, 1) * sin_ref[r0:r1, :]) * scale)

    def gates_part(_, slot, r0, r1):
        gates_ref[r0:r1, :] = raw_ref[slot, r0:r1, 0:128]

    groups = [(wqkv_ref, 0, hd, functools.partial(conv_part, 0)),
              (wrest_ref, 3 * hd, hd, functools.partial(rest_part, 3)),
              (wqkv_ref, hd, hd, functools.partial(conv_part, 1)),
              (wrest_ref, 0, hd, functools.partial(rest_part, 0)),
              (wqkv_ref, 2 * hd, hd, functools.partial(conv_part, 2)),
              (wrest_ref, hd, hd, functools.partial(rest_part, 1)),
              (wgate_ref, 0, 128, functools.partial(gates_part, 0)),
              (wrest_ref, 2 * hd, hd, functools.partial(rest_part, 2)),
              (wrest_ref, 4 * hd, hd, functools.partial(rest_part, 4))]

    def project(n):
        w_ref, col0, width, _ = groups[n]
        for j in range(0, width, 256):
            wj = min(256, width - j)
            raw_ref[n % 2, :, j:j + wj] = jnp.dot(xn, w_ref[:, col0 + j:col0 + j + wj],
                                                  preferred_element_type=F32)
            yield

    def epilogue(n):
        for r0 in range(0, tm, 128):
            groups[n][3](n % 2, r0, min(r0 + 128, tm))
            yield

    _interleave([project(0)])
    for n in range(len(groups)):
        _interleave([epilogue(n)] + ([project(n + 1)] if n + 1 < len(groups) else []))


def _proj_a_prompt(x, x_s, gain, weights, w_conv, batch, seq, tm):
    t, ts = x.shape[0], x_s.shape[0]
    assert t % tm == 0 and seq % tm == 0
    tiles_per_seq = seq // tm
    cosf, sinf = _rope_tables(jnp.arange(seq))
    row = lambda i: (i, 0)
    const = lambda i: (0, 0)
    pos = lambda i: (i % tiles_per_seq, 0)
    n_rest = 5 * HEADS * DK
    outs = pl.pallas_call(
        functools.partial(_proj_a_prompt_kernel, tiles_per_seq=tiles_per_seq),
        grid=(t // tm,),
        in_specs=[pl.BlockSpec((tm, D_MODEL), row),
                  pl.BlockSpec((ts, D_MODEL), const),
                  pl.BlockSpec((1, D_MODEL), const)]
                 + [pl.BlockSpec(w.shape, const, pipeline_mode=pl.Buffered(1)) for w in weights]
                 + [pl.BlockSpec((CONV_W, A_CONV_CH), const),
                    pl.BlockSpec((tm, DK), pos),
                    pl.BlockSpec((tm, DK), pos)],
        out_specs=[pl.BlockSpec((tm, A_CONV_CH), row),
                   pl.BlockSpec((tm, n_rest), row),
                   pl.BlockSpec((tm, 128), row),
                   pl.BlockSpec((1, CONV_W - 1, A_CONV_CH), lambda i: (i // tiles_per_seq, 0, 0)),
                   pl.BlockSpec((ts, A_CONV_CH), const),
                   pl.BlockSpec((ts, n_rest), const),
                   pl.BlockSpec((ts, 128), const)],
        out_shape=[jax.ShapeDtypeStruct((t, A_CONV_CH), F32),
                   jax.ShapeDtypeStruct((t, n_rest), F32),
                   jax.ShapeDtypeStruct((t, 128), F32),
                   jax.ShapeDtypeStruct((batch, CONV_W - 1, A_CONV_CH), F32),
                   jax.ShapeDtypeStruct((ts, A_CONV_CH), F32),
                   jax.ShapeDtypeStruct((ts, n_rest), F32),
                   jax.ShapeDtypeStruct((ts, 128), F32)],
        scratch_shapes=[pltpu.VMEM((tm + 8, A_CONV_CH), F32), pltpu.VMEM((2, tm, HEADS * DK), F32)],
        compiler_params=_cparams("arbitrary"),
        name="proj_a_prompt",
    )(x, x_s, gain.reshape(1, D_MODEL), *weights, w_conv, cosf, sinf)
    return outs[:4], outs[4:]


def _out_mlp_kernel(h_ref, m_ref, hs_ref, ms_ref, wout_ref, gffn_ref, wup_ref, wdown_ref, *rest, final):
    if final:
        gfin_ref, o_ref, os_ref = rest
    else:
        o_ref, os_ref = rest

    def block(h_r, m_r, o_r):
        h = h_r[...] + jnp.dot(m_r[...].astype(BF16), wout_ref[...], preferred_element_type=F32)
        xn = _rms(h, gffn_ref[...]).astype(BF16)
        acc = h
        step = 1024
        for j in range(D_FF // step):
            hid = jnp.dot(xn, wup_ref[:, j * step:(j + 1) * step], preferred_element_type=F32)
            hid = jnp.maximum(hid, 0.0)
            acc = acc + jnp.dot((hid * hid).astype(BF16), wdown_ref[j * step:(j + 1) * step, :],
                                preferred_element_type=F32)
        if final:
            acc = _rms(acc, gfin_ref[...])
        o_r[...] = acc

    block(h_ref, m_ref, o_ref)

    @pl.when(pl.program_id(0) == 0)
    def _():
        block(hs_ref, ms_ref, os_ref)


def _out_mlp(h, mix, h_s, mix_s, w_out, g_ffn, w_up, w_down, layer, g_final, tm):
    t, ts = h.shape[0], h_s.shape[0]
    assert t % tm == 0
    final = g_final is not None
    row = lambda i: (i, 0)
    const = lambda i: (0, 0)
    this_layer = lambda i: (layer, 0, 0)
    once = pl.Buffered(1)
    in_specs = [pl.BlockSpec((tm, D_MODEL), row), pl.BlockSpec((tm, mix.shape[1]), row),
                pl.BlockSpec((ts, D_MODEL), const), pl.BlockSpec((ts, mix_s.shape[1]), const),
                pl.BlockSpec(w_out.shape, const, pipeline_mode=once), pl.BlockSpec((1, D_MODEL), const),
                pl.BlockSpec((None,) + w_up.shape[1:], this_layer, pipeline_mode=once),
                pl.BlockSpec((None,) + w_down.shape[1:], this_layer, pipeline_mode=once)]
    args = [h, mix, h_s, mix_s, w_out, g_ffn.reshape(1, D_MODEL), w_up, w_down]
    if final:
        in_specs.append(pl.BlockSpec((1, D_MODEL), const))
        args.append(g_final.reshape(1, D_MODEL))
    return pl.pallas_call(
        functools.partial(_out_mlp_kernel, final=final),
        grid=(t // tm,),
        in_specs=in_specs,
        out_specs=[pl.BlockSpec((tm, D_MODEL), row), pl.BlockSpec((ts, D_MODEL), const)],
        out_shape=[jax.ShapeDtypeStruct((t, D_MODEL), F32), jax.ShapeDtypeStruct((ts, D_MODEL), F32)],
        compiler_params=_cparams("arbitrary"),
        name="out_mlp",
    )(*args)


def _chunk_masks(c):
    ii = lax.broadcasted_iota(jnp.int32, (c, c), 0)
    jj = lax.broadcasted_iota(jnp.int32, (c, c), 1)
    return ii, jj


def _mix_ab_kernel(qkv_ref, rest_ref, gates_ref, prow_ref, pcol_ref, tri_ref, spread_ref,
                   ggain_ref, rgain_ref, mixed_ref, sd_ref, sr_ref, *, n_sub):
    c = CHUNK
    lb = n_sub * c
    step = pl.program_id(1)

    @pl.when(step == 0)
    def _():
        sd_ref[...] = jnp.zeros_like(sd_ref)
        sr_ref[...] = jnp.zeros_like(sr_ref)

    g = gates_ref[...]
    g_t = g.T
    neg_a_row = -jnp.exp(prow_ref[0:1, :])
    dt_row = prow_ref[1:2, :]
    neg_a_col = -jnp.exp(pcol_ref[0:HEADS, 0:1])
    dt_col = pcol_ref[0:HEADS, 1:2]
    la_cols = neg_a_row * _softplus(g + dt_row)
    beta_cols = jax.nn.sigmoid(g)
    la_rows = neg_a_col * _softplus(g_t[0:HEADS, :] + dt_col)

    tri = tri_ref[...]
    la_hi, la_lo = _split(la_cols)
    g_cols = (jnp.dot(tri, la_hi, preferred_element_type=F32)
              + jnp.dot(tri, la_lo, preferred_element_type=F32))
    lr_hi, lr_lo = _split(jnp.concatenate([la_rows, jnp.zeros_like(la_rows)], axis=0))
    nt = lambda x, y: lax.dot_general(x, y, (((1,), (1,)), ((), ())), preferred_element_type=F32)
    g_rows = nt(lr_hi, tri) + nt(lr_lo, tri)
    lane = lax.broadcasted_iota(jnp.int32, (lb, 128), 1)
    x_hi, x_lo = _split(jnp.where(lane < HEADS, g_cols, beta_cols))
    spread = (jnp.dot(x_hi, spread_ref[...], preferred_element_type=F32)
              + jnp.dot(x_lo, spread_ref[...], preferred_element_type=F32))
    hd = HEADS * DK
    wd = HEADS * c
    g_wide = spread[:, 0:hd]
    beta_wide = spread[:, hd:2 * hd]
    g_half = spread[:, 2 * hd:2 * hd + wd]
    eg_wide = jnp.exp(g_wide)

    ii = lax.broadcasted_iota(jnp.int32, (c, wd), 0)
    jl = lax.broadcasted_iota(jnp.int32, (c, wd), 1) & (c - 1)
    causal = ii >= jl
    strict = ii > jl
    lane_hd = lax.broadcasted_iota(jnp.int32, (1, hd), 1)
    lane_wd = lax.broadcasted_iota(jnp.int32, (1, wd), 1)
    lg_hd = jnp.full((1, hd), LOG_GAMMA[HEADS - 1], F32)
    lg_wd = jnp.full((1, wd), LOG_GAMMA[HEADS - 1], F32)
    for h in range(HEADS - 2, -1, -1):
        lg_hd = jnp.where(lane_hd < (h + 1) * DK, LOG_GAMMA[h], lg_hd)
        lg_wd = jnp.where(lane_wd < (h + 1) * c, LOG_GAMMA[h], lg_wd)
    pos_col = lax.broadcasted_iota(jnp.int32, (c, hd), 0).astype(F32)
    ret_dec = jnp.exp(jnp.where(causal, (ii - jl).astype(F32) * lg_wd, NEG))
    ret_q_scale = jnp.exp((pos_col + 1.0) * lg_hd)
    ret_k_scale = jnp.exp((float(c - 1) - pos_col) * lg_hd)

    ggain = ggain_ref[...]
    rgain = rgain_ref[...]
    base = hd
    tile = lambda x, h: x[:, h * DK:(h + 1) * DK]
    s_delta = [sd_ref[0, h] for h in range(HEADS)]
    s_ret = [sr_ref[0, h] for h in range(HEADS)]

    chunks = list(range(n_sub))
    items = [(ci, h) for ci in chunks for h in range(HEADS)]
    rng = lambda ci: (ci * c, (ci + 1) * c)
    pre = {it: {} for it in items}
    pre_b = {it: {} for it in items}
    wide = {ci: {} for ci in chunks}
    nt = lambda x, y: lax.dot_general(x, y, (((1,), (1,)), ((), ())), preferred_element_type=F32)

    def gdn_stage(ci):
        w_ = wide[ci]
        r0, r1 = rng(ci)
        q = qkv_ref[r0:r1, 0:hd]
        k = qkv_ref[r0:r1, hd:2 * hd]
        beta = beta_wide[r0:r1]
        g_col = g_wide[r0:r1]
        kb = k * beta
        g_row = jnp.concatenate([g_rows[h:h + 1, r0:r1] for h in range(HEADS)], axis=1)
        dec_causal = jnp.exp(jnp.where(causal, g_half[r0:r1] - g_row, NEG))
        prod = nt(jnp.concatenate([q, kb], axis=0).astype(BF16), _head_block_rows(k.astype(BF16)))
        yield
        w_['n'] = prod[c:2 * c] * jnp.where(strict, dec_causal, 0.0)
        w_['qk'] = prod[0:c] * dec_causal
        yield
        v = qkv_ref[r0:r1, 2 * hd:3 * hd]
        eg = eg_wide[r0:r1]
        g_last = g_col[c - 1:c, :]
        rhs_u, rhs_w, qd = v * beta, kb * eg, q * eg
        kd = k * jnp.exp(g_last - g_col)
        gl = jnp.exp(g_last)
        for h in range(HEADS):
            pre[ci, h].update(rhs=jnp.concatenate([tile(rhs_u, h), tile(rhs_w, h)], axis=1), qd=tile(qd, h),
                              kd=tile(kd, h), gl=tile(gl, h))

    def ret_stage(ci):
        r0, r1 = rng(ci)
        qr = rest_ref[r0:r1, base:2 * base]
        kr = rest_ref[r0:r1, 2 * base:3 * base]
        vb = rest_ref[r0:r1, 3 * base:4 * base]
        qk = nt(qr.astype(BF16), _head_block_rows(kr.astype(BF16))) * ret_dec
        qd = qr * ret_q_scale
        kd = kr * ret_k_scale
        yield
        for a in range(HEADS // 2):
            h0, h1 = 2 * a, 2 * a + 1
            z = jnp.zeros((c, DK), F32)
            rhs = jnp.concatenate([jnp.concatenate([tile(vb, h0), z], axis=1),
                                   jnp.concatenate([z, tile(vb, h1)], axis=1)], axis=0)
            intra = _mm(qk[:, a * 2 * c:(a + 1) * 2 * c], rhs)
            pre_b[ci, h0]['intra'] = intra[:, 0:DK]
            pre_b[ci, h1]['intra'] = intra[:, DK:2 * DK]
        yield
        for h in range(HEADS):
            pre_b[ci, h]['qd'] = tile(qd, h)
            pre_b[ci, h]['kv'] = _mm_tn(tile(kd, h), tile(vb, h))

    def solve_stage():
        t_offs = yield from _tri_inv_wide([wide[ci]['n'] for ci in chunks])
        for ci, t_off in zip(chunks, t_offs):
            wide[ci]['t_off'] = t_off

    def pair_rhs(x0, x1):
        z = jnp.zeros_like(x0)
        return jnp.concatenate([jnp.concatenate([x0, z], axis=1), jnp.concatenate([z, x1], axis=1)], axis=0)

    def sol_stage(ci):
        w_ = wide[ci]
        for a in range(HEADS // 2):
            h0, h1 = 2 * a, 2 * a + 1
            r = _mm(w_['t_off'][:, a * 2 * c:(a + 1) * 2 * c], pair_rhs(pre[ci, h0]['rhs'], pre[ci, h1]['rhs']))
            pre[ci, h0]['sol'] = pre[ci, h0]['rhs'] + r[:, 0:2 * A_DV]
            pre[ci, h1]['sol'] = pre[ci, h1]['rhs'] + r[:, 2 * A_DV:4 * A_DV]
        yield
        for a in range(HEADS // 2):
            h0, h1 = 2 * a, 2 * a + 1
            r = _mm(w_['qk'][:, a * 2 * c:(a + 1) * 2 * c], pair_rhs(pre[ci, h0]['sol'], pre[ci, h1]['sol']))
            pre[ci, h0]['qs'] = r[:, 0:2 * A_DV]
            pre[ci, h1]['qs'] = r[:, 2 * A_DV:4 * A_DV]
        for h in range(HEADS):
            d = pre[ci, h]
            d['kts'] = _mm_tn(d['kd'], d['sol'])
        yield
        for h in range(HEADS):
            d = pre[ci, h]
            d['lhs'] = jnp.concatenate([d['kts'][:, A_DV:2 * A_DV], d['qd'] - d['qs'][:, A_DV:2 * A_DV]], axis=0)


    def ret_out_stage(ci, h):
        d = pre_b[ci, h]
        r0, r1 = rng(ci)
        o = _mm(d['qd'], d['s_in']) + d['intra']
        yield
        ms_ = jnp.mean(o * o, axis=-1, keepdims=True)
        yield
        gb = rest_ref[r0:r1, 4 * base + h * DK:4 * base + (h + 1) * DK]
        mixed_ref[r0:r1, base + h * DK:base + (h + 1) * DK] = o * lax.rsqrt(ms_ + EPS) * rgain * gb

    def delta_step(ci, h):
        d = pre[ci, h]
        r = _mm(d['lhs'], s_delta[h])
        yield
        d['o'] = r[DK:DK + c] + d['qs'][:, 0:A_DV]
        s_delta[h] = s_delta[h] * d['gl'] - r[0:DK] + d['kts'][:, 0:A_DV]

    def gdn_out_stage(ci, h):
        o = pre[ci, h]['o']
        r0, r1 = rng(ci)
        ms_ = jnp.mean(o * o, axis=-1, keepdims=True)
        yield
        ga = rest_ref[r0:r1, h * DK:(h + 1) * DK]
        mixed_ref[r0:r1, h * A_DV:(h + 1) * A_DV] = o * lax.rsqrt(ms_ + EPS) * ggain * ga

    def delta_track():
        yield from solve_stage()
        yield from _lockstep([sol_stage(ci) for ci in chunks])
        for ci in chunks:
            yield from _lockstep([delta_step(ci, h) for h in range(HEADS)])
            if ci > 0:
                yield from _lockstep([gdn_out_stage(ci - 1, h) for h in range(HEADS)])
        yield from _lockstep([gdn_out_stage(chunks[-1], h) for h in range(HEADS)])

    def ret_track():
        for ci in chunks:
            yield from ret_stage(ci)
            for h in range(HEADS):
                pre_b[ci, h]['s_in'] = s_ret[h]
                s_ret[h] = s_ret[h] * math.exp(c * LOG_GAMMA[h]) + pre_b[ci, h]['kv']
            yield
            yield from _lockstep([ret_out_stage(ci, h) for h in range(HEADS)])

    _interleave([gdn_stage(ci) for ci in chunks])
    _interleave([delta_track(), ret_track()])

    for h in range(HEADS):
        sd_ref[0, h] = s_delta[h]
        sr_ref[0, h] = s_ret[h]


def _rope_tables(pos):
    half = DK // 2
    inv = ROPE_BASE ** (-jnp.arange(half, dtype=F32) / half)
    ang = pos.astype(F32)[:, None] * inv[None, :]
    cos, sin = jnp.cos(ang), jnp.sin(ang)
    return jnp.concatenate([cos, cos], axis=-1), jnp.concatenate([-sin, sin], axis=-1)


def _chunk_tri(lb):
    i = jnp.arange(lb)[:, None]
    j = jnp.arange(lb)[None, :]
    return ((i >= j) & (i // CHUNK == j // CHUNK)).astype(BF16)


def _head_spread(groups):
    n = groups * HEADS
    src = jnp.arange(128)[:, None]
    dst = jnp.arange(n * 128)[None, :] // 128
    return (src == dst).astype(BF16)


def _head_spread_ab():
    src = jnp.arange(128)[:, None]
    full = jnp.arange(2 * HEADS * DK)[None, :] // DK
    half = jnp.arange(HEADS * CHUNK)[None, :] // CHUNK
    return jnp.concatenate([src == full, src == half], axis=1).astype(BF16)


def _gate_params(a_log, dt_bias):
    prow = jnp.zeros((8, 128), F32).at[0, 0:HEADS].set(a_log).at[1, 0:HEADS].set(dt_bias)
    pcol = jnp.zeros((8, 128), F32).at[0:HEADS, 0].set(a_log).at[0:HEADS, 1].set(dt_bias)
    return prow, pcol


def _mix_ab_prompt(qkv, rest, gates, batch, seq, a_log, dt_bias, gdn_gain, ret_gain):
    lb = AB_SUB * CHUNK
    assert seq % lb == 0
    nc = seq // lb
    prow, pcol = _gate_params(a_log, dt_bias)
    tri = _chunk_tri(lb)
    spread = _head_spread_ab()
    tok = lambda b, c: (b * nc + c, 0)
    const = lambda b, c: (0, 0)
    mixed, sd, sr = pl.pallas_call(
        functools.partial(_mix_ab_kernel, n_sub=AB_SUB),
        grid=(batch, nc),
        in_specs=[pl.BlockSpec((lb, A_CONV_CH), tok),
                  pl.BlockSpec((lb, rest.shape[1]), tok),
                  pl.BlockSpec((lb, 128), tok),
                  pl.BlockSpec((8, 128), const),
                  pl.BlockSpec((8, 128), const),
                  pl.BlockSpec(tri.shape, const),
                  pl.BlockSpec(spread.shape, const),
                  pl.BlockSpec((1, A_DV), const),
                  pl.BlockSpec((1, DK), const)],
        out_specs=[pl.BlockSpec((lb, 2 * HEADS * DK), tok),
                   pl.BlockSpec((1, HEADS, DK, A_DV), lambda b, c: (b, 0, 0, 0)),
                   pl.BlockSpec((1, HEADS, DK, DK), lambda b, c: (b, 0, 0, 0))],
        out_shape=[jax.ShapeDtypeStruct((batch * seq, 2 * HEADS * DK), F32),
                   jax.ShapeDtypeStruct((batch, HEADS, DK, A_DV), F32),
                   jax.ShapeDtypeStruct((batch, HEADS, DK, DK), F32)],
        compiler_params=_cparams("parallel", "arbitrary"),
        name="mix_ab_prompt",
    )(qkv, rest, gates, prow, pcol, tri, spread, gdn_gain.reshape(1, A_DV), ret_gain.reshape(1, DK))
    return mixed, sd, sr


def _mix_c_kernel(qk_ref, v_ref, opre_ref, gates_ref, brow_ref, bcol_ref, tri_ref, spread_ref, gain_ref,
                  h_ref, cm_ref, nv_ref, m_ref, *, n_sub):
    c = CHUNK
    lb = n_sub * c
    step = pl.program_id(1)

    @pl.when(step == 0)
    def _():
        cm_ref[...] = jnp.zeros_like(cm_ref)
        nv_ref[...] = jnp.zeros_like(nv_ref)
        m_ref[...] = jnp.zeros_like(m_ref)

    items = [(ci, h) for ci in range(n_sub) for h in range(HEADS)]
    rng = lambda ci: (ci * c, (ci + 1) * c)
    pre = {it: {} for it in items}

    for ci, h in items:
        r0, r1 = rng(ci)
        q = qk_ref[r0:r1, h * DK:(h + 1) * DK]
        k = qk_ref[r0:r1, (HEADS + h) * DK:(HEADS + h + 1) * DK] * (DK ** -0.5)
        pre[ci, h].update(q=q, k=k, qk=_mm_nt(q, k))

    g = gates_ref[...] + brow_ref[0:1, :]
    g_t = gates_ref[...].T[0:2 * HEADS, :] + bcol_ref[0:2 * HEADS, 0:1]
    i_rows = g_t[0:HEADS, :]
    logf_rows = _log_sigmoid(g_t[HEADS:2 * HEADS, :])
    tri = tri_ref[...]
    lf_hi, lf_lo = _split(_log_sigmoid(g))
    b_cols = (jnp.dot(tri, lf_hi, preferred_element_type=F32)
              + jnp.dot(tri, lf_lo, preferred_element_type=F32))
    lr_hi, lr_lo = _split(jnp.concatenate([logf_rows, jnp.zeros_like(logf_rows)], axis=0))
    nt = lambda x, y: lax.dot_general(x, y, (((1,), (1,)), ((), ())), preferred_element_type=F32)
    b_rows = nt(lr_hi, tri) + nt(lr_lo, tri)
    lane = lax.broadcasted_iota(jnp.int32, (lb, 128), 1)
    x_hi, x_lo = _split(jnp.where(lane < HEADS, g, b_cols))
    spread = (jnp.dot(x_hi, spread_ref[...], preferred_element_type=F32)
              + jnp.dot(x_lo, spread_ref[...], preferred_element_type=F32))
    i_wide = spread[:, 0:HEADS * 128]
    b_wide = spread[:, HEADS * 128:2 * HEADS * 128]

    ii, jj = _chunk_masks(c)
    causal = ii >= jj
    gain = gain_ref[...]
    wide2 = lambda x: jnp.concatenate([x, x], axis=-1)

    def gates_stage(ci, h):
        d = pre[ci, h]
        r0, r1 = rng(ci)
        b_col = b_wide[r0:r1, h * 128:(h + 1) * 128]
        i_col = i_wide[r0:r1, h * 128:(h + 1) * 128]
        b_last = b_col[c - 1:c, :]
        d_log = jnp.where(causal, b_col[:, 0:c] - b_rows[h:h + 1, r0:r1] + i_rows[h:h + 1, r0:r1], NEG)
        d.update(b_col=b_col, b_last=b_last, d_log=d_log, k_log=b_last - b_col + i_col)
        yield
        d['d_max'] = jnp.max(d_log, axis=1, keepdims=True)

    _interleave(gates_stage(ci, h) for ci, h in items)

    ms = [m_ref[0, h:h + 1, :] for h in range(HEADS)]
    for ci, h in items:
        d = pre[ci, h]
        inter = d['b_col'] + ms[h]
        m_row = jnp.maximum(inter, d['d_max'])
        m_new = m_row[c - 1:c, :]
        d.update(inter=inter, m_row=m_row, m_old=ms[h], m_new=m_new)
        ms[h] = m_new

    def weights_stage(ci, h):
        d = pre[ci, h]
        r0, r1 = rng(ci)
        d['w_inter'] = jnp.exp(d['inter'] - d['m_row'])
        d['f_state'] = jnp.exp(d['b_last'] + d['m_old'] - d['m_new'])
        yield
        d['w_intra'] = jnp.exp(d['d_log'] - d['m_row'][:, 0:c]) * d['qk']
        d['kw'] = d['k'] * jnp.exp(d['k_log'] - d['m_new'])
        yield
        v = v_ref[r0:r1, h * C_DV:(h + 1) * C_DV]
        d['intra'] = _mm(d['w_intra'], v)
        d['kv'] = _mm_tn(d['kw'], v)
        yield
        d['sum_intra'] = jnp.sum(d['w_intra'], axis=1, keepdims=True)
        d['sum_kw'] = jnp.sum(d['kw'], axis=0, keepdims=True)
        d['inv_floor'] = jnp.exp(-d['m_row'])

    _interleave(weights_stage(ci, h) for ci, h in items)

    cms = [cm_ref[0, h] for h in range(HEADS)]
    nvs = [nv_ref[0, h:h + 1, :] for h in range(HEADS)]
    for ci, h in items:
        d = pre[ci, h]
        d['cm_in'], d['nv_in'] = cms[h], nvs[h]
        cms[h] = cms[h] * wide2(d['f_state']) + d['kv']
        nvs[h] = nvs[h] * d['f_state'] + d['sum_kw']

    def output_stage(ci, h):
        d = pre[ci, h]
        r0, r1 = rng(ci)
        qc = _mm(d['q'], d['cm_in'])
        qn = jnp.sum(d['q'] * d['nv_in'], axis=1, keepdims=True)
        yield
        num = wide2(d['w_inter']) * qc + d['intra']
        den = d['w_inter'] * qn + d['sum_intra']
        hh = num / wide2(jnp.maximum(jnp.abs(den), d['inv_floor']))
        yield
        ms_ = jnp.mean(hh * hh, axis=-1, keepdims=True)
        yield
        op = opre_ref[r0:r1, h * C_DV:(h + 1) * C_DV]
        h_ref[r0:r1, h * C_DV:(h + 1) * C_DV] = hh * lax.rsqrt(ms_ + EPS) * gain * jax.nn.sigmoid(op)

    _interleave(output_stage(ci, h) for ci, h in items)

    for h in range(HEADS):
        cm_ref[0, h] = cms[h]
        nv_ref[0, h:h + 1, :] = nvs[h]
        m_ref[0, h:h + 1, :] = ms[h]


def _bias_params(b_gate):
    brow = jnp.zeros((8, 128), F32).at[0, 0:2 * HEADS].set(b_gate)
    bcol = jnp.zeros((8, 128), F32).at[0:2 * HEADS, 0].set(b_gate)
    return brow, bcol


def _mix_c_prompt(qk, v, opre, gates, batch, seq, b_gate, gain):
    lb = C_SUB * CHUNK
    assert seq % lb == 0
    nc = seq // lb
    brow, bcol = _bias_params(b_gate)
    tri = _chunk_tri(lb)
    spread = _head_spread(2)
    tok = lambda b, c: (b * nc + c, 0)
    const = lambda b, c: (0, 0)
    return pl.pallas_call(
        functools.partial(_mix_c_kernel, n_sub=C_SUB),
        grid=(batch, nc),
        in_specs=[pl.BlockSpec((lb, 2 * HEADS * DK), tok),
                  pl.BlockSpec((lb, HEADS * C_DV), tok),
                  pl.BlockSpec((lb, HEADS * C_DV), tok),
                  pl.BlockSpec((lb, 128), tok),
                  pl.BlockSpec((8, 128), const),
                  pl.BlockSpec((8, 128), const),
                  pl.BlockSpec(tri.shape, const),
                  pl.BlockSpec(spread.shape, const),
                  pl.BlockSpec((1, C_DV), const)],
        out_specs=[pl.BlockSpec((lb, HEADS * C_DV), tok),
                   pl.BlockSpec((1, HEADS, DK, C_DV), lambda b, c: (b, 0, 0, 0)),
                   pl.BlockSpec((1, HEADS, DK), lambda b, c: (b, 0, 0)),
                   pl.BlockSpec((1, HEADS, 128), lambda b, c: (b, 0, 0))],
        out_shape=[jax.ShapeDtypeStruct((batch * seq, HEADS * C_DV), F32),
                   jax.ShapeDtypeStruct((batch, HEADS, DK, C_DV), F32),
                   jax.ShapeDtypeStruct((batch, HEADS, DK), F32),
                   jax.ShapeDtypeStruct((batch, HEADS, 128), F32)],
        compiler_params=_cparams("parallel", "arbitrary"),
        name="mix_c_prompt",
    )(qk, v, opre, gates, brow, bcol, tri, spread, gain.reshape(1, C_DV))


def _row_select(rows, t, new, old):
    return jnp.where(rows == t, new, old)


def _step_ab_kernel(qkv_ref, rest_ref, gates_ref, cos_ref, sin_ref, wconv_ref, prow_ref, ggain_ref, rgain_ref,
                    buf_ref, sd_ref, sr_ref, mixed_ref, nbuf_ref, nsd_ref, nsr_ref):
    tb = qkv_ref.shape[0]
    u = qkv_ref[...]
    w = wconv_ref[...]
    b0 = buf_ref[:, 0:A_CONV_CH]
    b1 = buf_ref[:, A_CONV_CH:2 * A_CONV_CH]
    b2 = buf_ref[:, 2 * A_CONV_CH:3 * A_CONV_CH]
    conv = b0 * w[0:1] + b1 * w[1:2] + b2 * w[2:3] + u * w[3:4]
    nbuf_ref[:, 0:A_CONV_CH] = b1
    nbuf_ref[:, A_CONV_CH:2 * A_CONV_CH] = b2
    nbuf_ref[:, 2 * A_CONV_CH:3 * A_CONV_CH] = u
    act = _silu(conv)
    g = gates_ref[...]
    eg_all = jnp.exp(-jnp.exp(prow_ref[0:1, :]) * _softplus(g + prow_ref[1:2, :]))
    beta_all = jax.nn.sigmoid(g)
    cosf = cos_ref[...]
    sinf = sin_ref[...]
    base = HEADS * DK
    rows = lax.broadcasted_iota(jnp.int32, (tb, DK), 0)
    items = [(t, h) for t in range(tb) for h in range(HEADS)]

    qs = [_l2(act[:, h * DK:(h + 1) * DK]) * (DK ** -0.5) for h in range(HEADS)]
    ks = [_l2(act[:, (HEADS + h) * DK:(HEADS + h + 1) * DK]) for h in range(HEADS)]
    k_s = [jnp.zeros((tb, A_DV), F32) for _ in range(HEADS)]
    for t, h in items:
        k_s[h] = _row_select(rows, t, _mm(ks[h], sd_ref[t, h]), k_s[h])
    v_new = []
    for h in range(HEADS):
        v = act[:, (2 * HEADS + h) * DK:(2 * HEADS + h + 1) * DK]
        v_new.append(beta_all[:, HEADS + h:HEADS + h + 1] * (v - eg_all[:, h:h + 1] * k_s[h]))
    for t, h in items:
        nsd_ref[t, h] = (sd_ref[t, h] * eg_all[t:t + 1, h:h + 1]
                         + _mm3_tn(jnp.where(rows == t, ks[h], 0.0), v_new[h]))
    o_a = [jnp.zeros((tb, A_DV), F32) for _ in range(HEADS)]
    for t, h in items:
        o_a[h] = _row_select(rows, t, _mm(qs[h], nsd_ref[t, h]), o_a[h])
    for h in range(HEADS):
        ga = rest_ref[:, h * DK:(h + 1) * DK]
        mixed_ref[:, h * DK:(h + 1) * DK] = _rms(o_a[h], ggain_ref[...]) * _silu(ga)

    qrs, krs = [], []
    for h in range(HEADS):
        qb = rest_ref[:, base + h * DK:base + (h + 1) * DK]
        kb = rest_ref[:, 2 * base + h * DK:2 * base + (h + 1) * DK]
        qrs.append(qb * cosf + pltpu.roll(qb, DK // 2, 1) * sinf)
        krs.append((kb * cosf + pltpu.roll(kb, DK // 2, 1) * sinf) * (DK ** -0.5))
    for t, h in items:
        vb = rest_ref[:, 3 * base + h * DK:3 * base + (h + 1) * DK]
        nsr_ref[t, h] = (sr_ref[t, h] * math.exp(LOG_GAMMA[h])
                         + _mm3_tn(jnp.where(rows == t, krs[h], 0.0), vb))
    o_b = [jnp.zeros((tb, DK), F32) for _ in range(HEADS)]
    for t, h in items:
        o_b[h] = _row_select(rows, t, _mm(qrs[h], nsr_ref[t, h]), o_b[h])
    for h in range(HEADS):
        gb = rest_ref[:, 4 * base + h * DK:4 * base + (h + 1) * DK]
        mixed_ref[:, base + h * DK:base + (h + 1) * DK] = _rms(o_b[h], rgain_ref[...]) * _silu(gb)


def _mix_ab_sample(qkv, rest, gates, conv_buf, s_delta, s_ret, w_conv, a_log, dt_bias, gdn_gain, ret_gain):
    nb = qkv.shape[0]
    tb = SAMPLE_TB
    assert nb % tb == 0
    cosf, sinf = _rope_tables(PAST_LEN + jnp.arange(1))
    prow, _ = _gate_params(a_log, dt_bias)
    tok = lambda b: (b, 0)
    tok4 = lambda b: (b, 0, 0, 0)
    const = lambda b: (0, 0)
    n_buf = (CONV_W - 1) * A_CONV_CH
    mixed, nbuf, nsd, nsr = pl.pallas_call(
        _step_ab_kernel,
        grid=(nb // tb,),
        in_specs=[pl.BlockSpec((tb, A_CONV_CH), tok),
                  pl.BlockSpec((tb, rest.shape[1]), tok),
                  pl.BlockSpec((tb, 128), tok),
                  pl.BlockSpec((1, DK), const),
                  pl.BlockSpec((1, DK), const),
                  pl.BlockSpec((CONV_W, A_CONV_CH), const),
                  pl.BlockSpec((8, 128), const),
                  pl.BlockSpec((1, A_DV), const),
                  pl.BlockSpec((1, DK), const),
                  pl.BlockSpec((tb, n_buf), tok),
                  pl.BlockSpec((tb, HEADS, DK, A_DV), tok4),
                  pl.BlockSpec((tb, HEADS, DK, DK), tok4)],
        out_specs=[pl.BlockSpec((tb, 2 * HEADS * DK), tok),
                   pl.BlockSpec((tb, n_buf), tok),
                   pl.BlockSpec((tb, HEADS, DK, A_DV), tok4),
                   pl.BlockSpec((tb, HEADS, DK, DK), tok4)],
        out_shape=[jax.ShapeDtypeStruct((nb, 2 * HEADS * DK), F32),
                   jax.ShapeDtypeStruct((nb, n_buf), F32),
                   jax.ShapeDtypeStruct((nb, HEADS, DK, A_DV), F32),
                   jax.ShapeDtypeStruct((nb, HEADS, DK, DK), F32)],
        compiler_params=_cparams("parallel"),
        name="mix_ab_sample",
    )(qkv, rest, gates, cosf, sinf, w_conv, prow, gdn_gain.reshape(1, A_DV), ret_gain.reshape(1, DK),
      conv_buf.reshape(nb, n_buf), s_delta, s_ret)
    return mixed, nbuf.reshape(nb, CONV_W - 1, A_CONV_CH), nsd, nsr


def _step_c_kernel(qk_ref, v_ref, opre_ref, gates_ref, brow_ref, gain_ref, cm_ref, nv_ref, m_ref,
                   h_ref, ncm_ref, nnv_ref, nm_ref):
    tb = qk_ref.shape[0]
    g = gates_ref[...] + brow_ref[0:1, :]
    logf = _log_sigmoid(g)
    rows = lax.broadcasted_iota(jnp.int32, (tb, DK), 0)
    items = [(t, h) for t in range(tb) for h in range(HEADS)]
    qs, kws, fs, ms_new, nvs_new = [], [], [], [], []
    for h in range(HEADS):
        i_pre = g[:, h:h + 1]
        q = qk_ref[:, h * DK:(h + 1) * DK]
        k = qk_ref[:, (HEADS + h) * DK:(HEADS + h + 1) * DK] * (DK ** -0.5)
        inter = logf[:, HEADS + h:HEADS + h + 1] + m_ref[:, h:h + 1]
        m_new = jnp.maximum(inter, i_pre)
        f_state = jnp.exp(inter - m_new)
        kw = k * jnp.exp(i_pre - m_new)
        nv_new = nv_ref[:, h * DK:(h + 1) * DK] * f_state + kw
        nnv_ref[:, h * DK:(h + 1) * DK] = nv_new
        nm_ref[:, h:h + 1] = m_new
        qs.append(q)
        kws.append(kw)
        fs.append(f_state)
        ms_new.append(m_new)
        nvs_new.append(nv_new)
    for t, h in items:
        v = v_ref[:, h * C_DV:(h + 1) * C_DV]
        ncm_ref[t, h] = (cm_ref[t, h] * fs[h][t:t + 1, :]
                         + _mm3_tn(jnp.where(rows == t, kws[h], 0.0), v))
    rows_v = lax.broadcasted_iota(jnp.int32, (tb, C_DV), 0)
    nums = [jnp.zeros((tb, C_DV), F32) for _ in range(HEADS)]
    for t, h in items:
        nums[h] = _row_select(rows_v, t, _mm(qs[h], ncm_ref[t, h]), nums[h])
    for h in range(HEADS):
        den = jnp.sum(qs[h] * nvs_new[h], axis=1, keepdims=True)
        hh = nums[h] / jnp.maximum(jnp.abs(den), jnp.exp(-ms_new[h]))
        op = opre_ref[:, h * C_DV:(h + 1) * C_DV]
        h_ref[:, h * C_DV:(h + 1) * C_DV] = _rms(hh, gain_ref[...]) * jax.nn.sigmoid(op)


def _mix_c_sample(qk, v, opre, gates, s_c, s_n, s_m, b_gate, gain):
    nb = qk.shape[0]
    tb = SAMPLE_TB
    assert nb % tb == 0
    brow, _ = _bias_params(b_gate)
    tok = lambda b: (b, 0)
    tok4 = lambda b: (b, 0, 0, 0)
    const = lambda b: (0, 0)
    hh, ncm, nnv, nm = pl.pallas_call(
        _step_c_kernel,
        grid=(nb // tb,),
        in_specs=[pl.BlockSpec((tb, 2 * HEADS * DK), tok),
                  pl.BlockSpec((tb, HEADS * C_DV), tok),
                  pl.BlockSpec((tb, HEADS * C_DV), tok),
                  pl.BlockSpec((tb, 128), tok),
                  pl.BlockSpec((8, 128), const),
                  pl.BlockSpec((1, C_DV), const),
                  pl.BlockSpec((tb, HEADS, DK, C_DV), tok4),
                  pl.BlockSpec((tb, HEADS * DK), tok),
                  pl.BlockSpec((tb, HEADS), tok)],
        out_specs=[pl.BlockSpec((tb, HEADS * C_DV), tok),
                   pl.BlockSpec((tb, HEADS, DK, C_DV), tok4),
                   pl.BlockSpec((tb, HEADS * DK), tok),
                   pl.BlockSpec((tb, HEADS), tok)],
        out_shape=[jax.ShapeDtypeStruct((nb, HEADS * C_DV), F32),
                   jax.ShapeDtypeStruct((nb, HEADS, DK, C_DV), F32),
                   jax.ShapeDtypeStruct((nb, HEADS * DK), F32),
                   jax.ShapeDtypeStruct((nb, HEADS), F32)],
        compiler_params=_cparams("parallel"),
        name="mix_c_sample",
    )(qk, v, opre, gates, brow, gain.reshape(1, C_DV), s_c, s_n.reshape(nb, HEADS * DK), s_m)
    return hh, ncm, nnv.reshape(nb, HEADS, DK), nm


A_SPLITS = ((A_CONV_CH,), (5 * HEADS * DK,), (128,))
C_SPLITS = ((2 * HEADS * DK, HEADS * C_DV, HEADS * C_DV), (128,))


def _gate_columns(w_gates):
    return jnp.pad(w_gates, ((0, 0), (0, 128 - w_gates.shape[1]))).astype(BF16)


def _prep_w_in_a(w):
    n_gate = 2 * HEADS
    return (w[:, :A_CONV_CH].astype(BF16), w[:, A_CONV_CH + n_gate:].astype(BF16),
            _gate_columns(w[:, A_CONV_CH:A_CONV_CH + n_gate]))


def _prep_w_in_c(w):
    n_main = 2 * HEADS * DK + 2 * HEADS * C_DV
    return (w[:, :n_main].astype(BF16), _gate_columns(w[:, n_main:]))


def kernel(x_prompt, x_sample, state_conv_a, state_delta_a, state_ret_b, state_mlstm_C, state_mlstm_n, state_mlstm_m,
           norm_mix_a, w_in_a, w_conv_a, a_log, dt_bias, gdn_gain, ret_gain, w_out_a,
           norm_mix_c, w_in_c, b_gate_c, mlstm_gain, w_out_c, norm_ffn, w_up, w_down, final_gain):
    batch, seq, _ = x_prompt.shape
    n_s = x_sample.shape[0] * x_sample.shape[1]
    tm = PROMPT_TM
    w_a = _prep_w_in_a(w_in_a[0])
    w_c = _prep_w_in_c(w_in_c[0])
    w_ups = w_up.astype(BF16)
    w_downs = w_down.astype(BF16)
    h = x_prompt.reshape(batch * seq, D_MODEL)
    h_s = x_sample.reshape(n_s, D_MODEL)

    (qkv, rest, gates, conv), (qkv_s, rest_s, gates_s) = _proj_a_prompt(
        h, h_s, norm_mix_a[0], w_a, w_conv_a[0], batch, seq, tm)
    mixed, sd, sr = _mix_ab_prompt(qkv, rest, gates, batch, seq, a_log[0], dt_bias[0], gdn_gain[0], ret_gain[0])
    mixed_s, conv_s, sd_s, sr_s = _mix_ab_sample(qkv_s, rest_s, gates_s, state_conv_a[0], state_delta_a[0],
                                                 state_ret_b[0], w_conv_a[0], a_log[0], dt_bias[0], gdn_gain[0],
                                                 ret_gain[0])
    h, h_s = _out_mlp(h, mixed, h_s, mixed_s, w_out_a[0].astype(BF16), norm_ffn[0], w_ups, w_downs, 0, None, tm)

    (qk, v, opre, gates_c), (qk_s, v_s, opre_s, gates_cs) = _norm_proj(h, h_s, norm_mix_c[0], w_c, C_SPLITS, tm)
    hm, cm, nv, m = _mix_c_prompt(qk, v, opre, gates_c, batch, seq, b_gate_c[0], mlstm_gain[0])
    hm_s, cm_s, nv_s, m_s = _mix_c_sample(qk_s, v_s, opre_s, gates_cs, state_mlstm_C[0], state_mlstm_n[0],
                                          state_mlstm_m[0], b_gate_c[0], mlstm_gain[0])
    y, y_s = _out_mlp(h, hm, h_s, hm_s, w_out_c[0].astype(BF16), norm_ffn[1], w_ups, w_downs, 1, final_gain, tm)

    return (y.reshape(x_prompt.shape), y_s.reshape(x_sample.shape),
            conv[None], sd[None], sr[None], cm[None], nv[None], m[:, :, 0][None],
            conv_s[None], sd_s[None], sr_s[None], cm_s[None], nv_s[None], m_s[None])
```

```python
import functools
import math

import jax
import jax.numpy as jnp
from jax import lax
from jax.experimental import pallas as pl
from jax.experimental.pallas import tpu as pltpu

F32 = jnp.float32
BF16 = jnp.bfloat16

D_MODEL = 1024
D_FF = 4 * D_MODEL
CHUNK = 64
EPS = 1e-6
NEG = -1e30
HEADS = 4
DK = 128
A_DV = 128
C_DV = 256
CONV_W = 4
A_CONV_CH = 3 * HEADS * DK
ROPE_BASE = 10000.0
PAST_LEN = 16384
LOG_GAMMA = tuple(math.log1p(-(2.0 ** (-5.0 - h))) for h in range(HEADS))

VMEM_LIMIT_BYTES = 56 * 1024 * 1024
PROMPT_TM = 512
AB_SUB = 8
C_SUB = 8
SAMPLE_TB = 8


def _cparams(*sem):
    return pltpu.CompilerParams(dimension_semantics=sem, vmem_limit_bytes=VMEM_LIMIT_BYTES)


def _mm(a, b):
    return jnp.dot(a.astype(BF16), b.astype(BF16), preferred_element_type=F32)


def _mm_nt(a, b):
    return lax.dot_general(a.astype(BF16), b.astype(BF16), (((1,), (1,)), ((), ())), preferred_element_type=F32)


def _mm_tn(a, b):
    return lax.dot_general(a.astype(BF16), b.astype(BF16), (((0,), (0,)), ((), ())), preferred_element_type=F32)


def _split(a):
    hi = a.astype(BF16)
    lo = (a - hi.astype(F32)).astype(BF16)
    return hi, lo


def _mm3(a, b):
    ah, al = _split(a)
    bh, bl = _split(b)
    d = lambda x, y: jnp.dot(x, y, preferred_element_type=F32)
    return d(ah, bh) + (d(ah, bl) + d(al, bh))


def _mm3_tn(a, b):
    ah, al = _split(a)
    bh, bl = _split(b)
    d = lambda x, y: lax.dot_general(x, y, (((0,), (0,)), ((), ())), preferred_element_type=F32)
    return d(ah, bh) + (d(ah, bl) + d(al, bh))


def _softplus(x):
    return jnp.maximum(x, 0.0) + jnp.log1p(jnp.exp(-jnp.abs(x)))


def _log_sigmoid(x):
    return -_softplus(-x)


def _silu(x):
    return x * jax.nn.sigmoid(x)


def _rms(x, gain):
    return x * lax.rsqrt(jnp.mean(x * x, axis=-1, keepdims=True) + EPS) * gain


def _l2(t):
    return t * lax.rsqrt(jnp.sum(t * t, axis=-1, keepdims=True) + EPS)


def _interleave(gens):
    gens = list(gens)
    while gens:
        alive = []
        for gen in gens:
            try:
                next(gen)
                alive.append(gen)
            except StopIteration:
                pass
        gens = alive


def _lockstep(gens):
    gens = list(gens)
    while gens:
        alive = []
        for gen in gens:
            try:
                next(gen)
                alive.append(gen)
            except StopIteration:
                pass
        gens = alive
        if gens:
            yield


def _head_block_rows(x):
    c, n = x.shape
    t = n // HEADS
    z = jnp.zeros((c, t), x.dtype)
    return jnp.concatenate(
        [jnp.concatenate([x[:, h * t:(h + 1) * t] if g == h else z for g in range(HEADS)], axis=1)
         for h in range(HEADS)], axis=0)


def _head_block_diag(y):
    c, n = y.shape
    per_tile = 128 // c
    assert n == HEADS * c and 128 % c == 0 and HEADS % per_tile == 0
    lane = lax.broadcasted_iota(jnp.int32, (c, 128), 1)
    z = jnp.zeros((c, 128), y.dtype)
    blocks = []
    for h in range(HEADS):
        t = h // per_tile
        lo = (h % per_tile) * c
        kept = jnp.where((lane >= lo) & (lane < lo + c), y[:, t * 128:(t + 1) * 128], z)
        blocks.append(jnp.concatenate([kept if g == t else z for g in range(n // 128)], axis=1))
    return jnp.concatenate(blocks, axis=0)


def _wide_mm(x, y):
    return jnp.dot(x.astype(BF16), _head_block_diag(y.astype(BF16)), preferred_element_type=F32)


def _tri_inv_wide(ns):
    c, wd = ns[0].shape
    ii = lax.broadcasted_iota(jnp.int32, (c, wd), 0)
    jl = lax.broadcasted_iota(jnp.int32, (c, wd), 1) & (c - 1)
    eye = (ii == jl).astype(F32)
    ts = [eye - jnp.where((ii >> 1) == (jl >> 1), n, 0.0) for n in ns]
    for lvl in range(1, int(math.log2(c))):
        off = ((ii >> (lvl + 1)) == (jl >> (lvl + 1))) & ((ii >> lvl) != (jl >> lvl))
        ys = [_wide_mm(jnp.where(off, n, 0.0), t) for n, t in zip(ns, ts)]
        yield
        ts = [t - _wide_mm(t, y) for t, y in zip(ts, ys)]
        yield
    return [t - eye for t in ts]


def _project_rows(x_ref, g_ref, w_refs, out_refs, splits):
    xn = _rms(x_ref[...], g_ref[...]).astype(BF16)
    o = 0
    for w_ref, widths in zip(w_refs, splits):
        off = 0
        for n in widths:
            out_refs[o][...] = jnp.dot(xn, w_ref[:, off:off + n], preferred_element_type=F32)
            off += n
            o += 1


def _norm_proj_kernel(x_ref, xs_ref, g_ref, *refs, splits):
    n_out = sum(len(s) for s in splits)
    w_refs = refs[:len(splits)]
    out_refs = refs[len(splits):len(splits) + n_out]
    outs_refs = refs[len(splits) + n_out:]
    _project_rows(x_ref, g_ref, w_refs, out_refs, splits)

    @pl.when(pl.program_id(0) == 0)
    def _():
        _project_rows(xs_ref, g_ref, w_refs, outs_refs, splits)


def _norm_proj(x, x_s, gain, weights, splits, tm):
    t, ts = x.shape[0], x_s.shape[0]
    assert t % tm == 0 and all(sum(s) == w.shape[1] for w, s in zip(weights, splits))
    widths = [n for s in splits for n in s]
    row = lambda i: (i, 0)
    const = lambda i: (0, 0)
    outs = pl.pallas_call(
        functools.partial(_norm_proj_kernel, splits=splits),
        grid=(t // tm,),
        in_specs=[pl.BlockSpec((tm, D_MODEL), row), pl.BlockSpec((ts, D_MODEL), const),
                  pl.BlockSpec((1, D_MODEL), const)]
                 + [pl.BlockSpec(w.shape, const, pipeline_mode=pl.Buffered(1)) for w in weights],
        out_specs=[pl.BlockSpec((tm, n), row) for n in widths] + [pl.BlockSpec((ts, n), const) for n in widths],
        out_shape=[jax.ShapeDtypeStruct((t, n), F32) for n in widths]
                  + [jax.ShapeDtypeStruct((ts, n), F32) for n in widths],
        compiler_params=_cparams("arbitrary"),
        name="norm_proj",
    )(x, x_s, gain.reshape(1, D_MODEL), *weights)
    return outs[:len(widths)], outs[len(widths):]


def _proj_a_prompt_kernel(x_ref, xs_ref, g_ref, wqkv_ref, wrest_ref, wgate_ref, wconv_ref, cos_ref, sin_ref,
                          qkv_ref, rest_ref, gates_ref, conv_ref, qkvs_ref, rests_ref, gatess_ref,
                          xp_ref, raw_ref, *, tiles_per_seq):
    tm = x_ref.shape[0]
    hd = HEADS * DK

    @pl.when(pl.program_id(0) == 0)
    def _():
        _project_rows(xs_ref, g_ref, (wqkv_ref, wrest_ref, wgate_ref), (qkvs_ref, rests_ref, gatess_ref), A_SPLITS)

    xn = _rms(x_ref[...], g_ref[...]).astype(BF16)

    @pl.when(pl.program_id(0) % tiles_per_seq == 0)
    def _():
        xp_ref[0:8, :] = jnp.zeros((8, A_CONV_CH), F32)

    w = wconv_ref[...]
    base = A_CONV_CH

    def conv_part(part, slot, r0, r1):
        c0, c1 = part * hd, (part + 1) * hd
        raw = raw_ref[slot, r0:r1, :]
        xp_ref[8 + r0:8 + r1, c0:c1] = raw
        conv = (xp_ref[5 + r0:5 + r1, c0:c1] * w[0:1, c0:c1] + xp_ref[6 + r0:6 + r1, c0:c1] * w[1:2, c0:c1]
                + xp_ref[7 + r0:7 + r1, c0:c1] * w[2:3, c0:c1] + raw * w[3:4, c0:c1])
        if r1 == tm:
            rows = r1 - r0
            xp_ref[0:8, c0:c1] = raw[rows - 8:rows, :]
            conv_ref[0, :, c0:c1] = raw[rows - 3:rows, :]
        act = _silu(conv)
        if part == 2:
            qkv_ref[r0:r1, c0:c1] = act
        else:
            scale = DK ** -0.5 if part == 0 else 1.0
            for h in range(HEADS):
                qkv_ref[r0:r1, c0 + h * DK:c0 + (h + 1) * DK] = _l2(act[:, h * DK:(h + 1) * DK]) * scale

    def rest_part(part, slot, r0, r1):
        c0, c1 = part * hd, (part + 1) * hd
        raw = raw_ref[slot, r0:r1, :]
        if part in (0, 4):
            rest_ref[r0:r1, c0:c1] = _silu(raw)
        elif part == 3:
            rest_ref[r0:r1, c0:c1] = raw
        else:
            scale = 1.0 if part == 1 else DK ** -0.5
            for h in range(HEADS):
                t = raw[:, h * DK:(h + 1) * DK]
                rest_ref[r0:r1, c0 + h * DK:c0 + (h + 1) * DK] = (
                    (t * cos_ref[r0:r1, :] + pltpu.roll(t, DK // 2, 1) * sin_ref[r0:r1, :]) * scale)

    def gates_part(_, slot, r0, r1):
        gates_ref[r0:r1, :] = raw_ref[slot, r0:r1, 0:128]

    groups = [(wqkv_ref, 0, hd, functools.partial(conv_part, 0)),
              (wrest_ref, 3 * hd, hd, functools.partial(rest_part, 3)),
              (wqkv_ref, hd, hd, functools.partial(conv_part, 1)),
              (wrest_ref, 0, hd, functools.partial(rest_part, 0)),
              (wqkv_ref, 2 * hd, hd, functools.partial(conv_part, 2)),
              (wrest_ref, hd, hd, functools.partial(rest_part, 1)),
              (wgate_ref, 0, 128, functools.partial(gates_part, 0)),
              (wrest_ref, 2 * hd, hd, functools.partial(rest_part, 2)),
              (wrest_ref, 4 * hd, hd, functools.partial(rest_part, 4))]

    def project(n):
        w_ref, col0, width, _ = groups[n]
        for j in range(0, width, 256):
            wj = min(256, width - j)
            raw_ref[n % 2, :, j:j + wj] = jnp.dot(xn, w_ref[:, col0 + j:col0 + j + wj],
                                                  preferred_element_type=F32)
            yield

    def epilogue(n):
        for r0 in range(0, tm, 128):
            groups[n][3](n % 2, r0, min(r0 + 128, tm))
            yield

    _interleave([project(0)])
    for n in range(len(groups)):
        _interleave([epilogue(n)] + ([project(n + 1)] if n + 1 < len(groups) else []))


def _proj_a_prompt(x, x_s, gain, weights, w_conv, batch, seq, tm):
    t, ts = x.shape[0], x_s.shape[0]
    assert t % tm == 0 and seq % tm == 0
    tiles_per_seq = seq // tm
    cosf, sinf = _rope_tables(jnp.arange(seq))
    row = lambda i: (i, 0)
    const = lambda i: (0, 0)
    pos = lambda i: (i % tiles_per_seq, 0)
    n_rest = 5 * HEADS * DK
    outs = pl.pallas_call(
        functools.partial(_proj_a_prompt_kernel, tiles_per_seq=tiles_per_seq),
        grid=(t // tm,),
        in_specs=[pl.BlockSpec((tm, D_MODEL), row),
                  pl.BlockSpec((ts, D_MODEL), const),
                  pl.BlockSpec((1, D_MODEL), const)]
                 + [pl.BlockSpec(w.shape, const, pipeline_mode=pl.Buffered(1)) for w in weights]
                 + [pl.BlockSpec((CONV_W, A_CONV_CH), const),
                    pl.BlockSpec((tm, DK), pos),
                    pl.BlockSpec((tm, DK), pos)],
        out_specs=[pl.BlockSpec((tm, A_CONV_CH), row),
                   pl.BlockSpec((tm, n_rest), row),
                   pl.BlockSpec((tm, 128), row),
                   pl.BlockSpec((1, CONV_W - 1, A_CONV_CH), lambda i: (i // tiles_per_seq, 0, 0)),
                   pl.BlockSpec((ts, A_CONV_CH), const),
                   pl.BlockSpec((ts, n_rest), const),
                   pl.BlockSpec((ts, 128), const)],
        out_shape=[jax.ShapeDtypeStruct((t, A_CONV_CH), F32),
                   jax.ShapeDtypeStruct((t, n_rest), F32),
                   jax.ShapeDtypeStruct((t, 128), F32),
                   jax.ShapeDtypeStruct((batch, CONV_W - 1, A_CONV_CH), F32),
                   jax.ShapeDtypeStruct((ts, A_CONV_CH), F32),
                   jax.ShapeDtypeStruct((ts, n_rest), F32),
                   jax.ShapeDtypeStruct((ts, 128), F32)],
        scratch_shapes=[pltpu.VMEM((tm + 8, A_CONV_CH), F32), pltpu.VMEM((2, tm, HEADS * DK), F32)],
        compiler_params=_cparams("arbitrary"),
        name="proj_a_prompt",
    )(x, x_s, gain.reshape(1, D_MODEL), *weights, w_conv, cosf, sinf)
    return outs[:4], outs[4:]


def _out_mlp_kernel(h_ref, m_ref, hs_ref, ms_ref, wout_ref, gffn_ref, wup_ref, wdown_ref, *rest, final):
    if final:
        gfin_ref, o_ref, os_ref = rest
    else:
        o_ref, os_ref = rest

    def block(h_r, m_r, o_r):
        h = h_r[...] + jnp.dot(m_r[...].astype(BF16), wout_ref[...], preferred_element_type=F32)
        xn = _rms(h, gffn_ref[...]).astype(BF16)
        acc = h
        step = 1024
        for j in range(D_FF // step):
            hid = jnp.dot(xn, wup_ref[:, j * step:(j + 1) * step], preferred_element_type=F32)
            hid = jnp.maximum(hid, 0.0)
            acc = acc + jnp.dot((hid * hid).astype(BF16), wdown_ref[j * step:(j + 1) * step, :],
                                preferred_element_type=F32)
        if final:
            acc = _rms(acc, gfin_ref[...])
        o_r[...] = acc

    block(h_ref, m_ref, o_ref)

    @pl.when(pl.program_id(0) == 0)
    def _():
        block(hs_ref, ms_ref, os_ref)


def _out_mlp(h, mix, h_s, mix_s, w_out, g_ffn, w_up, w_down, layer, g_final, tm):
    t, ts = h.shape[0], h_s.shape[0]
    assert t % tm == 0
    final = g_final is not None
    row = lambda i: (i, 0)
    const = lambda i: (0, 0)
    this_layer = lambda i: (layer, 0, 0)
    once = pl.Buffered(1)
    in_specs = [pl.BlockSpec((tm, D_MODEL), row), pl.BlockSpec((tm, mix.shape[1]), row),
                pl.BlockSpec((ts, D_MODEL), const), pl.BlockSpec((ts, mix_s.shape[1]), const),
                pl.BlockSpec(w_out.shape, const, pipeline_mode=once), pl.BlockSpec((1, D_MODEL), const),
                pl.BlockSpec((None,) + w_up.shape[1:], this_layer, pipeline_mode=once),
                pl.BlockSpec((None,) + w_down.shape[1:], this_layer, pipeline_mode=once)]
    args = [h, mix, h_s, mix_s, w_out, g_ffn.reshape(1, D_MODEL), w_up, w_down]
    if final:
        in_specs.append(pl.BlockSpec((1, D_MODEL), const))
        args.append(g_final.reshape(1, D_MODEL))
    return pl.pallas_call(
        functools.partial(_out_mlp_kernel, final=final),
        grid=(t // tm,),
        in_specs=in_specs,
        out_specs=[pl.BlockSpec((tm, D_MODEL), row), pl.BlockSpec((ts, D_MODEL), const)],
        out_shape=[jax.ShapeDtypeStruct((t, D_MODEL), F32), jax.ShapeDtypeStruct((ts, D_MODEL), F32)],
        compiler_params=_cparams("arbitrary"),
        name="out_mlp",
    )(*args)


def _chunk_masks(c):
    ii = lax.broadcasted_iota(jnp.int32, (c, c), 0)
    jj = lax.broadcasted_iota(jnp.int32, (c, c), 1)
    return ii, jj


def _mix_ab_kernel(qkv_ref, rest_ref, gates_ref, prow_ref, pcol_ref, tri_ref, spread_ref,
                   ggain_ref, rgain_ref, mixed_ref, sd_ref, sr_ref, *, n_sub):
    c = CHUNK
    lb = n_sub * c
    step = pl.program_id(1)

    @pl.when(step == 0)
    def _():
        sd_ref[...] = jnp.zeros_like(sd_ref)
        sr_ref[...] = jnp.zeros_like(sr_ref)

    g = gates_ref[...]
    g_t = g.T
    neg_a_row = -jnp.exp(prow_ref[0:1, :])
    dt_row = prow_ref[1:2, :]
    neg_a_col = -jnp.exp(pcol_ref[0:HEADS, 0:1])
    dt_col = pcol_ref[0:HEADS, 1:2]
    la_cols = neg_a_row * _softplus(g + dt_row)
    beta_cols = jax.nn.sigmoid(g)
    la_rows = neg_a_col * _softplus(g_t[0:HEADS, :] + dt_col)

    tri = tri_ref[...]
    la_hi, la_lo = _split(la_cols)
    g_cols = (jnp.dot(tri, la_hi, preferred_element_type=F32)
              + jnp.dot(tri, la_lo, preferred_element_type=F32))
    lr_hi, lr_lo = _split(jnp.concatenate([la_rows, jnp.zeros_like(la_rows)], axis=0))
    nt = lambda x, y: lax.dot_general(x, y, (((1,), (1,)), ((), ())), preferred_element_type=F32)
    g_rows = nt(lr_hi, tri) + nt(lr_lo, tri)
    lane = lax.broadcasted_iota(jnp.int32, (lb, 128), 1)
    x_hi, x_lo = _split(jnp.where(lane < HEADS, g_cols, beta_cols))
    spread = (jnp.dot(x_hi, spread_ref[...], preferred_element_type=F32)
              + jnp.dot(x_lo, spread_ref[...], preferred_element_type=F32))
    hd = HEADS * DK
    wd = HEADS * c
    g_wide = spread[:, 0:hd]
    beta_wide = spread[:, hd:2 * hd]
    g_half = spread[:, 2 * hd:2 * hd + wd]
    eg_wide = jnp.exp(g_wide)

    ii = lax.broadcasted_iota(jnp.int32, (c, wd), 0)
    jl = lax.broadcasted_iota(jnp.int32, (c, wd), 1) & (c - 1)
    causal = ii >= jl
    strict = ii > jl
    lane_hd = lax.broadcasted_iota(jnp.int32, (1, hd), 1)
    lane_wd = lax.broadcasted_iota(jnp.int32, (1, wd), 1)
    lg_hd = jnp.full((1, hd), LOG_GAMMA[HEADS - 1], F32)
    lg_wd = jnp.full((1, wd), LOG_GAMMA[HEADS - 1], F32)
    for h in range(HEADS - 2, -1, -1):
        lg_hd = jnp.where(lane_hd < (h + 1) * DK, LOG_GAMMA[h], lg_hd)
        lg_wd = jnp.where(lane_wd < (h + 1) * c, LOG_GAMMA[h], lg_wd)
    pos_col = lax.broadcasted_iota(jnp.int32, (c, hd), 0).astype(F32)
    ret_dec = jnp.exp(jnp.where(causal, (ii - jl).astype(F32) * lg_wd, NEG))
    ret_q_scale = jnp.exp((pos_col + 1.0) * lg_hd)
    ret_k_scale = jnp.exp((float(c - 1) - pos_col) * lg_hd)

    ggain = ggain_ref[...]
    rgain = rgain_ref[...]
    base = hd
    tile = lambda x, h: x[:, h * DK:(h + 1) * DK]
    s_delta = [sd_ref[0, h] for h in range(HEADS)]
    s_ret = [sr_ref[0, h] for h in range(HEADS)]

    chunks = list(range(n_sub))
    items = [(ci, h) for ci in chunks for h in range(HEADS)]
    rng = lambda ci: (ci * c, (ci + 1) * c)
    pre = {it: {} for it in items}
    pre_b = {it: {} for it in items}
    wide = {ci: {} for ci in chunks}
    nt = lambda x, y: lax.dot_general(x, y, (((1,), (1,)), ((), ())), preferred_element_type=F32)

    def gdn_stage(ci):
        w_ = wide[ci]
        r0, r1 = rng(ci)
        q = qkv_ref[r0:r1, 0:hd]
        k = qkv_ref[r0:r1, hd:2 * hd]
        beta = beta_wide[r0:r1]
        g_col = g_wide[r0:r1]
        kb = k * beta
        g_row = jnp.concatenate([g_rows[h:h + 1, r0:r1] for h in range(HEADS)], axis=1)
        dec_causal = jnp.exp(jnp.where(causal, g_half[r0:r1] - g_row, NEG))
        prod = nt(jnp.concatenate([q, kb], axis=0).astype(BF16), _head_block_rows(k.astype(BF16)))
        yield
        w_['n'] = prod[c:2 * c] * jnp.where(strict, dec_causal, 0.0)
        w_['qk'] = prod[0:c] * dec_causal
        yield
        v = qkv_ref[r0:r1, 2 * hd:3 * hd]
        eg = eg_wide[r0:r1]
        g_last = g_col[c - 1:c, :]
        rhs_u, rhs_w, qd = v * beta, kb * eg, q * eg
        kd = k * jnp.exp(g_last - g_col)
        gl = jnp.exp(g_last)
        for h in range(HEADS):
            pre[ci, h].update(rhs=jnp.concatenate([tile(rhs_u, h), tile(rhs_w, h)], axis=1), qd=tile(qd, h),
                              kd=tile(kd, h), gl=tile(gl, h))

    def ret_stage(ci):
        r0, r1 = rng(ci)
        qr = rest_ref[r0:r1, base:2 * base]
        kr = rest_ref[r0:r1, 2 * base:3 * base]
        vb = rest_ref[r0:r1, 3 * base:4 * base]
        qk = nt(qr.astype(BF16), _head_block_rows(kr.astype(BF16))) * ret_dec
        qd = qr * ret_q_scale
        kd = kr * ret_k_scale
        yield
        for a in range(HEADS // 2):
            h0, h1 = 2 * a, 2 * a + 1
            z = jnp.zeros((c, DK), F32)
            rhs = jnp.concatenate([jnp.concatenate([tile(vb, h0), z], axis=1),
                                   jnp.concatenate([z, tile(vb, h1)], axis=1)], axis=0)
            intra = _mm(qk[:, a * 2 * c:(a + 1) * 2 * c], rhs)
            pre_b[ci, h0]['intra'] = intra[:, 0:DK]
            pre_b[ci, h1]['intra'] = intra[:, DK:2 * DK]
        yield
        for h in range(HEADS):
            pre_b[ci, h]['qd'] = tile(qd, h)
            pre_b[ci, h]['kv'] = _mm_tn(tile(kd, h), tile(vb, h))

    def solve_stage():
        t_offs = yield from _tri_inv_wide([wide[ci]['n'] for ci in chunks])
        for ci, t_off in zip(chunks, t_offs):
            wide[ci]['t_off'] = t_off

    def pair_rhs(x0, x1):
        z = jnp.zeros_like(x0)
        return jnp.concatenate([jnp.concatenate([x0, z], axis=1), jnp.concatenate([z, x1], axis=1)], axis=0)

    def sol_stage(ci):
        w_ = wide[ci]
        for a in range(HEADS // 2):
            h0, h1 = 2 * a, 2 * a + 1
            r = _mm(w_['t_off'][:, a * 2 * c:(a + 1) * 2 * c], pair_rhs(pre[ci, h0]['rhs'], pre[ci, h1]['rhs']))
            pre[ci, h0]['sol'] = pre[ci, h0]['rhs'] + r[:, 0:2 * A_DV]
            pre[ci, h1]['sol'] = pre[ci, h1]['rhs'] + r[:, 2 * A_DV:4 * A_DV]
        yield
        for a in range(HEADS // 2):
            h0, h1 = 2 * a, 2 * a + 1
            r = _mm(w_['qk'][:, a * 2 * c:(a + 1) * 2 * c], pair_rhs(pre[ci, h0]['sol'], pre[ci, h1]['sol']))
            pre[ci, h0]['qs'] = r[:, 0:2 * A_DV]
            pre[ci, h1]['qs'] = r[:, 2 * A_DV:4 * A_DV]
        for h in range(HEADS):
            d = pre[ci, h]
            d['kts'] = _mm_tn(d['kd'], d['sol'])
        yield
        for h in range(HEADS):
            d = pre[ci, h]
            d['lhs'] = jnp.concatenate([d['kts'][:, A_DV:2 * A_DV], d['qd'] - d['qs'][:, A_DV:2 * A_DV]], axis=0)


    def ret_out_stage(ci, h):
        d = pre_b[ci, h]
        r0, r1 = rng(ci)
        o = _mm(d['qd'], d['s_in']) + d['intra']
        yield
        ms_ = jnp.mean(o * o, axis=-1, keepdims=True)
        yield
        gb = rest_ref[r0:r1, 4 * base + h * DK:4 * base + (h + 1) * DK]
        mixed_ref[r0:r1, base + h * DK:base + (h + 1) * DK] = o * lax.rsqrt(ms_ + EPS) * rgain * gb

    def delta_step(ci, h):
        d = pre[ci, h]
        r = _mm(d['lhs'], s_delta[h])
        yield
        d['o'] = r[DK:DK + c] + d['qs'][:, 0:A_DV]
        s_delta[h] = s_delta[h] * d['gl'] - r[0:DK] + d['kts'][:, 0:A_DV]

    def gdn_out_stage(ci, h):
        o = pre[ci, h]['o']
        r0, r1 = rng(ci)
        ms_ = jnp.mean(o * o, axis=-1, keepdims=True)
        yield
        ga = rest_ref[r0:r1, h * DK:(h + 1) * DK]
        mixed_ref[r0:r1, h * A_DV:(h + 1) * A_DV] = o * lax.rsqrt(ms_ + EPS) * ggain * ga

    def delta_track():
        yield from solve_stage()
        yield from _lockstep([sol_stage(ci) for ci in chunks])
        for ci in chunks:
            yield from _lockstep([delta_step(ci, h) for h in range(HEADS)])
            if ci > 0:
                yield from _lockstep([gdn_out_stage(ci - 1, h) for h in range(HEADS)])
        yield from _lockstep([gdn_out_stage(chunks[-1], h) for h in range(HEADS)])

    def ret_track():
        for ci in chunks:
            yield from ret_stage(ci)
            for h in range(HEADS):
                pre_b[ci, h]['s_in'] = s_ret[h]
                s_ret[h] = s_ret[h] * math.exp(c * LOG_GAMMA[h]) + pre_b[ci, h]['kv']
            yield
            yield from _lockstep([ret_out_stage(ci, h) for h in range(HEADS)])

    _interleave([gdn_stage(ci) for ci in chunks])
    _interleave([delta_track(), ret_track()])

    for h in range(HEADS):
        sd_ref[0, h] = s_delta[h]
        sr_ref[0, h] = s_ret[h]


def _rope_tables(pos):
    half = DK // 2
    inv = ROPE_BASE ** (-jnp.arange(half, dtype=F32) / half)
    ang = pos.astype(F32)[:, None] * inv[None, :]
    cos, sin = jnp.cos(ang), jnp.sin(ang)
    return jnp.concatenate([cos, cos], axis=-1), jnp.concatenate([-sin, sin], axis=-1)


def _chunk_tri(lb):
    i = jnp.arange(lb)[:, None]
    j = jnp.arange(lb)[None, :]
    return ((i >= j) & (i // CHUNK == j // CHUNK)).astype(BF16)


def _head_spread(groups):
    n = groups * HEADS
    src = jnp.arange(128)[:, None]
    dst = jnp.arange(n * 128)[None, :] // 128
    return (src == dst).astype(BF16)


def _head_spread_ab():
    src = jnp.arange(128)[:, None]
    full = jnp.arange(2 * HEADS * DK)[None, :] // DK
    half = jnp.arange(HEADS * CHUNK)[None, :] // CHUNK
    return jnp.concatenate([src == full, src == half], axis=1).astype(BF16)


def _gate_params(a_log, dt_bias):
    prow = jnp.zeros((8, 128), F32).at[0, 0:HEADS].set(a_log).at[1, 0:HEADS].set(dt_bias)
    pcol = jnp.zeros((8, 128), F32).at[0:HEADS, 0].set(a_log).at[0:HEADS, 1].set(dt_bias)
    return prow, pcol


def _mix_ab_prompt(qkv, rest, gates, batch, seq, a_log, dt_bias, gdn_gain, ret_gain):
    lb = AB_SUB * CHUNK
    assert seq % lb == 0
    nc = seq // lb
    prow, pcol = _gate_params(a_log, dt_bias)
    tri = _chunk_tri(lb)
    spread = _head_spread_ab()
    tok = lambda b, c: (b * nc + c, 0)
    const = lambda b, c: (0, 0)
    mixed, sd, sr = pl.pallas_call(
        functools.partial(_mix_ab_kernel, n_sub=AB_SUB),
        grid=(batch, nc),
        in_specs=[pl.BlockSpec((lb, A_CONV_CH), tok),
                  pl.BlockSpec((lb, rest.shape[1]), tok),
                  pl.BlockSpec((lb, 128), tok),
                  pl.BlockSpec((8, 128), const),
                  pl.BlockSpec((8, 128), const),
                  pl.BlockSpec(tri.shape, const),
                  pl.BlockSpec(spread.shape, const),
                  pl.BlockSpec((1, A_DV), const),
                  pl.BlockSpec((1, DK), const)],
        out_specs=[pl.BlockSpec((lb, 2 * HEADS * DK), tok),
                   pl.BlockSpec((1, HEADS, DK, A_DV), lambda b, c: (b, 0, 0, 0)),
                   pl.BlockSpec((1, HEADS, DK, DK), lambda b, c: (b, 0, 0, 0))],
        out_shape=[jax.ShapeDtypeStruct((batch * seq, 2 * HEADS * DK), F32),
                   jax.ShapeDtypeStruct((batch, HEADS, DK, A_DV), F32),
                   jax.ShapeDtypeStruct((batch, HEADS, DK, DK), F32)],
        compiler_params=_cparams("parallel", "arbitrary"),
        name="mix_ab_prompt",
    )(qkv, rest, gates, prow, pcol, tri, spread, gdn_gain.reshape(1, A_DV), ret_gain.reshape(1, DK))
    return mixed, sd, sr


def _mix_c_kernel(qk_ref, v_ref, opre_ref, gates_ref, brow_ref, bcol_ref, tri_ref, spread_ref, gain_ref,
                  h_ref, cm_ref, nv_ref, m_ref, *, n_sub):
    c = CHUNK
    lb = n_sub * c
    step = pl.program_id(1)

    @pl.when(step == 0)
    def _():
        cm_ref[...] = jnp.zeros_like(cm_ref)
        nv_ref[...] = jnp.zeros_like(nv_ref)
        m_ref[...] = jnp.zeros_like(m_ref)

    items = [(ci, h) for ci in range(n_sub) for h in range(HEADS)]
    rng = lambda ci: (ci * c, (ci + 1) * c)
    pre = {it: {} for it in items}

    for ci, h in items:
        r0, r1 = rng(ci)
        q = qk_ref[r0:r1, h * DK:(h + 1) * DK]
        k = qk_ref[r0:r1, (HEADS + h) * DK:(HEADS + h + 1) * DK] * (DK ** -0.5)
        pre[ci, h].update(q=q, k=k, qk=_mm_nt(q, k))

    g = gates_ref[...] + brow_ref[0:1, :]
    g_t = gates_ref[...].T[0:2 * HEADS, :] + bcol_ref[0:2 * HEADS, 0:1]
    i_rows = g_t[0:HEADS, :]
    logf_rows = _log_sigmoid(g_t[HEADS:2 * HEADS, :])
    tri = tri_ref[...]
    lf_hi, lf_lo = _split(_log_sigmoid(g))
    b_cols = (jnp.dot(tri, lf_hi, preferred_element_type=F32)
              + jnp.dot(tri, lf_lo, preferred_element_type=F32))
    lr_hi, lr_lo = _split(jnp.concatenate([logf_rows, jnp.zeros_like(logf_rows)], axis=0))
    nt = lambda x, y: lax.dot_general(x, y, (((1,), (1,)), ((), ())), preferred_element_type=F32)
    b_rows = nt(lr_hi, tri) + nt(lr_lo, tri)
    lane = lax.broadcasted_iota(jnp.int32, (lb, 128), 1)
    x_hi, x_lo = _split(jnp.where(lane < HEADS, g, b_cols))
    spread = (jnp.dot(x_hi, spread_ref[...], preferred_element_type=F32)
              + jnp.dot(x_lo, spread_ref[...], preferred_element_type=F32))
    i_wide = spread[:, 0:HEADS * 128]
    b_wide = spread[:, HEADS * 128:2 * HEADS * 128]

    ii, jj = _chunk_masks(c)
    causal = ii >= jj
    gain = gain_ref[...]
    wide2 = lambda x: jnp.concatenate([x, x], axis=-1)

    def gates_stage(ci, h):
        d = pre[ci, h]
        r0, r1 = rng(ci)
        b_col = b_wide[r0:r1, h * 128:(h + 1) * 128]
        i_col = i_wide[r0:r1, h * 128:(h + 1) * 128]
        b_last = b_col[c - 1:c, :]
        d_log = jnp.where(causal, b_col[:, 0:c] - b_rows[h:h + 1, r0:r1] + i_rows[h:h + 1, r0:r1], NEG)
        d.update(b_col=b_col, b_last=b_last, d_log=d_log, k_log=b_last - b_col + i_col)
        yield
        d['d_max'] = jnp.max(d_log, axis=1, keepdims=True)

    _interleave(gates_stage(ci, h) for ci, h in items)

    ms = [m_ref[0, h:h + 1, :] for h in range(HEADS)]
    for ci, h in items:
        d = pre[ci, h]
        inter = d['b_col'] + ms[h]
        m_row = jnp.maximum(inter, d['d_max'])
        m_new = m_row[c - 1:c, :]
        d.update(inter=inter, m_row=m_row, m_old=ms[h], m_new=m_new)
        ms[h] = m_new

    def weights_stage(ci, h):
        d = pre[ci, h]
        r0, r1 = rng(ci)
        d['w_inter'] = jnp.exp(d['inter'] - d['m_row'])
        d['f_state'] = jnp.exp(d['b_last'] + d['m_old'] - d['m_new'])
        yield
        d['w_intra'] = jnp.exp(d['d_log'] - d['m_row'][:, 0:c]) * d['qk']
        d['kw'] = d['k'] * jnp.exp(d['k_log'] - d['m_new'])
        yield
        v = v_ref[r0:r1, h * C_DV:(h + 1) * C_DV]
        d['intra'] = _mm(d['w_intra'], v)
        d['kv'] = _mm_tn(d['kw'], v)
        yield
        d['sum_intra'] = jnp.sum(d['w_intra'], axis=1, keepdims=True)
        d['sum_kw'] = jnp.sum(d['kw'], axis=0, keepdims=True)
        d['inv_floor'] = jnp.exp(-d['m_row'])

    _interleave(weights_stage(ci, h) for ci, h in items)

    cms = [cm_ref[0, h] for h in range(HEADS)]
    nvs = [nv_ref[0, h:h + 1, :] for h in range(HEADS)]
    for ci, h in items:
        d = pre[ci, h]
        d['cm_in'], d['nv_in'] = cms[h], nvs[h]
        cms[h] = cms[h] * wide2(d['f_state']) + d['kv']
        nvs[h] = nvs[h] * d['f_state'] + d['sum_kw']

    def output_stage(ci, h):
        d = pre[ci, h]
        r0, r1 = rng(ci)
        qc = _mm(d['q'], d['cm_in'])
        qn = jnp.sum(d['q'] * d['nv_in'], axis=1, keepdims=True)
        yield
        num = wide2(d['w_inter']) * qc + d['intra']
        den = d['w_inter'] * qn + d['sum_intra']
        hh = num / wide2(jnp.maximum(jnp.abs(den), d['inv_floor']))
        yield
        ms_ = jnp.mean(hh * hh, axis=-1, keepdims=True)
        yield
        op = opre_ref[r0:r1, h * C_DV:(h + 1) * C_DV]
        h_ref[r0:r1, h * C_DV:(h + 1) * C_DV] = hh * lax.rsqrt(ms_ + EPS) * gain * jax.nn.sigmoid(op)

    _interleave(output_stage(ci, h) for ci, h in items)

    for h in range(HEADS):
        cm_ref[0, h] = cms[h]
        nv_ref[0, h:h + 1, :] = nvs[h]
        m_ref[0, h:h + 1, :] = ms[h]


def _bias_params(b_gate):
    brow = jnp.zeros((8, 128), F32).at[0, 0:2 * HEADS].set(b_gate)
    bcol = jnp.zeros((8, 128), F32).at[0:2 * HEADS, 0].set(b_gate)
    return brow, bcol


def _mix_c_prompt(qk, v, opre, gates, batch, seq, b_gate, gain):
    lb = C_SUB * CHUNK
    assert seq % lb == 0
    nc = seq // lb
    brow, bcol = _bias_params(b_gate)
    tri = _chunk_tri(lb)
    spread = _head_spread(2)
    tok = lambda b, c: (b * nc + c, 0)
    const = lambda b, c: (0, 0)
    return pl.pallas_call(
        functools.partial(_mix_c_kernel, n_sub=C_SUB),
        grid=(batch, nc),
        in_specs=[pl.BlockSpec((lb, 2 * HEADS * DK), tok),
                  pl.BlockSpec((lb, HEADS * C_DV), tok),
                  pl.BlockSpec((lb, HEADS * C_DV), tok),
                  pl.BlockSpec((lb, 128), tok),
                  pl.BlockSpec((8, 128), const),
                  pl.BlockSpec((8, 128), const),
                  pl.BlockSpec(tri.shape, const),
                  pl.BlockSpec(spread.shape, const),
                  pl.BlockSpec((1, C_DV), const)],
        out_specs=[pl.BlockSpec((lb, HEADS * C_DV), tok),
                   pl.BlockSpec((1, HEADS, DK, C_DV), lambda b, c: (b, 0, 0, 0)),
                   pl.BlockSpec((1, HEADS, DK), lambda b, c: (b, 0, 0)),
                   pl.BlockSpec((1, HEADS, 128), lambda b, c: (b, 0, 0))],
        out_shape=[jax.ShapeDtypeStruct((batch * seq, HEADS * C_DV), F32),
                   jax.ShapeDtypeStruct((batch, HEADS, DK, C_DV), F32),
                   jax.ShapeDtypeStruct((batch, HEADS, DK), F32),
                   jax.ShapeDtypeStruct((batch, HEADS, 128), F32)],
        compiler_params=_cparams("parallel", "arbitrary"),
        name="mix_c_prompt",
    )(qk, v, opre, gates, brow, bcol, tri, spread, gain.reshape(1, C_DV))


def _row_select(rows, t, new, old):
    return jnp.where(rows == t, new, old)


def _step_ab_kernel(qkv_ref, rest_ref, gates_ref, cos_ref, sin_ref, wconv_ref, prow_ref, ggain_ref, rgain_ref,
                    buf_ref, sd_ref, sr_ref, mixed_ref, nbuf_ref, nsd_ref, nsr_ref):
    tb = qkv_ref.shape[0]
    u = qkv_ref[...]
    w = wconv_ref[...]
    b0 = buf_ref[:, 0:A_CONV_CH]
    b1 = buf_ref[:, A_CONV_CH:2 * A_CONV_CH]
    b2 = buf_ref[:, 2 * A_CONV_CH:3 * A_CONV_CH]
    conv = b0 * w[0:1] + b1 * w[1:2] + b2 * w[2:3] + u * w[3:4]
    nbuf_ref[:, 0:A_CONV_CH] = b1
    nbuf_ref[:, A_CONV_CH:2 * A_CONV_CH] = b2
    nbuf_ref[:, 2 * A_CONV_CH:3 * A_CONV_CH] = u
    act = _silu(conv)
    g = gates_ref[...]
    eg_all = jnp.exp(-jnp.exp(prow_ref[0:1, :]) * _softplus(g + prow_ref[1:2, :]))
    beta_all = jax.nn.sigmoid(g)
    cosf = cos_ref[...]
    sinf = sin_ref[...]
    base = HEADS * DK
    rows = lax.broadcasted_iota(jnp.int32, (tb, DK), 0)
    items = [(t, h) for t in range(tb) for h in range(HEADS)]

    qs = [_l2(act[:, h * DK:(h + 1) * DK]) * (DK ** -0.5) for h in range(HEADS)]
    ks = [_l2(act[:, (HEADS + h) * DK:(HEADS + h + 1) * DK]) for h in range(HEADS)]
    k_s = [jnp.zeros((tb, A_DV), F32) for _ in range(HEADS)]
    for t, h in items:
        k_s[h] = _row_select(rows, t, _mm(ks[h], sd_ref[t, h]), k_s[h])
    v_new = []
    for h in range(HEADS):
        v = act[:, (2 * HEADS + h) * DK:(2 * HEADS + h + 1) * DK]
        v_new.append(beta_all[:, HEADS + h:HEADS + h + 1] * (v - eg_all[:, h:h + 1] * k_s[h]))
    for t, h in items:
        nsd_ref[t, h] = (sd_ref[t, h] * eg_all[t:t + 1, h:h + 1]
                         + _mm3_tn(jnp.where(rows == t, ks[h], 0.0), v_new[h]))
    o_a = [jnp.zeros((tb, A_DV), F32) for _ in range(HEADS)]
    for t, h in items:
        o_a[h] = _row_select(rows, t, _mm(qs[h], nsd_ref[t, h]), o_a[h])
    for h in range(HEADS):
        ga = rest_ref[:, h * DK:(h + 1) * DK]
        mixed_ref[:, h * DK:(h + 1) * DK] = _rms(o_a[h], ggain_ref[...]) * _silu(ga)

    qrs, krs = [], []
    for h in range(HEADS):
        qb = rest_ref[:, base + h * DK:base + (h + 1) * DK]
        kb = rest_ref[:, 2 * base + h * DK:2 * base + (h + 1) * DK]
        qrs.append(qb * cosf + pltpu.roll(qb, DK // 2, 1) * sinf)
        krs.append((kb * cosf + pltpu.roll(kb, DK // 2, 1) * sinf) * (DK ** -0.5))
    for t, h in items:
        vb = rest_ref[:, 3 * base + h * DK:3 * base + (h + 1) * DK]
        nsr_ref[t, h] = (sr_ref[t, h] * math.exp(LOG_GAMMA[h])
                         + _mm3_tn(jnp.where(rows == t, krs[h], 0.0), vb))
    o_b = [jnp.zeros((tb, DK), F32) for _ in range(HEADS)]
    for t, h in items:
        o_b[h] = _row_select(rows, t, _mm(qrs[h], nsr_ref[t, h]), o_b[h])
    for h in range(HEADS):
        gb = rest_ref[:, 4 * base + h * DK:4 * base + (h + 1) * DK]
        mixed_ref[:, base + h * DK:base + (h + 1) * DK] = _rms(o_b[h], rgain_ref[...]) * _silu(gb)


def _mix_ab_sample(qkv, rest, gates, conv_buf, s_delta, s_ret, w_conv, a_log, dt_bias, gdn_gain, ret_gain):
    nb = qkv.shape[0]
    tb = SAMPLE_TB
    assert nb % tb == 0
    cosf, sinf = _rope_tables(PAST_LEN + jnp.arange(1))
    prow, _ = _gate_params(a_log, dt_bias)
    tok = lambda b: (b, 0)
    tok4 = lambda b: (b, 0, 0, 0)
    const = lambda b: (0, 0)
    n_buf = (CONV_W - 1) * A_CONV_CH
    mixed, nbuf, nsd, nsr = pl.pallas_call(
        _step_ab_kernel,
        grid=(nb // tb,),
        in_specs=[pl.BlockSpec((tb, A_CONV_CH), tok),
                  pl.BlockSpec((tb, rest.shape[1]), tok),
                  pl.BlockSpec((tb, 128), tok),
                  pl.BlockSpec((1, DK), const),
                  pl.BlockSpec((1, DK), const),
                  pl.BlockSpec((CONV_W, A_CONV_CH), const),
                  pl.BlockSpec((8, 128), const),
                  pl.BlockSpec((1, A_DV), const),
                  pl.BlockSpec((1, DK), const),
                  pl.BlockSpec((tb, n_buf), tok),
                  pl.BlockSpec((tb, HEADS, DK, A_DV), tok4),
                  pl.BlockSpec((tb, HEADS, DK, DK), tok4)],
        out_specs=[pl.BlockSpec((tb, 2 * HEADS * DK), tok),
                   pl.BlockSpec((tb, n_buf), tok),
                   pl.BlockSpec((tb, HEADS, DK, A_DV), tok4),
                   pl.BlockSpec((tb, HEADS, DK, DK), tok4)],
        out_shape=[jax.ShapeDtypeStruct((nb, 2 * HEADS * DK), F32),
                   jax.ShapeDtypeStruct((nb, n_buf), F32),
                   jax.ShapeDtypeStruct((nb, HEADS, DK, A_DV), F32),
                   jax.ShapeDtypeStruct((nb, HEADS, DK, DK), F32)],
        compiler_params=_cparams("parallel"),
        name="mix_ab_sample",
    )(qkv, rest, gates, cosf, sinf, w_conv, prow, gdn_gain.reshape(1, A_DV), ret_gain.reshape(1, DK),
      conv_buf.reshape(nb, n_buf), s_delta, s_ret)
    return mixed, nbuf.reshape(nb, CONV_W - 1, A_CONV_CH), nsd, nsr


def _step_c_kernel(qk_ref, v_ref, opre_ref, gates_ref, brow_ref, gain_ref, cm_ref, nv_ref, m_ref,
                   h_ref, ncm_ref, nnv_ref, nm_ref):
    tb = qk_ref.shape[0]
    g = gates_ref[...] + brow_ref[0:1, :]
    logf = _log_sigmoid(g)
    rows = lax.broadcasted_iota(jnp.int32, (tb, DK), 0)
    items = [(t, h) for t in range(tb) for h in range(HEADS)]
    qs, kws, fs, ms_new, nvs_new = [], [], [], [], []
    for h in range(HEADS):
        i_pre = g[:, h:h + 1]
        q = qk_ref[:, h * DK:(h + 1) * DK]
        k = qk_ref[:, (HEADS + h) * DK:(HEADS + h + 1) * DK] * (DK ** -0.5)
        inter = logf[:, HEADS + h:HEADS + h + 1] + m_ref[:, h:h + 1]
        m_new = jnp.maximum(inter, i_pre)
        f_state = jnp.exp(inter - m_new)
        kw = k * jnp.exp(i_pre - m_new)
        nv_new = nv_ref[:, h * DK:(h + 1) * DK] * f_state + kw
        nnv_ref[:, h * DK:(h + 1) * DK] = nv_new
        nm_ref[:, h:h + 1] = m_new
        qs.append(q)
        kws.append(kw)
        fs.append(f_state)
        ms_new.append(m_new)
        nvs_new.append(nv_new)
    for t, h in items:
        v = v_ref[:, h * C_DV:(h + 1) * C_DV]
        ncm_ref[t, h] = (cm_ref[t, h] * fs[h][t:t + 1, :]
                         + _mm3_tn(jnp.where(rows == t, kws[h], 0.0), v))
    rows_v = lax.broadcasted_iota(jnp.int32, (tb, C_DV), 0)
    nums = [jnp.zeros((tb, C_DV), F32) for _ in range(HEADS)]
    for t, h in items:
        nums[h] = _row_select(rows_v, t, _mm(qs[h], ncm_ref[t, h]), nums[h])
    for h in range(HEADS):
        den = jnp.sum(qs[h] * nvs_new[h], axis=1, keepdims=True)
        hh = nums[h] / jnp.maximum(jnp.abs(den), jnp.exp(-ms_new[h]))
        op = opre_ref[:, h * C_DV:(h + 1) * C_DV]
        h_ref[:, h * C_DV:(h + 1) * C_DV] = _rms(hh, gain_ref[...]) * jax.nn.sigmoid(op)


def _mix_c_sample(qk, v, opre, gates, s_c, s_n, s_m, b_gate, gain):
    nb = qk.shape[0]
    tb = SAMPLE_TB
    assert nb % tb == 0
    brow, _ = _bias_params(b_gate)
    tok = lambda b: (b, 0)
    tok4 = lambda b: (b, 0, 0, 0)
    const = lambda b: (0, 0)
    hh, ncm, nnv, nm = pl.pallas_call(
        _step_c_kernel,
        grid=(nb // tb,),
        in_specs=[pl.BlockSpec((tb, 2 * HEADS * DK), tok),
                  pl.BlockSpec((tb, HEADS * C_DV), tok),
                  pl.BlockSpec((tb, HEADS * C_DV), tok),
                  pl.BlockSpec((tb, 128), tok),
                  pl.BlockSpec((8, 128), const),
                  pl.BlockSpec((1, C_DV), const),
                  pl.BlockSpec((tb, HEADS, DK, C_DV), tok4),
                  pl.BlockSpec((tb, HEADS * DK), tok),
                  pl.BlockSpec((tb, HEADS), tok)],
        out_specs=[pl.BlockSpec((tb, HEADS * C_DV), tok),
                   pl.BlockSpec((tb, HEADS, DK, C_DV), tok4),
                   pl.BlockSpec((tb, HEADS * DK), tok),
                   pl.BlockSpec((tb, HEADS), tok)],
        out_shape=[jax.ShapeDtypeStruct((nb, HEADS * C_DV), F32),
                   jax.ShapeDtypeStruct((nb, HEADS, DK, C_DV), F32),
                   jax.ShapeDtypeStruct((nb, HEADS * DK), F32),
                   jax.ShapeDtypeStruct((nb, HEADS), F32)],
        compiler_params=_cparams("parallel"),
        name="mix_c_sample",
    )(qk, v, opre, gates, brow, gain.reshape(1, C_DV), s_c, s_n.reshape(nb, HEADS * DK), s_m)
    return hh, ncm, nnv.reshape(nb, HEADS, DK), nm


A_SPLITS = ((A_CONV_CH,), (5 * HEADS * DK,), (128,))
C_SPLITS = ((2 * HEADS * DK, HEADS * C_DV, HEADS * C_DV), (128,))


def _gate_columns(w_gates):
    return jnp.pad(w_gates, ((0, 0), (0, 128 - w_gates.shape[1]))).astype(BF16)


def _prep_w_in_a(w):
    n_gate = 2 * HEADS
    return (w[:, :A_CONV_CH].astype(BF16), w[:, A_CONV_CH + n_gate:].astype(BF16),
            _gate_columns(w[:, A_CONV_CH:A_CONV_CH + n_gate]))


def _prep_w_in_c(w):
    n_main = 2 * HEADS * DK + 2 * HEADS * C_DV
    return (w[:, :n_main].astype(BF16), _gate_columns(w[:, n_main:]))


def kernel(x_prompt, x_sample, state_conv_a, state_delta_a, state_ret_b, state_mlstm_C, state_mlstm_n, state_mlstm_m,
           norm_mix_a, w_in_a, w_conv_a, a_log, dt_bias, gdn_gain, ret_gain, w_out_a,
           norm_mix_c, w_in_c, b_gate_c, mlstm_gain, w_out_c, norm_ffn, w_up, w_down, final_gain):
    batch, seq, _ = x_prompt.shape
    n_s = x_sample.shape[0] * x_sample.shape[1]
    tm = PROMPT_TM
    w_a = _prep_w_in_a(w_in_a[0])
    w_c = _prep_w_in_c(w_in_c[0])
    w_ups = w_up.astype(BF16)
    w_downs = w_down.astype(BF16)
    h = x_prompt.reshape(batch * seq, D_MODEL)
    h_s = x_sample.reshape(n_s, D_MODEL)

    (qkv, rest, gates, conv), (qkv_s, rest_s, gates_s) = _proj_a_prompt(
        h, h_s, norm_mix_a[0], w_a, w_conv_a[0], batch, seq, tm)
    mixed, sd, sr = _mix_ab_prompt(qkv, rest, gates, batch, seq, a_log[0], dt_bias[0], gdn_gain[0], ret_gain[0])
    mixed_s, conv_s, sd_s, sr_s = _mix_ab_sample(qkv_s, rest_s, gates_s, state_conv_a[0], state_delta_a[0],
                                                 state_ret_b[0], w_conv_a[0], a_log[0], dt_bias[0], gdn_gain[0],
                                                 ret_gain[0])
    h, h_s = _out_mlp(h, mixed, h_s, mixed_s, w_out_a[0].astype(BF16), norm_ffn[0], w_ups, w_downs, 0, None, tm)

    (qk, v, opre, gates_c), (qk_s, v_s, opre_s, gates_cs) = _norm_proj(h, h_s, norm_mix_c[0], w_c, C_SPLITS, tm)
    hm, cm, nv, m = _mix_c_prompt(qk, v, opre, gates_c, batch, seq, b_gate_c[0], mlstm_gain[0])
    hm_s, cm_s, nv_s, m_s = _mix_c_sample(qk_s, v_s, opre_s, gates_cs, state_mlstm_C[0], state_mlstm_n[0],
                                          state_mlstm_m[0], b_gate_c[0], mlstm_gain[0])
    y, y_s = _out_mlp(h, hm, h_s, hm_s, w_out_c[0].astype(BF16), norm_ffn[1], w_ups, w_downs, 1, final_gain, tm)

    return (y.reshape(x_prompt.shape), y_s.reshape(x_sample.shape),
            conv[None], sd[None], sr[None], cm[None], nv[None], m[:, :, 0][None],
            conv_s[None], sd_s[None], sr_s[None], cm_s[None], nv_s[None], m_s[None])
```

```python
import functools
import math

import jax
import jax.numpy as jnp
from jax import lax
from jax.experimental import pallas as pl
from jax.experimental.pallas import tpu as pltpu

F32 = jnp.float32
BF16 = jnp.bfloat16

D_MODEL = 1024
D_FF = 4 * D_MODEL
CHUNK = 64
EPS = 1e-6
NEG = -1e30
HEADS = 4
DK = 128
A_DV = 128
C_DV = 256
CONV_W = 4
A_CONV_CH = 3 * HEADS * DK
ROPE_BASE = 10000.0
PAST_LEN = 16384
LOG_GAMMA = tuple(math.log1p(-(2.0 ** (-5.0 - h))) for h in range(HEADS))

VMEM_LIMIT_BYTES = 56 * 1024 * 1024
PROMPT_TM = 512
AB_SUB = 8
C_SUB = 8
SAMPLE_TB = 8


def _cparams(*sem):
    return pltpu.CompilerParams(dimension_semantics=sem, vmem_limit_bytes=VMEM_LIMIT_BYTES)


def _mm(a, b):
    return jnp.dot(a.astype(BF16), b.astype(BF16), preferred_element_type=F32)


def _mm_nt(a, b):
    return lax.dot_general(a.astype(BF16), b.astype(BF16), (((1,), (1,)), ((), ())), preferred_element_type=F32)


def _mm_tn(a, b):
    return lax.dot_general(a.astype(BF16), b.astype(BF16), (((0,), (0,)), ((), ())), preferred_element_type=F32)


def _split(a):
    hi = a.astype(BF16)
    lo = (a - hi.astype(F32)).astype(BF16)
    return hi, lo


def _softplus(x):
    return jnp.maximum(x, 0.0) + jnp.log1p(jnp.exp(-jnp.abs(x)))


def _log_sigmoid(x):
    return -_softplus(-x)


def _silu(x):
    return x * jax.nn.sigmoid(x)


def _rms(x, gain):
    return x * lax.rsqrt(jnp.mean(x * x, axis=-1, keepdims=True) + EPS) * gain


def _l2(t):
    return t * lax.rsqrt(jnp.sum(t * t, axis=-1, keepdims=True) + EPS)


def _interleave(gens):
    gens = list(gens)
    while gens:
        alive = []
        for gen in gens:
            try:
                next(gen)
                alive.append(gen)
            except StopIteration:
                pass
        gens = alive


def _lockstep(gens):
    gens = list(gens)
    while gens:
        alive = []
        for gen in gens:
            try:
                next(gen)
                alive.append(gen)
            except StopIteration:
                pass
        gens = alive
        if gens:
            yield


def _head_block_rows(x):
    c, n = x.shape
    t = n // HEADS
    z = jnp.zeros((c, t), x.dtype)
    return jnp.concatenate(
        [jnp.concatenate([x[:, h * t:(h + 1) * t] if g == h else z for g in range(HEADS)], axis=1)
         for h in range(HEADS)], axis=0)


def _head_block_diag(y):
    c, n = y.shape
    per_tile = 128 // c
    assert n == HEADS * c and 128 % c == 0 and HEADS % per_tile == 0
    lane = lax.broadcasted_iota(jnp.int32, (c, 128), 1)
    z = jnp.zeros((c, 128), y.dtype)
    blocks = []
    for h in range(HEADS):
        t = h // per_tile
        lo = (h % per_tile) * c
        kept = jnp.where((lane >= lo) & (lane < lo + c), y[:, t * 128:(t + 1) * 128], z)
        blocks.append(jnp.concatenate([kept if g == t else z for g in range(n // 128)], axis=1))
    return jnp.concatenate(blocks, axis=0)


def _wide_mm(x, y):
    return jnp.dot(x.astype(BF16), _head_block_diag(y.astype(BF16)), preferred_element_type=F32)


def _tri_inv_wide(ns):
    c, wd = ns[0].shape
    ii = lax.broadcasted_iota(jnp.int32, (c, wd), 0)
    jl = lax.broadcasted_iota(jnp.int32, (c, wd), 1) & (c - 1)
    eye = (ii == jl).astype(F32)
    ts = [eye - jnp.where((ii >> 1) == (jl >> 1), n, 0.0) for n in ns]
    for lvl in range(1, int(math.log2(c))):
        off = ((ii >> (lvl + 1)) == (jl >> (lvl + 1))) & ((ii >> lvl) != (jl >> lvl))
        ys = [_wide_mm(jnp.where(off, n, 0.0), t) for n, t in zip(ns, ts)]
        yield
        ts = [t - _wide_mm(t, y) for t, y in zip(ts, ys)]
        yield
    return [t - eye for t in ts]


def _project_rows(x_ref, g_ref, w_refs, out_refs, splits):
    xn = _rms(x_ref[...], g_ref[...]).astype(BF16)
    o = 0
    for w_ref, widths in zip(w_refs, splits):
        off = 0
        for n in widths:
            out_refs[o][...] = jnp.dot(xn, w_ref[:, off:off + n], preferred_element_type=F32)
            off += n
            o += 1


def _norm_proj_kernel(x_ref, xs_ref, g_ref, *refs, splits):
    n_out = sum(len(s) for s in splits)
    w_refs = refs[:len(splits)]
    out_refs = refs[len(splits):len(splits) + n_out]
    outs_refs = refs[len(splits) + n_out:]
    _project_rows(x_ref, g_ref, w_refs, out_refs, splits)

    @pl.when(pl.program_id(0) == 0)
    def _():
        _project_rows(xs_ref, g_ref, w_refs, outs_refs, splits)


def _norm_proj(x, x_s, gain, weights, splits, tm):
    t, ts = x.shape[0], x_s.shape[0]
    assert t % tm == 0 and all(sum(s) == w.shape[1] for w, s in zip(weights, splits))
    widths = [n for s in splits for n in s]
    row = lambda i: (i, 0)
    const = lambda i: (0, 0)
    outs = pl.pallas_call(
        functools.partial(_norm_proj_kernel, splits=splits),
        grid=(t // tm,),
        in_specs=[pl.BlockSpec((tm, D_MODEL), row), pl.BlockSpec((ts, D_MODEL), const),
                  pl.BlockSpec((1, D_MODEL), const)]
                 + [pl.BlockSpec(w.shape, const, pipeline_mode=pl.Buffered(1)) for w in weights],
        out_specs=[pl.BlockSpec((tm, n), row) for n in widths] + [pl.BlockSpec((ts, n), const) for n in widths],
        out_shape=[jax.ShapeDtypeStruct((t, n), F32) for n in widths]
                  + [jax.ShapeDtypeStruct((ts, n), F32) for n in widths],
        compiler_params=_cparams("arbitrary"),
        name="norm_proj",
    )(x, x_s, gain.reshape(1, D_MODEL), *weights)
    return outs[:len(widths)], outs[len(widths):]


def _proj_a_prompt_kernel(x_ref, xs_ref, g_ref, wqkv_ref, wrest_ref, wgate_ref, wconv_ref, cos_ref, sin_ref,
                          qkv_ref, rest_ref, gates_ref, conv_ref, qkvs_ref, rests_ref, gatess_ref,
                          xp_ref, raw_ref, *, tiles_per_seq):
    tm = x_ref.shape[0]
    hd = HEADS * DK

    @pl.when(pl.program_id(0) == 0)
    def _():
        _project_rows(xs_ref, g_ref, (wqkv_ref, wrest_ref, wgate_ref), (qkvs_ref, rests_ref, gatess_ref), A_SPLITS)

    xn = _rms(x_ref[...], g_ref[...]).astype(BF16)

    @pl.when(pl.program_id(0) % tiles_per_seq == 0)
    def _():
        xp_ref[0:8, :] = jnp.zeros((8, A_CONV_CH), F32)

    w = wconv_ref[...]
    base = A_CONV_CH

    def conv_part(part, slot, r0, r1):
        c0, c1 = part * hd, (part + 1) * hd
        raw = raw_ref[slot, r0:r1, :]
        xp_ref[8 + r0:8 + r1, c0:c1] = raw
        conv = (xp_ref[5 + r0:5 + r1, c0:c1] * w[0:1, c0:c1] + xp_ref[6 + r0:6 + r1, c0:c1] * w[1:2, c0:c1]
                + xp_ref[7 + r0:7 + r1, c0:c1] * w[2:3, c0:c1] + raw * w[3:4, c0:c1])
        if r1 == tm:
            rows = r1 - r0
            xp_ref[0:8, c0:c1] = raw[rows - 8:rows, :]
            conv_ref[0, :, c0:c1] = raw[rows - 3:rows, :]
        act = _silu(conv)
        if part == 2:
            qkv_ref[r0:r1, c0:c1] = act
        else:
            scale = DK ** -0.5 if part == 0 else 1.0
            for h in range(HEADS):
                qkv_ref[r0:r1, c0 + h * DK:c0 + (h + 1) * DK] = _l2(act[:, h * DK:(h + 1) * DK]) * scale

    def rest_part(part, slot, r0, r1):
        c0, c1 = part * hd, (part + 1) * hd
        raw = raw_ref[slot, r0:r1, :]
        if part in (0, 4):
            rest_ref[r0:r1, c0:c1] = _silu(raw)
        elif part == 3:
            rest_ref[r0:r1, c0:c1] = raw
        else:
            scale = 1.0 if part == 1 else DK ** -0.5
            for h in range(HEADS):
                t = raw[:, h * DK:(h + 1) * DK]
                rest_ref[r0:r1, c0 + h * DK:c0 + (h + 1) * DK] = (
                    (t * cos_ref[r0:r1, :] + pltpu.roll(t, DK // 2, 1) * sin_ref[r0:r1, :]) * scale)

    def gates_part(_, slot, r0, r1):
        gates_ref[r0:r1, :] = raw_ref[slot, r0:r1, 0:128]

    groups = [(wqkv_ref, 0, hd, functools.partial(conv_part, 0)),
              (wrest_ref, 3 * hd, hd, functools.partial(rest_part, 3)),
              (wqkv_ref, hd, hd, functools.partial(conv_part, 1)),
              (wrest_ref, 0, hd, functools.partial(rest_part, 0)),
              (wqkv_ref, 2 * hd, hd, functools.partial(conv_part, 2)),
              (wrest_ref, hd, hd, functools.partial(rest_part, 1)),
              (wgate_ref, 0, 128, functools.partial(gates_part, 0)),
              (wrest_ref, 2 * hd, hd, functools.partial(rest_part, 2)),
              (wrest_ref, 4 * hd, hd, functools.partial(rest_part, 4))]

    def project(n):
        w_ref, col0, width, _ = groups[n]
        for j in range(0, width, 256):
            wj = min(256, width - j)
            raw_ref[n % 2, :, j:j + wj] = jnp.dot(xn, w_ref[:, col0 + j:col0 + j + wj],
                                                  preferred_element_type=F32)
            yield

    def epilogue(n):
        for r0 in range(0, tm, 128):
            groups[n][3](n % 2, r0, min(r0 + 128, tm))
            yield

    _interleave([project(0)])
    for n in range(len(groups)):
        _interleave([epilogue(n)] + ([project(n + 1)] if n + 1 < len(groups) else []))


def _proj_a_prompt(x, x_s, gain, weights, w_conv, batch, seq, tm):
    t, ts = x.shape[0], x_s.shape[0]
    assert t % tm == 0 and seq % tm == 0
    tiles_per_seq = seq // tm
    cosf, sinf = _rope_tables(jnp.arange(seq))
    row = lambda i: (i, 0)
    const = lambda i: (0, 0)
    pos = lambda i: (i % tiles_per_seq, 0)
    n_rest = 5 * HEADS * DK
    outs = pl.pallas_call(
        functools.partial(_proj_a_prompt_kernel, tiles_per_seq=tiles_per_seq),
        grid=(t // tm,),
        in_specs=[pl.BlockSpec((tm, D_MODEL), row),
                  pl.BlockSpec((ts, D_MODEL), const),
                  pl.BlockSpec((1, D_MODEL), const)]
                 + [pl.BlockSpec(w.shape, const, pipeline_mode=pl.Buffered(1)) for w in weights]
                 + [pl.BlockSpec((CONV_W, A_CONV_CH), const),
                    pl.BlockSpec((tm, DK), pos),
                    pl.BlockSpec((tm, DK), pos)],
        out_specs=[pl.BlockSpec((tm, A_CONV_CH), row),
                   pl.BlockSpec((tm, n_rest), row),
                   pl.BlockSpec((tm, 128), row),
                   pl.BlockSpec((1, CONV_W - 1, A_CONV_CH), lambda i: (i // tiles_per_seq, 0, 0)),
                   pl.BlockSpec((ts, A_CONV_CH), const),
                   pl.BlockSpec((ts, n_rest), const),
                   pl.BlockSpec((ts, 128), const)],
        out_shape=[jax.ShapeDtypeStruct((t, A_CONV_CH), F32),
                   jax.ShapeDtypeStruct((t, n_rest), F32),
                   jax.ShapeDtypeStruct((t, 128), F32),
                   jax.ShapeDtypeStruct((batch, CONV_W - 1, A_CONV_CH), F32),
                   jax.ShapeDtypeStruct((ts, A_CONV_CH), F32),
                   jax.ShapeDtypeStruct((ts, n_rest), F32),
                   jax.ShapeDtypeStruct((ts, 128), F32)],
        scratch_shapes=[pltpu.VMEM((tm + 8, A_CONV_CH), F32), pltpu.VMEM((2, tm, HEADS * DK), F32)],
        compiler_params=_cparams("arbitrary"),
        name="proj_a_prompt",
    )(x, x_s, gain.reshape(1, D_MODEL), *weights, w_conv, cosf, sinf)
    return outs[:4], outs[4:]


def _out_mlp_kernel(h_ref, m_ref, hs_ref, ms_ref, wout_ref, gffn_ref, wup_ref, wdown_ref, *rest, final):
    if final:
        gfin_ref, o_ref, os_ref = rest
    else:
        o_ref, os_ref = rest

    def block(h_r, m_r, o_r):
        h = h_r[...] + jnp.dot(m_r[...].astype(BF16), wout_ref[...], preferred_element_type=F32)
        xn = _rms(h, gffn_ref[...]).astype(BF16)
        acc = h
        step = 1024
        for j in range(D_FF // step):
            hid = jnp.dot(xn, wup_ref[:, j * step:(j + 1) * step], preferred_element_type=F32)
            hid = jnp.maximum(hid, 0.0)
            acc = acc + jnp.dot((hid * hid).astype(BF16), wdown_ref[j * step:(j + 1) * step, :],
                                preferred_element_type=F32)
        if final:
            acc = _rms(acc, gfin_ref[...])
        o_r[...] = acc

    block(h_ref, m_ref, o_ref)

    @pl.when(pl.program_id(0) == 0)
    def _():
        block(hs_ref, ms_ref, os_ref)


def _out_mlp(h, mix, h_s, mix_s, w_out, g_ffn, w_up, w_down, layer, g_final, tm):
    t, ts = h.shape[0], h_s.shape[0]
    assert t % tm == 0
    final = g_final is not None
    row = lambda i: (i, 0)
    const = lambda i: (0, 0)
    this_layer = lambda i: (layer, 0, 0)
    once = pl.Buffered(1)
    in_specs = [pl.BlockSpec((tm, D_MODEL), row), pl.BlockSpec((tm, mix.shape[1]), row),
                pl.BlockSpec((ts, D_MODEL), const), pl.BlockSpec((ts, mix_s.shape[1]), const),
                pl.BlockSpec(w_out.shape, const, pipeline_mode=once), pl.BlockSpec((1, D_MODEL), const),
                pl.BlockSpec((None,) + w_up.shape[1:], this_layer, pipeline_mode=once),
                pl.BlockSpec((None,) + w_down.shape[1:], this_layer, pipeline_mode=once)]
    args = [h, mix, h_s, mix_s, w_out, g_ffn.reshape(1, D_MODEL), w_up, w_down]
    if final:
        in_specs.append(pl.BlockSpec((1, D_MODEL), const))
        args.append(g_final.reshape(1, D_MODEL))
    return pl.pallas_call(
        functools.partial(_out_mlp_kernel, final=final),
        grid=(t // tm,),
        in_specs=in_specs,
        out_specs=[pl.BlockSpec((tm, D_MODEL), row), pl.BlockSpec((ts, D_MODEL), const)],
        out_shape=[jax.ShapeDtypeStruct((t, D_MODEL), F32), jax.ShapeDtypeStruct((ts, D_MODEL), F32)],
        compiler_params=_cparams("arbitrary"),
        name="out_mlp",
    )(*args)


def _chunk_masks(c):
    ii = lax.broadcasted_iota(jnp.int32, (c, c), 0)
    jj = lax.broadcasted_iota(jnp.int32, (c, c), 1)
    return ii, jj


def _mix_ab_kernel(qkv_ref, rest_ref, gates_ref, prow_ref, pcol_ref, tri_ref, spread_ref,
                   ggain_ref, rgain_ref, mixed_ref, sd_ref, sr_ref, *, n_sub):
    c = CHUNK
    lb = n_sub * c
    step = pl.program_id(1)

    @pl.when(step == 0)
    def _():
        sd_ref[...] = jnp.zeros_like(sd_ref)
        sr_ref[...] = jnp.zeros_like(sr_ref)

    g = gates_ref[...]
    g_t = g.T
    neg_a_row = -jnp.exp(prow_ref[0:1, :])
    dt_row = prow_ref[1:2, :]
    neg_a_col = -jnp.exp(pcol_ref[0:HEADS, 0:1])
    dt_col = pcol_ref[0:HEADS, 1:2]
    la_cols = neg_a_row * _softplus(g + dt_row)
    beta_cols = jax.nn.sigmoid(g)
    la_rows = neg_a_col * _softplus(g_t[0:HEADS, :] + dt_col)

    tri = tri_ref[...]
    la_hi, la_lo = _split(la_cols)
    g_cols = (jnp.dot(tri, la_hi, preferred_element_type=F32)
              + jnp.dot(tri, la_lo, preferred_element_type=F32))
    lr_hi, lr_lo = _split(jnp.concatenate([la_rows, jnp.zeros_like(la_rows)], axis=0))
    nt = lambda x, y: lax.dot_general(x, y, (((1,), (1,)), ((), ())), preferred_element_type=F32)
    g_rows = nt(lr_hi, tri) + nt(lr_lo, tri)
    lane = lax.broadcasted_iota(jnp.int32, (lb, 128), 1)
    x_hi, x_lo = _split(jnp.where(lane < HEADS, g_cols, beta_cols))
    spread = (jnp.dot(x_hi, spread_ref[...], preferred_element_type=F32)
              + jnp.dot(x_lo, spread_ref[...], preferred_element_type=F32))
    hd = HEADS * DK
    wd = HEADS * c
    g_wide = spread[:, 0:hd]
    beta_wide = spread[:, hd:2 * hd]
    g_half = spread[:, 2 * hd:2 * hd + wd]
    eg_wide = jnp.exp(g_wide)

    ii = lax.broadcasted_iota(jnp.int32, (c, wd), 0)
    jl = lax.broadcasted_iota(jnp.int32, (c, wd), 1) & (c - 1)
    causal = ii >= jl
    strict = ii > jl
    lane_hd = lax.broadcasted_iota(jnp.int32, (1, hd), 1)
    lane_wd = lax.broadcasted_iota(jnp.int32, (1, wd), 1)
    lg_hd = jnp.full((1, hd), LOG_GAMMA[HEADS - 1], F32)
    lg_wd = jnp.full((1, wd), LOG_GAMMA[HEADS - 1], F32)
    for h in range(HEADS - 2, -1, -1):
        lg_hd = jnp.where(lane_hd < (h + 1) * DK, LOG_GAMMA[h], lg_hd)
        lg_wd = jnp.where(lane_wd < (h + 1) * c, LOG_GAMMA[h], lg_wd)
    pos_col = lax.broadcasted_iota(jnp.int32, (c, hd), 0).astype(F32)
    ret_dec = jnp.exp(jnp.where(causal, (ii - jl).astype(F32) * lg_wd, NEG))
    ret_q_scale = jnp.exp((pos_col + 1.0) * lg_hd)
    ret_k_scale = jnp.exp((float(c - 1) - pos_col) * lg_hd)

    ggain = ggain_ref[...]
    rgain = rgain_ref[...]
    base = hd
    tile = lambda x, h: x[:, h * DK:(h + 1) * DK]
    s_delta = [sd_ref[0, h] for h in range(HEADS)]
    s_ret = [sr_ref[0, h] for h in range(HEADS)]

    chunks = list(range(n_sub))
    items = [(ci, h) for ci in chunks for h in range(HEADS)]
    rng = lambda ci: (ci * c, (ci + 1) * c)
    pre = {it: {} for it in items}
    pre_b = {it: {} for it in items}
    wide = {ci: {} for ci in chunks}
    nt = lambda x, y: lax.dot_general(x, y, (((1,), (1,)), ((), ())), preferred_element_type=F32)

    def gdn_stage(ci):
        w_ = wide[ci]
        r0, r1 = rng(ci)
        q = qkv_ref[r0:r1, 0:hd]
        k = qkv_ref[r0:r1, hd:2 * hd]
        beta = beta_wide[r0:r1]
        g_col = g_wide[r0:r1]
        kb = k * beta
        g_row = jnp.concatenate([g_rows[h:h + 1, r0:r1] for h in range(HEADS)], axis=1)
        dec_causal = jnp.exp(jnp.where(causal, g_half[r0:r1] - g_row, NEG))
        prod = nt(jnp.concatenate([q, kb], axis=0).astype(BF16), _head_block_rows(k.astype(BF16)))
        yield
        w_['n'] = prod[c:2 * c] * jnp.where(strict, dec_causal, 0.0)
        w_['qk'] = prod[0:c] * dec_causal
        yield
        v = qkv_ref[r0:r1, 2 * hd:3 * hd]
        eg = eg_wide[r0:r1]
        g_last = g_col[c - 1:c, :]
        rhs_u, rhs_w, qd = v * beta, kb * eg, q * eg
        kd = k * jnp.exp(g_last - g_col)
        gl = jnp.exp(g_last)
        for h in range(HEADS):
            pre[ci, h].update(rhs=jnp.concatenate([tile(rhs_u, h), tile(rhs_w, h)], axis=1), qd=tile(qd, h),
                              kd=tile(kd, h), gl=tile(gl, h))

    def ret_stage(ci):
        r0, r1 = rng(ci)
        qr = rest_ref[r0:r1, base:2 * base]
        kr = rest_ref[r0:r1, 2 * base:3 * base]
        vb = rest_ref[r0:r1, 3 * base:4 * base]
        qk = nt(qr.astype(BF16), _head_block_rows(kr.astype(BF16))) * ret_dec
        qd = qr * ret_q_scale
        kd = kr * ret_k_scale
        yield
        for a in range(HEADS // 2):
            h0, h1 = 2 * a, 2 * a + 1
            z = jnp.zeros((c, DK), F32)
            rhs = jnp.concatenate([jnp.concatenate([tile(vb, h0), z], axis=1),
                                   jnp.concatenate([z, tile(vb, h1)], axis=1)], axis=0)
            intra = _mm(qk[:, a * 2 * c:(a + 1) * 2 * c], rhs)
            pre_b[ci, h0]['intra'] = intra[:, 0:DK]
            pre_b[ci, h1]['intra'] = intra[:, DK:2 * DK]
        yield
        for h in range(HEADS):
            pre_b[ci, h]['qd'] = tile(qd, h)
            pre_b[ci, h]['kv'] = _mm_tn(tile(kd, h), tile(vb, h))

    def solve_stage():
        t_offs = yield from _tri_inv_wide([wide[ci]['n'] for ci in chunks])
        for ci, t_off in zip(chunks, t_offs):
            wide[ci]['t_off'] = t_off

    def pair_rhs(x0, x1):
        z = jnp.zeros_like(x0)
        return jnp.concatenate([jnp.concatenate([x0, z], axis=1), jnp.concatenate([z, x1], axis=1)], axis=0)

    def sol_stage(ci):
        w_ = wide[ci]
        for a in range(HEADS // 2):
            h0, h1 = 2 * a, 2 * a + 1
            r = _mm(w_['t_off'][:, a * 2 * c:(a + 1) * 2 * c], pair_rhs(pre[ci, h0]['rhs'], pre[ci, h1]['rhs']))
            pre[ci, h0]['sol'] = pre[ci, h0]['rhs'] + r[:, 0:2 * A_DV]
            pre[ci, h1]['sol'] = pre[ci, h1]['rhs'] + r[:, 2 * A_DV:4 * A_DV]
        yield
        for a in range(HEADS // 2):
            h0, h1 = 2 * a, 2 * a + 1
            r = _mm(w_['qk'][:, a * 2 * c:(a + 1) * 2 * c], pair_rhs(pre[ci, h0]['sol'], pre[ci, h1]['sol']))
            pre[ci, h0]['qs'] = r[:, 0:2 * A_DV]
            pre[ci, h1]['qs'] = r[:, 2 * A_DV:4 * A_DV]
        for h in range(HEADS):
            d = pre[ci, h]
            d['kts'] = _mm_tn(d['kd'], d['sol'])
        yield
        for h in range(HEADS):
            d = pre[ci, h]
            d['lhs'] = jnp.concatenate([d['kts'][:, A_DV:2 * A_DV], d['qd'] - d['qs'][:, A_DV:2 * A_DV]], axis=0)


    def ret_out_stage(ci, h):
        d = pre_b[ci, h]
        r0, r1 = rng(ci)
        o = _mm(d['qd'], d['s_in']) + d['intra']
        yield
        ms_ = jnp.mean(o * o, axis=-1, keepdims=True)
        yield
        gb = rest_ref[r0:r1, 4 * base + h * DK:4 * base + (h + 1) * DK]
        mixed_ref[r0:r1, base + h * DK:base + (h + 1) * DK] = o * lax.rsqrt(ms_ + EPS) * rgain * gb

    def delta_step(ci, h):
        d = pre[ci, h]
        r = _mm(d['lhs'], s_delta[h])
        yield
        d['o'] = r[DK:DK + c] + d['qs'][:, 0:A_DV]
        s_delta[h] = s_delta[h] * d['gl'] - r[0:DK] + d['kts'][:, 0:A_DV]

    def gdn_out_stage(ci, h):
        o = pre[ci, h]['o']
        r0, r1 = rng(ci)
        ms_ = jnp.mean(o * o, axis=-1, keepdims=True)
        yield
        ga = rest_ref[r0:r1, h * DK:(h + 1) * DK]
        mixed_ref[r0:r1, h * A_DV:(h + 1) * A_DV] = o * lax.rsqrt(ms_ + EPS) * ggain * ga

    def delta_track():
        yield from solve_stage()
        yield from _lockstep([sol_stage(ci) for ci in chunks])
        for ci in chunks:
            yield from _lockstep([delta_step(ci, h) for h in range(HEADS)])
            if ci > 0:
                yield from _lockstep([gdn_out_stage(ci - 1, h) for h in range(HEADS)])
        yield from _lockstep([gdn_out_stage(chunks[-1], h) for h in range(HEADS)])

    def ret_track():
        for ci in chunks:
            yield from ret_stage(ci)
            for h in range(HEADS):
                pre_b[ci, h]['s_in'] = s_ret[h]
                s_ret[h] = s_ret[h] * math.exp(c * LOG_GAMMA[h]) + pre_b[ci, h]['kv']
            yield
            yield from _lockstep([ret_out_stage(ci, h) for h in range(HEADS)])

    _interleave([gdn_stage(ci) for ci in chunks])
    _interleave([delta_track(), ret_track()])

    for h in range(HEADS):
        sd_ref[0, h] = s_delta[h]
        sr_ref[0, h] = s_ret[h]


def _rope_tables(pos):
    half = DK // 2
    inv = ROPE_BASE ** (-jnp.arange(half, dtype=F32) / half)
    ang = pos.astype(F32)[:, None] * inv[None, :]
    cos, sin = jnp.cos(ang), jnp.sin(ang)
    return jnp.concatenate([cos, cos], axis=-1), jnp.concatenate([-sin, sin], axis=-1)


def _chunk_tri(lb):
    i = jnp.arange(lb)[:, None]
    j = jnp.arange(lb)[None, :]
    return ((i >= j) & (i // CHUNK == j // CHUNK)).astype(BF16)


def _head_spread(groups):
    n = groups * HEADS
    src = jnp.arange(128)[:, None]
    dst = jnp.arange(n * 128)[None, :] // 128
    return (src == dst).astype(BF16)


def _head_spread_ab():
    src = jnp.arange(128)[:, None]
    full = jnp.arange(2 * HEADS * DK)[None, :] // DK
    half = jnp.arange(HEADS * CHUNK)[None, :] // CHUNK
    return jnp.concatenate([src == full, src == half], axis=1).astype(BF16)


def _gate_params(a_log, dt_bias):
    prow = jnp.zeros((8, 128), F32).at[0, 0:HEADS].set(a_log).at[1, 0:HEADS].set(dt_bias)
    pcol = jnp.zeros((8, 128), F32).at[0:HEADS, 0].set(a_log).at[0:HEADS, 1].set(dt_bias)
    return prow, pcol


def _mix_ab_prompt(qkv, rest, gates, batch, seq, a_log, dt_bias, gdn_gain, ret_gain):
    lb = AB_SUB * CHUNK
    assert seq % lb == 0
    nc = seq // lb
    prow, pcol = _gate_params(a_log, dt_bias)
    tri = _chunk_tri(lb)
    spread = _head_spread_ab()
    tok = lambda b, c: (b * nc + c, 0)
    const = lambda b, c: (0, 0)
    mixed, sd, sr = pl.pallas_call(
        functools.partial(_mix_ab_kernel, n_sub=AB_SUB),
        grid=(batch, nc),
        in_specs=[pl.BlockSpec((lb, A_CONV_CH), tok),
                  pl.BlockSpec((lb, rest.shape[1]), tok),
                  pl.BlockSpec((lb, 128), tok),
                  pl.BlockSpec((8, 128), const),
                  pl.BlockSpec((8, 128), const),
                  pl.BlockSpec(tri.shape, const),
                  pl.BlockSpec(spread.shape, const),
                  pl.BlockSpec((1, A_DV), const),
                  pl.BlockSpec((1, DK), const)],
        out_specs=[pl.BlockSpec((lb, 2 * HEADS * DK), tok),
                   pl.BlockSpec((1, HEADS, DK, A_DV), lambda b, c: (b, 0, 0, 0)),
                   pl.BlockSpec((1, HEADS, DK, DK), lambda b, c: (b, 0, 0, 0))],
        out_shape=[jax.ShapeDtypeStruct((batch * seq, 2 * HEADS * DK), F32),
                   jax.ShapeDtypeStruct((batch, HEADS, DK, A_DV), F32),
                   jax.ShapeDtypeStruct((batch, HEADS, DK, DK), F32)],
        compiler_params=_cparams("parallel", "arbitrary"),
        name="mix_ab_prompt",
    )(qkv, rest, gates, prow, pcol, tri, spread, gdn_gain.reshape(1, A_DV), ret_gain.reshape(1, DK))
    return mixed, sd, sr


def _mix_c_kernel(qk_ref, v_ref, opre_ref, gates_ref, brow_ref, bcol_ref, tri_ref, spread_ref, gain_ref,
                  h_ref, cm_ref, nv_ref, m_ref, *, n_sub):
    c = CHUNK
    lb = n_sub * c
    step = pl.program_id(1)

    @pl.when(step == 0)
    def _():
        cm_ref[...] = jnp.zeros_like(cm_ref)
        nv_ref[...] = jnp.zeros_like(nv_ref)
        m_ref[...] = jnp.zeros_like(m_ref)

    items = [(ci, h) for ci in range(n_sub) for h in range(HEADS)]
    rng = lambda ci: (ci * c, (ci + 1) * c)
    pre = {it: {} for it in items}

    for ci, h in items:
        r0, r1 = rng(ci)
        q = qk_ref[r0:r1, h * DK:(h + 1) * DK]
        k = qk_ref[r0:r1, (HEADS + h) * DK:(HEADS + h + 1) * DK] * (DK ** -0.5)
        pre[ci, h].update(q=q, k=k, qk=_mm_nt(q, k))

    g = gates_ref[...] + brow_ref[0:1, :]
    g_t = gates_ref[...].T[0:2 * HEADS, :] + bcol_ref[0:2 * HEADS, 0:1]
    i_rows = g_t[0:HEADS, :]
    logf_rows = _log_sigmoid(g_t[HEADS:2 * HEADS, :])
    tri = tri_ref[...]
    lf_hi, lf_lo = _split(_log_sigmoid(g))
    b_cols = (jnp.dot(tri, lf_hi, preferred_element_type=F32)
              + jnp.dot(tri, lf_lo, preferred_element_type=F32))
    lr_hi, lr_lo = _split(jnp.concatenate([logf_rows, jnp.zeros_like(logf_rows)], axis=0))
    nt = lambda x, y: lax.dot_general(x, y, (((1,), (1,)), ((), ())), preferred_element_type=F32)
    b_rows = nt(lr_hi, tri) + nt(lr_lo, tri)
    lane = lax.broadcasted_iota(jnp.int32, (lb, 128), 1)
    x_hi, x_lo = _split(jnp.where(lane < HEADS, g, b_cols))
    spread = (jnp.dot(x_hi, spread_ref[...], preferred_element_type=F32)
              + jnp.dot(x_lo, spread_ref[...], preferred_element_type=F32))
    i_wide = spread[:, 0:HEADS * 128]
    b_wide = spread[:, HEADS * 128:2 * HEADS * 128]

    ii, jj = _chunk_masks(c)
    causal = ii >= jj
    gain = gain_ref[...]
    wide2 = lambda x: jnp.concatenate([x, x], axis=-1)

    def gates_stage(ci, h):
        d = pre[ci, h]
        r0, r1 = rng(ci)
        b_col = b_wide[r0:r1, h * 128:(h + 1) * 128]
        i_col = i_wide[r0:r1, h * 128:(h + 1) * 128]
        b_last = b_col[c - 1:c, :]
        d_log = jnp.where(causal, b_col[:, 0:c] - b_rows[h:h + 1, r0:r1] + i_rows[h:h + 1, r0:r1], NEG)
        d.update(b_col=b_col, b_last=b_last, d_log=d_log, k_log=b_last - b_col + i_col)
        yield
        d['d_max'] = jnp.max(d_log, axis=1, keepdims=True)

    _interleave(gates_stage(ci, h) for ci, h in items)

    ms = [m_ref[0, h:h + 1, :] for h in range(HEADS)]
    for ci, h in items:
        d = pre[ci, h]
        inter = d['b_col'] + ms[h]
        m_row = jnp.maximum(inter, d['d_max'])
        m_new = m_row[c - 1:c, :]
        d.update(inter=inter, m_row=m_row, m_old=ms[h], m_new=m_new)
        ms[h] = m_new

    def weights_stage(ci, h):
        d = pre[ci, h]
        r0, r1 = rng(ci)
        d['w_inter'] = jnp.exp(d['inter'] - d['m_row'])
        d['f_state'] = jnp.exp(d['b_last'] + d['m_old'] - d['m_new'])
        yield
        d['w_intra'] = jnp.exp(d['d_log'] - d['m_row'][:, 0:c]) * d['qk']
        d['kw'] = d['k'] * jnp.exp(d['k_log'] - d['m_new'])
        yield
        v = v_ref[r0:r1, h * C_DV:(h + 1) * C_DV]
        d['intra'] = _mm(d['w_intra'], v)
        d['kv'] = _mm_tn(d['kw'], v)
        yield
        d['sum_intra'] = jnp.sum(d['w_intra'], axis=1, keepdims=True)
        d['sum_kw'] = jnp.sum(d['kw'], axis=0, keepdims=True)
        d['inv_floor'] = jnp.exp(-d['m_row'])

    _interleave(weights_stage(ci, h) for ci, h in items)

    cms = [cm_ref[0, h] for h in range(HEADS)]
    nvs = [nv_ref[0, h:h + 1, :] for h in range(HEADS)]
    for ci, h in items:
        d = pre[ci, h]
        d['cm_in'], d['nv_in'] = cms[h], nvs[h]
        cms[h] = cms[h] * wide2(d['f_state']) + d['kv']
        nvs[h] = nvs[h] * d['f_state'] + d['sum_kw']

    def output_stage(ci, h):
        d = pre[ci, h]
        r0, r1 = rng(ci)
        qc = _mm(d['q'], d['cm_in'])
        qn = jnp.sum(d['q'] * d['nv_in'], axis=1, keepdims=True)
        yield
        num = wide2(d['w_inter']) * qc + d['intra']
        den = d['w_inter'] * qn + d['sum_intra']
        hh = num / wide2(jnp.maximum(jnp.abs(den), d['inv_floor']))
        yield
        ms_ = jnp.mean(hh * hh, axis=-1, keepdims=True)
        yield
        op = opre_ref[r0:r1, h * C_DV:(h + 1) * C_DV]
        h_ref[r0:r1, h * C_DV:(h + 1) * C_DV] = hh * lax.rsqrt(ms_ + EPS) * gain * jax.nn.sigmoid(op)

    _interleave(output_stage(ci, h) for ci, h in items)

    for h in range(HEADS):
        cm_ref[0, h] = cms[h]
        nv_ref[0, h:h + 1, :] = nvs[h]
        m_ref[0, h:h + 1, :] = ms[h]


def _bias_params(b_gate):
    brow = jnp.zeros((8, 128), F32).at[0, 0:2 * HEADS].set(b_gate)
    bcol = jnp.zeros((8, 128), F32).at[0:2 * HEADS, 0].set(b_gate)
    return brow, bcol


def _mix_c_prompt(qk, v, opre, gates, batch, seq, b_gate, gain):
    lb = C_SUB * CHUNK
    assert seq % lb == 0
    nc = seq // lb
    brow, bcol = _bias_params(b_gate)
    tri = _chunk_tri(lb)
    spread = _head_spread(2)
    tok = lambda b, c: (b * nc + c, 0)
    const = lambda b, c: (0, 0)
    return pl.pallas_call(
        functools.partial(_mix_c_kernel, n_sub=C_SUB),
        grid=(batch, nc),
        in_specs=[pl.BlockSpec((lb, 2 * HEADS * DK), tok),
                  pl.BlockSpec((lb, HEADS * C_DV), tok),
                  pl.BlockSpec((lb, HEADS * C_DV), tok),
                  pl.BlockSpec((lb, 128), tok),
                  pl.BlockSpec((8, 128), const),
                  pl.BlockSpec((8, 128), const),
                  pl.BlockSpec(tri.shape, const),
                  pl.BlockSpec(spread.shape, const),
                  pl.BlockSpec((1, C_DV), const)],
        out_specs=[pl.BlockSpec((lb, HEADS * C_DV), tok),
                   pl.BlockSpec((1, HEADS, DK, C_DV), lambda b, c: (b, 0, 0, 0)),
                   pl.BlockSpec((1, HEADS, DK), lambda b, c: (b, 0, 0)),
                   pl.BlockSpec((1, HEADS, 128), lambda b, c: (b, 0, 0))],
        out_shape=[jax.ShapeDtypeStruct((batch * seq, HEADS * C_DV), F32),
                   jax.ShapeDtypeStruct((batch, HEADS, DK, C_DV), F32),
                   jax.ShapeDtypeStruct((batch, HEADS, DK), F32),
                   jax.ShapeDtypeStruct((batch, HEADS, 128), F32)],
        compiler_params=_cparams("parallel", "arbitrary"),
        name="mix_c_prompt",
    )(qk, v, opre, gates, brow, bcol, tri, spread, gain.reshape(1, C_DV))


def _row_select(rows, t, new, old):
    return jnp.where(rows == t, new, old)


def _step_ab_kernel(qkv_ref, rest_ref, gates_ref, cos_ref, sin_ref, wconv_ref, prow_ref, ggain_ref, rgain_ref,
                    buf_ref, sd_ref, sr_ref, mixed_ref, nbuf_ref, nsd_ref, nsr_ref):
    tb = qkv_ref.shape[0]
    u = qkv_ref[...]
    w = wconv_ref[...]
    b0 = buf_ref[:, 0:A_CONV_CH]
    b1 = buf_ref[:, A_CONV_CH:2 * A_CONV_CH]
    b2 = buf_ref[:, 2 * A_CONV_CH:3 * A_CONV_CH]
    conv = b0 * w[0:1] + b1 * w[1:2] + b2 * w[2:3] + u * w[3:4]
    nbuf_ref[:, 0:A_CONV_CH] = b1
    nbuf_ref[:, A_CONV_CH:2 * A_CONV_CH] = b2
    nbuf_ref[:, 2 * A_CONV_CH:3 * A_CONV_CH] = u
    act = _silu(conv)
    g = gates_ref[...]
    eg_all = jnp.exp(-jnp.exp(prow_ref[0:1, :]) * _softplus(g + prow_ref[1:2, :]))
    beta_all = jax.nn.sigmoid(g)
    cosf = cos_ref[...]
    sinf = sin_ref[...]
    base = HEADS * DK
    rows = lax.broadcasted_iota(jnp.int32, (tb, DK), 0)
    items = [(t, h) for t in range(tb) for h in range(HEADS)]

    qs = [_l2(act[:, h * DK:(h + 1) * DK]) * (DK ** -0.5) for h in range(HEADS)]
    ks = [_l2(act[:, (HEADS + h) * DK:(HEADS + h + 1) * DK]) for h in range(HEADS)]
    k_s = [jnp.zeros((tb, A_DV), F32) for _ in range(HEADS)]
    for t, h in items:
        k_s[h] = _row_select(rows, t, _mm(ks[h], sd_ref[t, h]), k_s[h])
    v_new = []
    for h in range(HEADS):
        v = act[:, (2 * HEADS + h) * DK:(2 * HEADS + h + 1) * DK]
        v_new.append(beta_all[:, HEADS + h:HEADS + h + 1] * (v - eg_all[:, h:h + 1] * k_s[h]))
    for t, h in items:
        nsd_ref[t, h] = (sd_ref[t, h] * eg_all[t:t + 1, h:h + 1]
                         + _mm_tn(jnp.where(rows == t, ks[h], 0.0), v_new[h]))
    o_a = [jnp.zeros((tb, A_DV), F32) for _ in range(HEADS)]
    for t, h in items:
        o_a[h] = _row_select(rows, t, _mm(qs[h], nsd_ref[t, h]), o_a[h])
    for h in range(HEADS):
        ga = rest_ref[:, h * DK:(h + 1) * DK]
        mixed_ref[:, h * DK:(h + 1) * DK] = _rms(o_a[h], ggain_ref[...]) * _silu(ga)

    qrs, krs = [], []
    for h in range(HEADS):
        qb = rest_ref[:, base + h * DK:base + (h + 1) * DK]
        kb = rest_ref[:, 2 * base + h * DK:2 * base + (h + 1) * DK]
        qrs.append(qb * cosf + pltpu.roll(qb, DK // 2, 1) * sinf)
        krs.append((kb * cosf + pltpu.roll(kb, DK // 2, 1) * sinf) * (DK ** -0.5))
    for t, h in items:
        vb = rest_ref[:, 3 * base + h * DK:3 * base + (h + 1) * DK]
        nsr_ref[t, h] = (sr_ref[t, h] * math.exp(LOG_GAMMA[h])
                         + _mm_tn(jnp.where(rows == t, krs[h], 0.0), vb))
    o_b = [jnp.zeros((tb, DK), F32) for _ in range(HEADS)]
    for t, h in items:
        o_b[h] = _row_select(rows, t, _mm(qrs[h], nsr_ref[t, h]), o_b[h])
    for h in range(HEADS):
        gb = rest_ref[:, 4 * base + h * DK:4 * base + (h + 1) * DK]
        mixed_ref[:, base + h * DK:base + (h + 1) * DK] = _rms(o_b[h], rgain_ref[...]) * _silu(gb)


def _mix_ab_sample(qkv, rest, gates, conv_buf, s_delta, s_ret, w_conv, a_log, dt_bias, gdn_gain, ret_gain):
    nb = qkv.shape[0]
    tb = SAMPLE_TB
    assert nb % tb == 0
    cosf, sinf = _rope_tables(PAST_LEN + jnp.arange(1))
    prow, _ = _gate_params(a_log, dt_bias)
    tok = lambda b: (b, 0)
    tok4 = lambda b: (b, 0, 0, 0)
    const = lambda b: (0, 0)
    n_buf = (CONV_W - 1) * A_CONV_CH
    mixed, nbuf, nsd, nsr = pl.pallas_call(
        _step_ab_kernel,
        grid=(nb // tb,),
        in_specs=[pl.BlockSpec((tb, A_CONV_CH), tok),
                  pl.BlockSpec((tb, rest.shape[1]), tok),
                  pl.BlockSpec((tb, 128), tok),
                  pl.BlockSpec((1, DK), const),
                  pl.BlockSpec((1, DK), const),
                  pl.BlockSpec((CONV_W, A_CONV_CH), const),
                  pl.BlockSpec((8, 128), const),
                  pl.BlockSpec((1, A_DV), const),
                  pl.BlockSpec((1, DK), const),
                  pl.BlockSpec((tb, n_buf), tok),
                  pl.BlockSpec((tb, HEADS, DK, A_DV), tok4),
                  pl.BlockSpec((tb, HEADS, DK, DK), tok4)],
        out_specs=[pl.BlockSpec((tb, 2 * HEADS * DK), tok),
                   pl.BlockSpec((tb, n_buf), tok),
                   pl.BlockSpec((tb, HEADS, DK, A_DV), tok4),
                   pl.BlockSpec((tb, HEADS, DK, DK), tok4)],
        out_shape=[jax.ShapeDtypeStruct((nb, 2 * HEADS * DK), F32),
                   jax.ShapeDtypeStruct((nb, n_buf), F32),
                   jax.ShapeDtypeStruct((nb, HEADS, DK, A_DV), F32),
                   jax.ShapeDtypeStruct((nb, HEADS, DK, DK), F32)],
        compiler_params=_cparams("parallel"),
        name="mix_ab_sample",
    )(qkv, rest, gates, cosf, sinf, w_conv, prow, gdn_gain.reshape(1, A_DV), ret_gain.reshape(1, DK),
      conv_buf.reshape(nb, n_buf), s_delta, s_ret)
    return mixed, nbuf.reshape(nb, CONV_W - 1, A_CONV_CH), nsd, nsr


def _step_c_kernel(qk_ref, v_ref, opre_ref, gates_ref, brow_ref, gain_ref, cm_ref, nv_ref, m_ref,
                   h_ref, ncm_ref, nnv_ref, nm_ref):
    tb = qk_ref.shape[0]
    g = gates_ref[...] + brow_ref[0:1, :]
    logf = _log_sigmoid(g)
    rows = lax.broadcasted_iota(jnp.int32, (tb, DK), 0)
    items = [(t, h) for t in range(tb) for h in range(HEADS)]
    qs, kws, fs, ms_new, nvs_new = [], [], [], [], []
    for h in range(HEADS):
        i_pre = g[:, h:h + 1]
        q = qk_ref[:, h * DK:(h + 1) * DK]
        k = qk_ref[:, (HEADS + h) * DK:(HEADS + h + 1) * DK] * (DK ** -0.5)
        inter = logf[:, HEADS + h:HEADS + h + 1] + m_ref[:, h:h + 1]
        m_new = jnp.maximum(inter, i_pre)
        f_state = jnp.exp(inter - m_new)
        kw = k * jnp.exp(i_pre - m_new)
        nv_new = nv_ref[:, h * DK:(h + 1) * DK] * f_state + kw
        nnv_ref[:, h * DK:(h + 1) * DK] = nv_new
        nm_ref[:, h:h + 1] = m_new
        qs.append(q)
        kws.append(kw)
        fs.append(f_state)
        ms_new.append(m_new)
        nvs_new.append(nv_new)
    for t, h in items:
        v = v_ref[:, h * C_DV:(h + 1) * C_DV]
        ncm_ref[t, h] = (cm_ref[t, h] * fs[h][t:t + 1, :]
                         + _mm_tn(jnp.where(rows == t, kws[h], 0.0), v))
    rows_v = lax.broadcasted_iota(jnp.int32, (tb, C_DV), 0)
    nums = [jnp.zeros((tb, C_DV), F32) for _ in range(HEADS)]
    for t, h in items:
        nums[h] = _row_select(rows_v, t, _mm(qs[h], ncm_ref[t, h]), nums[h])
    for h in range(HEADS):
        den = jnp.sum(qs[h] * nvs_new[h], axis=1, keepdims=True)
        hh = nums[h] / jnp.maximum(jnp.abs(den), jnp.exp(-ms_new[h]))
        op = opre_ref[:, h * C_DV:(h + 1) * C_DV]
        h_ref[:, h * C_DV:(h + 1) * C_DV] = _rms(hh, gain_ref[...]) * jax.nn.sigmoid(op)


def _mix_c_sample(qk, v, opre, gates, s_c, s_n, s_m, b_gate, gain):
    nb = qk.shape[0]
    tb = SAMPLE_TB
    assert nb % tb == 0
    brow, _ = _bias_params(b_gate)
    tok = lambda b: (b, 0)
    tok4 = lambda b: (b, 0, 0, 0)
    const = lambda b: (0, 0)
    hh, ncm, nnv, nm = pl.pallas_call(
        _step_c_kernel,
        grid=(nb // tb,),
        in_specs=[pl.BlockSpec((tb, 2 * HEADS * DK), tok),
                  pl.BlockSpec((tb, HEADS * C_DV), tok),
                  pl.BlockSpec((tb, HEADS * C_DV), tok),
                  pl.BlockSpec((tb, 128), tok),
                  pl.BlockSpec((8, 128), const),
                  pl.BlockSpec((1, C_DV), const),
                  pl.BlockSpec((tb, HEADS, DK, C_DV), tok4),
                  pl.BlockSpec((tb, HEADS * DK), tok),
                  pl.BlockSpec((tb, HEADS), tok)],
        out_specs=[pl.BlockSpec((tb, HEADS * C_DV), tok),
                   pl.BlockSpec((tb, HEADS, DK, C_DV), tok4),
                   pl.BlockSpec((tb, HEADS * DK), tok),
                   pl.BlockSpec((tb, HEADS), tok)],
        out_shape=[jax.ShapeDtypeStruct((nb, HEADS * C_DV), F32),
                   jax.ShapeDtypeStruct((nb, HEADS, DK, C_DV), F32),
                   jax.ShapeDtypeStruct((nb, HEADS * DK), F32),
                   jax.ShapeDtypeStruct((nb, HEADS), F32)],
        compiler_params=_cparams("parallel"),
        name="mix_c_sample",
    )(qk, v, opre, gates, brow, gain.reshape(1, C_DV), s_c, s_n.reshape(nb, HEADS * DK), s_m)
    return hh, ncm, nnv.reshape(nb, HEADS, DK), nm


A_SPLITS = ((A_CONV_CH,), (5 * HEADS * DK,), (128,))
C_SPLITS = ((2 * HEADS * DK, HEADS * C_DV, HEADS * C_DV), (128,))


def _gate_columns(w_gates):
    return jnp.pad(w_gates, ((0, 0), (0, 128 - w_gates.shape[1]))).astype(BF16)


def _prep_w_in_a(w):
    n_gate = 2 * HEADS
    return (w[:, :A_CONV_CH].astype(BF16), w[:, A_CONV_CH + n_gate:].astype(BF16),
            _gate_columns(w[:, A_CONV_CH:A_CONV_CH + n_gate]))


def _prep_w_in_c(w):
    n_main = 2 * HEADS * DK + 2 * HEADS * C_DV
    return (w[:, :n_main].astype(BF16), _gate_columns(w[:, n_main:]))


def kernel(x_prompt, x_sample, state_conv_a, state_delta_a, state_ret_b, state_mlstm_C, state_mlstm_n, state_mlstm_m,
           norm_mix_a, w_in_a, w_conv_a, a_log, dt_bias, gdn_gain, ret_gain, w_out_a,
           norm_mix_c, w_in_c, b_gate_c, mlstm_gain, w_out_c, norm_ffn, w_up, w_down, final_gain):
    batch, seq, _ = x_prompt.shape
    n_s = x_sample.shape[0] * x_sample.shape[1]
    tm = PROMPT_TM
    w_a = _prep_w_in_a(w_in_a[0])
    w_c = _prep_w_in_c(w_in_c[0])
    w_ups = w_up.astype(BF16)
    w_downs = w_down.astype(BF16)
    h = x_prompt.reshape(batch * seq, D_MODEL)
    h_s = x_sample.reshape(n_s, D_MODEL)

    (qkv, rest, gates, conv), (qkv_s, rest_s, gates_s) = _proj_a_prompt(
        h, h_s, norm_mix_a[0], w_a, w_conv_a[0], batch, seq, tm)
    mixed, sd, sr = _mix_ab_prompt(qkv, rest, gates, batch, seq, a_log[0], dt_bias[0], gdn_gain[0], ret_gain[0])
    mixed_s, conv_s, sd_s, sr_s = _mix_ab_sample(qkv_s, rest_s, gates_s, state_conv_a[0], state_delta_a[0],
                                                 state_ret_b[0], w_conv_a[0], a_log[0], dt_bias[0], gdn_gain[0],
                                                 ret_gain[0])
    h, h_s = _out_mlp(h, mixed, h_s, mixed_s, w_out_a[0].astype(BF16), norm_ffn[0], w_ups, w_downs, 0, None, tm)

    (qk, v, opre, gates_c), (qk_s, v_s, opre_s, gates_cs) = _norm_proj(h, h_s, norm_mix_c[0], w_c, C_SPLITS, tm)
    hm, cm, nv, m = _mix_c_prompt(qk, v, opre, gates_c, batch, seq, b_gate_c[0], mlstm_gain[0])
    hm_s, cm_s, nv_s, m_s = _mix_c_sample(qk_s, v_s, opre_s, gates_cs, state_mlstm_C[0], state_mlstm_n[0],
                                          state_mlstm_m[0], b_gate_c[0], mlstm_gain[0])
    y, y_s = _out_mlp(h, hm, h_s, hm_s, w_out_c[0].astype(BF16), norm_ffn[1], w_ups, w_downs, 1, final_gain, tm)

    return (y.reshape(x_prompt.shape), y_s.reshape(x_sample.shape),
            conv[None], sd[None], sr[None], cm[None], nv[None], m[:, :, 0][None],
            conv_s[None], sd_s[None], sr_s[None], cm_s[None], nv_s[None], m_s[None])
```

```python
import functools
import math

import jax
import jax.numpy as jnp
from jax import lax
from jax.experimental import pallas as pl
from jax.experimental.pallas import tpu as pltpu

F32 = jnp.float32
BF16 = jnp.bfloat16

D_MODEL = 1024
D_FF = 4 * D_MODEL
CHUNK = 64
EPS = 1e-6
NEG = -1e30
HEADS = 4
DK = 128
A_DV = 128
C_DV = 256
CONV_W = 4
A_CONV_CH = 3 * HEADS * DK
ROPE_BASE = 10000.0
PAST_LEN = 16384
LOG_GAMMA = tuple(math.log1p(-(2.0 ** (-5.0 - h))) for h in range(HEADS))

VMEM_LIMIT_BYTES = 56 * 1024 * 1024
PROMPT_TM = 512
AB_SUB = 8
C_SUB = 8
SAMPLE_TB = 8


def _cparams(*sem):
    return pltpu.CompilerParams(dimension_semantics=sem, vmem_limit_bytes=VMEM_LIMIT_BYTES)


def _mm(a, b):
    return jnp.dot(a.astype(BF16), b.astype(BF16), preferred_element_type=F32)


def _mm_nt(a, b):
    return lax.dot_general(a.astype(BF16), b.astype(BF16), (((1,), (1,)), ((), ())), preferred_element_type=F32)


def _mm_tn(a, b):
    return lax.dot_general(a.astype(BF16), b.astype(BF16), (((0,), (0,)), ((), ())), preferred_element_type=F32)


def _split(a):
    hi = a.astype(BF16)
    lo = (a - hi.astype(F32)).astype(BF16)
    return hi, lo


def _softplus(x):
    return jnp.maximum(x, 0.0) + jnp.log1p(jnp.exp(-jnp.abs(x)))


def _log_sigmoid(x):
    return -_softplus(-x)


def _silu(x):
    return x * jax.nn.sigmoid(x)


def _rms(x, gain):
    return x * lax.rsqrt(jnp.mean(x * x, axis=-1, keepdims=True) + EPS) * gain


def _l2(t):
    return t * lax.rsqrt(jnp.sum(t * t, axis=-1, keepdims=True) + EPS)


def _interleave(gens):
    gens = list(gens)
    while gens:
        alive = []
        for gen in gens:
            try:
                next(gen)
                alive.append(gen)
            except StopIteration:
                pass
        gens = alive


def _lockstep(gens):
    gens = list(gens)
    while gens:
        alive = []
        for gen in gens:
            try:
                next(gen)
                alive.append(gen)
            except StopIteration:
                pass
        gens = alive
        if gens:
            yield


def _head_block_rows(x):
    c, n = x.shape
    t = n // HEADS
    z = jnp.zeros((c, t), x.dtype)
    return jnp.concatenate(
        [jnp.concatenate([x[:, h * t:(h + 1) * t] if g == h else z for g in range(HEADS)], axis=1)
         for h in range(HEADS)], axis=0)


def _head_block_diag(y):
    c, n = y.shape
    per_tile = 128 // c
    assert n == HEADS * c and 128 % c == 0 and HEADS % per_tile == 0
    lane = lax.broadcasted_iota(jnp.int32, (c, 128), 1)
    z = jnp.zeros((c, 128), y.dtype)
    blocks = []
    for h in range(HEADS):
        t = h // per_tile
        lo = (h % per_tile) * c
        kept = jnp.where((lane >= lo) & (lane < lo + c), y[:, t * 128:(t + 1) * 128], z)
        blocks.append(jnp.concatenate([kept if g == t else z for g in range(n // 128)], axis=1))
    return jnp.concatenate(blocks, axis=0)


def _wide_mm(x, y):
    return jnp.dot(x.astype(BF16), _head_block_diag(y.astype(BF16)), preferred_element_type=F32)


def _tri_inv_wide(ns):
    c, wd = ns[0].shape
    ii = lax.broadcasted_iota(jnp.int32, (c, wd), 0)
    jl = lax.broadcasted_iota(jnp.int32, (c, wd), 1) & (c - 1)
    eye = (ii == jl).astype(F32)
    ts = [eye - jnp.where((ii >> 1) == (jl >> 1), n, 0.0) for n in ns]
    for lvl in range(1, int(math.log2(c))):
        off = ((ii >> (lvl + 1)) == (jl >> (lvl + 1))) & ((ii >> lvl) != (jl >> lvl))
        ys = [_wide_mm(jnp.where(off, n, 0.0), t) for n, t in zip(ns, ts)]
        yield
        ts = [t - _wide_mm(t, y) for t, y in zip(ts, ys)]
        yield
    return [t - eye for t in ts]


def _out_width(n):
    return max(n, 128)


def _project_rows(x_ref, g_ref, pieces, out_refs):
    xn = _rms(x_ref[...], g_ref[...]).astype(BF16)
    for (w_ref, col0, n), o_ref in zip(pieces, out_refs):
        res = jnp.dot(xn, w_ref[:, col0:col0 + n], preferred_element_type=F32)
        if n < _out_width(n):
            res = jnp.concatenate([res, jnp.zeros((res.shape[0], _out_width(n) - n), F32)], axis=1)
        o_ref[...] = res


def _norm_proj_kernel(x_ref, xs_ref, g_ref, w_ref, *out_refs, cols):
    pieces = [(w_ref, c0, n) for c0, n in cols]
    _project_rows(x_ref, g_ref, pieces, out_refs[:len(cols)])

    @pl.when(pl.program_id(0) == 0)
    def _():
        _project_rows(xs_ref, g_ref, pieces, out_refs[len(cols):])


def _norm_proj(x, x_s, gain, w, cols, tm):
    t, ts = x.shape[0], x_s.shape[0]
    assert t % tm == 0
    widths = [_out_width(n) for _, n in cols]
    row = lambda i: (i, 0)
    const = lambda i: (0, 0)
    outs = pl.pallas_call(
        functools.partial(_norm_proj_kernel, cols=cols),
        grid=(t // tm,),
        in_specs=[pl.BlockSpec((tm, D_MODEL), row), pl.BlockSpec((ts, D_MODEL), const),
                  pl.BlockSpec((1, D_MODEL), const),
                  pl.BlockSpec(w.shape, const, pipeline_mode=pl.Buffered(1))],
        out_specs=[pl.BlockSpec((tm, n), row) for n in widths] + [pl.BlockSpec((ts, n), const) for n in widths],
        out_shape=[jax.ShapeDtypeStruct((t, n), F32) for n in widths]
                  + [jax.ShapeDtypeStruct((ts, n), F32) for n in widths],
        compiler_params=_cparams("arbitrary"),
        name="norm_proj",
    )(x, x_s, gain.reshape(1, D_MODEL), w)
    return outs[:len(widths)], outs[len(widths):]


def _proj_a_prompt_kernel(x_ref, xs_ref, g_ref, w_ref, wconv_ref, cos_ref, sin_ref,
                          qkv_ref, rest_ref, gates_ref, conv_ref, qkvs_ref, rests_ref, gatess_ref,
                          xp_ref, raw_ref, wrest_ref, *, tiles_per_seq):
    tm = x_ref.shape[0]
    hd = HEADS * DK
    n_gate = 2 * HEADS
    gate_piece = (w_ref, A_CONV_CH, n_gate)

    @pl.when(pl.program_id(0) == 0)
    def _():
        wrest_ref[...] = w_ref[:, A_CONV_CH + n_gate:A_CONV_CH + n_gate + 5 * hd]
        _project_rows(xs_ref, g_ref, [(w_ref, 0, A_CONV_CH), (wrest_ref, 0, 5 * hd), gate_piece],
                      (qkvs_ref, rests_ref, gatess_ref))

    xn = _rms(x_ref[...], g_ref[...]).astype(BF16)

    @pl.when(pl.program_id(0) % tiles_per_seq == 0)
    def _():
        xp_ref[0:8, :] = jnp.zeros((8, A_CONV_CH), F32)

    w = wconv_ref[...]
    base = A_CONV_CH

    def conv_part(part, slot, r0, r1):
        c0, c1 = part * hd, (part + 1) * hd
        raw = raw_ref[slot, r0:r1, :]
        xp_ref[8 + r0:8 + r1, c0:c1] = raw
        conv = (xp_ref[5 + r0:5 + r1, c0:c1] * w[0:1, c0:c1] + xp_ref[6 + r0:6 + r1, c0:c1] * w[1:2, c0:c1]
                + xp_ref[7 + r0:7 + r1, c0:c1] * w[2:3, c0:c1] + raw * w[3:4, c0:c1])
        if r1 == tm:
            rows = r1 - r0
            xp_ref[0:8, c0:c1] = raw[rows - 8:rows, :]
            conv_ref[0, :, c0:c1] = raw[rows - 3:rows, :]
        act = _silu(conv)
        if part == 2:
            qkv_ref[r0:r1, c0:c1] = act
        else:
            scale = DK ** -0.5 if part == 0 else 1.0
            for h in range(HEADS):
                qkv_ref[r0:r1, c0 + h * DK:c0 + (h + 1) * DK] = _l2(act[:, h * DK:(h + 1) * DK]) * scale

    def rest_part(part, slot, r0, r1):
        c0, c1 = part * hd, (part + 1) * hd
        raw = raw_ref[slot, r0:r1, :]
        if part in (0, 4):
            rest_ref[r0:r1, c0:c1] = _silu(raw)
        elif part == 3:
            rest_ref[r0:r1, c0:c1] = raw
        else:
            scale = 1.0 if part == 1 else DK ** -0.5
            for h in range(HEADS):
                t = raw[:, h * DK:(h + 1) * DK]
                rest_ref[r0:r1, c0 + h * DK:c0 + (h + 1) * DK] = (
                    (t * cos_ref[r0:r1, :] + pltpu.roll(t, DK // 2, 1) * sin_ref[r0:r1, :]) * scale)

    def gates_part(_, slot, r0, r1):
        gates_ref[r0:r1, :] = jnp.concatenate(
            [raw_ref[slot, r0:r1, 0:n_gate], jnp.zeros((r1 - r0, 128 - n_gate), F32)], axis=1)

    groups = [(w_ref, 0, hd, functools.partial(conv_part, 0)),
              (wrest_ref, 3 * hd, hd, functools.partial(rest_part, 3)),
              (w_ref, hd, hd, functools.partial(conv_part, 1)),
              (wrest_ref, 0, hd, functools.partial(rest_part, 0)),
              (w_ref, 2 * hd, hd, functools.partial(conv_part, 2)),
              (wrest_ref, hd, hd, functools.partial(rest_part, 1)),
              gate_piece + (functools.partial(gates_part, 0),),
              (wrest_ref, 2 * hd, hd, functools.partial(rest_part, 2)),
              (wrest_ref, 4 * hd, hd, functools.partial(rest_part, 4))]

    def project(n):
        wg_ref, col0, width, _ = groups[n]
        for j in range(0, width, 256):
            wj = min(256, width - j)
            raw_ref[n % 2, :, j:j + wj] = jnp.dot(xn, wg_ref[:, col0 + j:col0 + j + wj],
                                                  preferred_element_type=F32)
            yield

    def epilogue(n):
        for r0 in range(0, tm, 128):
            groups[n][3](n % 2, r0, min(r0 + 128, tm))
            yield

    _interleave([project(0)])
    for n in range(len(groups)):
        _interleave([epilogue(n)] + ([project(n + 1)] if n + 1 < len(groups) else []))


def _proj_a_prompt(x, x_s, gain, w, w_conv, batch, seq, tm):
    t, ts = x.shape[0], x_s.shape[0]
    assert t % tm == 0 and seq % tm == 0
    tiles_per_seq = seq // tm
    cosf, sinf = _rope_tables(jnp.arange(seq))
    row = lambda i: (i, 0)
    const = lambda i: (0, 0)
    pos = lambda i: (i % tiles_per_seq, 0)
    n_rest = 5 * HEADS * DK
    outs = pl.pallas_call(
        functools.partial(_proj_a_prompt_kernel, tiles_per_seq=tiles_per_seq),
        grid=(t // tm,),
        in_specs=[pl.BlockSpec((tm, D_MODEL), row),
                  pl.BlockSpec((ts, D_MODEL), const),
                  pl.BlockSpec((1, D_MODEL), const),
                  pl.BlockSpec(w.shape, const, pipeline_mode=pl.Buffered(1)),
                  pl.BlockSpec((CONV_W, A_CONV_CH), const),
                  pl.BlockSpec((tm, DK), pos),
                  pl.BlockSpec((tm, DK), pos)],
        out_specs=[pl.BlockSpec((tm, A_CONV_CH), row),
                   pl.BlockSpec((tm, n_rest), row),
                   pl.BlockSpec((tm, 128), row),
                   pl.BlockSpec((1, CONV_W - 1, A_CONV_CH), lambda i: (i // tiles_per_seq, 0, 0)),
                   pl.BlockSpec((ts, A_CONV_CH), const),
                   pl.BlockSpec((ts, n_rest), const),
                   pl.BlockSpec((ts, 128), const)],
        out_shape=[jax.ShapeDtypeStruct((t, A_CONV_CH), F32),
                   jax.ShapeDtypeStruct((t, n_rest), F32),
                   jax.ShapeDtypeStruct((t, 128), F32),
                   jax.ShapeDtypeStruct((batch, CONV_W - 1, A_CONV_CH), F32),
                   jax.ShapeDtypeStruct((ts, A_CONV_CH), F32),
                   jax.ShapeDtypeStruct((ts, n_rest), F32),
                   jax.ShapeDtypeStruct((ts, 128), F32)],
        scratch_shapes=[pltpu.VMEM((tm + 8, A_CONV_CH), F32), pltpu.VMEM((2, tm, HEADS * DK), F32),
                        pltpu.VMEM((D_MODEL, n_rest), BF16)],
        compiler_params=_cparams("arbitrary"),
        name="proj_a_prompt",
    )(x, x_s, gain.reshape(1, D_MODEL), w, w_conv, cosf, sinf)
    return outs[:4], outs[4:]


def _out_mlp_kernel(h_ref, m_ref, hs_ref, ms_ref, wout_ref, gffn_ref, wup_ref, wdown_ref, *rest, final):
    if final:
        gfin_ref, o_ref, os_ref = rest
    else:
        o_ref, os_ref = rest

    def block(h_r, m_r, o_r):
        h = h_r[...] + jnp.dot(m_r[...].astype(BF16), wout_ref[...], preferred_element_type=F32)
        xn = _rms(h, gffn_ref[...]).astype(BF16)
        acc = h
        step = 1024
        for j in range(D_FF // step):
            hid = jnp.dot(xn, wup_ref[:, j * step:(j + 1) * step], preferred_element_type=F32)
            hid = jnp.maximum(hid, 0.0)
            acc = acc + jnp.dot((hid * hid).astype(BF16), wdown_ref[j * step:(j + 1) * step, :],
                                preferred_element_type=F32)
        if final:
            acc = _rms(acc, gfin_ref[...])
        o_r[...] = acc

    block(h_ref, m_ref, o_ref)

    @pl.when(pl.program_id(0) == 0)
    def _():
        block(hs_ref, ms_ref, os_ref)


def _out_mlp(h, mix, h_s, mix_s, w_out, g_ffn, w_up, w_down, layer, g_final, tm):
    t, ts = h.shape[0], h_s.shape[0]
    assert t % tm == 0
    final = g_final is not None
    row = lambda i: (i, 0)
    const = lambda i: (0, 0)
    this_layer = lambda i: (layer, 0, 0)
    once = pl.Buffered(1)
    in_specs = [pl.BlockSpec((tm, D_MODEL), row), pl.BlockSpec((tm, mix.shape[1]), row),
                pl.BlockSpec((ts, D_MODEL), const), pl.BlockSpec((ts, mix_s.shape[1]), const),
                pl.BlockSpec(w_out.shape, const, pipeline_mode=once), pl.BlockSpec((1, D_MODEL), const),
                pl.BlockSpec((None,) + w_up.shape[1:], this_layer, pipeline_mode=once),
                pl.BlockSpec((None,) + w_down.shape[1:], this_layer, pipeline_mode=once)]
    args = [h, mix, h_s, mix_s, w_out, g_ffn.reshape(1, D_MODEL), w_up, w_down]
    if final:
        in_specs.append(pl.BlockSpec((1, D_MODEL), const))
        args.append(g_final.reshape(1, D_MODEL))
    return pl.pallas_call(
        functools.partial(_out_mlp_kernel, final=final),
        grid=(t // tm,),
        in_specs=in_specs,
        out_specs=[pl.BlockSpec((tm, D_MODEL), row), pl.BlockSpec((ts, D_MODEL), const)],
        out_shape=[jax.ShapeDtypeStruct((t, D_MODEL), F32), jax.ShapeDtypeStruct((ts, D_MODEL), F32)],
        compiler_params=_cparams("arbitrary"),
        name="out_mlp",
    )(*args)


def _chunk_masks(c):
    ii = lax.broadcasted_iota(jnp.int32, (c, c), 0)
    jj = lax.broadcasted_iota(jnp.int32, (c, c), 1)
    return ii, jj


def _mix_ab_kernel(qkv_ref, rest_ref, gates_ref, prow_ref, pcol_ref, tri_ref, spread_ref,
                   ggain_ref, rgain_ref, mixed_ref, sd_ref, sr_ref, *, n_sub):
    c = CHUNK
    lb = n_sub * c
    step = pl.program_id(1)

    @pl.when(step == 0)
    def _():
        sd_ref[...] = jnp.zeros_like(sd_ref)
        sr_ref[...] = jnp.zeros_like(sr_ref)

    g = gates_ref[...]
    g_t = g.T
    neg_a_row = -jnp.exp(prow_ref[0:1, :])
    dt_row = prow_ref[1:2, :]
    neg_a_col = -jnp.exp(pcol_ref[0:HEADS, 0:1])
    dt_col = pcol_ref[0:HEADS, 1:2]
    la_cols = neg_a_row * _softplus(g + dt_row)
    beta_cols = jax.nn.sigmoid(g)
    la_rows = neg_a_col * _softplus(g_t[0:HEADS, :] + dt_col)

    tri = tri_ref[...]
    la_hi, la_lo = _split(la_cols)
    g_cols = (jnp.dot(tri, la_hi, preferred_element_type=F32)
              + jnp.dot(tri, la_lo, preferred_element_type=F32))
    lr_hi, lr_lo = _split(jnp.concatenate([la_rows, jnp.zeros_like(la_rows)], axis=0))
    nt = lambda x, y: lax.dot_general(x, y, (((1,), (1,)), ((), ())), preferred_element_type=F32)
    g_rows = nt(lr_hi, tri) + nt(lr_lo, tri)
    lane = lax.broadcasted_iota(jnp.int32, (lb, 128), 1)
    x_hi, x_lo = _split(jnp.where(lane < HEADS, g_cols, beta_cols))
    spread = (jnp.dot(x_hi, spread_ref[...], preferred_element_type=F32)
              + jnp.dot(x_lo, spread_ref[...], preferred_element_type=F32))
    hd = HEADS * DK
    wd = HEADS * c
    g_wide = spread[:, 0:hd]
    beta_wide = spread[:, hd:2 * hd]
    g_half = spread[:, 2 * hd:2 * hd + wd]
    eg_wide = jnp.exp(g_wide)

    ii = lax.broadcasted_iota(jnp.int32, (c, wd), 0)
    jl = lax.broadcasted_iota(jnp.int32, (c, wd), 1) & (c - 1)
    causal = ii >= jl
    strict = ii > jl
    lane_hd = lax.broadcasted_iota(jnp.int32, (1, hd), 1)
    lane_wd = lax.broadcasted_iota(jnp.int32, (1, wd), 1)
    lg_hd = jnp.full((1, hd), LOG_GAMMA[HEADS - 1], F32)
    lg_wd = jnp.full((1, wd), LOG_GAMMA[HEADS - 1], F32)
    for h in range(HEADS - 2, -1, -1):
        lg_hd = jnp.where(lane_hd < (h + 1) * DK, LOG_GAMMA[h], lg_hd)
        lg_wd = jnp.where(lane_wd < (h + 1) * c, LOG_GAMMA[h], lg_wd)
    pos_col = lax.broadcasted_iota(jnp.int32, (c, hd), 0).astype(F32)
    ret_dec = jnp.exp(jnp.where(causal, (ii - jl).astype(F32) * lg_wd, NEG))
    ret_q_scale = jnp.exp((pos_col + 1.0) * lg_hd)
    ret_k_scale = jnp.exp((float(c - 1) - pos_col) * lg_hd)

    ggain = ggain_ref[...]
    rgain = rgain_ref[...]
    base = hd
    tile = lambda x, h: x[:, h * DK:(h + 1) * DK]
    s_delta = [sd_ref[0, h] for h in range(HEADS)]
    s_ret = [sr_ref[0, h] for h in range(HEADS)]

    chunks = list(range(n_sub))
    items = [(ci, h) for ci in chunks for h in range(HEADS)]
    rng = lambda ci: (ci * c, (ci + 1) * c)
    pre = {it: {} for it in items}
    pre_b = {it: {} for it in items}
    wide = {ci: {} for ci in chunks}
    nt = lambda x, y: lax.dot_general(x, y, (((1,), (1,)), ((), ())), preferred_element_type=F32)

    def gdn_stage(ci):
        w_ = wide[ci]
        r0, r1 = rng(ci)
        q = qkv_ref[r0:r1, 0:hd]
        k = qkv_ref[r0:r1, hd:2 * hd]
        beta = beta_wide[r0:r1]
        g_col = g_wide[r0:r1]
        kb = k * beta
        g_row = jnp.concatenate([g_rows[h:h + 1, r0:r1] for h in range(HEADS)], axis=1)
        dec_causal = jnp.exp(jnp.where(causal, g_half[r0:r1] - g_row, NEG))
        prod = nt(jnp.concatenate([q, kb], axis=0).astype(BF16), _head_block_rows(k.astype(BF16)))
        yield
        w_['n'] = prod[c:2 * c] * jnp.where(strict, dec_causal, 0.0)
        w_['qk'] = prod[0:c] * dec_causal
        yield
        v = qkv_ref[r0:r1, 2 * hd:3 * hd]
        eg = eg_wide[r0:r1]
        g_last = g_col[c - 1:c, :]
        rhs_u, rhs_w, qd = v * beta, kb * eg, q * eg
        kd = k * jnp.exp(g_last - g_col)
        gl = jnp.exp(g_last)
        for h in range(HEADS):
            pre[ci, h].update(rhs=jnp.concatenate([tile(rhs_u, h), tile(rhs_w, h)], axis=1), qd=tile(qd, h),
                              kd=tile(kd, h), gl=tile(gl, h))

    def ret_stage(ci):
        r0, r1 = rng(ci)
        qr = rest_ref[r0:r1, base:2 * base]
        kr = rest_ref[r0:r1, 2 * base:3 * base]
        vb = rest_ref[r0:r1, 3 * base:4 * base]
        qk = nt(qr.astype(BF16), _head_block_rows(kr.astype(BF16))) * ret_dec
        qd = qr * ret_q_scale
        kd = kr * ret_k_scale
        yield
        for a in range(HEADS // 2):
            h0, h1 = 2 * a, 2 * a + 1
            z = jnp.zeros((c, DK), F32)
            rhs = jnp.concatenate([jnp.concatenate([tile(vb, h0), z], axis=1),
                                   jnp.concatenate([z, tile(vb, h1)], axis=1)], axis=0)
            intra = _mm(qk[:, a * 2 * c:(a + 1) * 2 * c], rhs)
            pre_b[ci, h0]['intra'] = intra[:, 0:DK]
            pre_b[ci, h1]['intra'] = intra[:, DK:2 * DK]
        yield
        for h in range(HEADS):
            pre_b[ci, h]['qd'] = tile(qd, h)
            pre_b[ci, h]['kv'] = _mm_tn(tile(kd, h), tile(vb, h))

    def solve_stage():
        t_offs = yield from _tri_inv_wide([wide[ci]['n'] for ci in chunks])
        for ci, t_off in zip(chunks, t_offs):
            wide[ci]['t_off'] = t_off

    def pair_rhs(x0, x1):
        z = jnp.zeros_like(x0)
        return jnp.concatenate([jnp.concatenate([x0, z], axis=1), jnp.concatenate([z, x1], axis=1)], axis=0)

    def sol_stage(ci):
        w_ = wide[ci]
        for a in range(HEADS // 2):
            h0, h1 = 2 * a, 2 * a + 1
            r = _mm(w_['t_off'][:, a * 2 * c:(a + 1) * 2 * c], pair_rhs(pre[ci, h0]['rhs'], pre[ci, h1]['rhs']))
            pre[ci, h0]['sol'] = pre[ci, h0]['rhs'] + r[:, 0:2 * A_DV]
            pre[ci, h1]['sol'] = pre[ci, h1]['rhs'] + r[:, 2 * A_DV:4 * A_DV]
        yield
        for a in range(HEADS // 2):
            h0, h1 = 2 * a, 2 * a + 1
            r = _mm(w_['qk'][:, a * 2 * c:(a + 1) * 2 * c], pair_rhs(pre[ci, h0]['sol'], pre[ci, h1]['sol']))
            pre[ci, h0]['qs'] = r[:, 0:2 * A_DV]
            pre[ci, h1]['qs'] = r[:, 2 * A_DV:4 * A_DV]
        for h in range(HEADS):
            d = pre[ci, h]
            d['kts'] = _mm_tn(d['kd'], d['sol'])
        yield
        for h in range(HEADS):
            d = pre[ci, h]
            d['lhs'] = jnp.concatenate([d['kts'][:, A_DV:2 * A_DV], d['qd'] - d['qs'][:, A_DV:2 * A_DV]], axis=0)


    def ret_out_stage(ci, h):
        d = pre_b[ci, h]
        r0, r1 = rng(ci)
        o = _mm(d['qd'], d['s_in']) + d['intra']
        yield
        ms_ = jnp.mean(o * o, axis=-1, keepdims=True)
        yield
        gb = rest_ref[r0:r1, 4 * base + h * DK:4 * base + (h + 1) * DK]
        mixed_ref[r0:r1, base + h * DK:base + (h + 1) * DK] = o * lax.rsqrt(ms_ + EPS) * rgain * gb

    def delta_step(ci, h):
        d = pre[ci, h]
        r = _mm(d['lhs'], s_delta[h])
        yield
        d['o'] = r[DK:DK + c] + d['qs'][:, 0:A_DV]
        s_delta[h] = s_delta[h] * d['gl'] - r[0:DK] + d['kts'][:, 0:A_DV]

    def gdn_out_stage(ci, h):
        o = pre[ci, h]['o']
        r0, r1 = rng(ci)
        ms_ = jnp.mean(o * o, axis=-1, keepdims=True)
        yield
        ga = rest_ref[r0:r1, h * DK:(h + 1) * DK]
        mixed_ref[r0:r1, h * A_DV:(h + 1) * A_DV] = o * lax.rsqrt(ms_ + EPS) * ggain * ga

    def delta_track():
        yield from solve_stage()
        yield from _lockstep([sol_stage(ci) for ci in chunks])
        for ci in chunks:
            yield from _lockstep([delta_step(ci, h) for h in range(HEADS)])
            if ci > 0:
                yield from _lockstep([gdn_out_stage(ci - 1, h) for h in range(HEADS)])
        yield from _lockstep([gdn_out_stage(chunks[-1], h) for h in range(HEADS)])

    def ret_track():
        for ci in chunks:
            yield from ret_stage(ci)
            for h in range(HEADS):
                pre_b[ci, h]['s_in'] = s_ret[h]
                s_ret[h] = s_ret[h] * math.exp(c * LOG_GAMMA[h]) + pre_b[ci, h]['kv']
            yield
            yield from _lockstep([ret_out_stage(ci, h) for h in range(HEADS)])

    _interleave([gdn_stage(ci) for ci in chunks])
    _interleave([delta_track(), ret_track()])

    for h in range(HEADS):
        sd_ref[0, h] = s_delta[h]
        sr_ref[0, h] = s_ret[h]


def _rope_tables(pos):
    half = DK // 2
    inv = ROPE_BASE ** (-jnp.arange(half, dtype=F32) / half)
    ang = pos.astype(F32)[:, None] * inv[None, :]
    cos, sin = jnp.cos(ang), jnp.sin(ang)
    return jnp.concatenate([cos, cos], axis=-1), jnp.concatenate([-sin, sin], axis=-1)


def _chunk_tri(lb):
    i = jnp.arange(lb)[:, None]
    j = jnp.arange(lb)[None, :]
    return ((i >= j) & (i // CHUNK == j // CHUNK)).astype(BF16)


def _head_spread(groups):
    n = groups * HEADS
    src = jnp.arange(128)[:, None]
    dst = jnp.arange(n * 128)[None, :] // 128
    return (src == dst).astype(BF16)


def _head_spread_ab():
    src = jnp.arange(128)[:, None]
    full = jnp.arange(2 * HEADS * DK)[None, :] // DK
    half = jnp.arange(HEADS * CHUNK)[None, :] // CHUNK
    return jnp.concatenate([src == full, src == half], axis=1).astype(BF16)


def _gate_params(a_log, dt_bias):
    prow = jnp.zeros((8, 128), F32).at[0, 0:HEADS].set(a_log).at[1, 0:HEADS].set(dt_bias)
    pcol = jnp.zeros((8, 128), F32).at[0:HEADS, 0].set(a_log).at[0:HEADS, 1].set(dt_bias)
    return prow, pcol


def _mix_ab_prompt(qkv, rest, gates, batch, seq, a_log, dt_bias, gdn_gain, ret_gain):
    lb = AB_SUB * CHUNK
    assert seq % lb == 0
    nc = seq // lb
    prow, pcol = _gate_params(a_log, dt_bias)
    tri = _chunk_tri(lb)
    spread = _head_spread_ab()
    tok = lambda b, c: (b * nc + c, 0)
    const = lambda b, c: (0, 0)
    mixed, sd, sr = pl.pallas_call(
        functools.partial(_mix_ab_kernel, n_sub=AB_SUB),
        grid=(batch, nc),
        in_specs=[pl.BlockSpec((lb, A_CONV_CH), tok),
                  pl.BlockSpec((lb, rest.shape[1]), tok),
                  pl.BlockSpec((lb, 128), tok),
                  pl.BlockSpec((8, 128), const),
                  pl.BlockSpec((8, 128), const),
                  pl.BlockSpec(tri.shape, const),
                  pl.BlockSpec(spread.shape, const),
                  pl.BlockSpec((1, A_DV), const),
                  pl.BlockSpec((1, DK), const)],
        out_specs=[pl.BlockSpec((lb, 2 * HEADS * DK), tok),
                   pl.BlockSpec((1, HEADS, DK, A_DV), lambda b, c: (b, 0, 0, 0)),
                   pl.BlockSpec((1, HEADS, DK, DK), lambda b, c: (b, 0, 0, 0))],
        out_shape=[jax.ShapeDtypeStruct((batch * seq, 2 * HEADS * DK), F32),
                   jax.ShapeDtypeStruct((batch, HEADS, DK, A_DV), F32),
                   jax.ShapeDtypeStruct((batch, HEADS, DK, DK), F32)],
        compiler_params=_cparams("parallel", "arbitrary"),
        name="mix_ab_prompt",
    )(qkv, rest, gates, prow, pcol, tri, spread, gdn_gain.reshape(1, A_DV), ret_gain.reshape(1, DK))
    return mixed, sd, sr


def _mix_c_kernel(qk_ref, v_ref, opre_ref, gates_ref, brow_ref, bcol_ref, tri_ref, spread_ref, gain_ref,
                  h_ref, cm_ref, nv_ref, m_ref, *, n_sub):
    c = CHUNK
    lb = n_sub * c
    step = pl.program_id(1)

    @pl.when(step == 0)
    def _():
        cm_ref[...] = jnp.zeros_like(cm_ref)
        nv_ref[...] = jnp.zeros_like(nv_ref)
        m_ref[...] = jnp.zeros_like(m_ref)

    items = [(ci, h) for ci in range(n_sub) for h in range(HEADS)]
    rng = lambda ci: (ci * c, (ci + 1) * c)
    pre = {it: {} for it in items}

    for ci, h in items:
        r0, r1 = rng(ci)
        q = qk_ref[r0:r1, h * DK:(h + 1) * DK]
        k = qk_ref[r0:r1, (HEADS + h) * DK:(HEADS + h + 1) * DK] * (DK ** -0.5)
        pre[ci, h].update(q=q, k=k, qk=_mm_nt(q, k))

    g = gates_ref[...] + brow_ref[0:1, :]
    g_t = gates_ref[...].T[0:2 * HEADS, :] + bcol_ref[0:2 * HEADS, 0:1]
    i_rows = g_t[0:HEADS, :]
    logf_rows = _log_sigmoid(g_t[HEADS:2 * HEADS, :])
    tri = tri_ref[...]
    lf_hi, lf_lo = _split(_log_sigmoid(g))
    b_cols = (jnp.dot(tri, lf_hi, preferred_element_type=F32)
              + jnp.dot(tri, lf_lo, preferred_element_type=F32))
    lr_hi, lr_lo = _split(jnp.concatenate([logf_rows, jnp.zeros_like(logf_rows)], axis=0))
    nt = lambda x, y: lax.dot_general(x, y, (((1,), (1,)), ((), ())), preferred_element_type=F32)
    b_rows = nt(lr_hi, tri) + nt(lr_lo, tri)
    lane = lax.broadcasted_iota(jnp.int32, (lb, 128), 1)
    x_hi, x_lo = _split(jnp.where(lane < HEADS, g, b_cols))
    spread = (jnp.dot(x_hi, spread_ref[...], preferred_element_type=F32)
              + jnp.dot(x_lo, spread_ref[...], preferred_element_type=F32))
    i_wide = spread[:, 0:HEADS * 128]
    b_wide = spread[:, HEADS * 128:2 * HEADS * 128]

    ii, jj = _chunk_masks(c)
    causal = ii >= jj
    gain = gain_ref[...]
    wide2 = lambda x: jnp.concatenate([x, x], axis=-1)

    def gates_stage(ci, h):
        d = pre[ci, h]
        r0, r1 = rng(ci)
        b_col = b_wide[r0:r1, h * 128:(h + 1) * 128]
        i_col = i_wide[r0:r1, h * 128:(h + 1) * 128]
        b_last = b_col[c - 1:c, :]
        d_log = jnp.where(causal, b_col[:, 0:c] - b_rows[h:h + 1, r0:r1] + i_rows[h:h + 1, r0:r1], NEG)
        d.update(b_col=b_col, b_last=b_last, d_log=d_log, k_log=b_last - b_col + i_col)
        yield
        d['d_max'] = jnp.max(d_log, axis=1, keepdims=True)

    _interleave(gates_stage(ci, h) for ci, h in items)

    ms = [m_ref[0, h:h + 1, :] for h in range(HEADS)]
    for ci, h in items:
        d = pre[ci, h]
        inter = d['b_col'] + ms[h]
        m_row = jnp.maximum(inter, d['d_max'])
        m_new = m_row[c - 1:c, :]
        d.update(inter=inter, m_row=m_row, m_old=ms[h], m_new=m_new)
        ms[h] = m_new

    def weights_stage(ci, h):
        d = pre[ci, h]
        r0, r1 = rng(ci)
        d['w_inter'] = jnp.exp(d['inter'] - d['m_row'])
        d['f_state'] = jnp.exp(d['b_last'] + d['m_old'] - d['m_new'])
        yield
        d['w_intra'] = jnp.exp(d['d_log'] - d['m_row'][:, 0:c]) * d['qk']
        d['kw'] = d['k'] * jnp.exp(d['k_log'] - d['m_new'])
        yield
        v = v_ref[r0:r1, h * C_DV:(h + 1) * C_DV]
        d['intra'] = _mm(d['w_intra'], v)
        d['kv'] = _mm_tn(d['kw'], v)
        yield
        d['sum_intra'] = jnp.sum(d['w_intra'], axis=1, keepdims=True)
        d['sum_kw'] = jnp.sum(d['kw'], axis=0, keepdims=True)
        d['inv_floor'] = jnp.exp(-d['m_row'])

    _interleave(weights_stage(ci, h) for ci, h in items)

    cms = [cm_ref[0, h] for h in range(HEADS)]
    nvs = [nv_ref[0, h:h + 1, :] for h in range(HEADS)]
    for ci, h in items:
        d = pre[ci, h]
        d['cm_in'], d['nv_in'] = cms[h], nvs[h]
        cms[h] = cms[h] * wide2(d['f_state']) + d['kv']
        nvs[h] = nvs[h] * d['f_state'] + d['sum_kw']

    def output_stage(ci, h):
        d = pre[ci, h]
        r0, r1 = rng(ci)
        qc = _mm(d['q'], d['cm_in'])
        qn = jnp.sum(d['q'] * d['nv_in'], axis=1, keepdims=True)
        yield
        num = wide2(d['w_inter']) * qc + d['intra']
        den = d['w_inter'] * qn + d['sum_intra']
        hh = num / wide2(jnp.maximum(jnp.abs(den), d['inv_floor']))
        yield
        ms_ = jnp.mean(hh * hh, axis=-1, keepdims=True)
        yield
        op = opre_ref[r0:r1, h * C_DV:(h + 1) * C_DV]
        h_ref[r0:r1, h * C_DV:(h + 1) * C_DV] = hh * lax.rsqrt(ms_ + EPS) * gain * jax.nn.sigmoid(op)

    _interleave(output_stage(ci, h) for ci, h in items)

    for h in range(HEADS):
        cm_ref[0, h] = cms[h]
        nv_ref[0, h:h + 1, :] = nvs[h]
        m_ref[0, h:h + 1, :] = ms[h]


def _bias_params(b_gate):
    brow = jnp.zeros((8, 128), F32).at[0, 0:2 * HEADS].set(b_gate)
    bcol = jnp.zeros((8, 128), F32).at[0:2 * HEADS, 0].set(b_gate)
    return brow, bcol


def _mix_c_prompt(qk, v, opre, gates, batch, seq, b_gate, gain):
    lb = C_SUB * CHUNK
    assert seq % lb == 0
    nc = seq // lb
    brow, bcol = _bias_params(b_gate)
    tri = _chunk_tri(lb)
    spread = _head_spread(2)
    tok = lambda b, c: (b * nc + c, 0)
    const = lambda b, c: (0, 0)
    return pl.pallas_call(
        functools.partial(_mix_c_kernel, n_sub=C_SUB),
        grid=(batch, nc),
        in_specs=[pl.BlockSpec((lb, 2 * HEADS * DK), tok),
                  pl.BlockSpec((lb, HEADS * C_DV), tok),
                  pl.BlockSpec((lb, HEADS * C_DV), tok),
                  pl.BlockSpec((lb, 128), tok),
                  pl.BlockSpec((8, 128), const),
                  pl.BlockSpec((8, 128), const),
                  pl.BlockSpec(tri.shape, const),
                  pl.BlockSpec(spread.shape, const),
                  pl.BlockSpec((1, C_DV), const)],
        out_specs=[pl.BlockSpec((lb, HEADS * C_DV), tok),
                   pl.BlockSpec((1, HEADS, DK, C_DV), lambda b, c: (b, 0, 0, 0)),
                   pl.BlockSpec((1, HEADS, DK), lambda b, c: (b, 0, 0)),
                   pl.BlockSpec((1, HEADS, 128), lambda b, c: (b, 0, 0))],
        out_shape=[jax.ShapeDtypeStruct((batch * seq, HEADS * C_DV), F32),
                   jax.ShapeDtypeStruct((batch, HEADS, DK, C_DV), F32),
                   jax.ShapeDtypeStruct((batch, HEADS, DK), F32),
                   jax.ShapeDtypeStruct((batch, HEADS, 128), F32)],
        compiler_params=_cparams("parallel", "arbitrary"),
        name="mix_c_prompt",
    )(qk, v, opre, gates, brow, bcol, tri, spread, gain.reshape(1, C_DV))


def _row_select(rows, t, new, old):
    return jnp.where(rows == t, new, old)


def _step_ab_kernel(qkv_ref, rest_ref, gates_ref, cos_ref, sin_ref, wconv_ref, prow_ref, ggain_ref, rgain_ref,
                    buf_ref, sd_ref, sr_ref, mixed_ref, nbuf_ref, nsd_ref, nsr_ref):
    tb = qkv_ref.shape[0]
    u = qkv_ref[...]
    w = wconv_ref[...]
    b0 = buf_ref[:, 0:A_CONV_CH]
    b1 = buf_ref[:, A_CONV_CH:2 * A_CONV_CH]
    b2 = buf_ref[:, 2 * A_CONV_CH:3 * A_CONV_CH]
    conv = b0 * w[0:1] + b1 * w[1:2] + b2 * w[2:3] + u * w[3:4]
    nbuf_ref[:, 0:A_CONV_CH] = b1
    nbuf_ref[:, A_CONV_CH:2 * A_CONV_CH] = b2
    nbuf_ref[:, 2 * A_CONV_CH:3 * A_CONV_CH] = u
    act = _silu(conv)
    g = gates_ref[...]
    eg_all = jnp.exp(-jnp.exp(prow_ref[0:1, :]) * _softplus(g + prow_ref[1:2, :]))
    beta_all = jax.nn.sigmoid(g)
    cosf = cos_ref[...]
    sinf = sin_ref[...]
    base = HEADS * DK
    rows = lax.broadcasted_iota(jnp.int32, (tb, DK), 0)
    items = [(t, h) for t in range(tb) for h in range(HEADS)]

    qs = [_l2(act[:, h * DK:(h + 1) * DK]) * (DK ** -0.5) for h in range(HEADS)]
    ks = [_l2(act[:, (HEADS + h) * DK:(HEADS + h + 1) * DK]) for h in range(HEADS)]
    k_s = [jnp.zeros((tb, A_DV), F32) for _ in range(HEADS)]
    for t, h in items:
        k_s[h] = _row_select(rows, t, _mm(ks[h], sd_ref[t, h]), k_s[h])
    v_new = []
    for h in range(HEADS):
        v = act[:, (2 * HEADS + h) * DK:(2 * HEADS + h + 1) * DK]
        v_new.append(beta_all[:, HEADS + h:HEADS + h + 1] * (v - eg_all[:, h:h + 1] * k_s[h]))
    for t, h in items:
        nsd_ref[t, h] = (sd_ref[t, h] * eg_all[t:t + 1, h:h + 1]
                         + _mm_tn(jnp.where(rows == t, ks[h], 0.0), v_new[h]))
    o_a = [jnp.zeros((tb, A_DV), F32) for _ in range(HEADS)]
    for t, h in items:
        o_a[h] = _row_select(rows, t, _mm(qs[h], nsd_ref[t, h]), o_a[h])
    for h in range(HEADS):
        ga = rest_ref[:, h * DK:(h + 1) * DK]
        mixed_ref[:, h * DK:(h + 1) * DK] = _rms(o_a[h], ggain_ref[...]) * _silu(ga)

    qrs, krs = [], []
    for h in range(HEADS):
        qb = rest_ref[:, base + h * DK:base + (h + 1) * DK]
        kb = rest_ref[:, 2 * base + h * DK:2 * base + (h + 1) * DK]
        qrs.append(qb * cosf + pltpu.roll(qb, DK // 2, 1) * sinf)
        krs.append((kb * cosf + pltpu.roll(kb, DK // 2, 1) * sinf) * (DK ** -0.5))
    for t, h in items:
        vb = rest_ref[:, 3 * base + h * DK:3 * base + (h + 1) * DK]
        nsr_ref[t, h] = (sr_ref[t, h] * math.exp(LOG_GAMMA[h])
                         + _mm_tn(jnp.where(rows == t, krs[h], 0.0), vb))
    o_b = [jnp.zeros((tb, DK), F32) for _ in range(HEADS)]
    for t, h in items:
        o_b[h] = _row_select(rows, t, _mm(qrs[h], nsr_ref[t, h]), o_b[h])
    for h in range(HEADS):
        gb = rest_ref[:, 4 * base + h * DK:4 * base + (h + 1) * DK]
        mixed_ref[:, base + h * DK:base + (h + 1) * DK] = _rms(o_b[h], rgain_ref[...]) * _silu(gb)


def _mix_ab_sample(qkv, rest, gates, conv_buf, s_delta, s_ret, w_conv, a_log, dt_bias, gdn_gain, ret_gain):
    nb = qkv.shape[0]
    tb = SAMPLE_TB
    assert nb % tb == 0
    cosf, sinf = _rope_tables(PAST_LEN + jnp.arange(1))
    prow, _ = _gate_params(a_log, dt_bias)
    tok = lambda b: (b, 0)
    tok4 = lambda b: (b, 0, 0, 0)
    const = lambda b: (0, 0)
    n_buf = (CONV_W - 1) * A_CONV_CH
    mixed, nbuf, nsd, nsr = pl.pallas_call(
        _step_ab_kernel,
        grid=(nb // tb,),
        in_specs=[pl.BlockSpec((tb, A_CONV_CH), tok),
                  pl.BlockSpec((tb, rest.shape[1]), tok),
                  pl.BlockSpec((tb, 128), tok),
                  pl.BlockSpec((1, DK), const),
                  pl.BlockSpec((1, DK), const),
                  pl.BlockSpec((CONV_W, A_CONV_CH), const),
                  pl.BlockSpec((8, 128), const),
                  pl.BlockSpec((1, A_DV), const),
                  pl.BlockSpec((1, DK), const),
                  pl.BlockSpec((tb, n_buf), tok),
                  pl.BlockSpec((tb, HEADS, DK, A_DV), tok4),
                  pl.BlockSpec((tb, HEADS, DK, DK), tok4)],
        out_specs=[pl.BlockSpec((tb, 2 * HEADS * DK), tok),
                   pl.BlockSpec((tb, n_buf), tok),
                   pl.BlockSpec((tb, HEADS, DK, A_DV), tok4),
                   pl.BlockSpec((tb, HEADS, DK, DK), tok4)],
        out_shape=[jax.ShapeDtypeStruct((nb, 2 * HEADS * DK), F32),
                   jax.ShapeDtypeStruct((nb, n_buf), F32),
                   jax.ShapeDtypeStruct((nb, HEADS, DK, A_DV), F32),
                   jax.ShapeDtypeStruct((nb, HEADS, DK, DK), F32)],
        compiler_params=_cparams("parallel"),
        name="mix_ab_sample",
    )(qkv, rest, gates, cosf, sinf, w_conv, prow, gdn_gain.reshape(1, A_DV), ret_gain.reshape(1, DK),
      conv_buf.reshape(nb, n_buf), s_delta, s_ret)
    return mixed, nbuf.reshape(nb, CONV_W - 1, A_CONV_CH), nsd, nsr


def _step_c_kernel(qk_ref, v_ref, opre_ref, gates_ref, brow_ref, gain_ref, cm_ref, nv_ref, m_ref,
                   h_ref, ncm_ref, nnv_ref, nm_ref):
    tb = qk_ref.shape[0]
    g = gates_ref[...] + brow_ref[0:1, :]
    logf = _log_sigmoid(g)
    rows = lax.broadcasted_iota(jnp.int32, (tb, DK), 0)
    items = [(t, h) for t in range(tb) for h in range(HEADS)]
    qs, kws, fs, ms_new, nvs_new = [], [], [], [], []
    for h in range(HEADS):
        i_pre = g[:, h:h + 1]
        q = qk_ref[:, h * DK:(h + 1) * DK]
        k = qk_ref[:, (HEADS + h) * DK:(HEADS + h + 1) * DK] * (DK ** -0.5)
        inter = logf[:, HEADS + h:HEADS + h + 1] + m_ref[:, h:h + 1]
        m_new = jnp.maximum(inter, i_pre)
        f_state = jnp.exp(inter - m_new)
        kw = k * jnp.exp(i_pre - m_new)
        nv_new = nv_ref[:, h * DK:(h + 1) * DK] * f_state + kw
        nnv_ref[:, h * DK:(h + 1) * DK] = nv_new
        nm_ref[:, h:h + 1] = m_new
        qs.append(q)
        kws.append(kw)
        fs.append(f_state)
        ms_new.append(m_new)
        nvs_new.append(nv_new)
    for t, h in items:
        v = v_ref[:, h * C_DV:(h + 1) * C_DV]
        ncm_ref[t, h] = (cm_ref[t, h] * fs[h][t:t + 1, :]
                         + _mm_tn(jnp.where(rows == t, kws[h], 0.0), v))
    rows_v = lax.broadcasted_iota(jnp.int32, (tb, C_DV), 0)
    nums = [jnp.zeros((tb, C_DV), F32) for _ in range(HEADS)]
    for t, h in items:
        nums[h] = _row_select(rows_v, t, _mm(qs[h], ncm_ref[t, h]), nums[h])
    for h in range(HEADS):
        den = jnp.sum(qs[h] * nvs_new[h], axis=1, keepdims=True)
        hh = nums[h] / jnp.maximum(jnp.abs(den), jnp.exp(-ms_new[h]))
        op = opre_ref[:, h * C_DV:(h + 1) * C_DV]
        h_ref[:, h * C_DV:(h + 1) * C_DV] = _rms(hh, gain_ref[...]) * jax.nn.sigmoid(op)


def _mix_c_sample(qk, v, opre, gates, s_c, s_n, s_m, b_gate, gain):
    nb = qk.shape[0]
    tb = SAMPLE_TB
    assert nb % tb == 0
    brow, _ = _bias_params(b_gate)
    tok = lambda b: (b, 0)
    tok4 = lambda b: (b, 0, 0, 0)
    const = lambda b: (0, 0)
    hh, ncm, nnv, nm = pl.pallas_call(
        _step_c_kernel,
        grid=(nb // tb,),
        in_specs=[pl.BlockSpec((tb, 2 * HEADS * DK), tok),
                  pl.BlockSpec((tb, HEADS * C_DV), tok),
                  pl.BlockSpec((tb, HEADS * C_DV), tok),
                  pl.BlockSpec((tb, 128), tok),
                  pl.BlockSpec((8, 128), const),
                  pl.BlockSpec((1, C_DV), const),
                  pl.BlockSpec((tb, HEADS, DK, C_DV), tok4),
                  pl.BlockSpec((tb, HEADS * DK), tok),
                  pl.BlockSpec((tb, HEADS), tok)],
        out_specs=[pl.BlockSpec((tb, HEADS * C_DV), tok),
                   pl.BlockSpec((tb, HEADS, DK, C_DV), tok4),
                   pl.BlockSpec((tb, HEADS * DK), tok),
                   pl.BlockSpec((tb, HEADS), tok)],
        out_shape=[jax.ShapeDtypeStruct((nb, HEADS * C_DV), F32),
                   jax.ShapeDtypeStruct((nb, HEADS, DK, C_DV), F32),
                   jax.ShapeDtypeStruct((nb, HEADS * DK), F32),
                   jax.ShapeDtypeStruct((nb, HEADS), F32)],
        compiler_params=_cparams("parallel"),
        name="mix_c_sample",
    )(qk, v, opre, gates, brow, gain.reshape(1, C_DV), s_c, s_n.reshape(nb, HEADS * DK), s_m)
    return hh, ncm, nnv.reshape(nb, HEADS, DK), nm


C_COLS = ((0, 2 * HEADS * DK), (2 * HEADS * DK, HEADS * C_DV), (2 * HEADS * DK + HEADS * C_DV, HEADS * C_DV),
          (2 * HEADS * DK + 2 * HEADS * C_DV, 2 * HEADS))


def kernel(x_prompt, x_sample, state_conv_a, state_delta_a, state_ret_b, state_mlstm_C, state_mlstm_n, state_mlstm_m,
           norm_mix_a, w_in_a, w_conv_a, a_log, dt_bias, gdn_gain, ret_gain, w_out_a,
           norm_mix_c, w_in_c, b_gate_c, mlstm_gain, w_out_c, norm_ffn, w_up, w_down, final_gain):
    batch, seq, _ = x_prompt.shape
    n_s = x_sample.shape[0] * x_sample.shape[1]
    tm = PROMPT_TM
    w_a = w_in_a[0].astype(BF16)
    w_c = w_in_c[0].astype(BF16)
    w_ups = w_up.astype(BF16)
    w_downs = w_down.astype(BF16)
    h = x_prompt.reshape(batch * seq, D_MODEL)
    h_s = x_sample.reshape(n_s, D_MODEL)

    (qkv, rest, gates, conv), (qkv_s, rest_s, gates_s) = _proj_a_prompt(
        h, h_s, norm_mix_a[0], w_a, w_conv_a[0], batch, seq, tm)
    mixed, sd, sr = _mix_ab_prompt(qkv, rest, gates, batch, seq, a_log[0], dt_bias[0], gdn_gain[0], ret_gain[0])
    mixed_s, conv_s, sd_s, sr_s = _mix_ab_sample(qkv_s, rest_s, gates_s, state_conv_a[0], state_delta_a[0],
                                                 state_ret_b[0], w_conv_a[0], a_log[0], dt_bias[0], gdn_gain[0],
                                                 ret_gain[0])
    h, h_s = _out_mlp(h, mixed, h_s, mixed_s, w_out_a[0].astype(BF16), norm_ffn[0], w_ups, w_downs, 0, None, tm)

    (qk, v, opre, gates_c), (qk_s, v_s, opre_s, gates_cs) = _norm_proj(h, h_s, norm_mix_c[0], w_c, C_COLS, tm)
    hm, cm, nv, m = _mix_c_prompt(qk, v, opre, gates_c, batch, seq, b_gate_c[0], mlstm_gain[0])
    hm_s, cm_s, nv_s, m_s = _mix_c_sample(qk_s, v_s, opre_s, gates_cs, state_mlstm_C[0], state_mlstm_n[0],
                                          state_mlstm_m[0], b_gate_c[0], mlstm_gain[0])
    y, y_s = _out_mlp(h, hm, h_s, hm_s, w_out_c[0].astype(BF16), norm_ffn[1], w_ups, w_downs, 1, final_gain, tm)

    return (y.reshape(x_prompt.shape), y_s.reshape(x_sample.shape),
            conv[None], sd[None], sr[None], cm[None], nv[None], m[:, :, 0][None],
            conv_s[None], sd_s[None], sr_s[None], cm_s[None], nv_s[None], m_s[None])
```

```python
import functools
import math

import jax
import jax.numpy as jnp
from jax import lax
from jax.experimental import pallas as pl
from jax.experimental.pallas import tpu as pltpu

F32 = jnp.float32
BF16 = jnp.bfloat16

D_MODEL = 1024
D_FF = 4 * D_MODEL
CHUNK = 64
EPS = 1e-6
NEG = -1e30
HEADS = 4
DK = 128
A_DV = 128
C_DV = 256
CONV_W = 4
A_CONV_CH = 3 * HEADS * DK
ROPE_BASE = 10000.0
PAST_LEN = 16384
LOG_GAMMA = tuple(math.log1p(-(2.0 ** (-5.0 - h))) for h in range(HEADS))

VMEM_LIMIT_BYTES = 56 * 1024 * 1024
PROMPT_TM = 512
AB_SUB = 8
C_SUB = 8
SAMPLE_TB = 8


def _cparams(*sem):
    return pltpu.CompilerParams(dimension_semantics=sem, vmem_limit_bytes=VMEM_LIMIT_BYTES)


def _mm(a, b):
    return jnp.dot(a.astype(BF16), b.astype(BF16), preferred_element_type=F32)


def _mm_nt(a, b):
    return lax.dot_general(a.astype(BF16), b.astype(BF16), (((1,), (1,)), ((), ())), preferred_element_type=F32)


def _mm_tn(a, b):
    return lax.dot_general(a.astype(BF16), b.astype(BF16), (((0,), (0,)), ((), ())), preferred_element_type=F32)


def _split(a):
    hi = a.astype(BF16)
    lo = (a - hi.astype(F32)).astype(BF16)
    return hi, lo


def _softplus(x):
    return jnp.maximum(x, 0.0) + jnp.log1p(jnp.exp(-jnp.abs(x)))


def _log_sigmoid(x):
    return -_softplus(-x)


def _silu(x):
    return x * jax.nn.sigmoid(x)


def _rms(x, gain):
    return x * lax.rsqrt(jnp.mean(x * x, axis=-1, keepdims=True) + EPS) * gain


def _l2(t):
    return t * lax.rsqrt(jnp.sum(t * t, axis=-1, keepdims=True) + EPS)


def _interleave(gens):
    gens = list(gens)
    while gens:
        alive = []
        for gen in gens:
            try:
                next(gen)
                alive.append(gen)
            except StopIteration:
                pass
        gens = alive


def _lockstep(gens):
    gens = list(gens)
    while gens:
        alive = []
        for gen in gens:
            try:
                next(gen)
                alive.append(gen)
            except StopIteration:
                pass
        gens = alive
        if gens:
            yield


def _head_block_rows(x):
    c, n = x.shape
    t = n // HEADS
    z = jnp.zeros((c, t), x.dtype)
    return jnp.concatenate(
        [jnp.concatenate([x[:, h * t:(h + 1) * t] if g == h else z for g in range(HEADS)], axis=1)
         for h in range(HEADS)], axis=0)


def _head_block_diag(y):
    c, n = y.shape
    per_tile = 128 // c
    assert n == HEADS * c and 128 % c == 0 and HEADS % per_tile == 0
    lane = lax.broadcasted_iota(jnp.int32, (c, 128), 1)
    z = jnp.zeros((c, 128), y.dtype)
    blocks = []
    for h in range(HEADS):
        t = h // per_tile
        lo = (h % per_tile) * c
        kept = jnp.where((lane >= lo) & (lane < lo + c), y[:, t * 128:(t + 1) * 128], z)
        blocks.append(jnp.concatenate([kept if g == t else z for g in range(n // 128)], axis=1))
    return jnp.concatenate(blocks, axis=0)


def _wide_mm(x, y):
    return jnp.dot(x.astype(BF16), _head_block_diag(y.astype(BF16)), preferred_element_type=F32)


def _tri_inv_wide(ns):
    c, wd = ns[0].shape
    ii = lax.broadcasted_iota(jnp.int32, (c, wd), 0)
    jl = lax.broadcasted_iota(jnp.int32, (c, wd), 1) & (c - 1)
    eye = (ii == jl).astype(F32)
    ts = [eye - jnp.where((ii >> 1) == (jl >> 1), n, 0.0) for n in ns]
    for lvl in range(1, int(math.log2(c))):
        off = ((ii >> (lvl + 1)) == (jl >> (lvl + 1))) & ((ii >> lvl) != (jl >> lvl))
        ys = [_wide_mm(jnp.where(off, n, 0.0), t) for n, t in zip(ns, ts)]
        yield
        ts = [t - _wide_mm(t, y) for t, y in zip(ts, ys)]
        yield
    return [t - eye for t in ts]


def _out_width(n):
    return max(n, 128)


def _project_rows(x_ref, g_ref, pieces, out_refs):
    xn = _rms(x_ref[...], g_ref[...]).astype(BF16)
    for (w_ref, col0, n), o_ref in zip(pieces, out_refs):
        res = jnp.dot(xn, w_ref[:, col0:col0 + n], preferred_element_type=F32)
        if n < _out_width(n):
            res = jnp.concatenate([res, jnp.zeros((res.shape[0], _out_width(n) - n), F32)], axis=1)
        o_ref[...] = res


def _norm_proj_kernel(x_ref, xs_ref, g_ref, w_ref, *out_refs, cols):
    pieces = [(w_ref, c0, n) for c0, n in cols]
    _project_rows(x_ref, g_ref, pieces, out_refs[:len(cols)])

    @pl.when(pl.program_id(0) == 0)
    def _():
        _project_rows(xs_ref, g_ref, pieces, out_refs[len(cols):])


def _norm_proj(x, x_s, gain, w, cols, tm):
    t, ts = x.shape[0], x_s.shape[0]
    assert t % tm == 0
    widths = [_out_width(n) for _, n in cols]
    row = lambda i: (i, 0)
    const = lambda i: (0, 0)
    outs = pl.pallas_call(
        functools.partial(_norm_proj_kernel, cols=cols),
        grid=(t // tm,),
        in_specs=[pl.BlockSpec((tm, D_MODEL), row), pl.BlockSpec((ts, D_MODEL), const),
                  pl.BlockSpec((1, D_MODEL), const),
                  pl.BlockSpec(w.shape, const, pipeline_mode=pl.Buffered(1))],
        out_specs=[pl.BlockSpec((tm, n), row) for n in widths] + [pl.BlockSpec((ts, n), const) for n in widths],
        out_shape=[jax.ShapeDtypeStruct((t, n), F32) for n in widths]
                  + [jax.ShapeDtypeStruct((ts, n), F32) for n in widths],
        compiler_params=_cparams("arbitrary"),
        name="norm_proj",
    )(x, x_s, gain.reshape(1, D_MODEL), w)
    return outs[:len(widths)], outs[len(widths):]


def _layer0_prompt_kernel(x_ref, xs_ref, g_ref, w_ref, wconv_ref, cos_ref, sin_ref,
                          prow_ref, pcol_ref, tri_ref, spread_ref, ggain_ref, rgain_ref,
                          mixed_ref, conv_ref, sd_ref, sr_ref, qkvs_ref, rests_ref, gatess_ref,
                          xp_ref, raw_ref, wrest_ref, qkv_ref, rest_ref, gates_ref, *, tiles_per_seq, n_sub):
    tm = x_ref.shape[0]
    hd = HEADS * DK
    n_gate = 2 * HEADS
    gate_piece = (w_ref, A_CONV_CH, n_gate)

    @pl.when(pl.program_id(0) == 0)
    def _():
        wrest_ref[...] = w_ref[:, A_CONV_CH + n_gate:A_CONV_CH + n_gate + 5 * hd]
        _project_rows(xs_ref, g_ref, [(w_ref, 0, A_CONV_CH), (wrest_ref, 0, 5 * hd), gate_piece],
                      (qkvs_ref, rests_ref, gatess_ref))

    xn = _rms(x_ref[...], g_ref[...]).astype(BF16)

    @pl.when(pl.program_id(0) % tiles_per_seq == 0)
    def _():
        xp_ref[0:8, :] = jnp.zeros((8, A_CONV_CH), F32)
        sd_ref[...] = jnp.zeros_like(sd_ref)
        sr_ref[...] = jnp.zeros_like(sr_ref)

    w = wconv_ref[...]
    base = A_CONV_CH

    def conv_part(part, slot, r0, r1):
        c0, c1 = part * hd, (part + 1) * hd
        raw = raw_ref[slot, r0:r1, :]
        xp_ref[8 + r0:8 + r1, c0:c1] = raw
        conv = (xp_ref[5 + r0:5 + r1, c0:c1] * w[0:1, c0:c1] + xp_ref[6 + r0:6 + r1, c0:c1] * w[1:2, c0:c1]
                + xp_ref[7 + r0:7 + r1, c0:c1] * w[2:3, c0:c1] + raw * w[3:4, c0:c1])
        if r1 == tm:
            rows = r1 - r0
            xp_ref[0:8, c0:c1] = raw[rows - 8:rows, :]
            conv_ref[0, :, c0:c1] = raw[rows - 3:rows, :]
        act = _silu(conv)
        if part == 2:
            qkv_ref[r0:r1, c0:c1] = act
        else:
            scale = DK ** -0.5 if part == 0 else 1.0
            for h in range(HEADS):
                qkv_ref[r0:r1, c0 + h * DK:c0 + (h + 1) * DK] = _l2(act[:, h * DK:(h + 1) * DK]) * scale

    def rest_part(part, slot, r0, r1):
        c0, c1 = part * hd, (part + 1) * hd
        raw = raw_ref[slot, r0:r1, :]
        if part in (0, 4):
            rest_ref[r0:r1, c0:c1] = _silu(raw)
        elif part == 3:
            rest_ref[r0:r1, c0:c1] = raw
        else:
            scale = 1.0 if part == 1 else DK ** -0.5
            for h in range(HEADS):
                t = raw[:, h * DK:(h + 1) * DK]
                rest_ref[r0:r1, c0 + h * DK:c0 + (h + 1) * DK] = (
                    (t * cos_ref[r0:r1, :] + pltpu.roll(t, DK // 2, 1) * sin_ref[r0:r1, :]) * scale)

    def gates_part(_, slot, r0, r1):
        gates_ref[r0:r1, :] = jnp.concatenate(
            [raw_ref[slot, r0:r1, 0:n_gate], jnp.zeros((r1 - r0, 128 - n_gate), F32)], axis=1)

    groups = [(w_ref, 0, hd, functools.partial(conv_part, 0)),
              (wrest_ref, 3 * hd, hd, functools.partial(rest_part, 3)),
              (w_ref, hd, hd, functools.partial(conv_part, 1)),
              (wrest_ref, 0, hd, functools.partial(rest_part, 0)),
              (w_ref, 2 * hd, hd, functools.partial(conv_part, 2)),
              (wrest_ref, hd, hd, functools.partial(rest_part, 1)),
              gate_piece + (functools.partial(gates_part, 0),),
              (wrest_ref, 2 * hd, hd, functools.partial(rest_part, 2)),
              (wrest_ref, 4 * hd, hd, functools.partial(rest_part, 4))]

    def project(n, slot):
        wg_ref, col0, width, _ = groups[n]
        for j in range(0, width, 256):
            wj = min(256, width - j)
            raw_ref[slot, :, j:j + wj] = jnp.dot(xn, wg_ref[:, col0 + j:col0 + j + wj],
                                                 preferred_element_type=F32)
            yield

    def epilogue(n, slot):
        for r0 in range(0, tm, 128):
            groups[n][3](slot, r0, min(r0 + 128, tm))
            yield

    def projection_track(order):
        yield from project(order[0], 0)
        for idx, n in enumerate(order):
            nxt = [project(order[idx + 1], (idx + 1) % 2)] if idx + 1 < len(order) else []
            yield from _lockstep([epilogue(n, idx % 2)] + nxt)
            yield

    _interleave([projection_track([6, 0, 2, 4])])
    gdn_stages, delta_track, ret_track, finish = _mix_ab_tracks(
        qkv_ref, rest_ref, gates_ref, prow_ref, pcol_ref, tri_ref, spread_ref, ggain_ref, rgain_ref,
        mixed_ref, sd_ref, sr_ref, n_sub)
    _interleave(gdn_stages)

    def filler_track():
        yield from projection_track([1, 3, 5, 7, 8])
        yield from ret_track()

    _interleave([delta_track(), filler_track()])
    finish()


def _layer0_prompt(x, x_s, gain, w, w_conv, a_log, dt_bias, gdn_gain, ret_gain, batch, seq):
    t, ts = x.shape[0], x_s.shape[0]
    tm = AB_SUB * CHUNK
    assert t % tm == 0 and seq % tm == 0
    tiles_per_seq = seq // tm
    cosf, sinf = _rope_tables(jnp.arange(seq))
    prow, pcol = _gate_params(a_log, dt_bias)
    tri = _chunk_tri(tm)
    spread = _head_spread_ab()
    row = lambda i: (i, 0)
    const = lambda i: (0, 0)
    pos = lambda i: (i % tiles_per_seq, 0)
    per_seq3 = lambda i: (i // tiles_per_seq, 0, 0)
    per_seq4 = lambda i: (i // tiles_per_seq, 0, 0, 0)
    n_rest = 5 * HEADS * DK
    outs = pl.pallas_call(
        functools.partial(_layer0_prompt_kernel, tiles_per_seq=tiles_per_seq, n_sub=AB_SUB),
        grid=(t // tm,),
        in_specs=[pl.BlockSpec((tm, D_MODEL), row),
                  pl.BlockSpec((ts, D_MODEL), const),
                  pl.BlockSpec((1, D_MODEL), const),
                  pl.BlockSpec(w.shape, const, pipeline_mode=pl.Buffered(1)),
                  pl.BlockSpec((CONV_W, A_CONV_CH), const),
                  pl.BlockSpec((tm, DK), pos),
                  pl.BlockSpec((tm, DK), pos),
                  pl.BlockSpec((8, 128), const),
                  pl.BlockSpec((8, 128), const),
                  pl.BlockSpec(tri.shape, const),
                  pl.BlockSpec(spread.shape, const),
                  pl.BlockSpec((1, A_DV), const),
                  pl.BlockSpec((1, DK), const)],
        out_specs=[pl.BlockSpec((tm, 2 * HEADS * DK), row),
                   pl.BlockSpec((1, CONV_W - 1, A_CONV_CH), per_seq3),
                   pl.BlockSpec((1, HEADS, DK, A_DV), per_seq4),
                   pl.BlockSpec((1, HEADS, DK, DK), per_seq4),
                   pl.BlockSpec((ts, A_CONV_CH), const),
                   pl.BlockSpec((ts, n_rest), const),
                   pl.BlockSpec((ts, 128), const)],
        out_shape=[jax.ShapeDtypeStruct((t, 2 * HEADS * DK), F32),
                   jax.ShapeDtypeStruct((batch, CONV_W - 1, A_CONV_CH), F32),
                   jax.ShapeDtypeStruct((batch, HEADS, DK, A_DV), F32),
                   jax.ShapeDtypeStruct((batch, HEADS, DK, DK), F32),
                   jax.ShapeDtypeStruct((ts, A_CONV_CH), F32),
                   jax.ShapeDtypeStruct((ts, n_rest), F32),
                   jax.ShapeDtypeStruct((ts, 128), F32)],
        scratch_shapes=[pltpu.VMEM((tm + 8, A_CONV_CH), F32), pltpu.VMEM((2, tm, HEADS * DK), F32),
                        pltpu.VMEM((D_MODEL, n_rest), BF16),
                        pltpu.VMEM((tm, A_CONV_CH), F32), pltpu.VMEM((tm, n_rest), F32), pltpu.VMEM((tm, 128), F32)],
        compiler_params=_cparams("arbitrary"),
        name="layer0_prompt",
    )(x, x_s, gain.reshape(1, D_MODEL), w, w_conv, cosf, sinf, prow, pcol, tri, spread,
      gdn_gain.reshape(1, A_DV), ret_gain.reshape(1, DK))
    return outs[:4], outs[4:]


def _out_mlp_kernel(h_ref, m_ref, hs_ref, ms_ref, wout_ref, gffn_ref, wup_ref, wdown_ref, *rest, final):
    if final:
        gfin_ref, o_ref, os_ref = rest
    else:
        o_ref, os_ref = rest

    def block(h_r, m_r, o_r):
        h = h_r[...] + jnp.dot(m_r[...].astype(BF16), wout_ref[...], preferred_element_type=F32)
        xn = _rms(h, gffn_ref[...]).astype(BF16)
        acc = h
        step = 1024
        for j in range(D_FF // step):
            hid = jnp.dot(xn, wup_ref[:, j * step:(j + 1) * step], preferred_element_type=F32)
            hid = jnp.maximum(hid, 0.0)
            acc = acc + jnp.dot((hid * hid).astype(BF16), wdown_ref[j * step:(j + 1) * step, :],
                                preferred_element_type=F32)
        if final:
            acc = _rms(acc, gfin_ref[...])
        o_r[...] = acc

    block(h_ref, m_ref, o_ref)

    @pl.when(pl.program_id(0) == 0)
    def _():
        block(hs_ref, ms_ref, os_ref)


def _out_mlp(h, mix, h_s, mix_s, w_out, g_ffn, w_up, w_down, layer, g_final, tm):
    t, ts = h.shape[0], h_s.shape[0]
    assert t % tm == 0
    final = g_final is not None
    row = lambda i: (i, 0)
    const = lambda i: (0, 0)
    this_layer = lambda i: (layer, 0, 0)
    once = pl.Buffered(1)
    in_specs = [pl.BlockSpec((tm, D_MODEL), row), pl.BlockSpec((tm, mix.shape[1]), row),
                pl.BlockSpec((ts, D_MODEL), const), pl.BlockSpec((ts, mix_s.shape[1]), const),
                pl.BlockSpec(w_out.shape, const, pipeline_mode=once), pl.BlockSpec((1, D_MODEL), const),
                pl.BlockSpec((None,) + w_up.shape[1:], this_layer, pipeline_mode=once),
                pl.BlockSpec((None,) + w_down.shape[1:], this_layer, pipeline_mode=once)]
    args = [h, mix, h_s, mix_s, w_out, g_ffn.reshape(1, D_MODEL), w_up, w_down]
    if final:
        in_specs.append(pl.BlockSpec((1, D_MODEL), const))
        args.append(g_final.reshape(1, D_MODEL))
    return pl.pallas_call(
        functools.partial(_out_mlp_kernel, final=final),
        grid=(t // tm,),
        in_specs=in_specs,
        out_specs=[pl.BlockSpec((tm, D_MODEL), row), pl.BlockSpec((ts, D_MODEL), const)],
        out_shape=[jax.ShapeDtypeStruct((t, D_MODEL), F32), jax.ShapeDtypeStruct((ts, D_MODEL), F32)],
        compiler_params=_cparams("arbitrary"),
        name="out_mlp",
    )(*args)


def _chunk_masks(c):
    ii = lax.broadcasted_iota(jnp.int32, (c, c), 0)
    jj = lax.broadcasted_iota(jnp.int32, (c, c), 1)
    return ii, jj


def _mix_ab_tracks(qkv_ref, rest_ref, gates_ref, prow_ref, pcol_ref, tri_ref, spread_ref,
                   ggain_ref, rgain_ref, mixed_ref, sd_ref, sr_ref, n_sub):
    c = CHUNK
    lb = n_sub * c

    g = gates_ref[...]
    g_t = g.T
    neg_a_row = -jnp.exp(prow_ref[0:1, :])
    dt_row = prow_ref[1:2, :]
    neg_a_col = -jnp.exp(pcol_ref[0:HEADS, 0:1])
    dt_col = pcol_ref[0:HEADS, 1:2]
    la_cols = neg_a_row * _softplus(g + dt_row)
    beta_cols = jax.nn.sigmoid(g)
    la_rows = neg_a_col * _softplus(g_t[0:HEADS, :] + dt_col)

    tri = tri_ref[...]
    la_hi, la_lo = _split(la_cols)
    g_cols = (jnp.dot(tri, la_hi, preferred_element_type=F32)
              + jnp.dot(tri, la_lo, preferred_element_type=F32))
    lr_hi, lr_lo = _split(jnp.concatenate([la_rows, jnp.zeros_like(la_rows)], axis=0))
    nt = lambda x, y: lax.dot_general(x, y, (((1,), (1,)), ((), ())), preferred_element_type=F32)
    g_rows = nt(lr_hi, tri) + nt(lr_lo, tri)
    lane = lax.broadcasted_iota(jnp.int32, (lb, 128), 1)
    x_hi, x_lo = _split(jnp.where(lane < HEADS, g_cols, beta_cols))
    spread = (jnp.dot(x_hi, spread_ref[...], preferred_element_type=F32)
              + jnp.dot(x_lo, spread_ref[...], preferred_element_type=F32))
    hd = HEADS * DK
    wd = HEADS * c
    g_wide = spread[:, 0:hd]
    beta_wide = spread[:, hd:2 * hd]
    g_half = spread[:, 2 * hd:2 * hd + wd]
    eg_wide = jnp.exp(g_wide)

    ii = lax.broadcasted_iota(jnp.int32, (c, wd), 0)
    jl = lax.broadcasted_iota(jnp.int32, (c, wd), 1) & (c - 1)
    causal = ii >= jl
    strict = ii > jl
    lane_hd = lax.broadcasted_iota(jnp.int32, (1, hd), 1)
    lane_wd = lax.broadcasted_iota(jnp.int32, (1, wd), 1)
    lg_hd = jnp.full((1, hd), LOG_GAMMA[HEADS - 1], F32)
    lg_wd = jnp.full((1, wd), LOG_GAMMA[HEADS - 1], F32)
    for h in range(HEADS - 2, -1, -1):
        lg_hd = jnp.where(lane_hd < (h + 1) * DK, LOG_GAMMA[h], lg_hd)
        lg_wd = jnp.where(lane_wd < (h + 1) * c, LOG_GAMMA[h], lg_wd)
    pos_col = lax.broadcasted_iota(jnp.int32, (c, hd), 0).astype(F32)
    ret_dec = jnp.exp(jnp.where(causal, (ii - jl).astype(F32) * lg_wd, NEG))
    ret_q_scale = jnp.exp((pos_col + 1.0) * lg_hd)
    ret_k_scale = jnp.exp((float(c - 1) - pos_col) * lg_hd)

    ggain = ggain_ref[...]
    rgain = rgain_ref[...]
    base = hd
    tile = lambda x, h: x[:, h * DK:(h + 1) * DK]
    s_delta = [sd_ref[0, h] for h in range(HEADS)]
    s_ret = [sr_ref[0, h] for h in range(HEADS)]

    chunks = list(range(n_sub))
    items = [(ci, h) for ci in chunks for h in range(HEADS)]
    rng = lambda ci: (ci * c, (ci + 1) * c)
    pre = {it: {} for it in items}
    pre_b = {it: {} for it in items}
    wide = {ci: {} for ci in chunks}
    nt = lambda x, y: lax.dot_general(x, y, (((1,), (1,)), ((), ())), preferred_element_type=F32)

    def gdn_stage(ci):
        w_ = wide[ci]
        r0, r1 = rng(ci)
        q = qkv_ref[r0:r1, 0:hd]
        k = qkv_ref[r0:r1, hd:2 * hd]
        beta = beta_wide[r0:r1]
        g_col = g_wide[r0:r1]
        kb = k * beta
        g_row = jnp.concatenate([g_rows[h:h + 1, r0:r1] for h in range(HEADS)], axis=1)
        dec_causal = jnp.exp(jnp.where(causal, g_half[r0:r1] - g_row, NEG))
        prod = nt(jnp.concatenate([q, kb], axis=0).astype(BF16), _head_block_rows(k.astype(BF16)))
        yield
        w_['n'] = prod[c:2 * c] * jnp.where(strict, dec_causal, 0.0)
        w_['qk'] = prod[0:c] * dec_causal
        yield
        v = qkv_ref[r0:r1, 2 * hd:3 * hd]
        eg = eg_wide[r0:r1]
        g_last = g_col[c - 1:c, :]
        rhs_u, rhs_w, qd = v * beta, kb * eg, q * eg
        kd = k * jnp.exp(g_last - g_col)
        gl = jnp.exp(g_last)
        for h in range(HEADS):
            pre[ci, h].update(rhs=jnp.concatenate([tile(rhs_u, h), tile(rhs_w, h)], axis=1), qd=tile(qd, h),
                              kd=tile(kd, h), gl=tile(gl, h))

    def ret_stage(ci):
        r0, r1 = rng(ci)
        qr = rest_ref[r0:r1, base:2 * base]
        kr = rest_ref[r0:r1, 2 * base:3 * base]
        vb = rest_ref[r0:r1, 3 * base:4 * base]
        qk = nt(qr.astype(BF16), _head_block_rows(kr.astype(BF16))) * ret_dec
        qd = qr * ret_q_scale
        kd = kr * ret_k_scale
        yield
        for a in range(HEADS // 2):
            h0, h1 = 2 * a, 2 * a + 1
            z = jnp.zeros((c, DK), F32)
            rhs = jnp.concatenate([jnp.concatenate([tile(vb, h0), z], axis=1),
                                   jnp.concatenate([z, tile(vb, h1)], axis=1)], axis=0)
            intra = _mm(qk[:, a * 2 * c:(a + 1) * 2 * c], rhs)
            pre_b[ci, h0]['intra'] = intra[:, 0:DK]
            pre_b[ci, h1]['intra'] = intra[:, DK:2 * DK]
        yield
        for h in range(HEADS):
            pre_b[ci, h]['qd'] = tile(qd, h)
            pre_b[ci, h]['kv'] = _mm_tn(tile(kd, h), tile(vb, h))

    def solve_stage():
        t_offs = yield from _tri_inv_wide([wide[ci]['n'] for ci in chunks])
        for ci, t_off in zip(chunks, t_offs):
            wide[ci]['t_off'] = t_off

    def pair_rhs(x0, x1):
        z = jnp.zeros_like(x0)
        return jnp.concatenate([jnp.concatenate([x0, z], axis=1), jnp.concatenate([z, x1], axis=1)], axis=0)

    def sol_stage(ci):
        w_ = wide[ci]
        for a in range(HEADS // 2):
            h0, h1 = 2 * a, 2 * a + 1
            r = _mm(w_['t_off'][:, a * 2 * c:(a + 1) * 2 * c], pair_rhs(pre[ci, h0]['rhs'], pre[ci, h1]['rhs']))
            pre[ci, h0]['sol'] = pre[ci, h0]['rhs'] + r[:, 0:2 * A_DV]
            pre[ci, h1]['sol'] = pre[ci, h1]['rhs'] + r[:, 2 * A_DV:4 * A_DV]
        yield
        for a in range(HEADS // 2):
            h0, h1 = 2 * a, 2 * a + 1
            r = _mm(w_['qk'][:, a * 2 * c:(a + 1) * 2 * c], pair_rhs(pre[ci, h0]['sol'], pre[ci, h1]['sol']))
            pre[ci, h0]['qs'] = r[:, 0:2 * A_DV]
            pre[ci, h1]['qs'] = r[:, 2 * A_DV:4 * A_DV]
        for h in range(HEADS):
            d = pre[ci, h]
            d['kts'] = _mm_tn(d['kd'], d['sol'])
        yield
        for h in range(HEADS):
            d = pre[ci, h]
            d['lhs'] = jnp.concatenate([d['kts'][:, A_DV:2 * A_DV], d['qd'] - d['qs'][:, A_DV:2 * A_DV]], axis=0)


    def ret_out_stage(ci, h):
        d = pre_b[ci, h]
        r0, r1 = rng(ci)
        o = _mm(d['qd'], d['s_in']) + d['intra']
        yield
        ms_ = jnp.mean(o * o, axis=-1, keepdims=True)
        yield
        gb = rest_ref[r0:r1, 4 * base + h * DK:4 * base + (h + 1) * DK]
        mixed_ref[r0:r1, base + h * DK:base + (h + 1) * DK] = o * lax.rsqrt(ms_ + EPS) * rgain * gb

    def delta_step(ci, h):
        d = pre[ci, h]
        r = _mm(d['lhs'], s_delta[h])
        yield
        d['o'] = r[DK:DK + c] + d['qs'][:, 0:A_DV]
        s_delta[h] = s_delta[h] * d['gl'] - r[0:DK] + d['kts'][:, 0:A_DV]

    def gdn_out_stage(ci, h):
        o = pre[ci, h]['o']
        r0, r1 = rng(ci)
        ms_ = jnp.mean(o * o, axis=-1, keepdims=True)
        yield
        ga = rest_ref[r0:r1, h * DK:(h + 1) * DK]
        mixed_ref[r0:r1, h * A_DV:(h + 1) * A_DV] = o * lax.rsqrt(ms_ + EPS) * ggain * ga

    def delta_track():
        yield from solve_stage()
        yield from _lockstep([sol_stage(ci) for ci in chunks])
        for ci in chunks:
            yield from _lockstep([delta_step(ci, h) for h in range(HEADS)])
            if ci > 0:
                yield from _lockstep([gdn_out_stage(ci - 1, h) for h in range(HEADS)])
        yield from _lockstep([gdn_out_stage(chunks[-1], h) for h in range(HEADS)])

    def ret_track():
        for ci in chunks:
            yield from ret_stage(ci)
            for h in range(HEADS):
                pre_b[ci, h]['s_in'] = s_ret[h]
                s_ret[h] = s_ret[h] * math.exp(c * LOG_GAMMA[h]) + pre_b[ci, h]['kv']
            yield
            yield from _lockstep([ret_out_stage(ci, h) for h in range(HEADS)])

    def finish():
        for h in range(HEADS):
            sd_ref[0, h] = s_delta[h]
            sr_ref[0, h] = s_ret[h]

    return [gdn_stage(ci) for ci in chunks], delta_track, ret_track, finish


def _rope_tables(pos):
    half = DK // 2
    inv = ROPE_BASE ** (-jnp.arange(half, dtype=F32) / half)
    ang = pos.astype(F32)[:, None] * inv[None, :]
    cos, sin = jnp.cos(ang), jnp.sin(ang)
    return jnp.concatenate([cos, cos], axis=-1), jnp.concatenate([-sin, sin], axis=-1)


def _chunk_tri(lb):
    i = jnp.arange(lb)[:, None]
    j = jnp.arange(lb)[None, :]
    return ((i >= j) & (i // CHUNK == j // CHUNK)).astype(BF16)


def _head_spread(groups):
    n = groups * HEADS
    src = jnp.arange(128)[:, None]
    dst = jnp.arange(n * 128)[None, :] // 128
    return (src == dst).astype(BF16)


def _head_spread_ab():
    src = jnp.arange(128)[:, None]
    full = jnp.arange(2 * HEADS * DK)[None, :] // DK
    half = jnp.arange(HEADS * CHUNK)[None, :] // CHUNK
    return jnp.concatenate([src == full, src == half], axis=1).astype(BF16)


def _gate_params(a_log, dt_bias):
    prow = jnp.zeros((8, 128), F32).at[0, 0:HEADS].set(a_log).at[1, 0:HEADS].set(dt_bias)
    pcol = jnp.zeros((8, 128), F32).at[0:HEADS, 0].set(a_log).at[0:HEADS, 1].set(dt_bias)
    return prow, pcol


def _mix_c_kernel(qk_ref, v_ref, opre_ref, gates_ref, brow_ref, bcol_ref, tri_ref, spread_ref, gain_ref,
                  h_ref, cm_ref, nv_ref, m_ref, *, n_sub):
    c = CHUNK
    lb = n_sub * c
    step = pl.program_id(1)

    @pl.when(step == 0)
    def _():
        cm_ref[...] = jnp.zeros_like(cm_ref)
        nv_ref[...] = jnp.zeros_like(nv_ref)
        m_ref[...] = jnp.zeros_like(m_ref)

    items = [(ci, h) for ci in range(n_sub) for h in range(HEADS)]
    rng = lambda ci: (ci * c, (ci + 1) * c)
    pre = {it: {} for it in items}

    for ci, h in items:
        r0, r1 = rng(ci)
        q = qk_ref[r0:r1, h * DK:(h + 1) * DK]
        k = qk_ref[r0:r1, (HEADS + h) * DK:(HEADS + h + 1) * DK] * (DK ** -0.5)
        pre[ci, h].update(q=q, k=k, qk=_mm_nt(q, k))

    g = gates_ref[...] + brow_ref[0:1, :]
    g_t = gates_ref[...].T[0:2 * HEADS, :] + bcol_ref[0:2 * HEADS, 0:1]
    i_rows = g_t[0:HEADS, :]
    logf_rows = _log_sigmoid(g_t[HEADS:2 * HEADS, :])
    tri = tri_ref[...]
    lf_hi, lf_lo = _split(_log_sigmoid(g))
    b_cols = (jnp.dot(tri, lf_hi, preferred_element_type=F32)
              + jnp.dot(tri, lf_lo, preferred_element_type=F32))
    lr_hi, lr_lo = _split(jnp.concatenate([logf_rows, jnp.zeros_like(logf_rows)], axis=0))
    nt = lambda x, y: lax.dot_general(x, y, (((1,), (1,)), ((), ())), preferred_element_type=F32)
    b_rows = nt(lr_hi, tri) + nt(lr_lo, tri)
    lane = lax.broadcasted_iota(jnp.int32, (lb, 128), 1)
    x_hi, x_lo = _split(jnp.where(lane < HEADS, g, b_cols))
    spread = (jnp.dot(x_hi, spread_ref[...], preferred_element_type=F32)
              + jnp.dot(x_lo, spread_ref[...], preferred_element_type=F32))
    i_wide = spread[:, 0:HEADS * 128]
    b_wide = spread[:, HEADS * 128:2 * HEADS * 128]

    ii, jj = _chunk_masks(c)
    causal = ii >= jj
    gain = gain_ref[...]
    wide2 = lambda x: jnp.concatenate([x, x], axis=-1)

    def gates_stage(ci, h):
        d = pre[ci, h]
        r0, r1 = rng(ci)
        b_col = b_wide[r0:r1, h * 128:(h + 1) * 128]
        i_col = i_wide[r0:r1, h * 128:(h + 1) * 128]
        b_last = b_col[c - 1:c, :]
        d_log = jnp.where(causal, b_col[:, 0:c] - b_rows[h:h + 1, r0:r1] + i_rows[h:h + 1, r0:r1], NEG)
        d.update(b_col=b_col, b_last=b_last, d_log=d_log, k_log=b_last - b_col + i_col)
        yield
        d['d_max'] = jnp.max(d_log, axis=1, keepdims=True)

    _interleave(gates_stage(ci, h) for ci, h in items)

    ms = [m_ref[0, h:h + 1, :] for h in range(HEADS)]
    for ci, h in items:
        d = pre[ci, h]
        inter = d['b_col'] + ms[h]
        m_row = jnp.maximum(inter, d['d_max'])
        m_new = m_row[c - 1:c, :]
        d.update(inter=inter, m_row=m_row, m_old=ms[h], m_new=m_new)
        ms[h] = m_new

    def weights_stage(ci, h):
        d = pre[ci, h]
        r0, r1 = rng(ci)
        d['w_inter'] = jnp.exp(d['inter'] - d['m_row'])
        d['f_state'] = jnp.exp(d['b_last'] + d['m_old'] - d['m_new'])
        yield
        d['w_intra'] = jnp.exp(d['d_log'] - d['m_row'][:, 0:c]) * d['qk']
        d['kw'] = d['k'] * jnp.exp(d['k_log'] - d['m_new'])
        yield
        v = v_ref[r0:r1, h * C_DV:(h + 1) * C_DV]
        d['intra'] = _mm(d['w_intra'], v)
        d['kv'] = _mm_tn(d['kw'], v)
        yield
        d['sum_intra'] = jnp.sum(d['w_intra'], axis=1, keepdims=True)
        d['sum_kw'] = jnp.sum(d['kw'], axis=0, keepdims=True)
        d['inv_floor'] = jnp.exp(-d['m_row'])

    _interleave(weights_stage(ci, h) for ci, h in items)

    cms = [cm_ref[0, h] for h in range(HEADS)]
    nvs = [nv_ref[0, h:h + 1, :] for h in range(HEADS)]
    for ci, h in items:
        d = pre[ci, h]
        d['cm_in'], d['nv_in'] = cms[h], nvs[h]
        cms[h] = cms[h] * wide2(d['f_state']) + d['kv']
        nvs[h] = nvs[h] * d['f_state'] + d['sum_kw']

    def output_stage(ci, h):
        d = pre[ci, h]
        r0, r1 = rng(ci)
        qc = _mm(d['q'], d['cm_in'])
        qn = jnp.sum(d['q'] * d['nv_in'], axis=1, keepdims=True)
        yield
        num = wide2(d['w_inter']) * qc + d['intra']
        den = d['w_inter'] * qn + d['sum_intra']
        hh = num / wide2(jnp.maximum(jnp.abs(den), d['inv_floor']))
        yield
        ms_ = jnp.mean(hh * hh, axis=-1, keepdims=True)
        yield
        op = opre_ref[r0:r1, h * C_DV:(h + 1) * C_DV]
        h_ref[r0:r1, h * C_DV:(h + 1) * C_DV] = hh * lax.rsqrt(ms_ + EPS) * gain * jax.nn.sigmoid(op)

    _interleave(output_stage(ci, h) for ci, h in items)

    for h in range(HEADS):
        cm_ref[0, h] = cms[h]
        nv_ref[0, h:h + 1, :] = nvs[h]
        m_ref[0, h:h + 1, :] = ms[h]


def _bias_params(b_gate):
    brow = jnp.zeros((8, 128), F32).at[0, 0:2 * HEADS].set(b_gate)
    bcol = jnp.zeros((8, 128), F32).at[0:2 * HEADS, 0].set(b_gate)
    return brow, bcol


def _mix_c_prompt(qk, v, opre, gates, batch, seq, b_gate, gain):
    lb = C_SUB * CHUNK
    assert seq % lb == 0
    nc = seq // lb
    brow, bcol = _bias_params(b_gate)
    tri = _chunk_tri(lb)
    spread = _head_spread(2)
    tok = lambda b, c: (b * nc + c, 0)
    const = lambda b, c: (0, 0)
    return pl.pallas_call(
        functools.partial(_mix_c_kernel, n_sub=C_SUB),
        grid=(batch, nc),
        in_specs=[pl.BlockSpec((lb, 2 * HEADS * DK), tok),
                  pl.BlockSpec((lb, HEADS * C_DV), tok),
                  pl.BlockSpec((lb, HEADS * C_DV), tok),
                  pl.BlockSpec((lb, 128), tok),
                  pl.BlockSpec((8, 128), const),
                  pl.BlockSpec((8, 128), const),
                  pl.BlockSpec(tri.shape, const),
                  pl.BlockSpec(spread.shape, const),
                  pl.BlockSpec((1, C_DV), const)],
        out_specs=[pl.BlockSpec((lb, HEADS * C_DV), tok),
                   pl.BlockSpec((1, HEADS, DK, C_DV), lambda b, c: (b, 0, 0, 0)),
                   pl.BlockSpec((1, HEADS, DK), lambda b, c: (b, 0, 0)),
                   pl.BlockSpec((1, HEADS, 128), lambda b, c: (b, 0, 0))],
        out_shape=[jax.ShapeDtypeStruct((batch * seq, HEADS * C_DV), F32),
                   jax.ShapeDtypeStruct((batch, HEADS, DK, C_DV), F32),
                   jax.ShapeDtypeStruct((batch, HEADS, DK), F32),
                   jax.ShapeDtypeStruct((batch, HEADS, 128), F32)],
        compiler_params=_cparams("parallel", "arbitrary"),
        name="mix_c_prompt",
    )(qk, v, opre, gates, brow, bcol, tri, spread, gain.reshape(1, C_DV))


def _row_select(rows, t, new, old):
    return jnp.where(rows == t, new, old)


def _step_ab_kernel(qkv_ref, rest_ref, gates_ref, cos_ref, sin_ref, wconv_ref, prow_ref, ggain_ref, rgain_ref,
                    buf_ref, sd_ref, sr_ref, mixed_ref, nbuf_ref, nsd_ref, nsr_ref):
    tb = qkv_ref.shape[0]
    u = qkv_ref[...]
    w = wconv_ref[...]
    b0 = buf_ref[:, 0:A_CONV_CH]
    b1 = buf_ref[:, A_CONV_CH:2 * A_CONV_CH]
    b2 = buf_ref[:, 2 * A_CONV_CH:3 * A_CONV_CH]
    conv = b0 * w[0:1] + b1 * w[1:2] + b2 * w[2:3] + u * w[3:4]
    nbuf_ref[:, 0:A_CONV_CH] = b1
    nbuf_ref[:, A_CONV_CH:2 * A_CONV_CH] = b2
    nbuf_ref[:, 2 * A_CONV_CH:3 * A_CONV_CH] = u
    act = _silu(conv)
    g = gates_ref[...]
    eg_all = jnp.exp(-jnp.exp(prow_ref[0:1, :]) * _softplus(g + prow_ref[1:2, :]))
    beta_all = jax.nn.sigmoid(g)
    cosf = cos_ref[...]
    sinf = sin_ref[...]
    base = HEADS * DK
    rows = lax.broadcasted_iota(jnp.int32, (tb, DK), 0)
    items = [(t, h) for t in range(tb) for h in range(HEADS)]

    qs = [_l2(act[:, h * DK:(h + 1) * DK]) * (DK ** -0.5) for h in range(HEADS)]
    ks = [_l2(act[:, (HEADS + h) * DK:(HEADS + h + 1) * DK]) for h in range(HEADS)]
    k_s = [jnp.zeros((tb, A_DV), F32) for _ in range(HEADS)]
    for t, h in items:
        k_s[h] = _row_select(rows, t, _mm(ks[h], sd_ref[t, h]), k_s[h])
    v_new = []
    for h in range(HEADS):
        v = act[:, (2 * HEADS + h) * DK:(2 * HEADS + h + 1) * DK]
        v_new.append(beta_all[:, HEADS + h:HEADS + h + 1] * (v - eg_all[:, h:h + 1] * k_s[h]))
    for t, h in items:
        nsd_ref[t, h] = (sd_ref[t, h] * eg_all[t:t + 1, h:h + 1]
                         + _mm_tn(jnp.where(rows == t, ks[h], 0.0), v_new[h]))
    o_a = [jnp.zeros((tb, A_DV), F32) for _ in range(HEADS)]
    for t, h in items:
        o_a[h] = _row_select(rows, t, _mm(qs[h], nsd_ref[t, h]), o_a[h])
    for h in range(HEADS):
        ga = rest_ref[:, h * DK:(h + 1) * DK]
        mixed_ref[:, h * DK:(h + 1) * DK] = _rms(o_a[h], ggain_ref[...]) * _silu(ga)

    qrs, krs = [], []
    for h in range(HEADS):
        qb = rest_ref[:, base + h * DK:base + (h + 1) * DK]
        kb = rest_ref[:, 2 * base + h * DK:2 * base + (h + 1) * DK]
        qrs.append(qb * cosf + pltpu.roll(qb, DK // 2, 1) * sinf)
        krs.append((kb * cosf + pltpu.roll(kb, DK // 2, 1) * sinf) * (DK ** -0.5))
    for t, h in items:
        vb = rest_ref[:, 3 * base + h * DK:3 * base + (h + 1) * DK]
        nsr_ref[t, h] = (sr_ref[t, h] * math.exp(LOG_GAMMA[h])
                         + _mm_tn(jnp.where(rows == t, krs[h], 0.0), vb))
    o_b = [jnp.zeros((tb, DK), F32) for _ in range(HEADS)]
    for t, h in items:
        o_b[h] = _row_select(rows, t, _mm(qrs[h], nsr_ref[t, h]), o_b[h])
    for h in range(HEADS):
        gb = rest_ref[:, 4 * base + h * DK:4 * base + (h + 1) * DK]
        mixed_ref[:, base + h * DK:base + (h + 1) * DK] = _rms(o_b[h], rgain_ref[...]) * _silu(gb)


def _mix_ab_sample(qkv, rest, gates, conv_buf, s_delta, s_ret, w_conv, a_log, dt_bias, gdn_gain, ret_gain):
    nb = qkv.shape[0]
    tb = SAMPLE_TB
    assert nb % tb == 0
    cosf, sinf = _rope_tables(PAST_LEN + jnp.arange(1))
    prow, _ = _gate_params(a_log, dt_bias)
    tok = lambda b: (b, 0)
    tok4 = lambda b: (b, 0, 0, 0)
    const = lambda b: (0, 0)
    n_buf = (CONV_W - 1) * A_CONV_CH
    mixed, nbuf, nsd, nsr = pl.pallas_call(
        _step_ab_kernel,
        grid=(nb // tb,),
        in_specs=[pl.BlockSpec((tb, A_CONV_CH), tok),
                  pl.BlockSpec((tb, rest.shape[1]), tok),
                  pl.BlockSpec((tb, 128), tok),
                  pl.BlockSpec((1, DK), const),
                  pl.BlockSpec((1, DK), const),
                  pl.BlockSpec((CONV_W, A_CONV_CH), const),
                  pl.BlockSpec((8, 128), const),
                  pl.BlockSpec((1, A_DV), const),
                  pl.BlockSpec((1, DK), const),
                  pl.BlockSpec((tb, n_buf), tok),
                  pl.BlockSpec((tb, HEADS, DK, A_DV), tok4),
                  pl.BlockSpec((tb, HEADS, DK, DK), tok4)],
        out_specs=[pl.BlockSpec((tb, 2 * HEADS * DK), tok),
                   pl.BlockSpec((tb, n_buf), tok),
                   pl.BlockSpec((tb, HEADS, DK, A_DV), tok4),
                   pl.BlockSpec((tb, HEADS, DK, DK), tok4)],
        out_shape=[jax.ShapeDtypeStruct((nb, 2 * HEADS * DK), F32),
                   jax.ShapeDtypeStruct((nb, n_buf), F32),
                   jax.ShapeDtypeStruct((nb, HEADS, DK, A_DV), F32),
                   jax.ShapeDtypeStruct((nb, HEADS, DK, DK), F32)],
        compiler_params=_cparams("parallel"),
        name="mix_ab_sample",
    )(qkv, rest, gates, cosf, sinf, w_conv, prow, gdn_gain.reshape(1, A_DV), ret_gain.reshape(1, DK),
      conv_buf.reshape(nb, n_buf), s_delta, s_ret)
    return mixed, nbuf.reshape(nb, CONV_W - 1, A_CONV_CH), nsd, nsr


def _step_c_kernel(qk_ref, v_ref, opre_ref, gates_ref, brow_ref, gain_ref, cm_ref, nv_ref, m_ref,
                   h_ref, ncm_ref, nnv_ref, nm_ref):
    tb = qk_ref.shape[0]
    g = gates_ref[...] + brow_ref[0:1, :]
    logf = _log_sigmoid(g)
    rows = lax.broadcasted_iota(jnp.int32, (tb, DK), 0)
    items = [(t, h) for t in range(tb) for h in range(HEADS)]
    qs, kws, fs, ms_new, nvs_new = [], [], [], [], []
    for h in range(HEADS):
        i_pre = g[:, h:h + 1]
        q = qk_ref[:, h * DK:(h + 1) * DK]
        k = qk_ref[:, (HEADS + h) * DK:(HEADS + h + 1) * DK] * (DK ** -0.5)
        inter = logf[:, HEADS + h:HEADS + h + 1] + m_ref[:, h:h + 1]
        m_new = jnp.maximum(inter, i_pre)
        f_state = jnp.exp(inter - m_new)
        kw = k * jnp.exp(i_pre - m_new)
        nv_new = nv_ref[:, h * DK:(h + 1) * DK] * f_state + kw
        nnv_ref[:, h * DK:(h + 1) * DK] = nv_new
        nm_ref[:, h:h + 1] = m_new
        qs.append(q)
        kws.append(kw)
        fs.append(f_state)
        ms_new.append(m_new)
        nvs_new.append(nv_new)
    for t, h in items:
        v = v_ref[:, h * C_DV:(h + 1) * C_DV]
        ncm_ref[t, h] = (cm_ref[t, h] * fs[h][t:t + 1, :]
                         + _mm_tn(jnp.where(rows == t, kws[h], 0.0), v))
    rows_v = lax.broadcasted_iota(jnp.int32, (tb, C_DV), 0)
    nums = [jnp.zeros((tb, C_DV), F32) for _ in range(HEADS)]
    for t, h in items:
        nums[h] = _row_select(rows_v, t, _mm(qs[h], ncm_ref[t, h]), nums[h])
    for h in range(HEADS):
        den = jnp.sum(qs[h] * nvs_new[h], axis=1, keepdims=True)
        hh = nums[h] / jnp.maximum(jnp.abs(den), jnp.exp(-ms_new[h]))
        op = opre_ref[:, h * C_DV:(h + 1) * C_DV]
        h_ref[:, h * C_DV:(h + 1) * C_DV] = _rms(hh, gain_ref[...]) * jax.nn.sigmoid(op)


def _mix_c_sample(qk, v, opre, gates, s_c, s_n, s_m, b_gate, gain):
    nb = qk.shape[0]
    tb = SAMPLE_TB
    assert nb % tb == 0
    brow, _ = _bias_params(b_gate)
    tok = lambda b: (b, 0)
    tok4 = lambda b: (b, 0, 0, 0)
    const = lambda b: (0, 0)
    hh, ncm, nnv, nm = pl.pallas_call(
        _step_c_kernel,
        grid=(nb // tb,),
        in_specs=[pl.BlockSpec((tb, 2 * HEADS * DK), tok),
                  pl.BlockSpec((tb, HEADS * C_DV), tok),
                  pl.BlockSpec((tb, HEADS * C_DV), tok),
                  pl.BlockSpec((tb, 128), tok),
                  pl.BlockSpec((8, 128), const),
                  pl.BlockSpec((1, C_DV), const),
                  pl.BlockSpec((tb, HEADS, DK, C_DV), tok4),
                  pl.BlockSpec((tb, HEADS * DK), tok),
                  pl.BlockSpec((tb, HEADS), tok)],
        out_specs=[pl.BlockSpec((tb, HEADS * C_DV), tok),
                   pl.BlockSpec((tb, HEADS, DK, C_DV), tok4),
                   pl.BlockSpec((tb, HEADS * DK), tok),
                   pl.BlockSpec((tb, HEADS), tok)],
        out_shape=[jax.ShapeDtypeStruct((nb, HEADS * C_DV), F32),
                   jax.ShapeDtypeStruct((nb, HEADS, DK, C_DV), F32),
                   jax.ShapeDtypeStruct((nb, HEADS * DK), F32),
                   jax.ShapeDtypeStruct((nb, HEADS), F32)],
        compiler_params=_cparams("parallel"),
        name="mix_c_sample",
    )(qk, v, opre, gates, brow, gain.reshape(1, C_DV), s_c, s_n.reshape(nb, HEADS * DK), s_m)
    return hh, ncm, nnv.reshape(nb, HEADS, DK), nm


C_COLS = ((0, 2 * HEADS * DK), (2 * HEADS * DK, HEADS * C_DV), (2 * HEADS * DK + HEADS * C_DV, HEADS * C_DV),
          (2 * HEADS * DK + 2 * HEADS * C_DV, 2 * HEADS))


def kernel(x_prompt, x_sample, state_conv_a, state_delta_a, state_ret_b, state_mlstm_C, state_mlstm_n, state_mlstm_m,
           norm_mix_a, w_in_a, w_conv_a, a_log, dt_bias, gdn_gain, ret_gain, w_out_a,
           norm_mix_c, w_in_c, b_gate_c, mlstm_gain, w_out_c, norm_ffn, w_up, w_down, final_gain):
    batch, seq, _ = x_prompt.shape
    n_s = x_sample.shape[0] * x_sample.shape[1]
    tm = PROMPT_TM
    w_a = w_in_a[0].astype(BF16)
    w_c = w_in_c[0].astype(BF16)
    w_ups = w_up.astype(BF16)
    w_downs = w_down.astype(BF16)
    h = x_prompt.reshape(batch * seq, D_MODEL)
    h_s = x_sample.reshape(n_s, D_MODEL)

    (mixed, conv, sd, sr), (qkv_s, rest_s, gates_s) = _layer0_prompt(
        h, h_s, norm_mix_a[0], w_a, w_conv_a[0], a_log[0], dt_bias[0], gdn_gain[0], ret_gain[0], batch, seq)
    mixed_s, conv_s, sd_s, sr_s = _mix_ab_sample(qkv_s, rest_s, gates_s, state_conv_a[0], state_delta_a[0],
                                                 state_ret_b[0], w_conv_a[0], a_log[0], dt_bias[0], gdn_gain[0],
                                                 ret_gain[0])
    h, h_s = _out_mlp(h, mixed, h_s, mixed_s, w_out_a[0].astype(BF16), norm_ffn[0], w_ups, w_downs, 0, None, tm)

    (qk, v, opre, gates_c), (qk_s, v_s, opre_s, gates_cs) = _norm_proj(h, h_s, norm_mix_c[0], w_c, C_COLS, tm)
    hm, cm, nv, m = _mix_c_prompt(qk, v, opre, gates_c, batch, seq, b_gate_c[0], mlstm_gain[0])
    hm_s, cm_s, nv_s, m_s = _mix_c_sample(qk_s, v_s, opre_s, gates_cs, state_mlstm_C[0], state_mlstm_n[0],
                                          state_mlstm_m[0], b_gate_c[0], mlstm_gain[0])
    y, y_s = _out_mlp(h, hm, h_s, hm_s, w_out_c[0].astype(BF16), norm_ffn[1], w_ups, w_downs, 1, final_gain, tm)

    return (y.reshape(x_prompt.shape), y_s.reshape(x_sample.shape),
            conv[None], sd[None], sr[None], cm[None], nv[None], m[:, :, 0][None],
            conv_s[None], sd_s[None], sr_s[None], cm_s[None], nv_s[None], m_s[None])
```

```python
import functools
import math

import jax
import jax.numpy as jnp
import numpy as np
from jax import lax
from jax.experimental import pallas as pl
from jax.experimental.pallas import tpu as pltpu

F32 = jnp.float32
BF16 = jnp.bfloat16

D_MODEL = 1024
D_FF = 4 * D_MODEL
CHUNK = 64
EPS = 1e-6
NEG = -1e30
HEADS = 4
DK = 128
A_DV = 128
C_DV = 256
CONV_W = 4
A_CONV_CH = 3 * HEADS * DK
ROPE_BASE = 10000.0
PAST_LEN = 16384
LOG_GAMMA = tuple(math.log1p(-(2.0 ** (-5.0 - h))) for h in range(HEADS))

VMEM_LIMIT_BYTES = 56 * 1024 * 1024
PROMPT_TM = 512
AB_SUB = 8
C_SUB = 8
SAMPLE_TB = 8


def _cparams(*sem):
    return pltpu.CompilerParams(dimension_semantics=sem, vmem_limit_bytes=VMEM_LIMIT_BYTES)


def _mm(a, b):
    return jnp.dot(a.astype(BF16), b.astype(BF16), preferred_element_type=F32)


def _mm_nt(a, b):
    return lax.dot_general(a.astype(BF16), b.astype(BF16), (((1,), (1,)), ((), ())), preferred_element_type=F32)


def _mm_tn(a, b):
    return lax.dot_general(a.astype(BF16), b.astype(BF16), (((0,), (0,)), ((), ())), preferred_element_type=F32)


def _split(a):
    hi = a.astype(BF16)
    lo = (a - hi.astype(F32)).astype(BF16)
    return hi, lo


def _softplus(x):
    return jnp.maximum(x, 0.0) + jnp.log1p(jnp.exp(-jnp.abs(x)))


def _log_sigmoid(x):
    return -_softplus(-x)


def _silu(x):
    return x * jax.nn.sigmoid(x)


def _rms(x, gain):
    return x * lax.rsqrt(jnp.mean(x * x, axis=-1, keepdims=True) + EPS) * gain


def _l2(t):
    return t * lax.rsqrt(jnp.sum(t * t, axis=-1, keepdims=True) + EPS)


def _interleave(gens):
    gens = list(gens)
    while gens:
        alive = []
        for gen in gens:
            try:
                next(gen)
                alive.append(gen)
            except StopIteration:
                pass
        gens = alive


def _lockstep(gens):
    gens = list(gens)
    while gens:
        alive = []
        for gen in gens:
            try:
                next(gen)
                alive.append(gen)
            except StopIteration:
                pass
        gens = alive
        if gens:
            yield


def _head_block_rows(x):
    c, n = x.shape
    t = n // HEADS
    z = jnp.zeros((c, t), x.dtype)
    return jnp.concatenate(
        [jnp.concatenate([x[:, h * t:(h + 1) * t] if g == h else z for g in range(HEADS)], axis=1)
         for h in range(HEADS)], axis=0)


def _head_block_diag(y):
    c, n = y.shape
    per_tile = 128 // c
    assert n == HEADS * c and 128 % c == 0 and HEADS % per_tile == 0
    lane = lax.broadcasted_iota(jnp.int32, (c, 128), 1)
    z = jnp.zeros((c, 128), y.dtype)
    blocks = []
    for h in range(HEADS):
        t = h // per_tile
        lo = (h % per_tile) * c
        kept = jnp.where((lane >= lo) & (lane < lo + c), y[:, t * 128:(t + 1) * 128], z)
        blocks.append(jnp.concatenate([kept if g == t else z for g in range(n // 128)], axis=1))
    return jnp.concatenate(blocks, axis=0)


def _wide_mm(x, y):
    return jnp.dot(x.astype(BF16), _head_block_diag(y.astype(BF16)), preferred_element_type=F32)


def _tri_inv_wide(ns):
    c, wd = ns[0].shape
    ii = lax.broadcasted_iota(jnp.int32, (c, wd), 0)
    jl = lax.broadcasted_iota(jnp.int32, (c, wd), 1) & (c - 1)
    eye = (ii == jl).astype(F32)
    ts = [eye - jnp.where((ii >> 1) == (jl >> 1), n, 0.0) for n in ns]
    for lvl in range(1, int(math.log2(c))):
        off = ((ii >> (lvl + 1)) == (jl >> (lvl + 1))) & ((ii >> lvl) != (jl >> lvl))
        ys = [_wide_mm(jnp.where(off, n, 0.0), t) for n, t in zip(ns, ts)]
        yield
        ts = [t - _wide_mm(t, y) for t, y in zip(ts, ys)]
        yield
    return [t - eye for t in ts]


def _out_width(n):
    return max(n, 128)


def _project_rows(x_ref, g_ref, pieces, out_refs):
    xn = _rms(x_ref[...], g_ref[...]).astype(BF16)
    for (w_ref, col0, n), o_ref in zip(pieces, out_refs):
        res = jnp.dot(xn, w_ref[:, col0:col0 + n], preferred_element_type=F32)
        if n < _out_width(n):
            res = jnp.concatenate([res, jnp.zeros((res.shape[0], _out_width(n) - n), F32)], axis=1)
        o_ref[...] = res


def _norm_proj_kernel(x_ref, xs_ref, g_ref, w_ref, *out_refs, cols):
    pieces = [(w_ref, c0, n) for c0, n in cols]
    _project_rows(x_ref, g_ref, pieces, out_refs[:len(cols)])

    @pl.when(pl.program_id(0) == 0)
    def _():
        _project_rows(xs_ref, g_ref, pieces, out_refs[len(cols):])


def _norm_proj(x, x_s, gain, w, cols, tm):
    t, ts = x.shape[0], x_s.shape[0]
    assert t % tm == 0
    widths = [_out_width(n) for _, n in cols]
    row = lambda i: (i, 0)
    const = lambda i: (0, 0)
    outs = pl.pallas_call(
        functools.partial(_norm_proj_kernel, cols=cols),
        grid=(t // tm,),
        in_specs=[pl.BlockSpec((tm, D_MODEL), row), pl.BlockSpec((ts, D_MODEL), const),
                  pl.BlockSpec((1, D_MODEL), const),
                  pl.BlockSpec(w.shape, const, pipeline_mode=pl.Buffered(1))],
        out_specs=[pl.BlockSpec((tm, n), row) for n in widths] + [pl.BlockSpec((ts, n), const) for n in widths],
        out_shape=[jax.ShapeDtypeStruct((t, n), F32) for n in widths]
                  + [jax.ShapeDtypeStruct((ts, n), F32) for n in widths],
        compiler_params=_cparams("arbitrary"),
        name="norm_proj",
    )(x, x_s, gain.reshape(1, D_MODEL), w)
    return outs[:len(widths)], outs[len(widths):]


def _layer0_prompt_kernel(x_ref, xs_ref, g_ref, w_ref, wconv_ref, cos_ref, sin_ref,
                          prow_ref, pcol_ref, tri_ref, spread_ref, ggain_ref, rgain_ref,
                          mixed_ref, conv_ref, sd_ref, sr_ref, qkvs_ref, rests_ref, gatess_ref,
                          xp_ref, raw_ref, wrest_ref, qkv_ref, rest_ref, gates_ref, *, tiles_per_seq, n_sub):
    tm = x_ref.shape[0]
    hd = HEADS * DK
    n_gate = 2 * HEADS
    gate_piece = (w_ref, A_CONV_CH, n_gate)

    @pl.when(pl.program_id(0) == 0)
    def _():
        wrest_ref[...] = w_ref[:, A_CONV_CH + n_gate:A_CONV_CH + n_gate + 5 * hd]
        _project_rows(xs_ref, g_ref, [(w_ref, 0, A_CONV_CH), (wrest_ref, 0, 5 * hd), gate_piece],
                      (qkvs_ref, rests_ref, gatess_ref))

    xn = _rms(x_ref[...], g_ref[...]).astype(BF16)

    @pl.when(pl.program_id(0) % tiles_per_seq == 0)
    def _():
        xp_ref[0:8, :] = jnp.zeros((8, A_CONV_CH), F32)
        sd_ref[...] = jnp.zeros_like(sd_ref)
        sr_ref[...] = jnp.zeros_like(sr_ref)

    w = wconv_ref[...]
    base = A_CONV_CH

    def conv_part(part, slot, r0, r1):
        c0, c1 = part * hd, (part + 1) * hd
        raw = raw_ref[slot, r0:r1, :]
        xp_ref[8 + r0:8 + r1, c0:c1] = raw
        conv = (xp_ref[5 + r0:5 + r1, c0:c1] * w[0:1, c0:c1] + xp_ref[6 + r0:6 + r1, c0:c1] * w[1:2, c0:c1]
                + xp_ref[7 + r0:7 + r1, c0:c1] * w[2:3, c0:c1] + raw * w[3:4, c0:c1])
        if r1 == tm:
            rows = r1 - r0
            xp_ref[0:8, c0:c1] = raw[rows - 8:rows, :]
            conv_ref[0, :, c0:c1] = raw[rows - 3:rows, :]
        act = _silu(conv)
        if part == 2:
            qkv_ref[r0:r1, c0:c1] = act
        else:
            scale = DK ** -0.5 if part == 0 else 1.0
            for h in range(HEADS):
                qkv_ref[r0:r1, c0 + h * DK:c0 + (h + 1) * DK] = _l2(act[:, h * DK:(h + 1) * DK]) * scale

    def rest_part(part, slot, r0, r1):
        c0, c1 = part * hd, (part + 1) * hd
        raw = raw_ref[slot, r0:r1, :]
        if part in (0, 4):
            rest_ref[r0:r1, c0:c1] = _silu(raw)
        elif part == 3:
            rest_ref[r0:r1, c0:c1] = raw
        else:
            scale = 1.0 if part == 1 else DK ** -0.5
            for h in range(HEADS):
                t = raw[:, h * DK:(h + 1) * DK]
                rest_ref[r0:r1, c0 + h * DK:c0 + (h + 1) * DK] = (
                    (t * cos_ref[r0:r1, :] + pltpu.roll(t, DK // 2, 1) * sin_ref[r0:r1, :]) * scale)

    def gates_part(_, slot, r0, r1):
        gates_ref[r0:r1, :] = jnp.concatenate(
            [raw_ref[slot, r0:r1, 0:n_gate], jnp.zeros((r1 - r0, 128 - n_gate), F32)], axis=1)

    groups = [(w_ref, 0, hd, functools.partial(conv_part, 0)),
              (wrest_ref, 3 * hd, hd, functools.partial(rest_part, 3)),
              (w_ref, hd, hd, functools.partial(conv_part, 1)),
              (wrest_ref, 0, hd, functools.partial(rest_part, 0)),
              (w_ref, 2 * hd, hd, functools.partial(conv_part, 2)),
              (wrest_ref, hd, hd, functools.partial(rest_part, 1)),
              gate_piece + (functools.partial(gates_part, 0),),
              (wrest_ref, 2 * hd, hd, functools.partial(rest_part, 2)),
              (wrest_ref, 4 * hd, hd, functools.partial(rest_part, 4))]

    def project(n, slot):
        wg_ref, col0, width, _ = groups[n]
        for j in range(0, width, 256):
            wj = min(256, width - j)
            raw_ref[slot, :, j:j + wj] = jnp.dot(xn, wg_ref[:, col0 + j:col0 + j + wj],
                                                 preferred_element_type=F32)
            yield

    def epilogue(n, slot):
        for r0 in range(0, tm, 128):
            groups[n][3](slot, r0, min(r0 + 128, tm))
            yield

    def projection_track(order):
        yield from project(order[0], 0)
        for idx, n in enumerate(order):
            nxt = [project(order[idx + 1], (idx + 1) % 2)] if idx + 1 < len(order) else []
            yield from _lockstep([epilogue(n, idx % 2)] + nxt)
            yield

    _interleave([projection_track([6, 0, 2, 4])])
    gdn_stages, delta_track, ret_track, finish = _mix_ab_tracks(
        qkv_ref, rest_ref, gates_ref, prow_ref, pcol_ref, tri_ref, spread_ref, ggain_ref, rgain_ref,
        mixed_ref, sd_ref, sr_ref, n_sub)
    _interleave(gdn_stages)

    def filler_track():
        yield from projection_track([1, 3, 5, 7, 8])
        yield from ret_track()

    _interleave([delta_track(), filler_track()])
    finish()


def _layer0_prompt(x, x_s, gain, w, w_conv, a_log, dt_bias, gdn_gain, ret_gain, batch, seq):
    t, ts = x.shape[0], x_s.shape[0]
    tm = AB_SUB * CHUNK
    assert t % tm == 0 and seq % tm == 0
    tiles_per_seq = seq // tm
    cosf, sinf = _rope_tables(0, seq)
    prow, pcol = _gate_params(a_log, dt_bias)
    tri = _chunk_tri(tm)
    spread = _head_spread_ab()
    row = lambda i: (i, 0)
    const = lambda i: (0, 0)
    pos = lambda i: (i % tiles_per_seq, 0)
    per_seq3 = lambda i: (i // tiles_per_seq, 0, 0)
    per_seq4 = lambda i: (i // tiles_per_seq, 0, 0, 0)
    n_rest = 5 * HEADS * DK
    outs = pl.pallas_call(
        functools.partial(_layer0_prompt_kernel, tiles_per_seq=tiles_per_seq, n_sub=AB_SUB),
        grid=(t // tm,),
        in_specs=[pl.BlockSpec((tm, D_MODEL), row),
                  pl.BlockSpec((ts, D_MODEL), const),
                  pl.BlockSpec((1, D_MODEL), const),
                  pl.BlockSpec(w.shape, const, pipeline_mode=pl.Buffered(1)),
                  pl.BlockSpec((CONV_W, A_CONV_CH), const),
                  pl.BlockSpec((tm, DK), pos),
                  pl.BlockSpec((tm, DK), pos),
                  pl.BlockSpec((8, 128), const),
                  pl.BlockSpec((8, 128), const),
                  pl.BlockSpec(tri.shape, const),
                  pl.BlockSpec(spread.shape, const),
                  pl.BlockSpec((1, A_DV), const),
                  pl.BlockSpec((1, DK), const)],
        out_specs=[pl.BlockSpec((tm, 2 * HEADS * DK), row),
                   pl.BlockSpec((1, CONV_W - 1, A_CONV_CH), per_seq3),
                   pl.BlockSpec((1, HEADS, DK, A_DV), per_seq4),
                   pl.BlockSpec((1, HEADS, DK, DK), per_seq4),
                   pl.BlockSpec((ts, A_CONV_CH), const),
                   pl.BlockSpec((ts, n_rest), const),
                   pl.BlockSpec((ts, 128), const)],
        out_shape=[jax.ShapeDtypeStruct((t, 2 * HEADS * DK), BF16),
                   jax.ShapeDtypeStruct((batch, CONV_W - 1, A_CONV_CH), F32),
                   jax.ShapeDtypeStruct((batch, HEADS, DK, A_DV), F32),
                   jax.ShapeDtypeStruct((batch, HEADS, DK, DK), F32),
                   jax.ShapeDtypeStruct((ts, A_CONV_CH), F32),
                   jax.ShapeDtypeStruct((ts, n_rest), F32),
                   jax.ShapeDtypeStruct((ts, 128), F32)],
        scratch_shapes=[pltpu.VMEM((tm + 8, A_CONV_CH), F32), pltpu.VMEM((2, tm, HEADS * DK), F32),
                        pltpu.VMEM((D_MODEL, n_rest), BF16),
                        pltpu.VMEM((tm, A_CONV_CH), F32), pltpu.VMEM((tm, n_rest), F32), pltpu.VMEM((tm, 128), F32)],
        compiler_params=_cparams("arbitrary"),
        name="layer0_prompt",
    )(x, x_s, gain.reshape(1, D_MODEL), w, w_conv, cosf, sinf, prow, pcol, tri, spread,
      gdn_gain.reshape(1, A_DV), ret_gain.reshape(1, DK))
    return outs[:4], outs[4:]


def _out_mlp_kernel(h_ref, m_ref, hs_ref, ms_ref, wout_ref, gffn_ref, wup_ref, wdown_ref, *rest, final):
    if final:
        gfin_ref, o_ref, os_ref = rest
    else:
        o_ref, os_ref = rest

    def block(h_r, m_r, o_r):
        h = h_r[...] + jnp.dot(m_r[...].astype(BF16), wout_ref[...], preferred_element_type=F32)
        xn = _rms(h, gffn_ref[...]).astype(BF16)
        acc = h
        step = 1024
        for j in range(D_FF // step):
            hid = jnp.dot(xn, wup_ref[:, j * step:(j + 1) * step], preferred_element_type=F32)
            hid = jnp.maximum(hid, 0.0)
            acc = acc + jnp.dot((hid * hid).astype(BF16), wdown_ref[j * step:(j + 1) * step, :],
                                preferred_element_type=F32)
        if final:
            acc = _rms(acc, gfin_ref[...])
        o_r[...] = acc

    block(h_ref, m_ref, o_ref)

    @pl.when(pl.program_id(0) == 0)
    def _():
        block(hs_ref, ms_ref, os_ref)


def _out_mlp(h, mix, h_s, mix_s, w_out, g_ffn, w_up, w_down, layer, g_final, tm):
    t, ts = h.shape[0], h_s.shape[0]
    assert t % tm == 0
    final = g_final is not None
    row = lambda i: (i, 0)
    const = lambda i: (0, 0)
    this_layer = lambda i: (layer, 0, 0)
    once = pl.Buffered(1)
    in_specs = [pl.BlockSpec((tm, D_MODEL), row), pl.BlockSpec((tm, mix.shape[1]), row),
                pl.BlockSpec((ts, D_MODEL), const), pl.BlockSpec((ts, mix_s.shape[1]), const),
                pl.BlockSpec(w_out.shape, const, pipeline_mode=once), pl.BlockSpec((1, D_MODEL), const),
                pl.BlockSpec((None,) + w_up.shape[1:], this_layer, pipeline_mode=once),
                pl.BlockSpec((None,) + w_down.shape[1:], this_layer, pipeline_mode=once)]
    args = [h, mix, h_s, mix_s, w_out, g_ffn.reshape(1, D_MODEL), w_up, w_down]
    if final:
        in_specs.append(pl.BlockSpec((1, D_MODEL), const))
        args.append(g_final.reshape(1, D_MODEL))
    return pl.pallas_call(
        functools.partial(_out_mlp_kernel, final=final),
        grid=(t // tm,),
        in_specs=in_specs,
        out_specs=[pl.BlockSpec((tm, D_MODEL), row), pl.BlockSpec((ts, D_MODEL), const)],
        out_shape=[jax.ShapeDtypeStruct((t, D_MODEL), F32), jax.ShapeDtypeStruct((ts, D_MODEL), F32)],
        compiler_params=_cparams("arbitrary"),
        name="out_mlp",
    )(*args)


def _chunk_masks(c):
    ii = lax.broadcasted_iota(jnp.int32, (c, c), 0)
    jj = lax.broadcasted_iota(jnp.int32, (c, c), 1)
    return ii, jj


def _mix_ab_tracks(qkv_ref, rest_ref, gates_ref, prow_ref, pcol_ref, tri_ref, spread_ref,
                   ggain_ref, rgain_ref, mixed_ref, sd_ref, sr_ref, n_sub):
    c = CHUNK
    lb = n_sub * c

    g = gates_ref[...]
    g_t = g.T
    neg_a_row = -jnp.exp(prow_ref[0:1, :])
    dt_row = prow_ref[1:2, :]
    neg_a_col = -jnp.exp(pcol_ref[0:HEADS, 0:1])
    dt_col = pcol_ref[0:HEADS, 1:2]
    la_cols = neg_a_row * _softplus(g + dt_row)
    beta_cols = jax.nn.sigmoid(g)
    la_rows = neg_a_col * _softplus(g_t[0:HEADS, :] + dt_col)

    tri = tri_ref[...]
    la_hi, la_lo = _split(la_cols)
    g_cols = (jnp.dot(tri, la_hi, preferred_element_type=F32)
              + jnp.dot(tri, la_lo, preferred_element_type=F32))
    lr_hi, lr_lo = _split(jnp.concatenate([la_rows, jnp.zeros_like(la_rows)], axis=0))
    nt = lambda x, y: lax.dot_general(x, y, (((1,), (1,)), ((), ())), preferred_element_type=F32)
    g_rows = nt(lr_hi, tri) + nt(lr_lo, tri)
    lane = lax.broadcasted_iota(jnp.int32, (lb, 128), 1)
    x_hi, x_lo = _split(jnp.where(lane < HEADS, g_cols, beta_cols))
    spread = (jnp.dot(x_hi, spread_ref[...], preferred_element_type=F32)
              + jnp.dot(x_lo, spread_ref[...], preferred_element_type=F32))
    hd = HEADS * DK
    wd = HEADS * c
    g_wide = spread[:, 0:hd]
    beta_wide = spread[:, hd:2 * hd]
    g_half = spread[:, 2 * hd:2 * hd + wd]
    eg_wide = jnp.exp(g_wide)

    ii = lax.broadcasted_iota(jnp.int32, (c, wd), 0)
    jl = lax.broadcasted_iota(jnp.int32, (c, wd), 1) & (c - 1)
    causal = ii >= jl
    strict = ii > jl
    lane_hd = lax.broadcasted_iota(jnp.int32, (1, hd), 1)
    lane_wd = lax.broadcasted_iota(jnp.int32, (1, wd), 1)
    lg_hd = jnp.full((1, hd), LOG_GAMMA[HEADS - 1], F32)
    lg_wd = jnp.full((1, wd), LOG_GAMMA[HEADS - 1], F32)
    for h in range(HEADS - 2, -1, -1):
        lg_hd = jnp.where(lane_hd < (h + 1) * DK, LOG_GAMMA[h], lg_hd)
        lg_wd = jnp.where(lane_wd < (h + 1) * c, LOG_GAMMA[h], lg_wd)
    pos_col = lax.broadcasted_iota(jnp.int32, (c, hd), 0).astype(F32)
    ret_dec = jnp.exp(jnp.where(causal, (ii - jl).astype(F32) * lg_wd, NEG))
    ret_q_scale = jnp.exp((pos_col + 1.0) * lg_hd)
    ret_k_scale = jnp.exp((float(c - 1) - pos_col) * lg_hd)

    ggain = ggain_ref[...]
    rgain = rgain_ref[...]
    base = hd
    tile = lambda x, h: x[:, h * DK:(h + 1) * DK]
    s_delta = [sd_ref[0, h] for h in range(HEADS)]
    s_ret = [sr_ref[0, h] for h in range(HEADS)]

    chunks = list(range(n_sub))
    items = [(ci, h) for ci in chunks for h in range(HEADS)]
    rng = lambda ci: (ci * c, (ci + 1) * c)
    pre = {it: {} for it in items}
    pre_b = {it: {} for it in items}
    wide = {ci: {} for ci in chunks}
    nt = lambda x, y: lax.dot_general(x, y, (((1,), (1,)), ((), ())), preferred_element_type=F32)

    def gdn_stage(ci):
        w_ = wide[ci]
        r0, r1 = rng(ci)
        q = qkv_ref[r0:r1, 0:hd]
        k = qkv_ref[r0:r1, hd:2 * hd]
        beta = beta_wide[r0:r1]
        g_col = g_wide[r0:r1]
        kb = k * beta
        g_row = jnp.concatenate([g_rows[h:h + 1, r0:r1] for h in range(HEADS)], axis=1)
        dec_causal = jnp.exp(jnp.where(causal, g_half[r0:r1] - g_row, NEG))
        prod = nt(jnp.concatenate([q, kb], axis=0).astype(BF16), _head_block_rows(k.astype(BF16)))
        yield
        w_['n'] = prod[c:2 * c] * jnp.where(strict, dec_causal, 0.0)
        w_['qk'] = prod[0:c] * dec_causal
        yield
        v = qkv_ref[r0:r1, 2 * hd:3 * hd]
        eg = eg_wide[r0:r1]
        g_last = g_col[c - 1:c, :]
        rhs_u, rhs_w, qd = v * beta, kb * eg, q * eg
        kd = k * jnp.exp(g_last - g_col)
        gl = jnp.exp(g_last)
        for h in range(HEADS):
            pre[ci, h].update(rhs=jnp.concatenate([tile(rhs_u, h), tile(rhs_w, h)], axis=1), qd=tile(qd, h),
                              kd=tile(kd, h), gl=tile(gl, h))

    def ret_stage(ci):
        r0, r1 = rng(ci)
        qr = rest_ref[r0:r1, base:2 * base]
        kr = rest_ref[r0:r1, 2 * base:3 * base]
        vb = rest_ref[r0:r1, 3 * base:4 * base]
        qk = nt(qr.astype(BF16), _head_block_rows(kr.astype(BF16))) * ret_dec
        qd = qr * ret_q_scale
        kd = kr * ret_k_scale
        yield
        for a in range(HEADS // 2):
            h0, h1 = 2 * a, 2 * a + 1
            z = jnp.zeros((c, DK), F32)
            rhs = jnp.concatenate([jnp.concatenate([tile(vb, h0), z], axis=1),
                                   jnp.concatenate([z, tile(vb, h1)], axis=1)], axis=0)
            intra = _mm(qk[:, a * 2 * c:(a + 1) * 2 * c], rhs)
            pre_b[ci, h0]['intra'] = intra[:, 0:DK]
            pre_b[ci, h1]['intra'] = intra[:, DK:2 * DK]
        yield
        for h in range(HEADS):
            pre_b[ci, h]['qd'] = tile(qd, h)
            pre_b[ci, h]['kv'] = _mm_tn(tile(kd, h), tile(vb, h))

    def solve_stage():
        t_offs = yield from _tri_inv_wide([wide[ci]['n'] for ci in chunks])
        for ci, t_off in zip(chunks, t_offs):
            wide[ci]['t_off'] = t_off

    def pair_rhs(x0, x1):
        z = jnp.zeros_like(x0)
        return jnp.concatenate([jnp.concatenate([x0, z], axis=1), jnp.concatenate([z, x1], axis=1)], axis=0)

    def sol_stage(ci):
        w_ = wide[ci]
        for a in range(HEADS // 2):
            h0, h1 = 2 * a, 2 * a + 1
            r = _mm(w_['t_off'][:, a * 2 * c:(a + 1) * 2 * c], pair_rhs(pre[ci, h0]['rhs'], pre[ci, h1]['rhs']))
            pre[ci, h0]['sol'] = pre[ci, h0]['rhs'] + r[:, 0:2 * A_DV]
            pre[ci, h1]['sol'] = pre[ci, h1]['rhs'] + r[:, 2 * A_DV:4 * A_DV]
        yield
        for a in range(HEADS // 2):
            h0, h1 = 2 * a, 2 * a + 1
            r = _mm(w_['qk'][:, a * 2 * c:(a + 1) * 2 * c], pair_rhs(pre[ci, h0]['sol'], pre[ci, h1]['sol']))
            pre[ci, h0]['qs'] = r[:, 0:2 * A_DV]
            pre[ci, h1]['qs'] = r[:, 2 * A_DV:4 * A_DV]
        for h in range(HEADS):
            d = pre[ci, h]
            d['kts'] = _mm_tn(d['kd'], d['sol'])
        yield
        for h in range(HEADS):
            d = pre[ci, h]
            d['lhs'] = jnp.concatenate([d['kts'][:, A_DV:2 * A_DV], d['qd'] - d['qs'][:, A_DV:2 * A_DV]], axis=0)


    def ret_out_stage(ci, h):
        d = pre_b[ci, h]
        r0, r1 = rng(ci)
        o = _mm(d['qd'], d['s_in']) + d['intra']
        yield
        ms_ = jnp.mean(o * o, axis=-1, keepdims=True)
        yield
        gb = rest_ref[r0:r1, 4 * base + h * DK:4 * base + (h + 1) * DK]
        mixed_ref[r0:r1, base + h * DK:base + (h + 1) * DK] = (
            o * lax.rsqrt(ms_ + EPS) * rgain * gb).astype(mixed_ref.dtype)

    def delta_step(ci, h):
        d = pre[ci, h]
        r = _mm(d['lhs'], s_delta[h])
        yield
        d['o'] = r[DK:DK + c] + d['qs'][:, 0:A_DV]
        s_delta[h] = s_delta[h] * d['gl'] - r[0:DK] + d['kts'][:, 0:A_DV]

    def gdn_out_stage(ci, h):
        o = pre[ci, h]['o']
        r0, r1 = rng(ci)
        ms_ = jnp.mean(o * o, axis=-1, keepdims=True)
        yield
        ga = rest_ref[r0:r1, h * DK:(h + 1) * DK]
        mixed_ref[r0:r1, h * A_DV:(h + 1) * A_DV] = (
            o * lax.rsqrt(ms_ + EPS) * ggain * ga).astype(mixed_ref.dtype)

    def delta_track():
        yield from solve_stage()
        yield from _lockstep([sol_stage(ci) for ci in chunks])
        for ci in chunks:
            yield from _lockstep([delta_step(ci, h) for h in range(HEADS)])
            if ci > 0:
                yield from _lockstep([gdn_out_stage(ci - 1, h) for h in range(HEADS)])
        yield from _lockstep([gdn_out_stage(chunks[-1], h) for h in range(HEADS)])

    def ret_track():
        for ci in chunks:
            yield from ret_stage(ci)
            for h in range(HEADS):
                pre_b[ci, h]['s_in'] = s_ret[h]
                s_ret[h] = s_ret[h] * math.exp(c * LOG_GAMMA[h]) + pre_b[ci, h]['kv']
            yield
            yield from _lockstep([ret_out_stage(ci, h) for h in range(HEADS)])

    def finish():
        for h in range(HEADS):
            sd_ref[0, h] = s_delta[h]
            sr_ref[0, h] = s_ret[h]

    return [gdn_stage(ci) for ci in chunks], delta_track, ret_track, finish


def _rope_tables(first, count):
    half = DK // 2
    inv = ROPE_BASE ** (-np.arange(half, dtype=np.float64) / half)
    ang = (first + np.arange(count, dtype=np.float64))[:, None] * inv[None, :]
    cos, sin = np.cos(ang), np.sin(ang)
    return (jnp.asarray(np.concatenate([cos, cos], axis=-1), F32),
            jnp.asarray(np.concatenate([-sin, sin], axis=-1), F32))


def _chunk_tri(lb):
    i = np.arange(lb)[:, None]
    j = np.arange(lb)[None, :]
    return jnp.asarray((i >= j) & (i // CHUNK == j // CHUNK), BF16)


def _head_spread(groups):
    n = groups * HEADS
    src = np.arange(128)[:, None]
    dst = np.arange(n * 128)[None, :] // 128
    return jnp.asarray(src == dst, BF16)


def _head_spread_ab():
    src = np.arange(128)[:, None]
    full = np.arange(2 * HEADS * DK)[None, :] // DK
    half = np.arange(HEADS * CHUNK)[None, :] // CHUNK
    return jnp.asarray(np.concatenate([src == full, src == half], axis=1), BF16)


def _gate_params(a_log, dt_bias):
    prow = jnp.zeros((8, 128), F32).at[0, 0:HEADS].set(a_log).at[1, 0:HEADS].set(dt_bias)
    pcol = jnp.zeros((8, 128), F32).at[0:HEADS, 0].set(a_log).at[0:HEADS, 1].set(dt_bias)
    return prow, pcol


def _mix_c_kernel(qk_ref, v_ref, opre_ref, gates_ref, brow_ref, bcol_ref, tri_ref, spread_ref, gain_ref,
                  h_ref, cm_ref, nv_ref, m_ref, *, n_sub):
    c = CHUNK
    lb = n_sub * c
    step = pl.program_id(1)

    @pl.when(step == 0)
    def _():
        cm_ref[...] = jnp.zeros_like(cm_ref)
        nv_ref[...] = jnp.zeros_like(nv_ref)
        m_ref[...] = jnp.zeros_like(m_ref)

    items = [(ci, h) for ci in range(n_sub) for h in range(HEADS)]
    rng = lambda ci: (ci * c, (ci + 1) * c)
    pre = {it: {} for it in items}

    for ci, h in items:
        r0, r1 = rng(ci)
        q = qk_ref[r0:r1, h * DK:(h + 1) * DK]
        k = qk_ref[r0:r1, (HEADS + h) * DK:(HEADS + h + 1) * DK] * (DK ** -0.5)
        pre[ci, h].update(q=q, k=k, qk=_mm_nt(q, k))

    g = gates_ref[...] + brow_ref[0:1, :]
    g_t = gates_ref[...].T[0:2 * HEADS, :] + bcol_ref[0:2 * HEADS, 0:1]
    i_rows = g_t[0:HEADS, :]
    logf_rows = _log_sigmoid(g_t[HEADS:2 * HEADS, :])
    tri = tri_ref[...]
    lf_hi, lf_lo = _split(_log_sigmoid(g))
    b_cols = (jnp.dot(tri, lf_hi, preferred_element_type=F32)
              + jnp.dot(tri, lf_lo, preferred_element_type=F32))
    lr_hi, lr_lo = _split(jnp.concatenate([logf_rows, jnp.zeros_like(logf_rows)], axis=0))
    nt = lambda x, y: lax.dot_general(x, y, (((1,), (1,)), ((), ())), preferred_element_type=F32)
    b_rows = nt(lr_hi, tri) + nt(lr_lo, tri)
    lane = lax.broadcasted_iota(jnp.int32, (lb, 128), 1)
    x_hi, x_lo = _split(jnp.where(lane < HEADS, g, b_cols))
    spread = (jnp.dot(x_hi, spread_ref[...], preferred_element_type=F32)
              + jnp.dot(x_lo, spread_ref[...], preferred_element_type=F32))
    i_wide = spread[:, 0:HEADS * 128]
    b_wide = spread[:, HEADS * 128:2 * HEADS * 128]

    ii, jj = _chunk_masks(c)
    causal = ii >= jj
    gain = gain_ref[...]
    wide2 = lambda x: jnp.concatenate([x, x], axis=-1)

    def gates_stage(ci, h):
        d = pre[ci, h]
        r0, r1 = rng(ci)
        b_col = b_wide[r0:r1, h * 128:(h + 1) * 128]
        i_col = i_wide[r0:r1, h * 128:(h + 1) * 128]
        b_last = b_col[c - 1:c, :]
        d_log = jnp.where(causal, b_col[:, 0:c] - b_rows[h:h + 1, r0:r1] + i_rows[h:h + 1, r0:r1], NEG)
        d.update(b_col=b_col, b_last=b_last, d_log=d_log, k_log=b_last - b_col + i_col)
        yield
        d['d_max'] = jnp.max(d_log, axis=1, keepdims=True)

    _interleave(gates_stage(ci, h) for ci, h in items)

    ms = [m_ref[0, h:h + 1, :] for h in range(HEADS)]
    for ci, h in items:
        d = pre[ci, h]
        inter = d['b_col'] + ms[h]
        m_row = jnp.maximum(inter, d['d_max'])
        m_new = m_row[c - 1:c, :]
        d.update(inter=inter, m_row=m_row, m_old=ms[h], m_new=m_new)
        ms[h] = m_new

    def weights_stage(ci, h):
        d = pre[ci, h]
        r0, r1 = rng(ci)
        d['w_inter'] = jnp.exp(d['inter'] - d['m_row'])
        d['f_state'] = jnp.exp(d['b_last'] + d['m_old'] - d['m_new'])
        yield
        d['w_intra'] = jnp.exp(d['d_log'] - d['m_row'][:, 0:c]) * d['qk']
        d['kw'] = d['k'] * jnp.exp(d['k_log'] - d['m_new'])
        yield
        v = v_ref[r0:r1, h * C_DV:(h + 1) * C_DV]
        d['intra'] = _mm(d['w_intra'], v)
        d['kv'] = _mm_tn(d['kw'], v)
        yield
        d['sum_intra'] = jnp.sum(d['w_intra'], axis=1, keepdims=True)
        d['sum_kw'] = jnp.sum(d['kw'], axis=0, keepdims=True)
        d['inv_floor'] = jnp.exp(-d['m_row'])

    _interleave(weights_stage(ci, h) for ci, h in items)

    cms = [cm_ref[0, h] for h in range(HEADS)]
    nvs = [nv_ref[0, h:h + 1, :] for h in range(HEADS)]
    for ci, h in items:
        d = pre[ci, h]
        d['cm_in'], d['nv_in'] = cms[h], nvs[h]
        cms[h] = cms[h] * wide2(d['f_state']) + d['kv']
        nvs[h] = nvs[h] * d['f_state'] + d['sum_kw']

    def output_stage(ci, h):
        d = pre[ci, h]
        r0, r1 = rng(ci)
        qc = _mm(d['q'], d['cm_in'])
        qn = jnp.sum(d['q'] * d['nv_in'], axis=1, keepdims=True)
        yield
        num = wide2(d['w_inter']) * qc + d['intra']
        den = d['w_inter'] * qn + d['sum_intra']
        hh = num / wide2(jnp.maximum(jnp.abs(den), d['inv_floor']))
        yield
        ms_ = jnp.mean(hh * hh, axis=-1, keepdims=True)
        yield
        op = opre_ref[r0:r1, h * C_DV:(h + 1) * C_DV]
        h_ref[r0:r1, h * C_DV:(h + 1) * C_DV] = (
            hh * lax.rsqrt(ms_ + EPS) * gain * jax.nn.sigmoid(op)).astype(h_ref.dtype)

    _interleave(output_stage(ci, h) for ci, h in items)

    for h in range(HEADS):
        cm_ref[0, h] = cms[h]
        nv_ref[0, h:h + 1, :] = nvs[h]
        m_ref[0, h:h + 1, :] = ms[h]


def _bias_params(b_gate):
    brow = jnp.zeros((8, 128), F32).at[0, 0:2 * HEADS].set(b_gate)
    bcol = jnp.zeros((8, 128), F32).at[0:2 * HEADS, 0].set(b_gate)
    return brow, bcol


def _mix_c_prompt(qk, v, opre, gates, batch, seq, b_gate, gain):
    lb = C_SUB * CHUNK
    assert seq % lb == 0
    nc = seq // lb
    brow, bcol = _bias_params(b_gate)
    tri = _chunk_tri(lb)
    spread = _head_spread(2)
    tok = lambda b, c: (b * nc + c, 0)
    const = lambda b, c: (0, 0)
    return pl.pallas_call(
        functools.partial(_mix_c_kernel, n_sub=C_SUB),
        grid=(batch, nc),
        in_specs=[pl.BlockSpec((lb, 2 * HEADS * DK), tok),
                  pl.BlockSpec((lb, HEADS * C_DV), tok),
                  pl.BlockSpec((lb, HEADS * C_DV), tok),
                  pl.BlockSpec((lb, 128), tok),
                  pl.BlockSpec((8, 128), const),
                  pl.BlockSpec((8, 128), const),
                  pl.BlockSpec(tri.shape, const),
                  pl.BlockSpec(spread.shape, const),
                  pl.BlockSpec((1, C_DV), const)],
        out_specs=[pl.BlockSpec((lb, HEADS * C_DV), tok),
                   pl.BlockSpec((1, HEADS, DK, C_DV), lambda b, c: (b, 0, 0, 0)),
                   pl.BlockSpec((1, HEADS, DK), lambda b, c: (b, 0, 0)),
                   pl.BlockSpec((1, HEADS, 128), lambda b, c: (b, 0, 0))],
        out_shape=[jax.ShapeDtypeStruct((batch * seq, HEADS * C_DV), BF16),
                   jax.ShapeDtypeStruct((batch, HEADS, DK, C_DV), F32),
                   jax.ShapeDtypeStruct((batch, HEADS, DK), F32),
                   jax.ShapeDtypeStruct((batch, HEADS, 128), F32)],
        compiler_params=_cparams("parallel", "arbitrary"),
        name="mix_c_prompt",
    )(qk, v, opre, gates, brow, bcol, tri, spread, gain.reshape(1, C_DV))


def _row_select(rows, t, new, old):
    return jnp.where(rows == t, new, old)


def _step_ab_kernel(qkv_ref, rest_ref, gates_ref, cos_ref, sin_ref, wconv_ref, prow_ref, ggain_ref, rgain_ref,
                    buf_ref, sd_ref, sr_ref, mixed_ref, nbuf_ref, nsd_ref, nsr_ref):
    tb = qkv_ref.shape[0]
    u = qkv_ref[...]
    w = wconv_ref[...]
    b0 = buf_ref[:, 0:A_CONV_CH]
    b1 = buf_ref[:, A_CONV_CH:2 * A_CONV_CH]
    b2 = buf_ref[:, 2 * A_CONV_CH:3 * A_CONV_CH]
    conv = b0 * w[0:1] + b1 * w[1:2] + b2 * w[2:3] + u * w[3:4]
    nbuf_ref[:, 0:A_CONV_CH] = b1
    nbuf_ref[:, A_CONV_CH:2 * A_CONV_CH] = b2
    nbuf_ref[:, 2 * A_CONV_CH:3 * A_CONV_CH] = u
    act = _silu(conv)
    g = gates_ref[...]
    eg_all = jnp.exp(-jnp.exp(prow_ref[0:1, :]) * _softplus(g + prow_ref[1:2, :]))
    beta_all = jax.nn.sigmoid(g)
    cosf = cos_ref[...]
    sinf = sin_ref[...]
    base = HEADS * DK
    rows = lax.broadcasted_iota(jnp.int32, (tb, DK), 0)
    items = [(t, h) for t in range(tb) for h in range(HEADS)]

    qs = [_l2(act[:, h * DK:(h + 1) * DK]) * (DK ** -0.5) for h in range(HEADS)]
    ks = [_l2(act[:, (HEADS + h) * DK:(HEADS + h + 1) * DK]) for h in range(HEADS)]
    k_s = [jnp.zeros((tb, A_DV), F32) for _ in range(HEADS)]
    for t, h in items:
        k_s[h] = _row_select(rows, t, _mm(ks[h], sd_ref[t, h]), k_s[h])
    v_new = []
    for h in range(HEADS):
        v = act[:, (2 * HEADS + h) * DK:(2 * HEADS + h + 1) * DK]
        v_new.append(beta_all[:, HEADS + h:HEADS + h + 1] * (v - eg_all[:, h:h + 1] * k_s[h]))
    for t, h in items:
        nsd_ref[t, h] = (sd_ref[t, h] * eg_all[t:t + 1, h:h + 1]
                         + _mm_tn(jnp.where(rows == t, ks[h], 0.0), v_new[h]))
    o_a = [jnp.zeros((tb, A_DV), F32) for _ in range(HEADS)]
    for t, h in items:
        o_a[h] = _row_select(rows, t, _mm(qs[h], nsd_ref[t, h]), o_a[h])
    for h in range(HEADS):
        ga = rest_ref[:, h * DK:(h + 1) * DK]
        mixed_ref[:, h * DK:(h + 1) * DK] = _rms(o_a[h], ggain_ref[...]) * _silu(ga)

    qrs, krs = [], []
    for h in range(HEADS):
        qb = rest_ref[:, base + h * DK:base + (h + 1) * DK]
        kb = rest_ref[:, 2 * base + h * DK:2 * base + (h + 1) * DK]
        qrs.append(qb * cosf + pltpu.roll(qb, DK // 2, 1) * sinf)
        krs.append((kb * cosf + pltpu.roll(kb, DK // 2, 1) * sinf) * (DK ** -0.5))
    for t, h in items:
        vb = rest_ref[:, 3 * base + h * DK:3 * base + (h + 1) * DK]
        nsr_ref[t, h] = (sr_ref[t, h] * math.exp(LOG_GAMMA[h])
                         + _mm_tn(jnp.where(rows == t, krs[h], 0.0), vb))
    o_b = [jnp.zeros((tb, DK), F32) for _ in range(HEADS)]
    for t, h in items:
        o_b[h] = _row_select(rows, t, _mm(qrs[h], nsr_ref[t, h]), o_b[h])
    for h in range(HEADS):
        gb = rest_ref[:, 4 * base + h * DK:4 * base + (h + 1) * DK]
        mixed_ref[:, base + h * DK:base + (h + 1) * DK] = _rms(o_b[h], rgain_ref[...]) * _silu(gb)


def _mix_ab_sample(qkv, rest, gates, conv_buf, s_delta, s_ret, w_conv, a_log, dt_bias, gdn_gain, ret_gain):
    nb = qkv.shape[0]
    tb = SAMPLE_TB
    assert nb % tb == 0
    cosf, sinf = _rope_tables(PAST_LEN, 1)
    prow, _ = _gate_params(a_log, dt_bias)
    tok = lambda b: (b, 0)
    tok4 = lambda b: (b, 0, 0, 0)
    const = lambda b: (0, 0)
    n_buf = (CONV_W - 1) * A_CONV_CH
    mixed, nbuf, nsd, nsr = pl.pallas_call(
        _step_ab_kernel,
        grid=(nb // tb,),
        in_specs=[pl.BlockSpec((tb, A_CONV_CH), tok),
                  pl.BlockSpec((tb, rest.shape[1]), tok),
                  pl.BlockSpec((tb, 128), tok),
                  pl.BlockSpec((1, DK), const),
                  pl.BlockSpec((1, DK), const),
                  pl.BlockSpec((CONV_W, A_CONV_CH), const),
                  pl.BlockSpec((8, 128), const),
                  pl.BlockSpec((1, A_DV), const),
                  pl.BlockSpec((1, DK), const),
                  pl.BlockSpec((tb, n_buf), tok),
                  pl.BlockSpec((tb, HEADS, DK, A_DV), tok4),
                  pl.BlockSpec((tb, HEADS, DK, DK), tok4)],
        out_specs=[pl.BlockSpec((tb, 2 * HEADS * DK), tok),
                   pl.BlockSpec((tb, n_buf), tok),
                   pl.BlockSpec((tb, HEADS, DK, A_DV), tok4),
                   pl.BlockSpec((tb, HEADS, DK, DK), tok4)],
        out_shape=[jax.ShapeDtypeStruct((nb, 2 * HEADS * DK), F32),
                   jax.ShapeDtypeStruct((nb, n_buf), F32),
                   jax.ShapeDtypeStruct((nb, HEADS, DK, A_DV), F32),
                   jax.ShapeDtypeStruct((nb, HEADS, DK, DK), F32)],
        compiler_params=_cparams("parallel"),
        name="mix_ab_sample",
    )(qkv, rest, gates, cosf, sinf, w_conv, prow, gdn_gain.reshape(1, A_DV), ret_gain.reshape(1, DK),
      conv_buf.reshape(nb, n_buf), s_delta, s_ret)
    return mixed, nbuf.reshape(nb, CONV_W - 1, A_CONV_CH), nsd, nsr


def _step_c_kernel(qk_ref, v_ref, opre_ref, gates_ref, brow_ref, gain_ref, cm_ref, nv_ref, m_ref,
                   h_ref, ncm_ref, nnv_ref, nm_ref):
    tb = qk_ref.shape[0]
    g = gates_ref[...] + brow_ref[0:1, :]
    logf = _log_sigmoid(g)
    rows = lax.broadcasted_iota(jnp.int32, (tb, DK), 0)
    items = [(t, h) for t in range(tb) for h in range(HEADS)]
    qs, kws, fs, ms_new, nvs_new = [], [], [], [], []
    for h in range(HEADS):
        i_pre = g[:, h:h + 1]
        q = qk_ref[:, h * DK:(h + 1) * DK]
        k = qk_ref[:, (HEADS + h) * DK:(HEADS + h + 1) * DK] * (DK ** -0.5)
        inter = logf[:, HEADS + h:HEADS + h + 1] + m_ref[:, h:h + 1]
        m_new = jnp.maximum(inter, i_pre)
        f_state = jnp.exp(inter - m_new)
        kw = k * jnp.exp(i_pre - m_new)
        nv_new = nv_ref[:, h * DK:(h + 1) * DK] * f_state + kw
        nnv_ref[:, h * DK:(h + 1) * DK] = nv_new
        nm_ref[:, h:h + 1] = m_new
        qs.append(q)
        kws.append(kw)
        fs.append(f_state)
        ms_new.append(m_new)
        nvs_new.append(nv_new)
    for t, h in items:
        v = v_ref[:, h * C_DV:(h + 1) * C_DV]
        ncm_ref[t, h] = (cm_ref[t, h] * fs[h][t:t + 1, :]
                         + _mm_tn(jnp.where(rows == t, kws[h], 0.0), v))
    rows_v = lax.broadcasted_iota(jnp.int32, (tb, C_DV), 0)
    nums = [jnp.zeros((tb, C_DV), F32) for _ in range(HEADS)]
    for t, h in items:
        nums[h] = _row_select(rows_v, t, _mm(qs[h], ncm_ref[t, h]), nums[h])
    for h in range(HEADS):
        den = jnp.sum(qs[h] * nvs_new[h], axis=1, keepdims=True)
        hh = nums[h] / jnp.maximum(jnp.abs(den), jnp.exp(-ms_new[h]))
        op = opre_ref[:, h * C_DV:(h + 1) * C_DV]
        h_ref[:, h * C_DV:(h + 1) * C_DV] = _rms(hh, gain_ref[...]) * jax.nn.sigmoid(op)


def _mix_c_sample(qk, v, opre, gates, s_c, s_n, s_m, b_gate, gain):
    nb = qk.shape[0]
    tb = SAMPLE_TB
    assert nb % tb == 0
    brow, _ = _bias_params(b_gate)
    tok = lambda b: (b, 0)
    tok4 = lambda b: (b, 0, 0, 0)
    const = lambda b: (0, 0)
    hh, ncm, nnv, nm = pl.pallas_call(
        _step_c_kernel,
        grid=(nb // tb,),
        in_specs=[pl.BlockSpec((tb, 2 * HEADS * DK), tok),
                  pl.BlockSpec((tb, HEADS * C_DV), tok),
                  pl.BlockSpec((tb, HEADS * C_DV), tok),
                  pl.BlockSpec((tb, 128), tok),
                  pl.BlockSpec((8, 128), const),
                  pl.BlockSpec((1, C_DV), const),
                  pl.BlockSpec((tb, HEADS, DK, C_DV), tok4),
                  pl.BlockSpec((tb, HEADS * DK), tok),
                  pl.BlockSpec((tb, HEADS), tok)],
        out_specs=[pl.BlockSpec((tb, HEADS * C_DV), tok),
                   pl.BlockSpec((tb, HEADS, DK, C_DV), tok4),
                   pl.BlockSpec((tb, HEADS * DK), tok),
                   pl.BlockSpec((tb, HEADS), tok)],
        out_shape=[jax.ShapeDtypeStruct((nb, HEADS * C_DV), F32),
                   jax.ShapeDtypeStruct((nb, HEADS, DK, C_DV), F32),
                   jax.ShapeDtypeStruct((nb, HEADS * DK), F32),
                   jax.ShapeDtypeStruct((nb, HEADS), F32)],
        compiler_params=_cparams("parallel"),
        name="mix_c_sample",
    )(qk, v, opre, gates, brow, gain.reshape(1, C_DV), s_c, s_n.reshape(nb, HEADS * DK), s_m)
    return hh, ncm, nnv.reshape(nb, HEADS, DK), nm


C_COLS = ((0, 2 * HEADS * DK), (2 * HEADS * DK, HEADS * C_DV), (2 * HEADS * DK + HEADS * C_DV, HEADS * C_DV),
          (2 * HEADS * DK + 2 * HEADS * C_DV, 2 * HEADS))


def kernel(x_prompt, x_sample, state_conv_a, state_delta_a, state_ret_b, state_mlstm_C, state_mlstm_n, state_mlstm_m,
           norm_mix_a, w_in_a, w_conv_a, a_log, dt_bias, gdn_gain, ret_gain, w_out_a,
           norm_mix_c, w_in_c, b_gate_c, mlstm_gain, w_out_c, norm_ffn, w_up, w_down, final_gain):
    batch, seq, _ = x_prompt.shape
    n_s = x_sample.shape[0] * x_sample.shape[1]
    tm = PROMPT_TM
    w_a = w_in_a[0].astype(BF16)
    w_c = w_in_c[0].astype(BF16)
    w_ups = w_up.astype(BF16)
    w_downs = w_down.astype(BF16)
    h = x_prompt.reshape(batch * seq, D_MODEL)
    h_s = x_sample.reshape(n_s, D_MODEL)

    (mixed, conv, sd, sr), (qkv_s, rest_s, gates_s) = _layer0_prompt(
        h, h_s, norm_mix_a[0], w_a, w_conv_a[0], a_log[0], dt_bias[0], gdn_gain[0], ret_gain[0], batch, seq)
    mixed_s, conv_s, sd_s, sr_s = _mix_ab_sample(qkv_s, rest_s, gates_s, state_conv_a[0], state_delta_a[0],
                                                 state_ret_b[0], w_conv_a[0], a_log[0], dt_bias[0], gdn_gain[0],
                                                 ret_gain[0])
    h, h_s = _out_mlp(h, mixed, h_s, mixed_s, w_out_a[0].astype(BF16), norm_ffn[0], w_ups, w_downs, 0, None, tm)

    (qk, v, opre, gates_c), (qk_s, v_s, opre_s, gates_cs) = _norm_proj(h, h_s, norm_mix_c[0], w_c, C_COLS, tm)
    hm, cm, nv, m = _mix_c_prompt(qk, v, opre, gates_c, batch, seq, b_gate_c[0], mlstm_gain[0])
    hm_s, cm_s, nv_s, m_s = _mix_c_sample(qk_s, v_s, opre_s, gates_cs, state_mlstm_C[0], state_mlstm_n[0],
                                          state_mlstm_m[0], b_gate_c[0], mlstm_gain[0])
    y, y_s = _out_mlp(h, hm, h_s, hm_s, w_out_c[0].astype(BF16), norm_ffn[1], w_ups, w_downs, 1, final_gain, tm)

    return (y.reshape(x_prompt.shape), y_s.reshape(x_sample.shape),
            conv[None], sd[None], sr[None], cm[None], nv[None], m[:, :, 0][None],
            conv_s[None], sd_s[None], sr_s[None], cm_s[None], nv_s[None], m_s[None])
```

```python
import functools
import math

import jax
import jax.numpy as jnp
import numpy as np
from jax import lax
from jax.experimental import pallas as pl
from jax.experimental.pallas import tpu as pltpu

F32 = jnp.float32
BF16 = jnp.bfloat16

D_MODEL = 1024
D_FF = 4 * D_MODEL
CHUNK = 64
EPS = 1e-6
NEG = -1e30
HEADS = 4
DK = 128
A_DV = 128
C_DV = 256
CONV_W = 4
A_CONV_CH = 3 * HEADS * DK
ROPE_BASE = 10000.0
PAST_LEN = 16384
LOG_GAMMA = tuple(math.log1p(-(2.0 ** (-5.0 - h))) for h in range(HEADS))

VMEM_LIMIT_BYTES = 56 * 1024 * 1024
PROMPT_TM = 512
AB_SUB = 8
C_SUB = 8
SAMPLE_TB = 8


def _cparams(*sem):
    return pltpu.CompilerParams(dimension_semantics=sem, vmem_limit_bytes=VMEM_LIMIT_BYTES)


def _mm(a, b):
    return jnp.dot(a.astype(BF16), b.astype(BF16), preferred_element_type=F32)


def _mm_nt(a, b):
    return lax.dot_general(a.astype(BF16), b.astype(BF16), (((1,), (1,)), ((), ())), preferred_element_type=F32)


def _mm_tn(a, b):
    return lax.dot_general(a.astype(BF16), b.astype(BF16), (((0,), (0,)), ((), ())), preferred_element_type=F32)


def _split(a):
    hi = a.astype(BF16)
    lo = (a - hi.astype(F32)).astype(BF16)
    return hi, lo


def _softplus(x):
    return jnp.maximum(x, 0.0) + jnp.log1p(jnp.exp(-jnp.abs(x)))


def _log_sigmoid(x):
    return -_softplus(-x)


def _silu(x):
    return x * jax.nn.sigmoid(x)


def _rms(x, gain):
    return x * lax.rsqrt(jnp.mean(x * x, axis=-1, keepdims=True) + EPS) * gain


def _l2(t):
    return t * lax.rsqrt(jnp.sum(t * t, axis=-1, keepdims=True) + EPS)


def _interleave(gens):
    gens = list(gens)
    while gens:
        alive = []
        for gen in gens:
            try:
                next(gen)
                alive.append(gen)
            except StopIteration:
                pass
        gens = alive


def _chunk_cumsum_cols(x, tri_c):
    n = x.shape[0] // CHUNK
    side = lambda a: jnp.concatenate([a[i * CHUNK:(i + 1) * CHUNK] for i in range(n)], axis=1)
    hi, lo = _split(x)
    out = (jnp.dot(tri_c, side(hi), preferred_element_type=F32)
           + jnp.dot(tri_c, side(lo), preferred_element_type=F32))
    return jnp.concatenate([out[:, i * 128:(i + 1) * 128] for i in range(n)], axis=0)


def _lockstep(gens):
    gens = list(gens)
    while gens:
        alive = []
        for gen in gens:
            try:
                next(gen)
                alive.append(gen)
            except StopIteration:
                pass
        gens = alive
        if gens:
            yield


def _head_block_rows(x):
    c, n = x.shape
    t = n // HEADS
    z = jnp.zeros((c, t), x.dtype)
    return jnp.concatenate(
        [jnp.concatenate([x[:, h * t:(h + 1) * t] if g == h else z for g in range(HEADS)], axis=1)
         for h in range(HEADS)], axis=0)


def _head_block_diag(y):
    c, n = y.shape
    per_tile = 128 // c
    assert n == HEADS * c and 128 % c == 0 and HEADS % per_tile == 0
    lane = lax.broadcasted_iota(jnp.int32, (c, 128), 1)
    z = jnp.zeros((c, 128), y.dtype)
    blocks = []
    for h in range(HEADS):
        t = h // per_tile
        lo = (h % per_tile) * c
        kept = jnp.where((lane >= lo) & (lane < lo + c), y[:, t * 128:(t + 1) * 128], z)
        blocks.append(jnp.concatenate([kept if g == t else z for g in range(n // 128)], axis=1))
    return jnp.concatenate(blocks, axis=0)


def _wide_mm(x, y):
    return jnp.dot(x.astype(BF16), _head_block_diag(y.astype(BF16)), preferred_element_type=F32)


def _tri_inv_wide(ns):
    c, wd = ns[0].shape
    ii = lax.broadcasted_iota(jnp.int32, (c, wd), 0)
    jl = lax.broadcasted_iota(jnp.int32, (c, wd), 1) & (c - 1)
    eye = (ii == jl).astype(F32)
    ts = [eye - jnp.where((ii >> 1) == (jl >> 1), n, 0.0) for n in ns]
    for lvl in range(1, int(math.log2(c))):
        off = ((ii >> (lvl + 1)) == (jl >> (lvl + 1))) & ((ii >> lvl) != (jl >> lvl))
        ys = [_wide_mm(jnp.where(off, n, 0.0), t) for n, t in zip(ns, ts)]
        yield
        ts = [t - _wide_mm(t, y) for t, y in zip(ts, ys)]
        yield
    return [t - eye for t in ts]


def _out_width(n):
    return max(n, 128)


def _project_rows(x_ref, g_ref, pieces, out_refs):
    xn = _rms(x_ref[...], g_ref[...]).astype(BF16)
    for (w_ref, col0, n), o_ref in zip(pieces, out_refs):
        res = jnp.dot(xn, w_ref[:, col0:col0 + n], preferred_element_type=F32)
        if n < _out_width(n):
            res = jnp.concatenate([res, jnp.zeros((res.shape[0], _out_width(n) - n), F32)], axis=1)
        o_ref[...] = res


def _norm_proj_kernel(x_ref, xs_ref, g_ref, w_ref, *out_refs, cols):
    pieces = [(w_ref, c0, n) for c0, n in cols]
    _project_rows(x_ref, g_ref, pieces, out_refs[:len(cols)])

    @pl.when(pl.program_id(0) == 0)
    def _():
        _project_rows(xs_ref, g_ref, pieces, out_refs[len(cols):])


def _norm_proj(x, x_s, gain, w, cols, tm):
    t, ts = x.shape[0], x_s.shape[0]
    assert t % tm == 0
    widths = [_out_width(n) for _, n in cols]
    row = lambda i: (i, 0)
    const = lambda i: (0, 0)
    outs = pl.pallas_call(
        functools.partial(_norm_proj_kernel, cols=cols),
        grid=(t // tm,),
        in_specs=[pl.BlockSpec((tm, D_MODEL), row), pl.BlockSpec((ts, D_MODEL), const),
                  pl.BlockSpec((1, D_MODEL), const),
                  pl.BlockSpec(w.shape, const, pipeline_mode=pl.Buffered(1))],
        out_specs=[pl.BlockSpec((tm, n), row) for n in widths] + [pl.BlockSpec((ts, n), const) for n in widths],
        out_shape=[jax.ShapeDtypeStruct((t, n), F32) for n in widths]
                  + [jax.ShapeDtypeStruct((ts, n), F32) for n in widths],
        compiler_params=_cparams("arbitrary"),
        name="norm_proj",
    )(x, x_s, gain.reshape(1, D_MODEL), w)
    return outs[:len(widths)], outs[len(widths):]


def _layer0_prompt_kernel(x_ref, xs_ref, g_ref, w_ref, wconv_ref, cos_ref, sin_ref,
                          prow_ref, pcol_ref, tri_ref, spread_ref, ggain_ref, rgain_ref,
                          mixed_ref, conv_ref, sd_ref, sr_ref, qkvs_ref, rests_ref, gatess_ref,
                          xp_ref, raw_ref, wrest_ref, qkv_ref, rest_ref, gates_ref, *, tiles_per_seq, n_sub):
    tm = x_ref.shape[0]
    hd = HEADS * DK
    n_gate = 2 * HEADS
    gate_piece = (w_ref, A_CONV_CH, n_gate)

    @pl.when(pl.program_id(0) == 0)
    def _():
        wrest_ref[...] = w_ref[:, A_CONV_CH + n_gate:A_CONV_CH + n_gate + 5 * hd]
        _project_rows(xs_ref, g_ref, [(w_ref, 0, A_CONV_CH), (wrest_ref, 0, 5 * hd), gate_piece],
                      (qkvs_ref, rests_ref, gatess_ref))

    xn = _rms(x_ref[...], g_ref[...]).astype(BF16)

    @pl.when(pl.program_id(0) % tiles_per_seq == 0)
    def _():
        xp_ref[0:8, :] = jnp.zeros((8, A_CONV_CH), F32)
        sd_ref[...] = jnp.zeros_like(sd_ref)
        sr_ref[...] = jnp.zeros_like(sr_ref)

    w = wconv_ref[...]
    base = A_CONV_CH

    def conv_part(part, slot, r0, r1):
        c0, c1 = part * hd, (part + 1) * hd
        raw = raw_ref[slot, r0:r1, :]
        xp_ref[8 + r0:8 + r1, c0:c1] = raw
        conv = (xp_ref[5 + r0:5 + r1, c0:c1] * w[0:1, c0:c1] + xp_ref[6 + r0:6 + r1, c0:c1] * w[1:2, c0:c1]
                + xp_ref[7 + r0:7 + r1, c0:c1] * w[2:3, c0:c1] + raw * w[3:4, c0:c1])
        if r1 == tm:
            rows = r1 - r0
            xp_ref[0:8, c0:c1] = raw[rows - 8:rows, :]
            conv_ref[0, :, c0:c1] = raw[rows - 3:rows, :]
        act = _silu(conv)
        if part == 2:
            qkv_ref[r0:r1, c0:c1] = act
        else:
            scale = DK ** -0.5 if part == 0 else 1.0
            for h in range(HEADS):
                qkv_ref[r0:r1, c0 + h * DK:c0 + (h + 1) * DK] = _l2(act[:, h * DK:(h + 1) * DK]) * scale

    def rest_part(part, slot, r0, r1):
        c0, c1 = part * hd, (part + 1) * hd
        raw = raw_ref[slot, r0:r1, :]
        if part in (0, 4):
            rest_ref[r0:r1, c0:c1] = _silu(raw)
        elif part == 3:
            rest_ref[r0:r1, c0:c1] = raw
        else:
            scale = 1.0 if part == 1 else DK ** -0.5
            for h in range(HEADS):
                t = raw[:, h * DK:(h + 1) * DK]
                rest_ref[r0:r1, c0 + h * DK:c0 + (h + 1) * DK] = (
                    (t * cos_ref[r0:r1, :] + pltpu.roll(t, DK // 2, 1) * sin_ref[r0:r1, :]) * scale)

    def gates_part(_, slot, r0, r1):
        gates_ref[r0:r1, :] = jnp.concatenate(
            [raw_ref[slot, r0:r1, 0:n_gate], jnp.zeros((r1 - r0, 128 - n_gate), F32)], axis=1)

    groups = [(w_ref, 0, hd, functools.partial(conv_part, 0)),
              (wrest_ref, 3 * hd, hd, functools.partial(rest_part, 3)),
              (w_ref, hd, hd, functools.partial(conv_part, 1)),
              (wrest_ref, 0, hd, functools.partial(rest_part, 0)),
              (w_ref, 2 * hd, hd, functools.partial(conv_part, 2)),
              (wrest_ref, hd, hd, functools.partial(rest_part, 1)),
              gate_piece + (functools.partial(gates_part, 0),),
              (wrest_ref, 2 * hd, hd, functools.partial(rest_part, 2)),
              (wrest_ref, 4 * hd, hd, functools.partial(rest_part, 4))]

    def project(n, slot):
        wg_ref, col0, width, _ = groups[n]
        for j in range(0, width, 256):
            wj = min(256, width - j)
            raw_ref[slot, :, j:j + wj] = jnp.dot(xn, wg_ref[:, col0 + j:col0 + j + wj],
                                                 preferred_element_type=F32)
            yield

    def epilogue(n, slot):
        for r0 in range(0, tm, 128):
            groups[n][3](slot, r0, min(r0 + 128, tm))
            yield

    def projection_track(order):
        yield from project(order[0], 0)
        for idx, n in enumerate(order):
            nxt = [project(order[idx + 1], (idx + 1) % 2)] if idx + 1 < len(order) else []
            yield from _lockstep([epilogue(n, idx % 2)] + nxt)
            yield

    _interleave([projection_track([6, 0, 2, 4])])
    gdn_stages, delta_track, ret_track, finish = _mix_ab_tracks(
        qkv_ref, rest_ref, gates_ref, prow_ref, pcol_ref, tri_ref, spread_ref, ggain_ref, rgain_ref,
        mixed_ref, sd_ref, sr_ref, n_sub)
    _interleave(gdn_stages)

    def filler_track():
        yield from projection_track([1, 3, 5, 7, 8])
        yield from ret_track()

    _interleave([delta_track(), filler_track()])
    finish()


def _layer0_prompt(x, x_s, gain, w, w_conv, a_log, dt_bias, gdn_gain, ret_gain, batch, seq):
    t, ts = x.shape[0], x_s.shape[0]
    tm = AB_SUB * CHUNK
    assert t % tm == 0 and seq % tm == 0
    tiles_per_seq = seq // tm
    cosf, sinf = _rope_tables(0, seq)
    prow, pcol = _gate_params(a_log, dt_bias)
    tri = _chunk_tri(tm)
    spread = _head_spread_ab()
    row = lambda i: (i, 0)
    const = lambda i: (0, 0)
    pos = lambda i: (i % tiles_per_seq, 0)
    per_seq3 = lambda i: (i // tiles_per_seq, 0, 0)
    per_seq4 = lambda i: (i // tiles_per_seq, 0, 0, 0)
    n_rest = 5 * HEADS * DK
    outs = pl.pallas_call(
        functools.partial(_layer0_prompt_kernel, tiles_per_seq=tiles_per_seq, n_sub=AB_SUB),
        grid=(t // tm,),
        in_specs=[pl.BlockSpec((tm, D_MODEL), row),
                  pl.BlockSpec((ts, D_MODEL), const),
                  pl.BlockSpec((1, D_MODEL), const),
                  pl.BlockSpec(w.shape, const, pipeline_mode=pl.Buffered(1)),
                  pl.BlockSpec((CONV_W, A_CONV_CH), const),
                  pl.BlockSpec((tm, DK), pos),
                  pl.BlockSpec((tm, DK), pos),
                  pl.BlockSpec((8, 128), const),
                  pl.BlockSpec((8, 128), const),
                  pl.BlockSpec(tri.shape, const),
                  pl.BlockSpec(spread.shape, const),
                  pl.BlockSpec((1, A_DV), const),
                  pl.BlockSpec((1, DK), const)],
        out_specs=[pl.BlockSpec((tm, 2 * HEADS * DK), row),
                   pl.BlockSpec((1, CONV_W - 1, A_CONV_CH), per_seq3),
                   pl.BlockSpec((1, HEADS, DK, A_DV), per_seq4),
                   pl.BlockSpec((1, HEADS, DK, DK), per_seq4),
                   pl.BlockSpec((ts, A_CONV_CH), const),
                   pl.BlockSpec((ts, n_rest), const),
                   pl.BlockSpec((ts, 128), const)],
        out_shape=[jax.ShapeDtypeStruct((t, 2 * HEADS * DK), BF16),
                   jax.ShapeDtypeStruct((batch, CONV_W - 1, A_CONV_CH), F32),
                   jax.ShapeDtypeStruct((batch, HEADS, DK, A_DV), F32),
                   jax.ShapeDtypeStruct((batch, HEADS, DK, DK), F32),
                   jax.ShapeDtypeStruct((ts, A_CONV_CH), F32),
                   jax.ShapeDtypeStruct((ts, n_rest), F32),
                   jax.ShapeDtypeStruct((ts, 128), F32)],
        scratch_shapes=[pltpu.VMEM((tm + 8, A_CONV_CH), F32), pltpu.VMEM((2, tm, HEADS * DK), F32),
                        pltpu.VMEM((D_MODEL, n_rest), BF16),
                        pltpu.VMEM((tm, A_CONV_CH), F32), pltpu.VMEM((tm, n_rest), F32), pltpu.VMEM((tm, 128), F32)],
        compiler_params=_cparams("arbitrary"),
        name="layer0_prompt",
    )(x, x_s, gain.reshape(1, D_MODEL), w, w_conv, cosf, sinf, prow, pcol, tri, spread,
      gdn_gain.reshape(1, A_DV), ret_gain.reshape(1, DK))
    return outs[:4], outs[4:]


def _out_mlp_kernel(h_ref, m_ref, hs_ref, ms_ref, wout_ref, gffn_ref, wup_ref, wdown_ref, *rest, final):
    if final:
        gfin_ref, o_ref, os_ref = rest
    else:
        o_ref, os_ref = rest

    def block(h_r, m_r, o_r):
        h = h_r[...] + jnp.dot(m_r[...].astype(BF16), wout_ref[...], preferred_element_type=F32)
        xn = _rms(h, gffn_ref[...]).astype(BF16)
        acc = h
        step = 1024
        for j in range(D_FF // step):
            hid = jnp.dot(xn, wup_ref[:, j * step:(j + 1) * step], preferred_element_type=F32)
            hid = jnp.maximum(hid, 0.0)
            acc = acc + jnp.dot((hid * hid).astype(BF16), wdown_ref[j * step:(j + 1) * step, :],
                                preferred_element_type=F32)
        if final:
            acc = _rms(acc, gfin_ref[...])
        o_r[...] = acc

    block(h_ref, m_ref, o_ref)

    @pl.when(pl.program_id(0) == 0)
    def _():
        block(hs_ref, ms_ref, os_ref)


def _out_mlp(h, mix, h_s, mix_s, w_out, g_ffn, w_up, w_down, layer, g_final, tm):
    t, ts = h.shape[0], h_s.shape[0]
    assert t % tm == 0
    final = g_final is not None
    row = lambda i: (i, 0)
    const = lambda i: (0, 0)
    this_layer = lambda i: (layer, 0, 0)
    once = pl.Buffered(1)
    in_specs = [pl.BlockSpec((tm, D_MODEL), row), pl.BlockSpec((tm, mix.shape[1]), row),
                pl.BlockSpec((ts, D_MODEL), const), pl.BlockSpec((ts, mix_s.shape[1]), const),
                pl.BlockSpec(w_out.shape, const, pipeline_mode=once), pl.BlockSpec((1, D_MODEL), const),
                pl.BlockSpec((None,) + w_up.shape[1:], this_layer, pipeline_mode=once),
                pl.BlockSpec((None,) + w_down.shape[1:], this_layer, pipeline_mode=once)]
    args = [h, mix, h_s, mix_s, w_out, g_ffn.reshape(1, D_MODEL), w_up, w_down]
    if final:
        in_specs.append(pl.BlockSpec((1, D_MODEL), const))
        args.append(g_final.reshape(1, D_MODEL))
    return pl.pallas_call(
        functools.partial(_out_mlp_kernel, final=final),
        grid=(t // tm,),
        in_specs=in_specs,
        out_specs=[pl.BlockSpec((tm, D_MODEL), row), pl.BlockSpec((ts, D_MODEL), const)],
        out_shape=[jax.ShapeDtypeStruct((t, D_MODEL), F32), jax.ShapeDtypeStruct((ts, D_MODEL), F32)],
        compiler_params=_cparams("arbitrary"),
        name="out_mlp",
    )(*args)


def _chunk_masks(c):
    ii = lax.broadcasted_iota(jnp.int32, (c, c), 0)
    jj = lax.broadcasted_iota(jnp.int32, (c, c), 1)
    return ii, jj


def _mix_ab_tracks(qkv_ref, rest_ref, gates_ref, prow_ref, pcol_ref, tri_ref, spread_ref,
                   ggain_ref, rgain_ref, mixed_ref, sd_ref, sr_ref, n_sub):
    c = CHUNK
    lb = n_sub * c

    g = gates_ref[...]
    g_t = g.T
    neg_a_row = -jnp.exp(prow_ref[0:1, :])
    dt_row = prow_ref[1:2, :]
    neg_a_col = -jnp.exp(pcol_ref[0:HEADS, 0:1])
    dt_col = pcol_ref[0:HEADS, 1:2]
    la_cols = neg_a_row * _softplus(g + dt_row)
    beta_cols = jax.nn.sigmoid(g)
    la_rows = neg_a_col * _softplus(g_t[0:HEADS, :] + dt_col)

    tri = tri_ref[...]
    g_cols = _chunk_cumsum_cols(la_cols, tri_ref[0:c, 0:c])
    lr_hi, lr_lo = _split(jnp.concatenate([la_rows, jnp.zeros_like(la_rows)], axis=0))
    nt = lambda x, y: lax.dot_general(x, y, (((1,), (1,)), ((), ())), preferred_element_type=F32)
    g_rows = nt(lr_hi, tri) + nt(lr_lo, tri)
    lane = lax.broadcasted_iota(jnp.int32, (lb, 128), 1)
    x_hi, x_lo = _split(jnp.where(lane < HEADS, g_cols, beta_cols))
    spread = (jnp.dot(x_hi, spread_ref[...], preferred_element_type=F32)
              + jnp.dot(x_lo, spread_ref[...], preferred_element_type=F32))
    hd = HEADS * DK
    wd = HEADS * c
    g_wide = spread[:, 0:hd]
    beta_wide = spread[:, hd:2 * hd]
    g_half = spread[:, 2 * hd:2 * hd + wd]
    eg_wide = jnp.exp(g_wide)

    ii = lax.broadcasted_iota(jnp.int32, (c, wd), 0)
    jl = lax.broadcasted_iota(jnp.int32, (c, wd), 1) & (c - 1)
    causal = ii >= jl
    strict = ii > jl
    lane_hd = lax.broadcasted_iota(jnp.int32, (1, hd), 1)
    lane_wd = lax.broadcasted_iota(jnp.int32, (1, wd), 1)
    lg_hd = jnp.full((1, hd), LOG_GAMMA[HEADS - 1], F32)
    lg_wd = jnp.full((1, wd), LOG_GAMMA[HEADS - 1], F32)
    for h in range(HEADS - 2, -1, -1):
        lg_hd = jnp.where(lane_hd < (h + 1) * DK, LOG_GAMMA[h], lg_hd)
        lg_wd = jnp.where(lane_wd < (h + 1) * c, LOG_GAMMA[h], lg_wd)
    pos_col = lax.broadcasted_iota(jnp.int32, (c, hd), 0).astype(F32)
    ret_dec = jnp.exp(jnp.where(causal, (ii - jl).astype(F32) * lg_wd, NEG))
    ret_q_scale = jnp.exp((pos_col + 1.0) * lg_hd)
    ret_k_scale = jnp.exp((float(c - 1) - pos_col) * lg_hd)

    ggain = ggain_ref[...]
    rgain = rgain_ref[...]
    base = hd
    tile = lambda x, h: x[:, h * DK:(h + 1) * DK]
    s_delta = [sd_ref[0, h] for h in range(HEADS)]
    s_ret = [sr_ref[0, h] for h in range(HEADS)]

    chunks = list(range(n_sub))
    items = [(ci, h) for ci in chunks for h in range(HEADS)]
    rng = lambda ci: (ci * c, (ci + 1) * c)
    pre = {it: {} for it in items}
    pre_b = {it: {} for it in items}
    wide = {ci: {} for ci in chunks}
    nt = lambda x, y: lax.dot_general(x, y, (((1,), (1,)), ((), ())), preferred_element_type=F32)

    def gdn_stage(ci):
        w_ = wide[ci]
        r0, r1 = rng(ci)
        q = qkv_ref[r0:r1, 0:hd]
        k = qkv_ref[r0:r1, hd:2 * hd]
        beta = beta_wide[r0:r1]
        g_col = g_wide[r0:r1]
        kb = k * beta
        g_row = jnp.concatenate([g_rows[h:h + 1, r0:r1] for h in range(HEADS)], axis=1)
        dec_causal = jnp.exp(jnp.where(causal, g_half[r0:r1] - g_row, NEG))
        prod = nt(jnp.concatenate([q, kb], axis=0).astype(BF16), _head_block_rows(k.astype(BF16)))
        yield
        w_['n'] = prod[c:2 * c] * jnp.where(strict, dec_causal, 0.0)
        w_['qk'] = prod[0:c] * dec_causal
        yield
        v = qkv_ref[r0:r1, 2 * hd:3 * hd]
        eg = eg_wide[r0:r1]
        g_last = g_col[c - 1:c, :]
        rhs_u, rhs_w, qd = v * beta, kb * eg, q * eg
        kd = k * jnp.exp(g_last - g_col)
        gl = jnp.exp(g_last)
        for h in range(HEADS):
            pre[ci, h].update(rhs=jnp.concatenate([tile(rhs_u, h), tile(rhs_w, h)], axis=1), qd=tile(qd, h),
                              kd=tile(kd, h), gl=tile(gl, h))

    def ret_stage(ci):
        r0, r1 = rng(ci)
        qr = rest_ref[r0:r1, base:2 * base]
        kr = rest_ref[r0:r1, 2 * base:3 * base]
        vb = rest_ref[r0:r1, 3 * base:4 * base]
        qk = nt(qr.astype(BF16), _head_block_rows(kr.astype(BF16))) * ret_dec
        qd = qr * ret_q_scale
        kd = kr * ret_k_scale
        yield
        for a in range(HEADS // 2):
            h0, h1 = 2 * a, 2 * a + 1
            z = jnp.zeros((c, DK), F32)
            rhs = jnp.concatenate([jnp.concatenate([tile(vb, h0), z], axis=1),
                                   jnp.concatenate([z, tile(vb, h1)], axis=1)], axis=0)
            intra = _mm(qk[:, a * 2 * c:(a + 1) * 2 * c], rhs)
            pre_b[ci, h0]['intra'] = intra[:, 0:DK]
            pre_b[ci, h1]['intra'] = intra[:, DK:2 * DK]
        yield
        for h in range(HEADS):
            pre_b[ci, h]['qd'] = tile(qd, h)
            pre_b[ci, h]['kv'] = _mm_tn(tile(kd, h), tile(vb, h))

    def solve_stage():
        t_offs = yield from _tri_inv_wide([wide[ci]['n'] for ci in chunks])
        for ci, t_off in zip(chunks, t_offs):
            wide[ci]['t_off'] = t_off

    def pair_rhs(x0, x1):
        z = jnp.zeros_like(x0)
        return jnp.concatenate([jnp.concatenate([x0, z], axis=1), jnp.concatenate([z, x1], axis=1)], axis=0)

    def sol_stage(ci):
        w_ = wide[ci]
        for a in range(HEADS // 2):
            h0, h1 = 2 * a, 2 * a + 1
            r = _mm(w_['t_off'][:, a * 2 * c:(a + 1) * 2 * c], pair_rhs(pre[ci, h0]['rhs'], pre[ci, h1]['rhs']))
            pre[ci, h0]['sol'] = pre[ci, h0]['rhs'] + r[:, 0:2 * A_DV]
            pre[ci, h1]['sol'] = pre[ci, h1]['rhs'] + r[:, 2 * A_DV:4 * A_DV]
        yield
        for a in range(HEADS // 2):
            h0, h1 = 2 * a, 2 * a + 1
            r = _mm(w_['qk'][:, a * 2 * c:(a + 1) * 2 * c], pair_rhs(pre[ci, h0]['sol'], pre[ci, h1]['sol']))
            pre[ci, h0]['qs'] = r[:, 0:2 * A_DV]
            pre[ci, h1]['qs'] = r[:, 2 * A_DV:4 * A_DV]
        for h in range(HEADS):
            d = pre[ci, h]
            d['kts'] = _mm_tn(d['kd'], d['sol'])
        yield
        for h in range(HEADS):
            d = pre[ci, h]
            d['lhs'] = jnp.concatenate([d['kts'][:, A_DV:2 * A_DV], d['qd'] - d['qs'][:, A_DV:2 * A_DV]], axis=0)


    def ret_out_stage(ci, h):
        d = pre_b[ci, h]
        r0, r1 = rng(ci)
        o = _mm(d['qd'], d['s_in']) + d['intra']
        yield
        ms_ = jnp.mean(o * o, axis=-1, keepdims=True)
        yield
        gb = rest_ref[r0:r1, 4 * base + h * DK:4 * base + (h + 1) * DK]
        mixed_ref[r0:r1, base + h * DK:base + (h + 1) * DK] = (
            o * lax.rsqrt(ms_ + EPS) * rgain * gb).astype(mixed_ref.dtype)

    def delta_step(ci, h):
        d = pre[ci, h]
        r = _mm(d['lhs'], s_delta[h])
        yield
        d['o'] = r[DK:DK + c] + d['qs'][:, 0:A_DV]
        s_delta[h] = s_delta[h] * d['gl'] - r[0:DK] + d['kts'][:, 0:A_DV]

    def gdn_out_stage(ci, h):
        o = pre[ci, h]['o']
        r0, r1 = rng(ci)
        ms_ = jnp.mean(o * o, axis=-1, keepdims=True)
        yield
        ga = rest_ref[r0:r1, h * DK:(h + 1) * DK]
        mixed_ref[r0:r1, h * A_DV:(h + 1) * A_DV] = (
            o * lax.rsqrt(ms_ + EPS) * ggain * ga).astype(mixed_ref.dtype)

    def delta_track():
        yield from solve_stage()
        yield from _lockstep([sol_stage(ci) for ci in chunks])
        for ci in chunks:
            yield from _lockstep([delta_step(ci, h) for h in range(HEADS)])
            if ci > 0:
                yield from _lockstep([gdn_out_stage(ci - 1, h) for h in range(HEADS)])
        yield from _lockstep([gdn_out_stage(chunks[-1], h) for h in range(HEADS)])

    def ret_track():
        for ci in chunks:
            yield from ret_stage(ci)
            for h in range(HEADS):
                pre_b[ci, h]['s_in'] = s_ret[h]
                s_ret[h] = s_ret[h] * math.exp(c * LOG_GAMMA[h]) + pre_b[ci, h]['kv']
            yield
            yield from _lockstep([ret_out_stage(ci, h) for h in range(HEADS)])

    def finish():
        for h in range(HEADS):
            sd_ref[0, h] = s_delta[h]
            sr_ref[0, h] = s_ret[h]

    return [gdn_stage(ci) for ci in chunks], delta_track, ret_track, finish


def _rope_tables(first, count):
    half = DK // 2
    inv = ROPE_BASE ** (-np.arange(half, dtype=np.float64) / half)
    ang = (first + np.arange(count, dtype=np.float64))[:, None] * inv[None, :]
    cos, sin = np.cos(ang), np.sin(ang)
    return (jnp.asarray(np.concatenate([cos, cos], axis=-1), F32),
            jnp.asarray(np.concatenate([-sin, sin], axis=-1), F32))


def _chunk_tri(lb):
    i = np.arange(lb)[:, None]
    j = np.arange(lb)[None, :]
    return jnp.asarray((i >= j) & (i // CHUNK == j // CHUNK), BF16)


def _head_spread(groups):
    n = groups * HEADS
    src = np.arange(128)[:, None]
    dst = np.arange(n * 128)[None, :] // 128
    return jnp.asarray(src == dst, BF16)


def _head_spread_ab():
    src = np.arange(128)[:, None]
    full = np.arange(2 * HEADS * DK)[None, :] // DK
    half = np.arange(HEADS * CHUNK)[None, :] // CHUNK
    return jnp.asarray(np.concatenate([src == full, src == half], axis=1), BF16)


def _gate_params(a_log, dt_bias):
    prow = jnp.zeros((8, 128), F32).at[0, 0:HEADS].set(a_log).at[1, 0:HEADS].set(dt_bias)
    pcol = jnp.zeros((8, 128), F32).at[0:HEADS, 0].set(a_log).at[0:HEADS, 1].set(dt_bias)
    return prow, pcol


def _mix_c_kernel(qk_ref, v_ref, opre_ref, gates_ref, brow_ref, bcol_ref, tri_ref, spread_ref, gain_ref,
                  h_ref, cm_ref, nv_ref, m_ref, *, n_sub):
    c = CHUNK
    lb = n_sub * c
    step = pl.program_id(1)

    @pl.when(step == 0)
    def _():
        cm_ref[...] = jnp.zeros_like(cm_ref)
        nv_ref[...] = jnp.zeros_like(nv_ref)
        m_ref[...] = jnp.zeros_like(m_ref)

    items = [(ci, h) for ci in range(n_sub) for h in range(HEADS)]
    rng = lambda ci: (ci * c, (ci + 1) * c)
    pre = {it: {} for it in items}

    for ci, h in items:
        r0, r1 = rng(ci)
        q = qk_ref[r0:r1, h * DK:(h + 1) * DK]
        k = qk_ref[r0:r1, (HEADS + h) * DK:(HEADS + h + 1) * DK] * (DK ** -0.5)
        pre[ci, h].update(q=q, k=k, qk=_mm_nt(q, k))

    g = gates_ref[...] + brow_ref[0:1, :]
    g_t = gates_ref[...].T[0:2 * HEADS, :] + bcol_ref[0:2 * HEADS, 0:1]
    i_rows = g_t[0:HEADS, :]
    logf_rows = _log_sigmoid(g_t[HEADS:2 * HEADS, :])
    tri = tri_ref[...]
    b_cols = _chunk_cumsum_cols(_log_sigmoid(g), tri_ref[0:c, 0:c])
    lr_hi, lr_lo = _split(jnp.concatenate([logf_rows, jnp.zeros_like(logf_rows)], axis=0))
    nt = lambda x, y: lax.dot_general(x, y, (((1,), (1,)), ((), ())), preferred_element_type=F32)
    b_rows = nt(lr_hi, tri) + nt(lr_lo, tri)
    lane = lax.broadcasted_iota(jnp.int32, (lb, 128), 1)
    x_hi, x_lo = _split(jnp.where(lane < HEADS, g, b_cols))
    spread = (jnp.dot(x_hi, spread_ref[...], preferred_element_type=F32)
              + jnp.dot(x_lo, spread_ref[...], preferred_element_type=F32))
    i_wide = spread[:, 0:HEADS * 128]
    b_wide = spread[:, HEADS * 128:2 * HEADS * 128]

    ii, jj = _chunk_masks(c)
    causal = ii >= jj
    gain = gain_ref[...]
    wide2 = lambda x: jnp.concatenate([x, x], axis=-1)

    def gates_stage(ci, h):
        d = pre[ci, h]
        r0, r1 = rng(ci)
        b_col = b_wide[r0:r1, h * 128:(h + 1) * 128]
        i_col = i_wide[r0:r1, h * 128:(h + 1) * 128]
        b_last = b_col[c - 1:c, :]
        d_log = jnp.where(causal, b_col[:, 0:c] - b_rows[h:h + 1, r0:r1] + i_rows[h:h + 1, r0:r1], NEG)
        d.update(b_col=b_col, b_last=b_last, d_log=d_log, k_log=b_last - b_col + i_col)
        yield
        d['d_max'] = jnp.max(d_log, axis=1, keepdims=True)

    _interleave(gates_stage(ci, h) for ci, h in items)

    ms = [m_ref[0, h:h + 1, :] for h in range(HEADS)]
    for ci, h in items:
        d = pre[ci, h]
        inter = d['b_col'] + ms[h]
        m_row = jnp.maximum(inter, d['d_max'])
        m_new = m_row[c - 1:c, :]
        d.update(inter=inter, m_row=m_row, m_old=ms[h], m_new=m_new)
        ms[h] = m_new

    def weights_stage(ci, h):
        d = pre[ci, h]
        r0, r1 = rng(ci)
        d['w_inter'] = jnp.exp(d['inter'] - d['m_row'])
        d['f_state'] = jnp.exp(d['b_last'] + d['m_old'] - d['m_new'])
        yield
        d['w_intra'] = jnp.exp(d['d_log'] - d['m_row'][:, 0:c]) * d['qk']
        d['kw'] = d['k'] * jnp.exp(d['k_log'] - d['m_new'])
        yield
        v = v_ref[r0:r1, h * C_DV:(h + 1) * C_DV]
        d['intra'] = _mm(d['w_intra'], v)
        d['kv'] = _mm_tn(d['kw'], v)
        yield
        d['sum_intra'] = jnp.sum(d['w_intra'], axis=1, keepdims=True)
        d['sum_kw'] = jnp.sum(d['kw'], axis=0, keepdims=True)
        d['inv_floor'] = jnp.exp(-d['m_row'])

    _interleave(weights_stage(ci, h) for ci, h in items)

    cms = [cm_ref[0, h] for h in range(HEADS)]
    nvs = [nv_ref[0, h:h + 1, :] for h in range(HEADS)]
    for ci, h in items:
        d = pre[ci, h]
        d['cm_in'], d['nv_in'] = cms[h], nvs[h]
        cms[h] = cms[h] * wide2(d['f_state']) + d['kv']
        nvs[h] = nvs[h] * d['f_state'] + d['sum_kw']

    def output_stage(ci, h):
        d = pre[ci, h]
        r0, r1 = rng(ci)
        qc = _mm(d['q'], d['cm_in'])
        qn = jnp.sum(d['q'] * d['nv_in'], axis=1, keepdims=True)
        yield
        num = wide2(d['w_inter']) * qc + d['intra']
        den = d['w_inter'] * qn + d['sum_intra']
        hh = num / wide2(jnp.maximum(jnp.abs(den), d['inv_floor']))
        yield
        ms_ = jnp.mean(hh * hh, axis=-1, keepdims=True)
        yield
        op = opre_ref[r0:r1, h * C_DV:(h + 1) * C_DV]
        h_ref[r0:r1, h * C_DV:(h + 1) * C_DV] = (
            hh * lax.rsqrt(ms_ + EPS) * gain * jax.nn.sigmoid(op)).astype(h_ref.dtype)

    _interleave(output_stage(ci, h) for ci, h in items)

    for h in range(HEADS):
        cm_ref[0, h] = cms[h]
        nv_ref[0, h:h + 1, :] = nvs[h]
        m_ref[0, h:h + 1, :] = ms[h]


def _bias_params(b_gate):
    brow = jnp.zeros((8, 128), F32).at[0, 0:2 * HEADS].set(b_gate)
    bcol = jnp.zeros((8, 128), F32).at[0:2 * HEADS, 0].set(b_gate)
    return brow, bcol


def _mix_c_prompt(qk, v, opre, gates, batch, seq, b_gate, gain):
    lb = C_SUB * CHUNK
    assert seq % lb == 0
    nc = seq // lb
    brow, bcol = _bias_params(b_gate)
    tri = _chunk_tri(lb)
    spread = _head_spread(2)
    tok = lambda b, c: (b * nc + c, 0)
    const = lambda b, c: (0, 0)
    return pl.pallas_call(
        functools.partial(_mix_c_kernel, n_sub=C_SUB),
        grid=(batch, nc),
        in_specs=[pl.BlockSpec((lb, 2 * HEADS * DK), tok),
                  pl.BlockSpec((lb, HEADS * C_DV), tok),
                  pl.BlockSpec((lb, HEADS * C_DV), tok),
                  pl.BlockSpec((lb, 128), tok),
                  pl.BlockSpec((8, 128), const),
                  pl.BlockSpec((8, 128), const),
                  pl.BlockSpec(tri.shape, const),
                  pl.BlockSpec(spread.shape, const),
                  pl.BlockSpec((1, C_DV), const)],
        out_specs=[pl.BlockSpec((lb, HEADS * C_DV), tok),
                   pl.BlockSpec((1, HEADS, DK, C_DV), lambda b, c: (b, 0, 0, 0)),
                   pl.BlockSpec((1, HEADS, DK), lambda b, c: (b, 0, 0)),
                   pl.BlockSpec((1, HEADS, 128), lambda b, c: (b, 0, 0))],
        out_shape=[jax.ShapeDtypeStruct((batch * seq, HEADS * C_DV), BF16),
                   jax.ShapeDtypeStruct((batch, HEADS, DK, C_DV), F32),
                   jax.ShapeDtypeStruct((batch, HEADS, DK), F32),
                   jax.ShapeDtypeStruct((batch, HEADS, 128), F32)],
        compiler_params=_cparams("parallel", "arbitrary"),
        name="mix_c_prompt",
    )(qk, v, opre, gates, brow, bcol, tri, spread, gain.reshape(1, C_DV))


def _row_select(rows, t, new, old):
    return jnp.where(rows == t, new, old)


def _step_ab_kernel(qkv_ref, rest_ref, gates_ref, cos_ref, sin_ref, wconv_ref, prow_ref, ggain_ref, rgain_ref,
                    buf_ref, sd_ref, sr_ref, mixed_ref, nbuf_ref, nsd_ref, nsr_ref):
    tb = qkv_ref.shape[0]
    u = qkv_ref[...]
    w = wconv_ref[...]
    b0 = buf_ref[:, 0:A_CONV_CH]
    b1 = buf_ref[:, A_CONV_CH:2 * A_CONV_CH]
    b2 = buf_ref[:, 2 * A_CONV_CH:3 * A_CONV_CH]
    conv = b0 * w[0:1] + b1 * w[1:2] + b2 * w[2:3] + u * w[3:4]
    nbuf_ref[:, 0:A_CONV_CH] = b1
    nbuf_ref[:, A_CONV_CH:2 * A_CONV_CH] = b2
    nbuf_ref[:, 2 * A_CONV_CH:3 * A_CONV_CH] = u
    act = _silu(conv)
    g = gates_ref[...]
    eg_all = jnp.exp(-jnp.exp(prow_ref[0:1, :]) * _softplus(g + prow_ref[1:2, :]))
    beta_all = jax.nn.sigmoid(g)
    cosf = cos_ref[...]
    sinf = sin_ref[...]
    base = HEADS * DK
    rows = lax.broadcasted_iota(jnp.int32, (tb, DK), 0)
    items = [(t, h) for t in range(tb) for h in range(HEADS)]

    qs = [_l2(act[:, h * DK:(h + 1) * DK]) * (DK ** -0.5) for h in range(HEADS)]
    ks = [_l2(act[:, (HEADS + h) * DK:(HEADS + h + 1) * DK]) for h in range(HEADS)]
    k_s = [jnp.zeros((tb, A_DV), F32) for _ in range(HEADS)]
    for t, h in items:
        k_s[h] = _row_select(rows, t, _mm(ks[h], sd_ref[t, h]), k_s[h])
    v_new = []
    for h in range(HEADS):
        v = act[:, (2 * HEADS + h) * DK:(2 * HEADS + h + 1) * DK]
        v_new.append(beta_all[:, HEADS + h:HEADS + h + 1] * (v - eg_all[:, h:h + 1] * k_s[h]))
    for t, h in items:
        nsd_ref[t, h] = (sd_ref[t, h] * eg_all[t:t + 1, h:h + 1]
                         + _mm_tn(jnp.where(rows == t, ks[h], 0.0), v_new[h]))
    o_a = [jnp.zeros((tb, A_DV), F32) for _ in range(HEADS)]
    for t, h in items:
        o_a[h] = _row_select(rows, t, _mm(qs[h], nsd_ref[t, h]), o_a[h])
    for h in range(HEADS):
        ga = rest_ref[:, h * DK:(h + 1) * DK]
        mixed_ref[:, h * DK:(h + 1) * DK] = _rms(o_a[h], ggain_ref[...]) * _silu(ga)

    qrs, krs = [], []
    for h in range(HEADS):
        qb = rest_ref[:, base + h * DK:base + (h + 1) * DK]
        kb = rest_ref[:, 2 * base + h * DK:2 * base + (h + 1) * DK]
        qrs.append(qb * cosf + pltpu.roll(qb, DK // 2, 1) * sinf)
        krs.append((kb * cosf + pltpu.roll(kb, DK // 2, 1) * sinf) * (DK ** -0.5))
    for t, h in items:
        vb = rest_ref[:, 3 * base + h * DK:3 * base + (h + 1) * DK]
        nsr_ref[t, h] = (sr_ref[t, h] * math.exp(LOG_GAMMA[h])
                         + _mm_tn(jnp.where(rows == t, krs[h], 0.0), vb))
    o_b = [jnp.zeros((tb, DK), F32) for _ in range(HEADS)]
    for t, h in items:
        o_b[h] = _row_select(rows, t, _mm(qrs[h], nsr_ref[t, h]), o_b[h])
    for h in range(HEADS):
        gb = rest_ref[:, 4 * base + h * DK:4 * base + (h + 1) * DK]
        mixed_ref[:, base + h * DK:base + (h + 1) * DK] = _rms(o_b[h], rgain_ref[...]) * _silu(gb)


def _mix_ab_sample(qkv, rest, gates, conv_buf, s_delta, s_ret, w_conv, a_log, dt_bias, gdn_gain, ret_gain):
    nb = qkv.shape[0]
    tb = SAMPLE_TB
    assert nb % tb == 0
    cosf, sinf = _rope_tables(PAST_LEN, 1)
    prow, _ = _gate_params(a_log, dt_bias)
    tok = lambda b: (b, 0)
    tok4 = lambda b: (b, 0, 0, 0)
    const = lambda b: (0, 0)
    n_buf = (CONV_W - 1) * A_CONV_CH
    mixed, nbuf, nsd, nsr = pl.pallas_call(
        _step_ab_kernel,
        grid=(nb // tb,),
        in_specs=[pl.BlockSpec((tb, A_CONV_CH), tok),
                  pl.BlockSpec((tb, rest.shape[1]), tok),
                  pl.BlockSpec((tb, 128), tok),
                  pl.BlockSpec((1, DK), const),
                  pl.BlockSpec((1, DK), const),
                  pl.BlockSpec((CONV_W, A_CONV_CH), const),
                  pl.BlockSpec((8, 128), const),
                  pl.BlockSpec((1, A_DV), const),
                  pl.BlockSpec((1, DK), const),
                  pl.BlockSpec((tb, n_buf), tok),
                  pl.BlockSpec((tb, HEADS, DK, A_DV), tok4),
                  pl.BlockSpec((tb, HEADS, DK, DK), tok4)],
        out_specs=[pl.BlockSpec((tb, 2 * HEADS * DK), tok),
                   pl.BlockSpec((tb, n_buf), tok),
                   pl.BlockSpec((tb, HEADS, DK, A_DV), tok4),
                   pl.BlockSpec((tb, HEADS, DK, DK), tok4)],
        out_shape=[jax.ShapeDtypeStruct((nb, 2 * HEADS * DK), F32),
                   jax.ShapeDtypeStruct((nb, n_buf), F32),
                   jax.ShapeDtypeStruct((nb, HEADS, DK, A_DV), F32),
                   jax.ShapeDtypeStruct((nb, HEADS, DK, DK), F32)],
        compiler_params=_cparams("parallel"),
        name="mix_ab_sample",
    )(qkv, rest, gates, cosf, sinf, w_conv, prow, gdn_gain.reshape(1, A_DV), ret_gain.reshape(1, DK),
      conv_buf.reshape(nb, n_buf), s_delta, s_ret)
    return mixed, nbuf.reshape(nb, CONV_W - 1, A_CONV_CH), nsd, nsr


def _step_c_kernel(qk_ref, v_ref, opre_ref, gates_ref, brow_ref, gain_ref, cm_ref, nv_ref, m_ref,
                   h_ref, ncm_ref, nnv_ref, nm_ref):
    tb = qk_ref.shape[0]
    g = gates_ref[...] + brow_ref[0:1, :]
    logf = _log_sigmoid(g)
    rows = lax.broadcasted_iota(jnp.int32, (tb, DK), 0)
    items = [(t, h) for t in range(tb) for h in range(HEADS)]
    qs, kws, fs, ms_new, nvs_new = [], [], [], [], []
    for h in range(HEADS):
        i_pre = g[:, h:h + 1]
        q = qk_ref[:, h * DK:(h + 1) * DK]
        k = qk_ref[:, (HEADS + h) * DK:(HEADS + h + 1) * DK] * (DK ** -0.5)
        inter = logf[:, HEADS + h:HEADS + h + 1] + m_ref[:, h:h + 1]
        m_new = jnp.maximum(inter, i_pre)
        f_state = jnp.exp(inter - m_new)
        kw = k * jnp.exp(i_pre - m_new)
        nv_new = nv_ref[:, h * DK:(h + 1) * DK] * f_state + kw
        nnv_ref[:, h * DK:(h + 1) * DK] = nv_new
        nm_ref[:, h:h + 1] = m_new
        qs.append(q)
        kws.append(kw)
        fs.append(f_state)
        ms_new.append(m_new)
        nvs_new.append(nv_new)
    for t, h in items:
        v = v_ref[:, h * C_DV:(h + 1) * C_DV]
        ncm_ref[t, h] = (cm_ref[t, h] * fs[h][t:t + 1, :]
                         + _mm_tn(jnp.where(rows == t, kws[h], 0.0), v))
    rows_v = lax.broadcasted_iota(jnp.int32, (tb, C_DV), 0)
    nums = [jnp.zeros((tb, C_DV), F32) for _ in range(HEADS)]
    for t, h in items:
        nums[h] = _row_select(rows_v, t, _mm(qs[h], ncm_ref[t, h]), nums[h])
    for h in range(HEADS):
        den = jnp.sum(qs[h] * nvs_new[h], axis=1, keepdims=True)
        hh = nums[h] / jnp.maximum(jnp.abs(den), jnp.exp(-ms_new[h]))
        op = opre_ref[:, h * C_DV:(h + 1) * C_DV]
        h_ref[:, h * C_DV:(h + 1) * C_DV] = _rms(hh, gain_ref[...]) * jax.nn.sigmoid(op)


def _mix_c_sample(qk, v, opre, gates, s_c, s_n, s_m, b_gate, gain):
    nb = qk.shape[0]
    tb = SAMPLE_TB
    assert nb % tb == 0
    brow, _ = _bias_params(b_gate)
    tok = lambda b: (b, 0)
    tok4 = lambda b: (b, 0, 0, 0)
    const = lambda b: (0, 0)
    hh, ncm, nnv, nm = pl.pallas_call(
        _step_c_kernel,
        grid=(nb // tb,),
        in_specs=[pl.BlockSpec((tb, 2 * HEADS * DK), tok),
                  pl.BlockSpec((tb, HEADS * C_DV), tok),
                  pl.BlockSpec((tb, HEADS * C_DV), tok),
                  pl.BlockSpec((tb, 128), tok),
                  pl.BlockSpec((8, 128), const),
                  pl.BlockSpec((1, C_DV), const),
                  pl.BlockSpec((tb, HEADS, DK, C_DV), tok4),
                  pl.BlockSpec((tb, HEADS * DK), tok),
                  pl.BlockSpec((tb, HEADS), tok)],
        out_specs=[pl.BlockSpec((tb, HEADS * C_DV), tok),
                   pl.BlockSpec((tb, HEADS, DK, C_DV), tok4),
                   pl.BlockSpec((tb, HEADS * DK), tok),
                   pl.BlockSpec((tb, HEADS), tok)],
        out_shape=[jax.ShapeDtypeStruct((nb, HEADS * C_DV), F32),
                   jax.ShapeDtypeStruct((nb, HEADS, DK, C_DV), F32),
                   jax.ShapeDtypeStruct((nb, HEADS * DK), F32),
                   jax.ShapeDtypeStruct((nb, HEADS), F32)],
        compiler_params=_cparams("parallel"),
        name="mix_c_sample",
    )(qk, v, opre, gates, brow, gain.reshape(1, C_DV), s_c, s_n.reshape(nb, HEADS * DK), s_m)
    return hh, ncm, nnv.reshape(nb, HEADS, DK), nm


C_COLS = ((0, 2 * HEADS * DK), (2 * HEADS * DK, HEADS * C_DV), (2 * HEADS * DK + HEADS * C_DV, HEADS * C_DV),
          (2 * HEADS * DK + 2 * HEADS * C_DV, 2 * HEADS))


def kernel(x_prompt, x_sample, state_conv_a, state_delta_a, state_ret_b, state_mlstm_C, state_mlstm_n, state_mlstm_m,
           norm_mix_a, w_in_a, w_conv_a, a_log, dt_bias, gdn_gain, ret_gain, w_out_a,
           norm_mix_c, w_in_c, b_gate_c, mlstm_gain, w_out_c, norm_ffn, w_up, w_down, final_gain):
    batch, seq, _ = x_prompt.shape
    n_s = x_sample.shape[0] * x_sample.shape[1]
    tm = PROMPT_TM
    w_a = w_in_a[0].astype(BF16)
    w_c = w_in_c[0].astype(BF16)
    w_ups = w_up.astype(BF16)
    w_downs = w_down.astype(BF16)
    h = x_prompt.reshape(batch * seq, D_MODEL)
    h_s = x_sample.reshape(n_s, D_MODEL)

    (mixed, conv, sd, sr), (qkv_s, rest_s, gates_s) = _layer0_prompt(
        h, h_s, norm_mix_a[0], w_a, w_conv_a[0], a_log[0], dt_bias[0], gdn_gain[0], ret_gain[0], batch, seq)
    mixed_s, conv_s, sd_s, sr_s = _mix_ab_sample(qkv_s, rest_s, gates_s, state_conv_a[0], state_delta_a[0],
                                                 state_ret_b[0], w_conv_a[0], a_log[0], dt_bias[0], gdn_gain[0],
                                                 ret_gain[0])
    h, h_s = _out_mlp(h, mixed, h_s, mixed_s, w_out_a[0].astype(BF16), norm_ffn[0], w_ups, w_downs, 0, None, tm)

    (qk, v, opre, gates_c), (qk_s, v_s, opre_s, gates_cs) = _norm_proj(h, h_s, norm_mix_c[0], w_c, C_COLS, tm)
    hm, cm, nv, m = _mix_c_prompt(qk, v, opre, gates_c, batch, seq, b_gate_c[0], mlstm_gain[0])
    hm_s, cm_s, nv_s, m_s = _mix_c_sample(qk_s, v_s, opre_s, gates_cs, state_mlstm_C[0], state_mlstm_n[0],
                                          state_mlstm_m[0], b_gate_c[0], mlstm_gain[0])
    y, y_s = _out_mlp(h, hm, h_s, hm_s, w_out_c[0].astype(BF16), norm_ffn[1], w_ups, w_downs, 1, final_gain, tm)

    return (y.reshape(x_prompt.shape), y_s.reshape(x_sample.shape),
            conv[None], sd[None], sr[None], cm[None], nv[None], m[:, :, 0][None],
            conv_s[None], sd_s[None], sr_s[None], cm_s[None], nv_s[None], m_s[None])
```

```python
import functools
import math

import jax
import jax.numpy as jnp
import numpy as np
from jax import lax
from jax.experimental import pallas as pl
from jax.experimental.pallas import tpu as pltpu

F32 = jnp.float32
BF16 = jnp.bfloat16

D_MODEL = 1024
D_FF = 4 * D_MODEL
CHUNK = 64
EPS = 1e-6
NEG = -1e30
HEADS = 4
DK = 128
A_DV = 128
C_DV = 256
CONV_W = 4
A_CONV_CH = 3 * HEADS * DK
ROPE_BASE = 10000.0
PAST_LEN = 16384
LOG_GAMMA = tuple(math.log1p(-(2.0 ** (-5.0 - h))) for h in range(HEADS))

VMEM_LIMIT_BYTES = 56 * 1024 * 1024
PROMPT_TM = 512
AB_SUB = 8
C_SUB = 16
SAMPLE_TB = 8


def _cparams(*sem):
    return pltpu.CompilerParams(dimension_semantics=sem, vmem_limit_bytes=VMEM_LIMIT_BYTES)


def _mm(a, b):
    return jnp.dot(a.astype(BF16), b.astype(BF16), preferred_element_type=F32)


def _mm_nt(a, b):
    return lax.dot_general(a.astype(BF16), b.astype(BF16), (((1,), (1,)), ((), ())), preferred_element_type=F32)


def _mm_tn(a, b):
    return lax.dot_general(a.astype(BF16), b.astype(BF16), (((0,), (0,)), ((), ())), preferred_element_type=F32)


def _split(a):
    hi = a.astype(BF16)
    lo = (a - hi.astype(F32)).astype(BF16)
    return hi, lo


def _softplus(x):
    return jnp.maximum(x, 0.0) + jnp.log1p(jnp.exp(-jnp.abs(x)))


def _log_sigmoid(x):
    return -_softplus(-x)


def _silu(x):
    return x * jax.nn.sigmoid(x)


def _rms(x, gain):
    return x * lax.rsqrt(jnp.mean(x * x, axis=-1, keepdims=True) + EPS) * gain


def _l2(t):
    return t * lax.rsqrt(jnp.sum(t * t, axis=-1, keepdims=True) + EPS)


def _interleave(gens):
    gens = list(gens)
    while gens:
        alive = []
        for gen in gens:
            try:
                next(gen)
                alive.append(gen)
            except StopIteration:
                pass
        gens = alive


def _chunk_cumsum_cols(x, tri_c):
    n = x.shape[0] // CHUNK
    side = lambda a: jnp.concatenate([a[i * CHUNK:(i + 1) * CHUNK] for i in range(n)], axis=1)
    hi, lo = _split(x)
    out = (jnp.dot(tri_c, side(hi), preferred_element_type=F32)
           + jnp.dot(tri_c, side(lo), preferred_element_type=F32))
    return jnp.concatenate([out[:, i * 128:(i + 1) * 128] for i in range(n)], axis=0)


def _lockstep(gens):
    gens = list(gens)
    while gens:
        alive = []
        for gen in gens:
            try:
                next(gen)
                alive.append(gen)
            except StopIteration:
                pass
        gens = alive
        if gens:
            yield


def _head_block_rows(x):
    c, n = x.shape
    t = n // HEADS
    z = jnp.zeros((c, t), x.dtype)
    return jnp.concatenate(
        [jnp.concatenate([x[:, h * t:(h + 1) * t] if g == h else z for g in range(HEADS)], axis=1)
         for h in range(HEADS)], axis=0)


def _head_block_diag(y):
    c, n = y.shape
    per_tile = 128 // c
    assert n == HEADS * c and 128 % c == 0 and HEADS % per_tile == 0
    lane = lax.broadcasted_iota(jnp.int32, (c, 128), 1)
    z = jnp.zeros((c, 128), y.dtype)
    blocks = []
    for h in range(HEADS):
        t = h // per_tile
        lo = (h % per_tile) * c
        kept = jnp.where((lane >= lo) & (lane < lo + c), y[:, t * 128:(t + 1) * 128], z)
        blocks.append(jnp.concatenate([kept if g == t else z for g in range(n // 128)], axis=1))
    return jnp.concatenate(blocks, axis=0)


def _wide_mm(x, y):
    return jnp.dot(x.astype(BF16), _head_block_diag(y.astype(BF16)), preferred_element_type=F32)


def _tri_inv_wide(ns):
    c, wd = ns[0].shape
    ii = lax.broadcasted_iota(jnp.int32, (c, wd), 0)
    jl = lax.broadcasted_iota(jnp.int32, (c, wd), 1) & (c - 1)
    eye = (ii == jl).astype(F32)
    ts = [eye - jnp.where((ii >> 1) == (jl >> 1), n, 0.0) for n in ns]
    for lvl in range(1, int(math.log2(c))):
        off = ((ii >> (lvl + 1)) == (jl >> (lvl + 1))) & ((ii >> lvl) != (jl >> lvl))
        ys = [_wide_mm(jnp.where(off, n, 0.0), t) for n, t in zip(ns, ts)]
        yield
        ts = [t - _wide_mm(t, y) for t, y in zip(ts, ys)]
        yield
    return [t - eye for t in ts]


def _out_width(n):
    return max(n, 128)


def _project_rows(x_ref, g_ref, pieces, out_refs):
    xn = _rms(x_ref[...], g_ref[...]).astype(BF16)
    for (w_ref, col0, n), o_ref in zip(pieces, out_refs):
        res = jnp.dot(xn, w_ref[:, col0:col0 + n], preferred_element_type=F32)
        if n < _out_width(n):
            res = jnp.concatenate([res, jnp.zeros((res.shape[0], _out_width(n) - n), F32)], axis=1)
        o_ref[...] = res


def _norm_proj_kernel(x_ref, xs_ref, g_ref, w_ref, *out_refs, cols):
    pieces = [(w_ref, c0, n) for c0, n in cols]
    _project_rows(x_ref, g_ref, pieces, out_refs[:len(cols)])

    @pl.when(pl.program_id(0) == 0)
    def _():
        _project_rows(xs_ref, g_ref, pieces, out_refs[len(cols):])


def _norm_proj(x, x_s, gain, w, cols, tm):
    t, ts = x.shape[0], x_s.shape[0]
    assert t % tm == 0
    widths = [_out_width(n) for _, n in cols]
    row = lambda i: (i, 0)
    const = lambda i: (0, 0)
    outs = pl.pallas_call(
        functools.partial(_norm_proj_kernel, cols=cols),
        grid=(t // tm,),
        in_specs=[pl.BlockSpec((tm, D_MODEL), row), pl.BlockSpec((ts, D_MODEL), const),
                  pl.BlockSpec((1, D_MODEL), const),
                  pl.BlockSpec(w.shape, const, pipeline_mode=pl.Buffered(1))],
        out_specs=[pl.BlockSpec((tm, n), row) for n in widths] + [pl.BlockSpec((ts, n), const) for n in widths],
        out_shape=[jax.ShapeDtypeStruct((t, n), F32) for n in widths]
                  + [jax.ShapeDtypeStruct((ts, n), F32) for n in widths],
        compiler_params=_cparams("arbitrary"),
        name="norm_proj",
    )(x, x_s, gain.reshape(1, D_MODEL), w)
    return outs[:len(widths)], outs[len(widths):]


def _layer0_prompt_kernel(x_ref, xs_ref, g_ref, w_ref, wconv_ref, cos_ref, sin_ref,
                          prow_ref, pcol_ref, tri_ref, ggain_ref, rgain_ref,
                          mixed_ref, conv_ref, sd_ref, sr_ref, qkvs_ref, rests_ref, gatess_ref,
                          xp_ref, raw_ref, wrest_ref, qkv_ref, rest_ref, gates_ref, *, tiles_per_seq, n_sub):
    tm = x_ref.shape[0]
    hd = HEADS * DK
    n_gate = 2 * HEADS
    gate_piece = (w_ref, A_CONV_CH, n_gate)

    @pl.when(pl.program_id(0) == 0)
    def _():
        wrest_ref[...] = w_ref[:, A_CONV_CH + n_gate:A_CONV_CH + n_gate + 5 * hd]
        _project_rows(xs_ref, g_ref, [(w_ref, 0, A_CONV_CH), (wrest_ref, 0, 5 * hd), gate_piece],
                      (qkvs_ref, rests_ref, gatess_ref))

    xn = _rms(x_ref[...], g_ref[...]).astype(BF16)

    @pl.when(pl.program_id(0) % tiles_per_seq == 0)
    def _():
        xp_ref[0:8, :] = jnp.zeros((8, A_CONV_CH), F32)
        sd_ref[...] = jnp.zeros_like(sd_ref)
        sr_ref[...] = jnp.zeros_like(sr_ref)

    w = wconv_ref[...]
    base = A_CONV_CH

    def conv_part(part, slot, r0, r1):
        c0, c1 = part * hd, (part + 1) * hd
        raw = raw_ref[slot, r0:r1, :]
        xp_ref[8 + r0:8 + r1, c0:c1] = raw
        conv = (xp_ref[5 + r0:5 + r1, c0:c1] * w[0:1, c0:c1] + xp_ref[6 + r0:6 + r1, c0:c1] * w[1:2, c0:c1]
                + xp_ref[7 + r0:7 + r1, c0:c1] * w[2:3, c0:c1] + raw * w[3:4, c0:c1])
        if r1 == tm:
            rows = r1 - r0
            xp_ref[0:8, c0:c1] = raw[rows - 8:rows, :]
            conv_ref[0, :, c0:c1] = raw[rows - 3:rows, :]
        act = _silu(conv)
        if part == 2:
            qkv_ref[r0:r1, c0:c1] = act
        else:
            scale = DK ** -0.5 if part == 0 else 1.0
            for h in range(HEADS):
                qkv_ref[r0:r1, c0 + h * DK:c0 + (h + 1) * DK] = _l2(act[:, h * DK:(h + 1) * DK]) * scale

    def rest_part(part, slot, r0, r1):
        c0, c1 = part * hd, (part + 1) * hd
        raw = raw_ref[slot, r0:r1, :]
        if part in (0, 4):
            rest_ref[r0:r1, c0:c1] = _silu(raw)
        elif part == 3:
            rest_ref[r0:r1, c0:c1] = raw
        else:
            scale = 1.0 if part == 1 else DK ** -0.5
            for h in range(HEADS):
                t = raw[:, h * DK:(h + 1) * DK]
                rest_ref[r0:r1, c0 + h * DK:c0 + (h + 1) * DK] = (
                    (t * cos_ref[r0:r1, :] + pltpu.roll(t, DK // 2, 1) * sin_ref[r0:r1, :]) * scale)

    def gates_part(_, slot, r0, r1):
        gates_ref[r0:r1, :] = jnp.concatenate(
            [raw_ref[slot, r0:r1, 0:n_gate], jnp.zeros((r1 - r0, 128 - n_gate), F32)], axis=1)

    groups = [(w_ref, 0, hd, functools.partial(conv_part, 0)),
              (wrest_ref, 3 * hd, hd, functools.partial(rest_part, 3)),
              (w_ref, hd, hd, functools.partial(conv_part, 1)),
              (wrest_ref, 0, hd, functools.partial(rest_part, 0)),
              (w_ref, 2 * hd, hd, functools.partial(conv_part, 2)),
              (wrest_ref, hd, hd, functools.partial(rest_part, 1)),
              gate_piece + (functools.partial(gates_part, 0),),
              (wrest_ref, 2 * hd, hd, functools.partial(rest_part, 2)),
              (wrest_ref, 4 * hd, hd, functools.partial(rest_part, 4))]

    def project(n, slot):
        wg_ref, col0, width, _ = groups[n]
        for j in range(0, width, 256):
            wj = min(256, width - j)
            raw_ref[slot, :, j:j + wj] = jnp.dot(xn, wg_ref[:, col0 + j:col0 + j + wj],
                                                 preferred_element_type=F32)
            yield

    def epilogue(n, slot):
        for r0 in range(0, tm, 128):
            groups[n][3](slot, r0, min(r0 + 128, tm))
            yield

    def projection_track(order):
        yield from project(order[0], 0)
        for idx, n in enumerate(order):
            nxt = [project(order[idx + 1], (idx + 1) % 2)] if idx + 1 < len(order) else []
            yield from _lockstep([epilogue(n, idx % 2)] + nxt)
            yield

    _interleave([projection_track([6, 0, 1, 2, 3, 4, 5, 7])])
    gdn_stages, delta_track, ret_track, finish = _mix_ab_tracks(
        qkv_ref, rest_ref, gates_ref, prow_ref, pcol_ref, tri_ref, ggain_ref, rgain_ref,
        mixed_ref, sd_ref, sr_ref, n_sub)
    _interleave(gdn_stages)

    def filler_track():
        yield from projection_track([8])
        yield from ret_track()

    _interleave([delta_track(), filler_track()])
    finish()


def _layer0_prompt(x, x_s, gain, w, w_conv, a_log, dt_bias, gdn_gain, ret_gain, batch, seq):
    t, ts = x.shape[0], x_s.shape[0]
    tm = AB_SUB * CHUNK
    assert t % tm == 0 and seq % tm == 0
    tiles_per_seq = seq // tm
    cosf, sinf = _rope_tables(0, seq)
    prow, pcol = _gate_params(a_log, dt_bias)
    tri = _chunk_tri(tm)
    row = lambda i: (i, 0)
    const = lambda i: (0, 0)
    pos = lambda i: (i % tiles_per_seq, 0)
    per_seq3 = lambda i: (i // tiles_per_seq, 0, 0)
    per_seq4 = lambda i: (i // tiles_per_seq, 0, 0, 0)
    n_rest = 5 * HEADS * DK
    outs = pl.pallas_call(
        functools.partial(_layer0_prompt_kernel, tiles_per_seq=tiles_per_seq, n_sub=AB_SUB),
        grid=(t // tm,),
        in_specs=[pl.BlockSpec((tm, D_MODEL), row),
                  pl.BlockSpec((ts, D_MODEL), const),
                  pl.BlockSpec((1, D_MODEL), const),
                  pl.BlockSpec(w.shape, const, pipeline_mode=pl.Buffered(1)),
                  pl.BlockSpec((CONV_W, A_CONV_CH), const),
                  pl.BlockSpec((tm, DK), pos),
                  pl.BlockSpec((tm, DK), pos),
                  pl.BlockSpec((8, 128), const),
                  pl.BlockSpec((8, 128), const),
                  pl.BlockSpec(tri.shape, const),
                  pl.BlockSpec((1, A_DV), const),
                  pl.BlockSpec((1, DK), const)],
        out_specs=[pl.BlockSpec((tm, 2 * HEADS * DK), row),
                   pl.BlockSpec((1, CONV_W - 1, A_CONV_CH), per_seq3),
                   pl.BlockSpec((1, HEADS, DK, A_DV), per_seq4),
                   pl.BlockSpec((1, HEADS, DK, DK), per_seq4),
                   pl.BlockSpec((ts, A_CONV_CH), const),
                   pl.BlockSpec((ts, n_rest), const),
                   pl.BlockSpec((ts, 128), const)],
        out_shape=[jax.ShapeDtypeStruct((t, 2 * HEADS * DK), BF16),
                   jax.ShapeDtypeStruct((batch, CONV_W - 1, A_CONV_CH), F32),
                   jax.ShapeDtypeStruct((batch, HEADS, DK, A_DV), F32),
                   jax.ShapeDtypeStruct((batch, HEADS, DK, DK), F32),
                   jax.ShapeDtypeStruct((ts, A_CONV_CH), F32),
                   jax.ShapeDtypeStruct((ts, n_rest), F32),
                   jax.ShapeDtypeStruct((ts, 128), F32)],
        scratch_shapes=[pltpu.VMEM((tm + 8, A_CONV_CH), F32), pltpu.VMEM((2, tm, HEADS * DK), F32),
                        pltpu.VMEM((D_MODEL, n_rest), BF16),
                        pltpu.VMEM((tm, A_CONV_CH), F32), pltpu.VMEM((tm, n_rest), F32), pltpu.VMEM((tm, 128), F32)],
        compiler_params=_cparams("arbitrary"),
        name="layer0_prompt",
    )(x, x_s, gain.reshape(1, D_MODEL), w, w_conv, cosf, sinf, prow, pcol, tri,
      gdn_gain.reshape(1, A_DV), ret_gain.reshape(1, DK))
    return outs[:4], outs[4:]


def _out_mlp_kernel(h_ref, m_ref, hs_ref, ms_ref, wout_ref, gffn_ref, wup_ref, wdown_ref, *rest, final):
    if final:
        gfin_ref, o_ref, os_ref = rest
    else:
        o_ref, os_ref = rest

    def block(h_r, m_r, o_r):
        h = h_r[...] + jnp.dot(m_r[...].astype(BF16), wout_ref[...], preferred_element_type=F32)
        xn = _rms(h, gffn_ref[...]).astype(BF16)
        acc = h
        step = 1024
        for j in range(D_FF // step):
            hid = jnp.dot(xn, wup_ref[:, j * step:(j + 1) * step], preferred_element_type=F32)
            hid = jnp.maximum(hid, 0.0)
            acc = acc + jnp.dot((hid * hid).astype(BF16), wdown_ref[j * step:(j + 1) * step, :],
                                preferred_element_type=F32)
        if final:
            acc = _rms(acc, gfin_ref[...])
        o_r[...] = acc

    block(h_ref, m_ref, o_ref)

    @pl.when(pl.program_id(0) == 0)
    def _():
        block(hs_ref, ms_ref, os_ref)


def _out_mlp(h, mix, h_s, mix_s, w_out, g_ffn, w_up, w_down, layer, g_final, tm):
    t, ts = h.shape[0], h_s.shape[0]
    assert t % tm == 0
    final = g_final is not None
    row = lambda i: (i, 0)
    const = lambda i: (0, 0)
    this_layer = lambda i: (layer, 0, 0)
    once = pl.Buffered(1)
    in_specs = [pl.BlockSpec((tm, D_MODEL), row), pl.BlockSpec((tm, mix.shape[1]), row),
                pl.BlockSpec((ts, D_MODEL), const), pl.BlockSpec((ts, mix_s.shape[1]), const),
                pl.BlockSpec(w_out.shape, const, pipeline_mode=once), pl.BlockSpec((1, D_MODEL), const),
                pl.BlockSpec((None,) + w_up.shape[1:], this_layer, pipeline_mode=once),
                pl.BlockSpec((None,) + w_down.shape[1:], this_layer, pipeline_mode=once)]
    args = [h, mix, h_s, mix_s, w_out, g_ffn.reshape(1, D_MODEL), w_up, w_down]
    if final:
        in_specs.append(pl.BlockSpec((1, D_MODEL), const))
        args.append(g_final.reshape(1, D_MODEL))
    return pl.pallas_call(
        functools.partial(_out_mlp_kernel, final=final),
        grid=(t // tm,),
        in_specs=in_specs,
        out_specs=[pl.BlockSpec((tm, D_MODEL), row), pl.BlockSpec((ts, D_MODEL), const)],
        out_shape=[jax.ShapeDtypeStruct((t, D_MODEL), F32), jax.ShapeDtypeStruct((ts, D_MODEL), F32)],
        compiler_params=_cparams("arbitrary"),
        name="out_mlp",
    )(*args)


def _chunk_masks(c):
    ii = lax.broadcasted_iota(jnp.int32, (c, c), 0)
    jj = lax.broadcasted_iota(jnp.int32, (c, c), 1)
    return ii, jj


def _mix_ab_tracks(qkv_ref, rest_ref, gates_ref, prow_ref, pcol_ref, tri_ref,
                   ggain_ref, rgain_ref, mixed_ref, sd_ref, sr_ref, n_sub):
    c = CHUNK
    lb = n_sub * c

    g = gates_ref[...]
    g_t = g.T
    neg_a_row = -jnp.exp(prow_ref[0:1, :])
    dt_row = prow_ref[1:2, :]
    neg_a_col = -jnp.exp(pcol_ref[0:HEADS, 0:1])
    dt_col = pcol_ref[0:HEADS, 1:2]
    la_cols = neg_a_row * _softplus(g + dt_row)
    beta_cols = jax.nn.sigmoid(g)
    la_rows = neg_a_col * _softplus(g_t[0:HEADS, :] + dt_col)

    tri = tri_ref[...]
    g_cols = _chunk_cumsum_cols(la_cols, tri_ref[0:c, 0:c])
    lr_hi, lr_lo = _split(jnp.concatenate([la_rows, jnp.zeros_like(la_rows)], axis=0))
    nt = lambda x, y: lax.dot_general(x, y, (((1,), (1,)), ((), ())), preferred_element_type=F32)
    g_rows = nt(lr_hi, tri) + nt(lr_lo, tri)
    hd = HEADS * DK
    wd = HEADS * c
    lanes = lambda col, n: jnp.broadcast_to(col, (lb, n))
    g_wide = jnp.concatenate([lanes(g_cols[:, h:h + 1], DK) for h in range(HEADS)], axis=1)
    beta_wide = jnp.concatenate([lanes(beta_cols[:, HEADS + h:HEADS + h + 1], DK) for h in range(HEADS)], axis=1)
    g_half = jnp.concatenate([lanes(g_cols[:, h:h + 1], c) for h in range(HEADS)], axis=1)
    eg_wide = jnp.exp(g_wide)

    ii = lax.broadcasted_iota(jnp.int32, (c, wd), 0)
    jl = lax.broadcasted_iota(jnp.int32, (c, wd), 1) & (c - 1)
    causal = ii >= jl
    strict = ii > jl
    lane_hd = lax.broadcasted_iota(jnp.int32, (1, hd), 1)
    lane_wd = lax.broadcasted_iota(jnp.int32, (1, wd), 1)
    lg_hd = jnp.full((1, hd), LOG_GAMMA[HEADS - 1], F32)
    lg_wd = jnp.full((1, wd), LOG_GAMMA[HEADS - 1], F32)
    for h in range(HEADS - 2, -1, -1):
        lg_hd = jnp.where(lane_hd < (h + 1) * DK, LOG_GAMMA[h], lg_hd)
        lg_wd = jnp.where(lane_wd < (h + 1) * c, LOG_GAMMA[h], lg_wd)
    pos_col = lax.broadcasted_iota(jnp.int32, (c, hd), 0).astype(F32)
    ret_dec = jnp.exp(jnp.where(causal, (ii - jl).astype(F32) * lg_wd, NEG))
    ret_q_scale = jnp.exp((pos_col + 1.0) * lg_hd)
    ret_k_scale = jnp.exp((float(c - 1) - pos_col) * lg_hd)

    ggain = ggain_ref[...]
    rgain = rgain_ref[...]
    base = hd
    tile = lambda x, h: x[:, h * DK:(h + 1) * DK]
    s_delta = [sd_ref[0, h] for h in range(HEADS)]
    s_ret = [sr_ref[0, h] for h in range(HEADS)]

    chunks = list(range(n_sub))
    items = [(ci, h) for ci in chunks for h in range(HEADS)]
    rng = lambda ci: (ci * c, (ci + 1) * c)
    pre = {it: {} for it in items}
    pre_b = {it: {} for it in items}
    wide = {ci: {} for ci in chunks}
    nt = lambda x, y: lax.dot_general(x, y, (((1,), (1,)), ((), ())), preferred_element_type=F32)

    def gdn_stage(ci):
        w_ = wide[ci]
        r0, r1 = rng(ci)
        q = qkv_ref[r0:r1, 0:hd]
        k = qkv_ref[r0:r1, hd:2 * hd]
        beta = beta_wide[r0:r1]
        g_col = g_wide[r0:r1]
        kb = k * beta
        g_row = jnp.concatenate([g_rows[h:h + 1, r0:r1] for h in range(HEADS)], axis=1)
        dec_causal = jnp.exp(jnp.where(causal, g_half[r0:r1] - g_row, NEG))
        prod = nt(jnp.concatenate([q, kb], axis=0).astype(BF16), _head_block_rows(k.astype(BF16)))
        yield
        w_['n'] = prod[c:2 * c] * jnp.where(strict, dec_causal, 0.0)
        w_['qk'] = prod[0:c] * dec_causal
        yield
        v = qkv_ref[r0:r1, 2 * hd:3 * hd]
        eg = eg_wide[r0:r1]
        g_last = g_col[c - 1:c, :]
        rhs_u, rhs_w, qd = v * beta, kb * eg, q * eg
        kd = k * jnp.exp(g_last - g_col)
        gl = jnp.exp(g_last)
        for h in range(HEADS):
            pre[ci, h].update(rhs=jnp.concatenate([tile(rhs_u, h), tile(rhs_w, h)], axis=1), qd=tile(qd, h),
                              kd=tile(kd, h), gl=tile(gl, h))

    def ret_stage(ci):
        r0, r1 = rng(ci)
        qr = rest_ref[r0:r1, base:2 * base]
        kr = rest_ref[r0:r1, 2 * base:3 * base]
        vb = rest_ref[r0:r1, 3 * base:4 * base]
        qk = nt(qr.astype(BF16), _head_block_rows(kr.astype(BF16))) * ret_dec
        qd = qr * ret_q_scale
        kd = kr * ret_k_scale
        yield
        for a in range(HEADS // 2):
            h0, h1 = 2 * a, 2 * a + 1
            z = jnp.zeros((c, DK), F32)
            rhs = jnp.concatenate([jnp.concatenate([tile(vb, h0), z], axis=1),
                                   jnp.concatenate([z, tile(vb, h1)], axis=1)], axis=0)
            intra = _mm(qk[:, a * 2 * c:(a + 1) * 2 * c], rhs)
            pre_b[ci, h0]['intra'] = intra[:, 0:DK]
            pre_b[ci, h1]['intra'] = intra[:, DK:2 * DK]
        yield
        for h in range(HEADS):
            pre_b[ci, h]['qd'] = tile(qd, h)
            pre_b[ci, h]['kv'] = _mm_tn(tile(kd, h), tile(vb, h))

    def solve_stage():
        t_offs = yield from _tri_inv_wide([wide[ci]['n'] for ci in chunks])
        for ci, t_off in zip(chunks, t_offs):
            wide[ci]['t_off'] = t_off

    def pair_rhs(x0, x1):
        z = jnp.zeros_like(x0)
        return jnp.concatenate([jnp.concatenate([x0, z], axis=1), jnp.concatenate([z, x1], axis=1)], axis=0)

    def sol_stage(ci):
        w_ = wide[ci]
        for a in range(HEADS // 2):
            h0, h1 = 2 * a, 2 * a + 1
            r = _mm(w_['t_off'][:, a * 2 * c:(a + 1) * 2 * c], pair_rhs(pre[ci, h0]['rhs'], pre[ci, h1]['rhs']))
            pre[ci, h0]['sol'] = pre[ci, h0]['rhs'] + r[:, 0:2 * A_DV]
            pre[ci, h1]['sol'] = pre[ci, h1]['rhs'] + r[:, 2 * A_DV:4 * A_DV]
        yield
        for a in range(HEADS // 2):
            h0, h1 = 2 * a, 2 * a + 1
            r = _mm(w_['qk'][:, a * 2 * c:(a + 1) * 2 * c], pair_rhs(pre[ci, h0]['sol'], pre[ci, h1]['sol']))
            pre[ci, h0]['qs'] = r[:, 0:2 * A_DV]
            pre[ci, h1]['qs'] = r[:, 2 * A_DV:4 * A_DV]
        for h in range(HEADS):
            d = pre[ci, h]
            d['kts'] = _mm_tn(d['kd'], d['sol'])
        yield
        for h in range(HEADS):
            d = pre[ci, h]
            d['lhs'] = jnp.concatenate([d['kts'][:, A_DV:2 * A_DV], d['qd'] - d['qs'][:, A_DV:2 * A_DV]], axis=0)


    def pair_states(x0, x1):
        return lambda s0, s1: _mm(jnp.concatenate([x0, x1], axis=1), pair_rhs(s0, s1))

    def ret_out_stage(ci, h):
        d = pre_b[ci, h]
        r0, r1 = rng(ci)
        if h % 2 == 0:
            d1 = pre_b[ci, h + 1]
            both = pair_states(d['qd'], d1['qd'])(d['s_in'], d1['s_in'])
            d['qs'], d1['qs'] = both[:, 0:DK], both[:, DK:2 * DK]
        yield
        o = d['qs'] + d['intra']
        ms_ = jnp.mean(o * o, axis=-1, keepdims=True)
        yield
        gb = rest_ref[r0:r1, 4 * base + h * DK:4 * base + (h + 1) * DK]
        mixed_ref[r0:r1, base + h * DK:base + (h + 1) * DK] = (
            o * lax.rsqrt(ms_ + EPS) * rgain * gb).astype(mixed_ref.dtype)

    def delta_step(ci, h):
        d = pre[ci, h]
        if h % 2 == 0:
            d1 = pre[ci, h + 1]
            both = pair_states(d['lhs'], d1['lhs'])(s_delta[h], s_delta[h + 1])
            d['r'], d1['r'] = both[:, 0:A_DV], both[:, A_DV:2 * A_DV]
        yield
        r = d['r']
        d['o'] = r[DK:DK + c] + d['qs'][:, 0:A_DV]
        s_delta[h] = s_delta[h] * d['gl'] - r[0:DK] + d['kts'][:, 0:A_DV]

    def gdn_out_stage(ci, h):
        o = pre[ci, h]['o']
        r0, r1 = rng(ci)
        ms_ = jnp.mean(o * o, axis=-1, keepdims=True)
        yield
        ga = rest_ref[r0:r1, h * DK:(h + 1) * DK]
        mixed_ref[r0:r1, h * A_DV:(h + 1) * A_DV] = (
            o * lax.rsqrt(ms_ + EPS) * ggain * ga).astype(mixed_ref.dtype)

    def delta_track():
        yield from solve_stage()
        yield from _lockstep([sol_stage(ci) for ci in chunks])
        for ci in chunks:
            yield from _lockstep([delta_step(ci, h) for h in range(HEADS)])
            if ci > 0:
                yield from _lockstep([gdn_out_stage(ci - 1, h) for h in range(HEADS)])
        yield from _lockstep([gdn_out_stage(chunks[-1], h) for h in range(HEADS)])

    def ret_track():
        for ci in chunks:
            yield from ret_stage(ci)
            for h in range(HEADS):
                pre_b[ci, h]['s_in'] = s_ret[h]
                s_ret[h] = s_ret[h] * math.exp(c * LOG_GAMMA[h]) + pre_b[ci, h]['kv']
            yield
            yield from _lockstep([ret_out_stage(ci, h) for h in range(HEADS)])

    def finish():
        for h in range(HEADS):
            sd_ref[0, h] = s_delta[h]
            sr_ref[0, h] = s_ret[h]

    return [gdn_stage(ci) for ci in chunks], delta_track, ret_track, finish


def _rope_tables(first, count):
    half = DK // 2
    inv = ROPE_BASE ** (-np.arange(half, dtype=np.float64) / half)
    ang = (first + np.arange(count, dtype=np.float64))[:, None] * inv[None, :]
    cos, sin = np.cos(ang), np.sin(ang)
    return (jnp.asarray(np.concatenate([cos, cos], axis=-1), F32),
            jnp.asarray(np.concatenate([-sin, sin], axis=-1), F32))


def _chunk_tri(lb):
    i = np.arange(lb)[:, None]
    j = np.arange(lb)[None, :]
    return jnp.asarray((i >= j) & (i // CHUNK == j // CHUNK), BF16)


def _gate_params(a_log, dt_bias):
    prow = jnp.zeros((8, 128), F32).at[0, 0:HEADS].set(a_log).at[1, 0:HEADS].set(dt_bias)
    pcol = jnp.zeros((8, 128), F32).at[0:HEADS, 0].set(a_log).at[0:HEADS, 1].set(dt_bias)
    return prow, pcol


def _mix_c_kernel(qk_ref, v_ref, opre_ref, gates_ref, brow_ref, bcol_ref, tri_ref, gain_ref,
                  h_ref, cm_ref, nv_ref, m_ref, *, n_sub):
    c = CHUNK
    lb = n_sub * c
    step = pl.program_id(1)

    @pl.when(step == 0)
    def _():
        cm_ref[...] = jnp.zeros_like(cm_ref)
        nv_ref[...] = jnp.zeros_like(nv_ref)
        m_ref[...] = jnp.zeros_like(m_ref)

    items = [(ci, h) for ci in range(n_sub) for h in range(HEADS)]
    rng = lambda ci: (ci * c, (ci + 1) * c)
    pre = {it: {} for it in items}

    for ci, h in items:
        r0, r1 = rng(ci)
        q = qk_ref[r0:r1, h * DK:(h + 1) * DK]
        k = qk_ref[r0:r1, (HEADS + h) * DK:(HEADS + h + 1) * DK] * (DK ** -0.5)
        pre[ci, h].update(q=q, k=k, qk=_mm_nt(q, k))

    g = gates_ref[...] + brow_ref[0:1, :]
    g_t = gates_ref[...].T[0:2 * HEADS, :] + bcol_ref[0:2 * HEADS, 0:1]
    i_rows = g_t[0:HEADS, :]
    logf_rows = _log_sigmoid(g_t[HEADS:2 * HEADS, :])
    tri = tri_ref[...]
    b_cols = _chunk_cumsum_cols(_log_sigmoid(g), tri_ref[0:c, 0:c])
    lr_hi, lr_lo = _split(jnp.concatenate([logf_rows, jnp.zeros_like(logf_rows)], axis=0))
    nt = lambda x, y: lax.dot_general(x, y, (((1,), (1,)), ((), ())), preferred_element_type=F32)
    b_rows = nt(lr_hi, tri) + nt(lr_lo, tri)
    i_wide = jnp.concatenate([jnp.broadcast_to(g[:, h:h + 1], (lb, 128)) for h in range(HEADS)], axis=1)
    b_wide = jnp.concatenate([jnp.broadcast_to(b_cols[:, HEADS + h:HEADS + h + 1], (lb, 128))
                              for h in range(HEADS)], axis=1)

    ii, jj = _chunk_masks(c)
    causal = ii >= jj
    gain = gain_ref[...]
    wide2 = lambda x: jnp.concatenate([x, x], axis=-1)

    def gates_stage(ci, h):
        d = pre[ci, h]
        r0, r1 = rng(ci)
        b_col = b_wide[r0:r1, h * 128:(h + 1) * 128]
        i_col = i_wide[r0:r1, h * 128:(h + 1) * 128]
        b_last = b_col[c - 1:c, :]
        d_log = jnp.where(causal, b_col[:, 0:c] - b_rows[h:h + 1, r0:r1] + i_rows[h:h + 1, r0:r1], NEG)
        d.update(b_col=b_col, b_last=b_last, d_log=d_log, k_log=b_last - b_col + i_col)
        yield
        d['d_max'] = jnp.max(d_log, axis=1, keepdims=True)

    _interleave(gates_stage(ci, h) for ci, h in items)

    ms = [m_ref[0, h:h + 1, :] for h in range(HEADS)]
    for ci, h in items:
        d = pre[ci, h]
        inter = d['b_col'] + ms[h]
        m_row = jnp.maximum(inter, d['d_max'])
        m_new = m_row[c - 1:c, :]
        d.update(inter=inter, m_row=m_row, m_old=ms[h], m_new=m_new)
        ms[h] = m_new

    def weights_stage(ci, h):
        d = pre[ci, h]
        r0, r1 = rng(ci)
        d['w_inter'] = jnp.exp(d['inter'] - d['m_row'])
        d['f_state'] = jnp.exp(d['b_last'] + d['m_old'] - d['m_new'])
        yield
        d['w_intra'] = jnp.exp(d['d_log'] - d['m_row'][:, 0:c]) * d['qk']
        d['kw'] = d['k'] * jnp.exp(d['k_log'] - d['m_new'])
        yield
        v = v_ref[r0:r1, h * C_DV:(h + 1) * C_DV]
        d['intra'] = _mm(d['w_intra'], v)
        d['kv'] = _mm_tn(d['kw'], v)
        yield
        d['sum_intra'] = jnp.sum(d['w_intra'], axis=1, keepdims=True)
        d['sum_kw'] = jnp.sum(d['kw'], axis=0, keepdims=True)
        d['inv_floor'] = jnp.exp(-d['m_row'])

    _interleave(weights_stage(ci, h) for ci, h in items)

    cms = [cm_ref[0, h] for h in range(HEADS)]
    nvs = [nv_ref[0, h:h + 1, :] for h in range(HEADS)]
    for ci, h in items:
        d = pre[ci, h]
        d['cm_in'], d['nv_in'] = cms[h], nvs[h]
        cms[h] = cms[h] * wide2(d['f_state']) + d['kv']
        nvs[h] = nvs[h] * d['f_state'] + d['sum_kw']

    def output_stage(ci, h):
        d = pre[ci, h]
        r0, r1 = rng(ci)
        qc = _mm(d['q'], d['cm_in'])
        qn = jnp.sum(d['q'] * d['nv_in'], axis=1, keepdims=True)
        yield
        num = wide2(d['w_inter']) * qc + d['intra']
        den = d['w_inter'] * qn + d['sum_intra']
        hh = num / wide2(jnp.maximum(jnp.abs(den), d['inv_floor']))
        yield
        ms_ = jnp.mean(hh * hh, axis=-1, keepdims=True)
        yield
        op = opre_ref[r0:r1, h * C_DV:(h + 1) * C_DV]
        h_ref[r0:r1, h * C_DV:(h + 1) * C_DV] = (
            hh * lax.rsqrt(ms_ + EPS) * gain * jax.nn.sigmoid(op)).astype(h_ref.dtype)

    _interleave(output_stage(ci, h) for ci, h in items)

    for h in range(HEADS):
        cm_ref[0, h] = cms[h]
        nv_ref[0, h:h + 1, :] = nvs[h]
        m_ref[0, h:h + 1, :] = ms[h]


def _bias_params(b_gate):
    brow = jnp.zeros((8, 128), F32).at[0, 0:2 * HEADS].set(b_gate)
    bcol = jnp.zeros((8, 128), F32).at[0:2 * HEADS, 0].set(b_gate)
    return brow, bcol


def _mix_c_prompt(qk, v, opre, gates, batch, seq, b_gate, gain):
    lb = C_SUB * CHUNK
    assert seq % lb == 0
    nc = seq // lb
    brow, bcol = _bias_params(b_gate)
    tri = _chunk_tri(lb)
    tok = lambda b, c: (b * nc + c, 0)
    const = lambda b, c: (0, 0)
    return pl.pallas_call(
        functools.partial(_mix_c_kernel, n_sub=C_SUB),
        grid=(batch, nc),
        in_specs=[pl.BlockSpec((lb, 2 * HEADS * DK), tok),
                  pl.BlockSpec((lb, HEADS * C_DV), tok),
                  pl.BlockSpec((lb, HEADS * C_DV), tok),
                  pl.BlockSpec((lb, 128), tok),
                  pl.BlockSpec((8, 128), const),
                  pl.BlockSpec((8, 128), const),
                  pl.BlockSpec(tri.shape, const),
                  pl.BlockSpec((1, C_DV), const)],
        out_specs=[pl.BlockSpec((lb, HEADS * C_DV), tok),
                   pl.BlockSpec((1, HEADS, DK, C_DV), lambda b, c: (b, 0, 0, 0)),
                   pl.BlockSpec((1, HEADS, DK), lambda b, c: (b, 0, 0)),
                   pl.BlockSpec((1, HEADS, 128), lambda b, c: (b, 0, 0))],
        out_shape=[jax.ShapeDtypeStruct((batch * seq, HEADS * C_DV), BF16),
                   jax.ShapeDtypeStruct((batch, HEADS, DK, C_DV), F32),
                   jax.ShapeDtypeStruct((batch, HEADS, DK), F32),
                   jax.ShapeDtypeStruct((batch, HEADS, 128), F32)],
        compiler_params=_cparams("parallel", "arbitrary"),
        name="mix_c_prompt",
    )(qk, v, opre, gates, brow, bcol, tri, gain.reshape(1, C_DV))


def _row_select(rows, t, new, old):
    return jnp.where(rows == t, new, old)


def _step_ab_kernel(qkv_ref, rest_ref, gates_ref, cos_ref, sin_ref, wconv_ref, prow_ref, ggain_ref, rgain_ref,
                    buf_ref, sd_ref, sr_ref, mixed_ref, nbuf_ref, nsd_ref, nsr_ref):
    tb = qkv_ref.shape[0]
    u = qkv_ref[...]
    w = wconv_ref[...]
    b0 = buf_ref[:, 0:A_CONV_CH]
    b1 = buf_ref[:, A_CONV_CH:2 * A_CONV_CH]
    b2 = buf_ref[:, 2 * A_CONV_CH:3 * A_CONV_CH]
    conv = b0 * w[0:1] + b1 * w[1:2] + b2 * w[2:3] + u * w[3:4]
    nbuf_ref[:, 0:A_CONV_CH] = b1
    nbuf_ref[:, A_CONV_CH:2 * A_CONV_CH] = b2
    nbuf_ref[:, 2 * A_CONV_CH:3 * A_CONV_CH] = u
    act = _silu(conv)
    g = gates_ref[...]
    eg_all = jnp.exp(-jnp.exp(prow_ref[0:1, :]) * _softplus(g + prow_ref[1:2, :]))
    beta_all = jax.nn.sigmoid(g)
    cosf = cos_ref[...]
    sinf = sin_ref[...]
    base = HEADS * DK
    rows = lax.broadcasted_iota(jnp.int32, (tb, DK), 0)
    items = [(t, h) for t in range(tb) for h in range(HEADS)]

    qs = [_l2(act[:, h * DK:(h + 1) * DK]) * (DK ** -0.5) for h in range(HEADS)]
    ks = [_l2(act[:, (HEADS + h) * DK:(HEADS + h + 1) * DK]) for h in range(HEADS)]
    k_s = [jnp.zeros((tb, A_DV), F32) for _ in range(HEADS)]
    for t, h in items:
        k_s[h] = _row_select(rows, t, _mm(ks[h], sd_ref[t, h]), k_s[h])
    v_new = []
    for h in range(HEADS):
        v = act[:, (2 * HEADS + h) * DK:(2 * HEADS + h + 1) * DK]
        v_new.append(beta_all[:, HEADS + h:HEADS + h + 1] * (v - eg_all[:, h:h + 1] * k_s[h]))
    for t, h in items:
        nsd_ref[t, h] = (sd_ref[t, h] * eg_all[t:t + 1, h:h + 1]
                         + _mm_tn(jnp.where(rows == t, ks[h], 0.0), v_new[h]))
    o_a = [jnp.zeros((tb, A_DV), F32) for _ in range(HEADS)]
    for t, h in items:
        o_a[h] = _row_select(rows, t, _mm(qs[h], nsd_ref[t, h]), o_a[h])
    for h in range(HEADS):
        ga = rest_ref[:, h * DK:(h + 1) * DK]
        mixed_ref[:, h * DK:(h + 1) * DK] = _rms(o_a[h], ggain_ref[...]) * _silu(ga)

    qrs, krs = [], []
    for h in range(HEADS):
        qb = rest_ref[:, base + h * DK:base + (h + 1) * DK]
        kb = rest_ref[:, 2 * base + h * DK:2 * base + (h + 1) * DK]
        qrs.append(qb * cosf + pltpu.roll(qb, DK // 2, 1) * sinf)
        krs.append((kb * cosf + pltpu.roll(kb, DK // 2, 1) * sinf) * (DK ** -0.5))
    for t, h in items:
        vb = rest_ref[:, 3 * base + h * DK:3 * base + (h + 1) * DK]
        nsr_ref[t, h] = (sr_ref[t, h] * math.exp(LOG_GAMMA[h])
                         + _mm_tn(jnp.where(rows == t, krs[h], 0.0), vb))
    o_b = [jnp.zeros((tb, DK), F32) for _ in range(HEADS)]
    for t, h in items:
        o_b[h] = _row_select(rows, t, _mm(qrs[h], nsr_ref[t, h]), o_b[h])
    for h in range(HEADS):
        gb = rest_ref[:, 4 * base + h * DK:4 * base + (h + 1) * DK]
        mixed_ref[:, base + h * DK:base + (h + 1) * DK] = _rms(o_b[h], rgain_ref[...]) * _silu(gb)


def _mix_ab_sample(qkv, rest, gates, conv_buf, s_delta, s_ret, w_conv, a_log, dt_bias, gdn_gain, ret_gain):
    nb = qkv.shape[0]
    tb = SAMPLE_TB
    assert nb % tb == 0
    cosf, sinf = _rope_tables(PAST_LEN, 1)
    prow, _ = _gate_params(a_log, dt_bias)
    tok = lambda b: (b, 0)
    tok4 = lambda b: (b, 0, 0, 0)
    const = lambda b: (0, 0)
    n_buf = (CONV_W - 1) * A_CONV_CH
    mixed, nbuf, nsd, nsr = pl.pallas_call(
        _step_ab_kernel,
        grid=(nb // tb,),
        in_specs=[pl.BlockSpec((tb, A_CONV_CH), tok),
                  pl.BlockSpec((tb, rest.shape[1]), tok),
                  pl.BlockSpec((tb, 128), tok),
                  pl.BlockSpec((1, DK), const),
                  pl.BlockSpec((1, DK), const),
                  pl.BlockSpec((CONV_W, A_CONV_CH), const),
                  pl.BlockSpec((8, 128), const),
                  pl.BlockSpec((1, A_DV), const),
                  pl.BlockSpec((1, DK), const),
                  pl.BlockSpec((tb, n_buf), tok),
                  pl.BlockSpec((tb, HEADS, DK, A_DV), tok4),
                  pl.BlockSpec((tb, HEADS, DK, DK), tok4)],
        out_specs=[pl.BlockSpec((tb, 2 * HEADS * DK), tok),
                   pl.BlockSpec((tb, n_buf), tok),
                   pl.BlockSpec((tb, HEADS, DK, A_DV), tok4),
                   pl.BlockSpec((tb, HEADS, DK, DK), tok4)],
        out_shape=[jax.ShapeDtypeStruct((nb, 2 * HEADS * DK), F32),
                   jax.ShapeDtypeStruct((nb, n_buf), F32),
                   jax.ShapeDtypeStruct((nb, HEADS, DK, A_DV), F32),
                   jax.ShapeDtypeStruct((nb, HEADS, DK, DK), F32)],
        compiler_params=_cparams("parallel"),
        name="mix_ab_sample",
    )(qkv, rest, gates, cosf, sinf, w_conv, prow, gdn_gain.reshape(1, A_DV), ret_gain.reshape(1, DK),
      conv_buf.reshape(nb, n_buf), s_delta, s_ret)
    return mixed, nbuf.reshape(nb, CONV_W - 1, A_CONV_CH), nsd, nsr


def _step_c_kernel(qk_ref, v_ref, opre_ref, gates_ref, brow_ref, gain_ref, cm_ref, nv_ref, m_ref,
                   h_ref, ncm_ref, nnv_ref, nm_ref):
    tb = qk_ref.shape[0]
    g = gates_ref[...] + brow_ref[0:1, :]
    logf = _log_sigmoid(g)
    rows = lax.broadcasted_iota(jnp.int32, (tb, DK), 0)
    items = [(t, h) for t in range(tb) for h in range(HEADS)]
    qs, kws, fs, ms_new, nvs_new = [], [], [], [], []
    for h in range(HEADS):
        i_pre = g[:, h:h + 1]
        q = qk_ref[:, h * DK:(h + 1) * DK]
        k = qk_ref[:, (HEADS + h) * DK:(HEADS + h + 1) * DK] * (DK ** -0.5)
        inter = logf[:, HEADS + h:HEADS + h + 1] + m_ref[:, h:h + 1]
        m_new = jnp.maximum(inter, i_pre)
        f_state = jnp.exp(inter - m_new)
        kw = k * jnp.exp(i_pre - m_new)
        nv_new = nv_ref[:, h * DK:(h + 1) * DK] * f_state + kw
        nnv_ref[:, h * DK:(h + 1) * DK] = nv_new
        nm_ref[:, h:h + 1] = m_new
        qs.append(q)
        kws.append(kw)
        fs.append(f_state)
        ms_new.append(m_new)
        nvs_new.append(nv_new)
    for t, h in items:
        v = v_ref[:, h * C_DV:(h + 1) * C_DV]
        ncm_ref[t, h] = (cm_ref[t, h] * fs[h][t:t + 1, :]
                         + _mm_tn(jnp.where(rows == t, kws[h], 0.0), v))
    rows_v = lax.broadcasted_iota(jnp.int32, (tb, C_DV), 0)
    nums = [jnp.zeros((tb, C_DV), F32) for _ in range(HEADS)]
    for t, h in items:
        nums[h] = _row_select(rows_v, t, _mm(qs[h], ncm_ref[t, h]), nums[h])
    for h in range(HEADS):
        den = jnp.sum(qs[h] * nvs_new[h], axis=1, keepdims=True)
        hh = nums[h] / jnp.maximum(jnp.abs(den), jnp.exp(-ms_new[h]))
        op = opre_ref[:, h * C_DV:(h + 1) * C_DV]
        h_ref[:, h * C_DV:(h + 1) * C_DV] = _rms(hh, gain_ref[...]) * jax.nn.sigmoid(op)


def _mix_c_sample(qk, v, opre, gates, s_c, s_n, s_m, b_gate, gain):
    nb = qk.shape[0]
    tb = SAMPLE_TB
    assert nb % tb == 0
    brow, _ = _bias_params(b_gate)
    tok = lambda b: (b, 0)
    tok4 = lambda b: (b, 0, 0, 0)
    const = lambda b: (0, 0)
    hh, ncm, nnv, nm = pl.pallas_call(
        _step_c_kernel,
        grid=(nb // tb,),
        in_specs=[pl.BlockSpec((tb, 2 * HEADS * DK), tok),
                  pl.BlockSpec((tb, HEADS * C_DV), tok),
                  pl.BlockSpec((tb, HEADS * C_DV), tok),
                  pl.BlockSpec((tb, 128), tok),
                  pl.BlockSpec((8, 128), const),
                  pl.BlockSpec((1, C_DV), const),
                  pl.BlockSpec((tb, HEADS, DK, C_DV), tok4),
                  pl.BlockSpec((tb, HEADS * DK), tok),
                  pl.BlockSpec((tb, HEADS), tok)],
        out_specs=[pl.BlockSpec((tb, HEADS * C_DV), tok),
                   pl.BlockSpec((tb, HEADS, DK, C_DV), tok4),
                   pl.BlockSpec((tb, HEADS * DK), tok),
                   pl.BlockSpec((tb, HEADS), tok)],
        out_shape=[jax.ShapeDtypeStruct((nb, HEADS * C_DV), F32),
                   jax.ShapeDtypeStruct((nb, HEADS, DK, C_DV), F32),
                   jax.ShapeDtypeStruct((nb, HEADS * DK), F32),
                   jax.ShapeDtypeStruct((nb, HEADS), F32)],
        compiler_params=_cparams("parallel"),
        name="mix_c_sample",
    )(qk, v, opre, gates, brow, gain.reshape(1, C_DV), s_c, s_n.reshape(nb, HEADS * DK), s_m)
    return hh, ncm, nnv.reshape(nb, HEADS, DK), nm


C_COLS = ((0, 2 * HEADS * DK), (2 * HEADS * DK, HEADS * C_DV), (2 * HEADS * DK + HEADS * C_DV, HEADS * C_DV),
          (2 * HEADS * DK + 2 * HEADS * C_DV, 2 * HEADS))


def kernel(x_prompt, x_sample, state_conv_a, state_delta_a, state_ret_b, state_mlstm_C, state_mlstm_n, state_mlstm_m,
           norm_mix_a, w_in_a, w_conv_a, a_log, dt_bias, gdn_gain, ret_gain, w_out_a,
           norm_mix_c, w_in_c, b_gate_c, mlstm_gain, w_out_c, norm_ffn, w_up, w_down, final_gain):
    batch, seq, _ = x_prompt.shape
    n_s = x_sample.shape[0] * x_sample.shape[1]
    tm = PROMPT_TM
    w_a = w_in_a[0].astype(BF16)
    w_c = w_in_c[0].astype(BF16)
    w_ups = w_up.astype(BF16)
    w_downs = w_down.astype(BF16)
    h = x_prompt.reshape(batch * seq, D_MODEL)
    h_s = x_sample.reshape(n_s, D_MODEL)

    (mixed, conv, sd, sr), (qkv_s, rest_s, gates_s) = _layer0_prompt(
        h, h_s, norm_mix_a[0], w_a, w_conv_a[0], a_log[0], dt_bias[0], gdn_gain[0], ret_gain[0], batch, seq)
    mixed_s, conv_s, sd_s, sr_s = _mix_ab_sample(qkv_s, rest_s, gates_s, state_conv_a[0], state_delta_a[0],
                                                 state_ret_b[0], w_conv_a[0], a_log[0], dt_bias[0], gdn_gain[0],
                                                 ret_gain[0])
    h, h_s = _out_mlp(h, mixed, h_s, mixed_s, w_out_a[0].astype(BF16), norm_ffn[0], w_ups, w_downs, 0, None, tm)

    (qk, v, opre, gates_c), (qk_s, v_s, opre_s, gates_cs) = _norm_proj(h, h_s, norm_mix_c[0], w_c, C_COLS, tm)
    hm, cm, nv, m = _mix_c_prompt(qk, v, opre, gates_c, batch, seq, b_gate_c[0], mlstm_gain[0])
    hm_s, cm_s, nv_s, m_s = _mix_c_sample(qk_s, v_s, opre_s, gates_cs, state_mlstm_C[0], state_mlstm_n[0],
                                          state_mlstm_m[0], b_gate_c[0], mlstm_gain[0])
    y, y_s = _out_mlp(h, hm, h_s, hm_s, w_out_c[0].astype(BF16), norm_ffn[1], w_ups, w_downs, 1, final_gain, tm)

    return (y.reshape(x_prompt.shape), y_s.reshape(x_sample.shape),
            conv[None], sd[None], sr[None], cm[None], nv[None], m[:, :, 0][None],
            conv_s[None], sd_s[None], sr_s[None], cm_s[None], nv_s[None], m_s[None])
```

```python
import functools
import math

import jax
import jax.numpy as jnp
import numpy as np
from jax import lax
from jax.experimental import pallas as pl
from jax.experimental.pallas import tpu as pltpu

F32 = jnp.float32
BF16 = jnp.bfloat16

D_MODEL = 1024
D_FF = 4 * D_MODEL
CHUNK = 64
EPS = 1e-6
NEG = -1e30
HEADS = 4
DK = 128
A_DV = 128
C_DV = 256
CONV_W = 4
A_CONV_CH = 3 * HEADS * DK
ROPE_BASE = 10000.0
PAST_LEN = 16384
LOG_GAMMA = tuple(math.log1p(-(2.0 ** (-5.0 - h))) for h in range(HEADS))

VMEM_LIMIT_BYTES = 56 * 1024 * 1024
PROMPT_TM = 512
AB_SUB = 8
C_SUB = 16
SAMPLE_TB = 8


def _cparams(*sem):
    return pltpu.CompilerParams(dimension_semantics=sem, vmem_limit_bytes=VMEM_LIMIT_BYTES)


def _mm(a, b):
    return jnp.dot(a.astype(BF16), b.astype(BF16), preferred_element_type=F32)


def _mm_nt(a, b):
    return lax.dot_general(a.astype(BF16), b.astype(BF16), (((1,), (1,)), ((), ())), preferred_element_type=F32)


def _mm_tn(a, b):
    return lax.dot_general(a.astype(BF16), b.astype(BF16), (((0,), (0,)), ((), ())), preferred_element_type=F32)


def _split(a):
    hi = a.astype(BF16)
    lo = (a - hi.astype(F32)).astype(BF16)
    return hi, lo


def _softplus(x):
    return jnp.maximum(x, 0.0) + jnp.log1p(jnp.exp(-jnp.abs(x)))


def _log_sigmoid(x):
    return -_softplus(-x)


def _silu(x):
    return x * jax.nn.sigmoid(x)


def _rms(x, gain):
    return x * lax.rsqrt(jnp.mean(x * x, axis=-1, keepdims=True) + EPS) * gain


def _l2(t):
    return t * lax.rsqrt(jnp.sum(t * t, axis=-1, keepdims=True) + EPS)


def _interleave(gens):
    gens = list(gens)
    while gens:
        alive = []
        for gen in gens:
            try:
                next(gen)
                alive.append(gen)
            except StopIteration:
                pass
        gens = alive


def _chunk_cumsum_cols(x, tri_c):
    n = x.shape[0] // CHUNK
    side = lambda a: jnp.concatenate([a[i * CHUNK:(i + 1) * CHUNK] for i in range(n)], axis=1)
    hi, lo = _split(x)
    out = (jnp.dot(tri_c, side(hi), preferred_element_type=F32)
           + jnp.dot(tri_c, side(lo), preferred_element_type=F32))
    return jnp.concatenate([out[:, i * 128:(i + 1) * 128] for i in range(n)], axis=0)


def _lockstep(gens):
    gens = list(gens)
    while gens:
        alive = []
        for gen in gens:
            try:
                next(gen)
                alive.append(gen)
            except StopIteration:
                pass
        gens = alive
        if gens:
            yield


def _head_block_rows(x):
    c, n = x.shape
    t = n // HEADS
    z = jnp.zeros((c, t), x.dtype)
    return jnp.concatenate(
        [jnp.concatenate([x[:, h * t:(h + 1) * t] if g == h else z for g in range(HEADS)], axis=1)
         for h in range(HEADS)], axis=0)


def _head_block_diag(y):
    c, n = y.shape
    per_tile = 128 // c
    assert n == HEADS * c and 128 % c == 0 and HEADS % per_tile == 0
    lane = lax.broadcasted_iota(jnp.int32, (c, 128), 1)
    z = jnp.zeros((c, 128), y.dtype)
    blocks = []
    for h in range(HEADS):
        t = h // per_tile
        lo = (h % per_tile) * c
        kept = jnp.where((lane >= lo) & (lane < lo + c), y[:, t * 128:(t + 1) * 128], z)
        blocks.append(jnp.concatenate([kept if g == t else z for g in range(n // 128)], axis=1))
    return jnp.concatenate(blocks, axis=0)


def _wide_mm(x, y):
    return jnp.dot(x.astype(BF16), _head_block_diag(y.astype(BF16)), preferred_element_type=F32)


def _tri_inv_wide(ns):
    c, wd = ns[0].shape
    ii = lax.broadcasted_iota(jnp.int32, (c, wd), 0)
    jl = lax.broadcasted_iota(jnp.int32, (c, wd), 1) & (c - 1)
    eye = (ii == jl).astype(F32)
    ts = [eye - jnp.where((ii >> 1) == (jl >> 1), n, 0.0) for n in ns]
    for lvl in range(1, int(math.log2(c))):
        off = ((ii >> (lvl + 1)) == (jl >> (lvl + 1))) & ((ii >> lvl) != (jl >> lvl))
        ys = [_wide_mm(jnp.where(off, n, 0.0), t) for n, t in zip(ns, ts)]
        yield
        ts = [t - _wide_mm(t, y) for t, y in zip(ts, ys)]
        yield
    return [t - eye for t in ts]


def _out_width(n):
    return max(n, 128)


def _project_rows(x_ref, g_ref, pieces, out_refs):
    xn = _rms(x_ref[...], g_ref[...]).astype(BF16)
    for (w_ref, col0, n), o_ref in zip(pieces, out_refs):
        res = jnp.dot(xn, w_ref[:, col0:col0 + n], preferred_element_type=F32)
        if n < _out_width(n):
            res = jnp.concatenate([res, jnp.zeros((res.shape[0], _out_width(n) - n), F32)], axis=1)
        o_ref[...] = res


def _norm_proj_kernel(x_ref, xs_ref, g_ref, w_ref, *out_refs, cols):
    pieces = [(w_ref, c0, n) for c0, n in cols]
    _project_rows(x_ref, g_ref, pieces, out_refs[:len(cols)])

    @pl.when(pl.program_id(0) == 0)
    def _():
        _project_rows(xs_ref, g_ref, pieces, out_refs[len(cols):])


def _norm_proj(x, x_s, gain, w, cols, tm):
    t, ts = x.shape[0], x_s.shape[0]
    assert t % tm == 0
    widths = [_out_width(n) for _, n in cols]
    row = lambda i: (i, 0)
    const = lambda i: (0, 0)
    outs = pl.pallas_call(
        functools.partial(_norm_proj_kernel, cols=cols),
        grid=(t // tm,),
        in_specs=[pl.BlockSpec((tm, D_MODEL), row), pl.BlockSpec((ts, D_MODEL), const),
                  pl.BlockSpec((1, D_MODEL), const),
                  pl.BlockSpec(w.shape, const, pipeline_mode=pl.Buffered(1))],
        out_specs=[pl.BlockSpec((tm, n), row) for n in widths] + [pl.BlockSpec((ts, n), const) for n in widths],
        out_shape=[jax.ShapeDtypeStruct((t, n), F32) for n in widths]
                  + [jax.ShapeDtypeStruct((ts, n), F32) for n in widths],
        compiler_params=_cparams("arbitrary"),
        name="norm_proj",
    )(x, x_s, gain.reshape(1, D_MODEL), w)
    return outs[:len(widths)], outs[len(widths):]


def _layer0_prompt_kernel(x_ref, xs_ref, g_ref, w_ref, wconv_ref, cos_ref, sin_ref,
                          prow_ref, pcol_ref, tri_ref, ggain_ref, rgain_ref,
                          mixed_ref, conv_ref, sd_ref, sr_ref, qkvs_ref, rests_ref, gatess_ref,
                          xp_ref, raw_ref, wrest_ref, qkv_ref, rest_ref, gates_ref, *, tiles_per_seq, n_sub):
    tm = x_ref.shape[0]
    hd = HEADS * DK
    n_gate = 2 * HEADS
    gate_piece = (w_ref, A_CONV_CH, n_gate)

    @pl.when(pl.program_id(0) == 0)
    def _():
        wrest_ref[...] = w_ref[:, A_CONV_CH + n_gate:A_CONV_CH + n_gate + 5 * hd]
        _project_rows(xs_ref, g_ref, [(w_ref, 0, A_CONV_CH), (wrest_ref, 0, 5 * hd), gate_piece],
                      (qkvs_ref, rests_ref, gatess_ref))

    xn = _rms(x_ref[...], g_ref[...]).astype(BF16)

    @pl.when(pl.program_id(0) % tiles_per_seq == 0)
    def _():
        xp_ref[0:8, :] = jnp.zeros((8, A_CONV_CH), F32)
        sd_ref[...] = jnp.zeros_like(sd_ref)
        sr_ref[...] = jnp.zeros_like(sr_ref)

    w = wconv_ref[...]
    base = A_CONV_CH

    def conv_part(part, slot, r0, r1):
        c0, c1 = part * hd, (part + 1) * hd
        raw = raw_ref[slot, r0:r1, :]
        xp_ref[8 + r0:8 + r1, c0:c1] = raw
        conv = (xp_ref[5 + r0:5 + r1, c0:c1] * w[0:1, c0:c1] + xp_ref[6 + r0:6 + r1, c0:c1] * w[1:2, c0:c1]
                + xp_ref[7 + r0:7 + r1, c0:c1] * w[2:3, c0:c1] + raw * w[3:4, c0:c1])
        if r1 == tm:
            rows = r1 - r0
            xp_ref[0:8, c0:c1] = raw[rows - 8:rows, :]
            conv_ref[0, :, c0:c1] = raw[rows - 3:rows, :]
        act = _silu(conv)
        if part == 2:
            qkv_ref[r0:r1, c0:c1] = act
        else:
            scale = DK ** -0.5 if part == 0 else 1.0
            for h in range(HEADS):
                qkv_ref[r0:r1, c0 + h * DK:c0 + (h + 1) * DK] = _l2(act[:, h * DK:(h + 1) * DK]) * scale

    def rest_part(part, slot, r0, r1):
        c0, c1 = part * hd, (part + 1) * hd
        raw = raw_ref[slot, r0:r1, :]
        if part in (0, 4):
            rest_ref[r0:r1, c0:c1] = _silu(raw)
        elif part == 3:
            rest_ref[r0:r1, c0:c1] = raw
        else:
            scale = 1.0 if part == 1 else DK ** -0.5
            for h in range(HEADS):
                t = raw[:, h * DK:(h + 1) * DK]
                rest_ref[r0:r1, c0 + h * DK:c0 + (h + 1) * DK] = (
                    (t * cos_ref[r0:r1, :] + pltpu.roll(t, DK // 2, 1) * sin_ref[r0:r1, :]) * scale)

    def gates_part(_, slot, r0, r1):
        gates_ref[r0:r1, :] = jnp.concatenate(
            [raw_ref[slot, r0:r1, 0:n_gate], jnp.zeros((r1 - r0, 128 - n_gate), F32)], axis=1)

    groups = [(w_ref, 0, hd, functools.partial(conv_part, 0)),
              (wrest_ref, 3 * hd, hd, functools.partial(rest_part, 3)),
              (w_ref, hd, hd, functools.partial(conv_part, 1)),
              (wrest_ref, 0, hd, functools.partial(rest_part, 0)),
              (w_ref, 2 * hd, hd, functools.partial(conv_part, 2)),
              (wrest_ref, hd, hd, functools.partial(rest_part, 1)),
              gate_piece + (functools.partial(gates_part, 0),),
              (wrest_ref, 2 * hd, hd, functools.partial(rest_part, 2)),
              (wrest_ref, 4 * hd, hd, functools.partial(rest_part, 4))]

    def project(n, slot):
        wg_ref, col0, width, _ = groups[n]
        for j in range(0, width, 256):
            wj = min(256, width - j)
            raw_ref[slot, :, j:j + wj] = jnp.dot(xn, wg_ref[:, col0 + j:col0 + j + wj],
                                                 preferred_element_type=F32)
            yield

    def epilogue(n, slot):
        for r0 in range(0, tm, 128):
            groups[n][3](slot, r0, min(r0 + 128, tm))
            yield

    def projection_track(order):
        yield from project(order[0], 0)
        for idx, n in enumerate(order):
            nxt = [project(order[idx + 1], (idx + 1) % 2)] if idx + 1 < len(order) else []
            yield from _lockstep([epilogue(n, idx % 2)] + nxt)
            yield

    _interleave([projection_track([6, 0, 1, 2, 3, 4, 5, 7, 8])])
    gdn_stages, delta_track, ret_track, finish = _mix_ab_tracks(
        qkv_ref, rest_ref, gates_ref, prow_ref, pcol_ref, tri_ref, ggain_ref, rgain_ref,
        mixed_ref, sd_ref, sr_ref, n_sub)
    _interleave(gdn_stages)
    _interleave([delta_track(), ret_track()])
    finish()


def _layer0_prompt(x, x_s, gain, w, w_conv, a_log, dt_bias, gdn_gain, ret_gain, batch, seq):
    t, ts = x.shape[0], x_s.shape[0]
    tm = AB_SUB * CHUNK
    assert t % tm == 0 and seq % tm == 0
    tiles_per_seq = seq // tm
    cosf, sinf = _rope_tables(0, seq)
    prow, pcol = _gate_params(a_log, dt_bias)
    tri = _chunk_tri(tm)
    row = lambda i: (i, 0)
    const = lambda i: (0, 0)
    pos = lambda i: (i % tiles_per_seq, 0)
    per_seq3 = lambda i: (i // tiles_per_seq, 0, 0)
    per_seq4 = lambda i: (i // tiles_per_seq, 0, 0, 0)
    n_rest = 5 * HEADS * DK
    outs = pl.pallas_call(
        functools.partial(_layer0_prompt_kernel, tiles_per_seq=tiles_per_seq, n_sub=AB_SUB),
        grid=(t // tm,),
        in_specs=[pl.BlockSpec((tm, D_MODEL), row),
                  pl.BlockSpec((ts, D_MODEL), const),
                  pl.BlockSpec((1, D_MODEL), const),
                  pl.BlockSpec(w.shape, const, pipeline_mode=pl.Buffered(1)),
                  pl.BlockSpec((CONV_W, A_CONV_CH), const),
                  pl.BlockSpec((tm, DK), pos),
                  pl.BlockSpec((tm, DK), pos),
                  pl.BlockSpec((8, 128), const),
                  pl.BlockSpec((8, 128), const),
                  pl.BlockSpec(tri.shape, const),
                  pl.BlockSpec((1, A_DV), const),
                  pl.BlockSpec((1, DK), const)],
        out_specs=[pl.BlockSpec((tm, 2 * HEADS * DK), row),
                   pl.BlockSpec((1, CONV_W - 1, A_CONV_CH), per_seq3),
                   pl.BlockSpec((1, HEADS, DK, A_DV), per_seq4),
                   pl.BlockSpec((1, HEADS, DK, DK), per_seq4),
                   pl.BlockSpec((ts, A_CONV_CH), const),
                   pl.BlockSpec((ts, n_rest), const),
                   pl.BlockSpec((ts, 128), const)],
        out_shape=[jax.ShapeDtypeStruct((t, 2 * HEADS * DK), BF16),
                   jax.ShapeDtypeStruct((batch, CONV_W - 1, A_CONV_CH), F32),
                   jax.ShapeDtypeStruct((batch, HEADS, DK, A_DV), F32),
                   jax.ShapeDtypeStruct((batch, HEADS, DK, DK), F32),
                   jax.ShapeDtypeStruct((ts, A_CONV_CH), F32),
                   jax.ShapeDtypeStruct((ts, n_rest), F32),
                   jax.ShapeDtypeStruct((ts, 128), F32)],
        scratch_shapes=[pltpu.VMEM((tm + 8, A_CONV_CH), F32), pltpu.VMEM((2, tm, HEADS * DK), F32),
                        pltpu.VMEM((D_MODEL, n_rest), BF16),
                        pltpu.VMEM((tm, A_CONV_CH), F32), pltpu.VMEM((tm, n_rest), F32), pltpu.VMEM((tm, 128), F32)],
        compiler_params=_cparams("arbitrary"),
        name="layer0_prompt",
    )(x, x_s, gain.reshape(1, D_MODEL), w, w_conv, cosf, sinf, prow, pcol, tri,
      gdn_gain.reshape(1, A_DV), ret_gain.reshape(1, DK))
    return outs[:4], outs[4:]


def _out_mlp_kernel(h_ref, m_ref, hs_ref, ms_ref, wout_ref, gffn_ref, wup_ref, wdown_ref, *rest, final):
    if final:
        gfin_ref, o_ref, os_ref = rest
    else:
        o_ref, os_ref = rest

    def block(h_r, m_r, o_r):
        h = h_r[...] + jnp.dot(m_r[...].astype(BF16), wout_ref[...], preferred_element_type=F32)
        xn = _rms(h, gffn_ref[...]).astype(BF16)
        acc = h
        step = 1024
        for j in range(D_FF // step):
            hid = jnp.dot(xn, wup_ref[:, j * step:(j + 1) * step], preferred_element_type=F32)
            hid = jnp.maximum(hid, 0.0)
            acc = acc + jnp.dot((hid * hid).astype(BF16), wdown_ref[j * step:(j + 1) * step, :],
                                preferred_element_type=F32)
        if final:
            acc = _rms(acc, gfin_ref[...])
        o_r[...] = acc

    block(h_ref, m_ref, o_ref)

    @pl.when(pl.program_id(0) == 0)
    def _():
        block(hs_ref, ms_ref, os_ref)


def _out_mlp(h, mix, h_s, mix_s, w_out, g_ffn, w_up, w_down, layer, g_final, tm):
    t, ts = h.shape[0], h_s.shape[0]
    assert t % tm == 0
    final = g_final is not None
    row = lambda i: (i, 0)
    const = lambda i: (0, 0)
    this_layer = lambda i: (layer, 0, 0)
    once = pl.Buffered(1)
    in_specs = [pl.BlockSpec((tm, D_MODEL), row), pl.BlockSpec((tm, mix.shape[1]), row),
                pl.BlockSpec((ts, D_MODEL), const), pl.BlockSpec((ts, mix_s.shape[1]), const),
                pl.BlockSpec(w_out.shape, const, pipeline_mode=once), pl.BlockSpec((1, D_MODEL), const),
                pl.BlockSpec((None,) + w_up.shape[1:], this_layer, pipeline_mode=once),
                pl.BlockSpec((None,) + w_down.shape[1:], this_layer, pipeline_mode=once)]
    args = [h, mix, h_s, mix_s, w_out, g_ffn.reshape(1, D_MODEL), w_up, w_down]
    if final:
        in_specs.append(pl.BlockSpec((1, D_MODEL), const))
        args.append(g_final.reshape(1, D_MODEL))
    return pl.pallas_call(
        functools.partial(_out_mlp_kernel, final=final),
        grid=(t // tm,),
        in_specs=in_specs,
        out_specs=[pl.BlockSpec((tm, D_MODEL), row), pl.BlockSpec((ts, D_MODEL), const)],
        out_shape=[jax.ShapeDtypeStruct((t, D_MODEL), F32), jax.ShapeDtypeStruct((ts, D_MODEL), F32)],
        compiler_params=_cparams("arbitrary"),
        name="out_mlp",
    )(*args)


def _chunk_masks(c):
    ii = lax.broadcasted_iota(jnp.int32, (c, c), 0)
    jj = lax.broadcasted_iota(jnp.int32, (c, c), 1)
    return ii, jj


def _mix_ab_tracks(qkv_ref, rest_ref, gates_ref, prow_ref, pcol_ref, tri_ref,
                   ggain_ref, rgain_ref, mixed_ref, sd_ref, sr_ref, n_sub):
    c = CHUNK
    lb = n_sub * c

    g = gates_ref[...]
    g_t = g.T
    neg_a_row = -jnp.exp(prow_ref[0:1, :])
    dt_row = prow_ref[1:2, :]
    neg_a_col = -jnp.exp(pcol_ref[0:HEADS, 0:1])
    dt_col = pcol_ref[0:HEADS, 1:2]
    la_cols = neg_a_row * _softplus(g + dt_row)
    beta_cols = jax.nn.sigmoid(g)
    la_rows = neg_a_col * _softplus(g_t[0:HEADS, :] + dt_col)

    tri = tri_ref[...]
    g_cols = _chunk_cumsum_cols(la_cols, tri_ref[0:c, 0:c])
    lr_hi, lr_lo = _split(jnp.concatenate([la_rows, jnp.zeros_like(la_rows)], axis=0))
    nt = lambda x, y: lax.dot_general(x, y, (((1,), (1,)), ((), ())), preferred_element_type=F32)
    g_rows = nt(lr_hi, tri) + nt(lr_lo, tri)
    hd = HEADS * DK
    wd = HEADS * c
    lanes = lambda col, n: jnp.broadcast_to(col, (lb, n))
    g_wide = jnp.concatenate([lanes(g_cols[:, h:h + 1], DK) for h in range(HEADS)], axis=1)
    beta_wide = jnp.concatenate([lanes(beta_cols[:, HEADS + h:HEADS + h + 1], DK) for h in range(HEADS)], axis=1)
    g_half = jnp.concatenate([lanes(g_cols[:, h:h + 1], c) for h in range(HEADS)], axis=1)
    eg_wide = jnp.exp(g_wide)

    ii = lax.broadcasted_iota(jnp.int32, (c, wd), 0)
    jl = lax.broadcasted_iota(jnp.int32, (c, wd), 1) & (c - 1)
    causal = ii >= jl
    strict = ii > jl
    lane_hd = lax.broadcasted_iota(jnp.int32, (1, hd), 1)
    lane_wd = lax.broadcasted_iota(jnp.int32, (1, wd), 1)
    lg_hd = jnp.full((1, hd), LOG_GAMMA[HEADS - 1], F32)
    lg_wd = jnp.full((1, wd), LOG_GAMMA[HEADS - 1], F32)
    for h in range(HEADS - 2, -1, -1):
        lg_hd = jnp.where(lane_hd < (h + 1) * DK, LOG_GAMMA[h], lg_hd)
        lg_wd = jnp.where(lane_wd < (h + 1) * c, LOG_GAMMA[h], lg_wd)
    pos_col = lax.broadcasted_iota(jnp.int32, (c, hd), 0).astype(F32)
    ret_dec = jnp.exp(jnp.where(causal, (ii - jl).astype(F32) * lg_wd, NEG))
    ret_q_scale = jnp.exp((pos_col + 1.0) * lg_hd)
    ret_k_scale = jnp.exp((float(c - 1) - pos_col) * lg_hd)

    ggain = ggain_ref[...]
    rgain = rgain_ref[...]
    base = hd
    tile = lambda x, h: x[:, h * DK:(h + 1) * DK]
    s_delta = [sd_ref[0, h] for h in range(HEADS)]
    s_ret = [sr_ref[0, h] for h in range(HEADS)]

    chunks = list(range(n_sub))
    items = [(ci, h) for ci in chunks for h in range(HEADS)]
    rng = lambda ci: (ci * c, (ci + 1) * c)
    pre = {it: {} for it in items}
    pre_b = {it: {} for it in items}
    wide = {ci: {} for ci in chunks}
    nt = lambda x, y: lax.dot_general(x, y, (((1,), (1,)), ((), ())), preferred_element_type=F32)

    def gdn_stage(ci):
        w_ = wide[ci]
        r0, r1 = rng(ci)
        q = qkv_ref[r0:r1, 0:hd]
        k = qkv_ref[r0:r1, hd:2 * hd]
        beta = beta_wide[r0:r1]
        g_col = g_wide[r0:r1]
        kb = k * beta
        g_row = jnp.concatenate([g_rows[h:h + 1, r0:r1] for h in range(HEADS)], axis=1)
        dec_causal = jnp.exp(jnp.where(causal, g_half[r0:r1] - g_row, NEG))
        prod = nt(jnp.concatenate([q, kb], axis=0).astype(BF16), _head_block_rows(k.astype(BF16)))
        yield
        w_['n'] = prod[c:2 * c] * jnp.where(strict, dec_causal, 0.0)
        w_['qk'] = prod[0:c] * dec_causal
        yield
        v = qkv_ref[r0:r1, 2 * hd:3 * hd]
        eg = eg_wide[r0:r1]
        g_last = g_col[c - 1:c, :]
        rhs_u, rhs_w, qd = v * beta, kb * eg, q * eg
        kd = k * jnp.exp(g_last - g_col)
        gl = jnp.exp(g_last)
        for h in range(HEADS):
            pre[ci, h].update(rhs=jnp.concatenate([tile(rhs_u, h), tile(rhs_w, h)], axis=1), qd=tile(qd, h),
                              kd=tile(kd, h), gl=tile(gl, h))

    def ret_stage(ci):
        r0, r1 = rng(ci)
        qr = rest_ref[r0:r1, base:2 * base]
        kr = rest_ref[r0:r1, 2 * base:3 * base]
        vb = rest_ref[r0:r1, 3 * base:4 * base]
        qk = nt(qr.astype(BF16), _head_block_rows(kr.astype(BF16))) * ret_dec
        qd = qr * ret_q_scale
        kd = kr * ret_k_scale
        yield
        for a in range(HEADS // 2):
            h0, h1 = 2 * a, 2 * a + 1
            z = jnp.zeros((c, DK), F32)
            rhs = jnp.concatenate([jnp.concatenate([tile(vb, h0), z], axis=1),
                                   jnp.concatenate([z, tile(vb, h1)], axis=1)], axis=0)
            intra = _mm(qk[:, a * 2 * c:(a + 1) * 2 * c], rhs)
            pre_b[ci, h0]['intra'] = intra[:, 0:DK]
            pre_b[ci, h1]['intra'] = intra[:, DK:2 * DK]
        yield
        for h in range(HEADS):
            pre_b[ci, h]['qd'] = tile(qd, h)
            pre_b[ci, h]['kv'] = _mm_tn(tile(kd, h), tile(vb, h))

    def solve_stage():
        t_offs = yield from _tri_inv_wide([wide[ci]['n'] for ci in chunks])
        for ci, t_off in zip(chunks, t_offs):
            wide[ci]['t_off'] = t_off

    def pair_rhs(x0, x1):
        z = jnp.zeros_like(x0)
        return jnp.concatenate([jnp.concatenate([x0, z], axis=1), jnp.concatenate([z, x1], axis=1)], axis=0)

    def sol_stage(ci):
        w_ = wide[ci]
        for a in range(HEADS // 2):
            h0, h1 = 2 * a, 2 * a + 1
            r = _mm(w_['t_off'][:, a * 2 * c:(a + 1) * 2 * c], pair_rhs(pre[ci, h0]['rhs'], pre[ci, h1]['rhs']))
            pre[ci, h0]['sol'] = pre[ci, h0]['rhs'] + r[:, 0:2 * A_DV]
            pre[ci, h1]['sol'] = pre[ci, h1]['rhs'] + r[:, 2 * A_DV:4 * A_DV]
        yield
        for a in range(HEADS // 2):
            h0, h1 = 2 * a, 2 * a + 1
            r = _mm(w_['qk'][:, a * 2 * c:(a + 1) * 2 * c], pair_rhs(pre[ci, h0]['sol'], pre[ci, h1]['sol']))
            pre[ci, h0]['qs'] = r[:, 0:2 * A_DV]
            pre[ci, h1]['qs'] = r[:, 2 * A_DV:4 * A_DV]
        for h in range(HEADS):
            d = pre[ci, h]
            d['kts'] = _mm_tn(d['kd'], d['sol'])
        yield
        for h in range(HEADS):
            d = pre[ci, h]
            d['lhs'] = jnp.concatenate([d['kts'][:, A_DV:2 * A_DV], d['qd'] - d['qs'][:, A_DV:2 * A_DV]], axis=0)


    def pair_states(x0, x1):
        return lambda s0, s1: _mm(jnp.concatenate([x0, x1], axis=1), pair_rhs(s0, s1))

    def ret_out_stage(ci, h):
        d = pre_b[ci, h]
        r0, r1 = rng(ci)
        if h % 2 == 0:
            d1 = pre_b[ci, h + 1]
            both = pair_states(d['qd'], d1['qd'])(d['s_in'], d1['s_in'])
            d['qs'], d1['qs'] = both[:, 0:DK], both[:, DK:2 * DK]
        yield
        o = d['qs'] + d['intra']
        ms_ = jnp.mean(o * o, axis=-1, keepdims=True)
        yield
        gb = rest_ref[r0:r1, 4 * base + h * DK:4 * base + (h + 1) * DK]
        mixed_ref[r0:r1, base + h * DK:base + (h + 1) * DK] = (
            o * lax.rsqrt(ms_ + EPS) * rgain * gb).astype(mixed_ref.dtype)

    def delta_step(ci, h):
        d = pre[ci, h]
        if h % 2 == 0:
            d1 = pre[ci, h + 1]
            both = pair_states(d['lhs'], d1['lhs'])(s_delta[h], s_delta[h + 1])
            d['r'], d1['r'] = both[:, 0:A_DV], both[:, A_DV:2 * A_DV]
        yield
        r = d['r']
        d['o'] = r[DK:DK + c] + d['qs'][:, 0:A_DV]
        s_delta[h] = s_delta[h] * d['gl'] - r[0:DK] + d['kts'][:, 0:A_DV]

    def gdn_out_stage(ci, h):
        o = pre[ci, h]['o']
        r0, r1 = rng(ci)
        ms_ = jnp.mean(o * o, axis=-1, keepdims=True)
        yield
        ga = rest_ref[r0:r1, h * DK:(h + 1) * DK]
        mixed_ref[r0:r1, h * A_DV:(h + 1) * A_DV] = (
            o * lax.rsqrt(ms_ + EPS) * ggain * ga).astype(mixed_ref.dtype)

    def delta_track():
        yield from solve_stage()
        yield from _lockstep([sol_stage(ci) for ci in chunks])
        for ci in chunks:
            yield from _lockstep([delta_step(ci, h) for h in range(HEADS)])
            if ci > 0:
                yield from _lockstep([gdn_out_stage(ci - 1, h) for h in range(HEADS)])
        yield from _lockstep([gdn_out_stage(chunks[-1], h) for h in range(HEADS)])

    def ret_track():
        for ci in chunks:
            yield from ret_stage(ci)
            for h in range(HEADS):
                pre_b[ci, h]['s_in'] = s_ret[h]
                s_ret[h] = s_ret[h] * math.exp(c * LOG_GAMMA[h]) + pre_b[ci, h]['kv']
            yield
            yield from _lockstep([ret_out_stage(ci, h) for h in range(HEADS)])

    def finish():
        for h in range(HEADS):
            sd_ref[0, h] = s_delta[h]
            sr_ref[0, h] = s_ret[h]

    return [gdn_stage(ci) for ci in chunks], delta_track, ret_track, finish


def _rope_tables(first, count):
    half = DK // 2
    inv = ROPE_BASE ** (-np.arange(half, dtype=np.float64) / half)
    ang = (first + np.arange(count, dtype=np.float64))[:, None] * inv[None, :]
    cos, sin = np.cos(ang), np.sin(ang)
    return (jnp.asarray(np.concatenate([cos, cos], axis=-1), F32),
            jnp.asarray(np.concatenate([-sin, sin], axis=-1), F32))


def _chunk_tri(lb):
    i = np.arange(lb)[:, None]
    j = np.arange(lb)[None, :]
    return jnp.asarray((i >= j) & (i // CHUNK == j // CHUNK), BF16)


def _gate_params(a_log, dt_bias):
    prow = jnp.zeros((8, 128), F32).at[0, 0:HEADS].set(a_log).at[1, 0:HEADS].set(dt_bias)
    pcol = jnp.zeros((8, 128), F32).at[0:HEADS, 0].set(a_log).at[0:HEADS, 1].set(dt_bias)
    return prow, pcol


def _mix_c_kernel(qk_ref, v_ref, opre_ref, gates_ref, brow_ref, bcol_ref, tri_ref, gain_ref,
                  h_ref, cm_ref, nv_ref, m_ref, *, n_sub):
    c = CHUNK
    lb = n_sub * c
    step = pl.program_id(1)

    @pl.when(step == 0)
    def _():
        cm_ref[...] = jnp.zeros_like(cm_ref)
        nv_ref[...] = jnp.zeros_like(nv_ref)
        m_ref[...] = jnp.zeros_like(m_ref)

    items = [(ci, h) for ci in range(n_sub) for h in range(HEADS)]
    rng = lambda ci: (ci * c, (ci + 1) * c)
    pre = {it: {} for it in items}

    for ci, h in items:
        r0, r1 = rng(ci)
        q = qk_ref[r0:r1, h * DK:(h + 1) * DK]
        k = qk_ref[r0:r1, (HEADS + h) * DK:(HEADS + h + 1) * DK] * (DK ** -0.5)
        pre[ci, h].update(q=q, k=k, qk=_mm_nt(q, k))

    g = gates_ref[...] + brow_ref[0:1, :]
    g_t = gates_ref[...].T[0:2 * HEADS, :] + bcol_ref[0:2 * HEADS, 0:1]
    i_rows = g_t[0:HEADS, :]
    logf_rows = _log_sigmoid(g_t[HEADS:2 * HEADS, :])
    tri = tri_ref[...]
    b_cols = _chunk_cumsum_cols(_log_sigmoid(g), tri_ref[0:c, 0:c])
    lr_hi, lr_lo = _split(jnp.concatenate([logf_rows, jnp.zeros_like(logf_rows)], axis=0))
    nt = lambda x, y: lax.dot_general(x, y, (((1,), (1,)), ((), ())), preferred_element_type=F32)
    b_rows = nt(lr_hi, tri) + nt(lr_lo, tri)
    i_wide = jnp.concatenate([jnp.broadcast_to(g[:, h:h + 1], (lb, 128)) for h in range(HEADS)], axis=1)
    b_wide = jnp.concatenate([jnp.broadcast_to(b_cols[:, HEADS + h:HEADS + h + 1], (lb, 128))
                              for h in range(HEADS)], axis=1)

    ii, jj = _chunk_masks(c)
    causal = ii >= jj
    gain = gain_ref[...]
    wide2 = lambda x: jnp.concatenate([x, x], axis=-1)

    def gates_stage(ci, h):
        d = pre[ci, h]
        r0, r1 = rng(ci)
        b_col = b_wide[r0:r1, h * 128:(h + 1) * 128]
        i_col = i_wide[r0:r1, h * 128:(h + 1) * 128]
        b_last = b_col[c - 1:c, :]
        d_log = jnp.where(causal, b_col[:, 0:c] - b_rows[h:h + 1, r0:r1] + i_rows[h:h + 1, r0:r1], NEG)
        d.update(b_col=b_col, b_last=b_last, d_log=d_log, k_log=b_last - b_col + i_col)
        yield
        d['d_max'] = jnp.max(d_log, axis=1, keepdims=True)

    _interleave(gates_stage(ci, h) for ci, h in items)

    ms = [m_ref[0, h:h + 1, :] for h in range(HEADS)]
    for ci, h in items:
        d = pre[ci, h]
        inter = d['b_col'] + ms[h]
        m_row = jnp.maximum(inter, d['d_max'])
        m_new = m_row[c - 1:c, :]
        d.update(inter=inter, m_row=m_row, m_old=ms[h], m_new=m_new)
        ms[h] = m_new

    def weights_stage(ci, h):
        d = pre[ci, h]
        r0, r1 = rng(ci)
        d['w_inter'] = jnp.exp(d['inter'] - d['m_row'])
        d['f_state'] = jnp.exp(d['b_last'] + d['m_old'] - d['m_new'])
        yield
        d['w_intra'] = jnp.exp(d['d_log'] - d['m_row'][:, 0:c]) * d['qk']
        d['kw'] = d['k'] * jnp.exp(d['k_log'] - d['m_new'])
        yield
        v = v_ref[r0:r1, h * C_DV:(h + 1) * C_DV]
        d['intra'] = _mm(d['w_intra'], v)
        d['kv'] = _mm_tn(d['kw'], v)
        yield
        d['sum_intra'] = jnp.sum(d['w_intra'], axis=1, keepdims=True)
        d['sum_kw'] = jnp.sum(d['kw'], axis=0, keepdims=True)
        d['inv_floor'] = jnp.exp(-d['m_row'])

    _interleave(weights_stage(ci, h) for ci, h in items)

    cms = [cm_ref[0, h] for h in range(HEADS)]
    nvs = [nv_ref[0, h:h + 1, :] for h in range(HEADS)]
    for ci, h in items:
        d = pre[ci, h]
        d['cm_in'], d['nv_in'] = cms[h], nvs[h]
        cms[h] = cms[h] * wide2(d['f_state']) + d['kv']
        nvs[h] = nvs[h] * d['f_state'] + d['sum_kw']

    def output_stage(ci, h):
        d = pre[ci, h]
        r0, r1 = rng(ci)
        qc = _mm(d['q'], d['cm_in'])
        qn = jnp.sum(d['q'] * d['nv_in'], axis=1, keepdims=True)
        yield
        num = wide2(d['w_inter']) * qc + d['intra']
        den = d['w_inter'] * qn + d['sum_intra']
        hh = num / wide2(jnp.maximum(jnp.abs(den), d['inv_floor']))
        yield
        ms_ = jnp.mean(hh * hh, axis=-1, keepdims=True)
        yield
        op = opre_ref[r0:r1, h * C_DV:(h + 1) * C_DV]
        h_ref[r0:r1, h * C_DV:(h + 1) * C_DV] = (
            hh * lax.rsqrt(ms_ + EPS) * gain * jax.nn.sigmoid(op)).astype(h_ref.dtype)

    _interleave(output_stage(ci, h) for ci, h in items)

    for h in range(HEADS):
        cm_ref[0, h] = cms[h]
        nv_ref[0, h:h + 1, :] = nvs[h]
        m_ref[0, h:h + 1, :] = ms[h]


def _bias_params(b_gate):
    brow = jnp.zeros((8, 128), F32).at[0, 0:2 * HEADS].set(b_gate)
    bcol = jnp.zeros((8, 128), F32).at[0:2 * HEADS, 0].set(b_gate)
    return brow, bcol


def _mix_c_prompt(qk, v, opre, gates, batch, seq, b_gate, gain):
    lb = C_SUB * CHUNK
    assert seq % lb == 0
    nc = seq // lb
    brow, bcol = _bias_params(b_gate)
    tri = _chunk_tri(lb)
    tok = lambda b, c: (b * nc + c, 0)
    const = lambda b, c: (0, 0)
    return pl.pallas_call(
        functools.partial(_mix_c_kernel, n_sub=C_SUB),
        grid=(batch, nc),
        in_specs=[pl.BlockSpec((lb, 2 * HEADS * DK), tok),
                  pl.BlockSpec((lb, HEADS * C_DV), tok),
                  pl.BlockSpec((lb, HEADS * C_DV), tok),
                  pl.BlockSpec((lb, 128), tok),
                  pl.BlockSpec((8, 128), const),
                  pl.BlockSpec((8, 128), const),
                  pl.BlockSpec(tri.shape, const),
                  pl.BlockSpec((1, C_DV), const)],
        out_specs=[pl.BlockSpec((lb, HEADS * C_DV), tok),
                   pl.BlockSpec((1, HEADS, DK, C_DV), lambda b, c: (b, 0, 0, 0)),
                   pl.BlockSpec((1, HEADS, DK), lambda b, c: (b, 0, 0)),
                   pl.BlockSpec((1, HEADS, 128), lambda b, c: (b, 0, 0))],
        out_shape=[jax.ShapeDtypeStruct((batch * seq, HEADS * C_DV), BF16),
                   jax.ShapeDtypeStruct((batch, HEADS, DK, C_DV), F32),
                   jax.ShapeDtypeStruct((batch, HEADS, DK), F32),
                   jax.ShapeDtypeStruct((batch, HEADS, 128), F32)],
        compiler_params=_cparams("parallel", "arbitrary"),
        name="mix_c_prompt",
    )(qk, v, opre, gates, brow, bcol, tri, gain.reshape(1, C_DV))


def _row_select(rows, t, new, old):
    return jnp.where(rows == t, new, old)


def _step_ab_kernel(qkv_ref, rest_ref, gates_ref, cos_ref, sin_ref, wconv_ref, prow_ref, ggain_ref, rgain_ref,
                    buf_ref, sd_ref, sr_ref, mixed_ref, nbuf_ref, nsd_ref, nsr_ref):
    tb = qkv_ref.shape[0]
    u = qkv_ref[...]
    w = wconv_ref[...]
    b0 = buf_ref[:, 0:A_CONV_CH]
    b1 = buf_ref[:, A_CONV_CH:2 * A_CONV_CH]
    b2 = buf_ref[:, 2 * A_CONV_CH:3 * A_CONV_CH]
    conv = b0 * w[0:1] + b1 * w[1:2] + b2 * w[2:3] + u * w[3:4]
    nbuf_ref[:, 0:A_CONV_CH] = b1
    nbuf_ref[:, A_CONV_CH:2 * A_CONV_CH] = b2
    nbuf_ref[:, 2 * A_CONV_CH:3 * A_CONV_CH] = u
    act = _silu(conv)
    g = gates_ref[...]
    eg_all = jnp.exp(-jnp.exp(prow_ref[0:1, :]) * _softplus(g + prow_ref[1:2, :]))
    beta_all = jax.nn.sigmoid(g)
    cosf = cos_ref[...]
    sinf = sin_ref[...]
    base = HEADS * DK
    rows = lax.broadcasted_iota(jnp.int32, (tb, DK), 0)
    items = [(t, h) for t in range(tb) for h in range(HEADS)]

    qs = [_l2(act[:, h * DK:(h + 1) * DK]) * (DK ** -0.5) for h in range(HEADS)]
    ks = [_l2(act[:, (HEADS + h) * DK:(HEADS + h + 1) * DK]) for h in range(HEADS)]
    k_s = [jnp.zeros((tb, A_DV), F32) for _ in range(HEADS)]
    for t, h in items:
        k_s[h] = _row_select(rows, t, _mm(ks[h], sd_ref[t, h]), k_s[h])
    v_new = []
    for h in range(HEADS):
        v = act[:, (2 * HEADS + h) * DK:(2 * HEADS + h + 1) * DK]
        v_new.append(beta_all[:, HEADS + h:HEADS + h + 1] * (v - eg_all[:, h:h + 1] * k_s[h]))
    for t, h in items:
        nsd_ref[t, h] = (sd_ref[t, h] * eg_all[t:t + 1, h:h + 1]
                         + _mm_tn(jnp.where(rows == t, ks[h], 0.0), v_new[h]))
    o_a = [jnp.zeros((tb, A_DV), F32) for _ in range(HEADS)]
    for t, h in items:
        o_a[h] = _row_select(rows, t, _mm(qs[h], nsd_ref[t, h]), o_a[h])
    for h in range(HEADS):
        ga = rest_ref[:, h * DK:(h + 1) * DK]
        mixed_ref[:, h * DK:(h + 1) * DK] = _rms(o_a[h], ggain_ref[...]) * _silu(ga)

    qrs, krs = [], []
    for h in range(HEADS):
        qb = rest_ref[:, base + h * DK:base + (h + 1) * DK]
        kb = rest_ref[:, 2 * base + h * DK:2 * base + (h + 1) * DK]
        qrs.append(qb * cosf + pltpu.roll(qb, DK // 2, 1) * sinf)
        krs.append((kb * cosf + pltpu.roll(kb, DK // 2, 1) * sinf) * (DK ** -0.5))
    for t, h in items:
        vb = rest_ref[:, 3 * base + h * DK:3 * base + (h + 1) * DK]
        nsr_ref[t, h] = (sr_ref[t, h] * math.exp(LOG_GAMMA[h])
                         + _mm_tn(jnp.where(rows == t, krs[h], 0.0), vb))
    o_b = [jnp.zeros((tb, DK), F32) for _ in range(HEADS)]
    for t, h in items:
        o_b[h] = _row_select(rows, t, _mm(qrs[h], nsr_ref[t, h]), o_b[h])
    for h in range(HEADS):
        gb = rest_ref[:, 4 * base + h * DK:4 * base + (h + 1) * DK]
        mixed_ref[:, base + h * DK:base + (h + 1) * DK] = _rms(o_b[h], rgain_ref[...]) * _silu(gb)


def _mix_ab_sample(qkv, rest, gates, conv_buf, s_delta, s_ret, w_conv, a_log, dt_bias, gdn_gain, ret_gain):
    nb = qkv.shape[0]
    tb = SAMPLE_TB
    assert nb % tb == 0
    cosf, sinf = _rope_tables(PAST_LEN, 1)
    prow, _ = _gate_params(a_log, dt_bias)
    tok = lambda b: (b, 0)
    tok4 = lambda b: (b, 0, 0, 0)
    const = lambda b: (0, 0)
    n_buf = (CONV_W - 1) * A_CONV_CH
    mixed, nbuf, nsd, nsr = pl.pallas_call(
        _step_ab_kernel,
        grid=(nb // tb,),
        in_specs=[pl.BlockSpec((tb, A_CONV_CH), tok),
                  pl.BlockSpec((tb, rest.shape[1]), tok),
                  pl.BlockSpec((tb, 128), tok),
                  pl.BlockSpec((1, DK), const),
                  pl.BlockSpec((1, DK), const),
                  pl.BlockSpec((CONV_W, A_CONV_CH), const),
                  pl.BlockSpec((8, 128), const),
                  pl.BlockSpec((1, A_DV), const),
                  pl.BlockSpec((1, DK), const),
                  pl.BlockSpec((tb, n_buf), tok),
                  pl.BlockSpec((tb, HEADS, DK, A_DV), tok4),
                  pl.BlockSpec((tb, HEADS, DK, DK), tok4)],
        out_specs=[pl.BlockSpec((tb, 2 * HEADS * DK), tok),
                   pl.BlockSpec((tb, n_buf), tok),
                   pl.BlockSpec((tb, HEADS, DK, A_DV), tok4),
                   pl.BlockSpec((tb, HEADS, DK, DK), tok4)],
        out_shape=[jax.ShapeDtypeStruct((nb, 2 * HEADS * DK), F32),
                   jax.ShapeDtypeStruct((nb, n_buf), F32),
                   jax.ShapeDtypeStruct((nb, HEADS, DK, A_DV), F32),
                   jax.ShapeDtypeStruct((nb, HEADS, DK, DK), F32)],
        compiler_params=_cparams("parallel"),
        name="mix_ab_sample",
    )(qkv, rest, gates, cosf, sinf, w_conv, prow, gdn_gain.reshape(1, A_DV), ret_gain.reshape(1, DK),
      conv_buf.reshape(nb, n_buf), s_delta, s_ret)
    return mixed, nbuf.reshape(nb, CONV_W - 1, A_CONV_CH), nsd, nsr


def _step_c_kernel(qk_ref, v_ref, opre_ref, gates_ref, brow_ref, gain_ref, cm_ref, nv_ref, m_ref,
                   h_ref, ncm_ref, nnv_ref, nm_ref):
    tb = qk_ref.shape[0]
    g = gates_ref[...] + brow_ref[0:1, :]
    logf = _log_sigmoid(g)
    rows = lax.broadcasted_iota(jnp.int32, (tb, DK), 0)
    items = [(t, h) for t in range(tb) for h in range(HEADS)]
    qs, kws, fs, ms_new, nvs_new = [], [], [], [], []
    for h in range(HEADS):
        i_pre = g[:, h:h + 1]
        q = qk_ref[:, h * DK:(h + 1) * DK]
        k = qk_ref[:, (HEADS + h) * DK:(HEADS + h + 1) * DK] * (DK ** -0.5)
        inter = logf[:, HEADS + h:HEADS + h + 1] + m_ref[:, h:h + 1]
        m_new = jnp.maximum(inter, i_pre)
        f_state = jnp.exp(inter - m_new)
        kw = k * jnp.exp(i_pre - m_new)
        nv_new = nv_ref[:, h * DK:(h + 1) * DK] * f_state + kw
        nnv_ref[:, h * DK:(h + 1) * DK] = nv_new
        nm_ref[:, h:h + 1] = m_new
        qs.append(q)
        kws.append(kw)
        fs.append(f_state)
        ms_new.append(m_new)
        nvs_new.append(nv_new)
    for t, h in items:
        v = v_ref[:, h * C_DV:(h + 1) * C_DV]
        ncm_ref[t, h] = (cm_ref[t, h] * fs[h][t:t + 1, :]
                         + _mm_tn(jnp.where(rows == t, kws[h], 0.0), v))
    rows_v = lax.broadcasted_iota(jnp.int32, (tb, C_DV), 0)
    nums = [jnp.zeros((tb, C_DV), F32) for _ in range(HEADS)]
    for t, h in items:
        nums[h] = _row_select(rows_v, t, _mm(qs[h], ncm_ref[t, h]), nums[h])
    for h in range(HEADS):
        den = jnp.sum(qs[h] * nvs_new[h], axis=1, keepdims=True)
        hh = nums[h] / jnp.maximum(jnp.abs(den), jnp.exp(-ms_new[h]))
        op = opre_ref[:, h * C_DV:(h + 1) * C_DV]
        h_ref[:, h * C_DV:(h + 1) * C_DV] = _rms(hh, gain_ref[...]) * jax.nn.sigmoid(op)


def _mix_c_sample(qk, v, opre, gates, s_c, s_n, s_m, b_gate, gain):
    nb = qk.shape[0]
    tb = SAMPLE_TB
    assert nb % tb == 0
    brow, _ = _bias_params(b_gate)
    tok = lambda b: (b, 0)
    tok4 = lambda b: (b, 0, 0, 0)
    const = lambda b: (0, 0)
    hh, ncm, nnv, nm = pl.pallas_call(
        _step_c_kernel,
        grid=(nb // tb,),
        in_specs=[pl.BlockSpec((tb, 2 * HEADS * DK), tok),
                  pl.BlockSpec((tb, HEADS * C_DV), tok),
                  pl.BlockSpec((tb, HEADS * C_DV), tok),
                  pl.BlockSpec((tb, 128), tok),
                  pl.BlockSpec((8, 128), const),
                  pl.BlockSpec((1, C_DV), const),
                  pl.BlockSpec((tb, HEADS, DK, C_DV), tok4),
                  pl.BlockSpec((tb, HEADS * DK), tok),
                  pl.BlockSpec((tb, HEADS), tok)],
        out_specs=[pl.BlockSpec((tb, HEADS * C_DV), tok),
                   pl.BlockSpec((tb, HEADS, DK, C_DV), tok4),
                   pl.BlockSpec((tb, HEADS * DK), tok),
                   pl.BlockSpec((tb, HEADS), tok)],
        out_shape=[jax.ShapeDtypeStruct((nb, HEADS * C_DV), F32),
                   jax.ShapeDtypeStruct((nb, HEADS, DK, C_DV), F32),
                   jax.ShapeDtypeStruct((nb, HEADS * DK), F32),
                   jax.ShapeDtypeStruct((nb, HEADS), F32)],
        compiler_params=_cparams("parallel"),
        name="mix_c_sample",
    )(qk, v, opre, gates, brow, gain.reshape(1, C_DV), s_c, s_n.reshape(nb, HEADS * DK), s_m)
    return hh, ncm, nnv.reshape(nb, HEADS, DK), nm


C_COLS = ((0, 2 * HEADS * DK), (2 * HEADS * DK, HEADS * C_DV), (2 * HEADS * DK + HEADS * C_DV, HEADS * C_DV),
          (2 * HEADS * DK + 2 * HEADS * C_DV, 2 * HEADS))


def kernel(x_prompt, x_sample, state_conv_a, state_delta_a, state_ret_b, state_mlstm_C, state_mlstm_n, state_mlstm_m,
           norm_mix_a, w_in_a, w_conv_a, a_log, dt_bias, gdn_gain, ret_gain, w_out_a,
           norm_mix_c, w_in_c, b_gate_c, mlstm_gain, w_out_c, norm_ffn, w_up, w_down, final_gain):
    batch, seq, _ = x_prompt.shape
    n_s = x_sample.shape[0] * x_sample.shape[1]
    tm = PROMPT_TM
    w_a = w_in_a[0].astype(BF16)
    w_c = w_in_c[0].astype(BF16)
    w_ups = w_up.astype(BF16)
    w_downs = w_down.astype(BF16)
    h = x_prompt.reshape(batch * seq, D_MODEL)
    h_s = x_sample.reshape(n_s, D_MODEL)

    (mixed, conv, sd, sr), (qkv_s, rest_s, gates_s) = _layer0_prompt(
        h, h_s, norm_mix_a[0], w_a, w_conv_a[0], a_log[0], dt_bias[0], gdn_gain[0], ret_gain[0], batch, seq)
    mixed_s, conv_s, sd_s, sr_s = _mix_ab_sample(qkv_s, rest_s, gates_s, state_conv_a[0], state_delta_a[0],
                                                 state_ret_b[0], w_conv_a[0], a_log[0], dt_bias[0], gdn_gain[0],
                                                 ret_gain[0])
    h, h_s = _out_mlp(h, mixed, h_s, mixed_s, w_out_a[0].astype(BF16), norm_ffn[0], w_ups, w_downs, 0, None, tm)

    (qk, v, opre, gates_c), (qk_s, v_s, opre_s, gates_cs) = _norm_proj(h, h_s, norm_mix_c[0], w_c, C_COLS, tm)
    hm, cm, nv, m = _mix_c_prompt(qk, v, opre, gates_c, batch, seq, b_gate_c[0], mlstm_gain[0])
    hm_s, cm_s, nv_s, m_s = _mix_c_sample(qk_s, v_s, opre_s, gates_cs, state_mlstm_C[0], state_mlstm_n[0],
                                          state_mlstm_m[0], b_gate_c[0], mlstm_gain[0])
    y, y_s = _out_mlp(h, hm, h_s, hm_s, w_out_c[0].astype(BF16), norm_ffn[1], w_ups, w_downs, 1, final_gain, tm)

    return (y.reshape(x_prompt.shape), y_s.reshape(x_sample.shape),
            conv[None], sd[None], sr[None], cm[None], nv[None], m[:, :, 0][None],
            conv_s[None], sd_s[None], sr_s[None], cm_s[None], nv_s[None], m_s[None])
```
